```python
import math
import jax, jax.numpy as jnp
from jax import lax
import numpy as np

D_MODEL = 1024
BATCH = 16
SEQ = 256
DEPTH = 1
DEC_BATCH = 4
DEC_SEQ = 2048
PAST_LEN = 256

GRID_W = 64
MIX_WIDTH = D_MODEL
RET_WIDTH = MIX_WIDTH // 2
RET_HEADS = 4
RET_DK = RET_WIDTH // RET_HEADS
RET_DV = RET_DK
S5_WIDTH = MIX_WIDTH - RET_WIDTH
S5_CH = 16
S5_GROUPS = S5_WIDTH // S5_CH
S5_STATE = 64
D_FF = 4 * D_MODEL
CTX_CHUNK = 64
N_DIR = 2
ALPHA = (2 * DEPTH) ** 0.25
BETA = (8 * DEPTH) ** -0.25
LN_EPS = 1e-5
IN_COLS = 4 * RET_WIDTH + S5_WIDTH

kernel_name = "hymba_retnet_s5_prefix_dit_step"


def _norm(x):
    xf = x.astype(jnp.float32)
    mu = jnp.mean(xf, axis=-1, keepdims=True)
    var = jnp.mean(jnp.square(xf - mu), axis=-1, keepdims=True)
    return ((xf - mu) * lax.rsqrt(var + LN_EPS)).astype(x.dtype)


def _ln_affine(x, g, b):
    return (_norm(x).astype(jnp.float32) * g.astype(jnp.float32) + b.astype(jnp.float32)).astype(x.dtype)


def _retention_forward(q, k, v, log_gamma, s0, chunk, n_chunks):
    b, h, _, dk = q.shape
    dv = v.shape[-1]
    qc = q.reshape(b, h, n_chunks, chunk, dk)
    kc = k.reshape(b, h, n_chunks, chunk, dk)
    vc = v.reshape(b, h, n_chunks, chunk, dv)
    pos = jnp.arange(chunk, dtype=jnp.float32)
    lg = log_gamma[:, None]
    rel = pos[:, None] - pos[None, :]
    decay_mat = jnp.where(rel[None] >= 0, jnp.exp(lg[..., None] * jnp.maximum(rel, 0.0)[None]), 0.0)
    scores = jnp.einsum('bhnid,bhnjd->bhnij', qc, kc) * decay_mat[None, :, None]
    o_intra = jnp.einsum('bhnij,bhnje->bhnie', scores, vc)
    k_dec = kc * jnp.exp(lg * (chunk - 1 - pos))[None, :, None, :, None]
    kv = jnp.einsum('bhnjd,bhnje->nbhde', k_dec, vc)
    chunk_decay = jnp.exp(log_gamma * chunk)[None, :, None, None]

    def step(s, kv_n):
        return chunk_decay * s + kv_n, s

    s_final, s_prev = lax.scan(step, s0, kv)
    o_inter = jnp.einsum('bhnid,nbhde->bhnie', qc, s_prev) * jnp.exp(lg * (pos + 1.0))[None, :, None, :, None]
    o = (o_intra + o_inter).reshape(b, h, n_chunks * chunk, dv)
    return o, s_final


def _bidir_retention(q, k, v, log_gamma2, s0_2, chunk, n_chunks):
    flip = lambda t: jnp.flip(t, axis=2)
    o_f, s_f = _retention_forward(q, k, v, log_gamma2[0], s0_2[:, 0], chunk, n_chunks)
    o_b, s_b = _retention_forward(flip(q), flip(k), flip(v), log_gamma2[1], s0_2[:, 1], chunk, n_chunks)
    return o_f + flip(o_b), jnp.stack([s_f, s_b], axis=1)


def _lin_combine(e1, e2):
    a1, x1 = e1
    a2, x2 = e2
    return a1 * a2, a2 * x1 + x2


def _s5_scan(u_c, a_re, a_im, log_dt, b_mat, s0):
    l = u_c.shape[1]
    a = lax.complex(a_re, a_im)
    a_dt = a * jnp.exp(log_dt)[:, None]
    a_bar = jnp.exp(a_dt)
    b_bar = ((a_bar - 1.0) / a)[..., None] * b_mat
    bu = jnp.einsum('gpc,blgc->blgp', b_bar, u_c)
    a_all = jnp.broadcast_to(a_bar, bu.shape)
    _, s = lax.associative_scan(_lin_combine, (a_all, bu), axis=1)
    steps = jnp.arange(1, l + 1, dtype=jnp.float32)[:, None, None]
    s = s + jnp.exp(steps * a_dt[None])[None] * s0[:, None]
    return s, s[:, -1]


def _bidir_s5(u, a_re2, a_im2, log_dt2, b_re, b_im, c_re, c_im, d_skip, s0_re2, s0_im2):
    f32 = lambda t: t.astype(jnp.float32)
    u_c = u.astype(jnp.complex64)
    b_mat = lax.complex(f32(b_re), f32(b_im))
    c_mat = lax.complex(f32(c_re), f32(c_im))
    s0 = lax.complex(f32(s0_re2), f32(s0_im2))
    s_f, fin_f = _s5_scan(u_c, f32(a_re2[0]), f32(a_im2[0]), f32(log_dt2[0]), b_mat, s0[:, 0])
    s_b, fin_b = _s5_scan(jnp.flip(u_c, axis=1), f32(a_re2[1]), f32(a_im2[1]), f32(log_dt2[1]), b_mat, s0[:, 1])
    s = s_f + jnp.flip(s_b, axis=1)
    y = jnp.real(jnp.einsum('gcp,blgp->blgc', c_mat, s)) + f32(d_skip).reshape(S5_GROUPS, S5_CH) * u
    fin = jnp.stack([fin_f, fin_b], axis=1)
    return y, jnp.real(fin), jnp.imag(fin)


def _trunk_layer(x, cond, ret_s0, s5_s0_re, s5_s0_im, chunk, n_chunks,
                 w_ada, b_ada, w_in, ret_decay, s5_a_re, s5_a_im, s5_log_dt,
                 s5_b_re, s5_b_im, s5_c_re, s5_c_im, s5_d, w_glu, b_glu, w_out,
                 ln1_g, ln1_b, w_ff1, b_ff1, w_ff2, b_ff2, ln2_g, ln2_b):
    bsz, l, _ = x.shape
    mod = jax.nn.silu(cond) @ w_ada + b_ada
    sh1, sc1, g1, sh2, sc2, g2 = jnp.split(mod[:, None, :], 6, axis=-1)
    h = _norm(x) * (1.0 + sc1) + sh1
    proj = h @ w_in
    q, k, v, g, u = jnp.split(proj, [RET_WIDTH, 2 * RET_WIDTH, 3 * RET_WIDTH, 4 * RET_WIDTH], axis=-1)
    heads = lambda t: t.reshape(bsz, l, RET_HEADS, RET_DK).transpose(0, 2, 1, 3).astype(jnp.float32)
    o_ret, ret_state = _bidir_retention(heads(q), heads(k) * (RET_DK ** -0.5), heads(v),
                                        jax.nn.log_sigmoid(ret_decay.astype(jnp.float32)),
                                        ret_s0.astype(jnp.float32), chunk, n_chunks)
    o_ret = _norm(o_ret).transpose(0, 2, 1, 3).reshape(bsz, l, RET_WIDTH)
    o_ret = o_ret * jax.nn.silu(g.astype(jnp.float32))
    y, s5_re, s5_im = _bidir_s5(u.reshape(bsz, l, S5_GROUPS, S5_CH).astype(jnp.float32),
                                s5_a_re, s5_a_im, s5_log_dt, s5_b_re, s5_b_im, s5_c_re, s5_c_im,
                                s5_d, s5_s0_re, s5_s0_im)
    y = jax.nn.gelu(y.reshape(bsz, l, S5_WIDTH))
    y = y * jax.nn.sigmoid(y @ w_glu.astype(jnp.float32) + b_glu.astype(jnp.float32))
    mix = jnp.concatenate([o_ret, y], axis=-1).astype(x.dtype) @ w_out
    x = _ln_affine(ALPHA * x + g1 * mix, ln1_g, ln1_b)
    h = _norm(x) * (1.0 + sc2) + sh2
    f = jnp.square(jax.nn.relu(h @ w_ff1 + b_ff1)) @ w_ff2 + b_ff2
    x = _ln_affine(ALPHA * x + g2 * f, ln2_g, ln2_b)
    return x, ret_state, s5_re, s5_im


def setup_inputs(seed: int = 0) -> dict:
    key = jax.random.key(seed)
    ks = jax.random.split(key, 40)
    nrm = lambda i, shape, s: s * jax.random.normal(ks[i], shape, jnp.float32)
    ret_init = jnp.asarray(np.log(2.0 ** (5 + np.arange(RET_HEADS)) - 1.0), jnp.float32)
    a_im_init = jnp.asarray(np.pi * np.arange(S5_STATE), jnp.float32)
    inp = {}
    inp["x_prompt"] = nrm(0, (BATCH, SEQ, D_MODEL), 1.0)
    inp["x_sample"] = nrm(1, (DEC_BATCH, DEC_SEQ, D_MODEL), 1.0)
    inp["state_ret"] = nrm(2, (DEC_BATCH, DEPTH, N_DIR, RET_HEADS, RET_DK, RET_DV), 1.0)
    inp["state_s5_re"] = nrm(3, (DEC_BATCH, DEPTH, N_DIR, S5_GROUPS, S5_STATE), 0.1)
    inp["state_s5_im"] = nrm(4, (DEC_BATCH, DEPTH, N_DIR, S5_GROUPS, S5_STATE), 0.1)
    inp["c"] = nrm(5, (DEC_BATCH, D_MODEL), 1.0)
    inp["c_ctx"] = nrm(6, (D_MODEL,), 1.0)
    inp["w_ada"] = nrm(7, (DEPTH, D_MODEL, 6 * D_MODEL), 0.5 * D_MODEL ** -0.5)
    inp["b_ada"] = nrm(8, (DEPTH, 6 * D_MODEL), 0.02)
    inp["w_in"] = nrm(9, (DEPTH, D_MODEL, IN_COLS), D_MODEL ** -0.5)
    inp["ret_decay"] = ret_init + nrm(10, (DEPTH, N_DIR, RET_HEADS), 0.05)
    inp["s5_a_re"] = -0.5 + nrm(11, (DEPTH, N_DIR, S5_GROUPS, S5_STATE), 0.01)
    inp["s5_a_im"] = a_im_init + nrm(12, (DEPTH, N_DIR, S5_GROUPS, S5_STATE), 0.01)
    inp["s5_log_dt"] = jax.random.uniform(ks[13], (DEPTH, N_DIR, S5_GROUPS), jnp.float32, math.log(1e-3), math.log(1e-1))
    inp["s5_b_re"] = nrm(14, (DEPTH, S5_GROUPS, S5_STATE, S5_CH), (2.0 * S5_CH) ** -0.5)
    inp["s5_b_im"] = nrm(15, (DEPTH, S5_GROUPS, S5_STATE, S5_CH), (2.0 * S5_CH) ** -0.5)
    inp["s5_c_re"] = nrm(16, (DEPTH, S5_GROUPS, S5_CH, S5_STATE), (2.0 * S5_STATE) ** -0.5)
    inp["s5_c_im"] = nrm(17, (DEPTH, S5_GROUPS, S5_CH, S5_STATE), (2.0 * S5_STATE) ** -0.5)
    inp["s5_d"] = nrm(18, (DEPTH, S5_WIDTH), 1.0)
    inp["w_glu"] = nrm(19, (DEPTH, S5_WIDTH, S5_WIDTH), S5_WIDTH ** -0.5)
    inp["b_glu"] = nrm(20, (DEPTH, S5_WIDTH), 0.02)
    inp["w_out"] = nrm(21, (DEPTH, MIX_WIDTH, D_MODEL), BETA * MIX_WIDTH ** -0.5)
    inp["ln1_g"] = 1.0 + nrm(22, (DEPTH, D_MODEL), 0.02)
    inp["ln1_b"] = nrm(23, (DEPTH, D_MODEL), 0.02)
    inp["w_ff1"] = nrm(24, (DEPTH, D_MODEL, D_FF), D_MODEL ** -0.5)
    inp["b_ff1"] = nrm(25, (DEPTH, D_FF), 0.02)
    inp["w_ff2"] = nrm(26, (DEPTH, D_FF, D_MODEL), BETA * D_FF ** -0.5)
    inp["b_ff2"] = nrm(27, (DEPTH, D_MODEL), 0.02)
    inp["ln2_g"] = 1.0 + nrm(28, (DEPTH, D_MODEL), 0.02)
    inp["ln2_b"] = nrm(29, (DEPTH, D_MODEL), 0.02)
    return inp


def reference(x_prompt, x_sample, state_ret, state_s5_re, state_s5_im, c, c_ctx,
              w_ada, b_ada, w_in, ret_decay, s5_a_re, s5_a_im, s5_log_dt,
              s5_b_re, s5_b_im, s5_c_re, s5_c_im, s5_d, w_glu, b_glu, w_out,
              ln1_g, ln1_b, w_ff1, b_ff1, w_ff2, b_ff2, ln2_g, ln2_b):
    n_ctx_chunks = x_prompt.shape[1] // CTX_CHUNK
    rows = x_sample.shape[1] // GRID_W
    bp = x_prompt.shape[0]
    zeros_ret = jnp.zeros((bp, N_DIR, RET_HEADS, RET_DK, RET_DV), jnp.float32)
    zeros_s5 = jnp.zeros((bp, N_DIR, S5_GROUPS, S5_STATE), jnp.float32)
    y_p, y_s = x_prompt, x_sample
    rets, s5rs, s5is = [], [], []
    for layer in range(DEPTH):
        params = (w_ada[layer], b_ada[layer], w_in[layer], ret_decay[layer], s5_a_re[layer], s5_a_im[layer],
                  s5_log_dt[layer], s5_b_re[layer], s5_b_im[layer], s5_c_re[layer], s5_c_im[layer], s5_d[layer],
                  w_glu[layer], b_glu[layer], w_out[layer], ln1_g[layer], ln1_b[layer], w_ff1[layer], b_ff1[layer],
                  w_ff2[layer], b_ff2[layer], ln2_g[layer], ln2_b[layer])
        y_p, r_st, s5_re, s5_im = _trunk_layer(y_p, c_ctx[None, :], zeros_ret, zeros_s5, zeros_s5,
                                               CTX_CHUNK, n_ctx_chunks, *params)
        rets.append(r_st)
        s5rs.append(s5_re)
        s5is.append(s5_im)
        y_s, _, _, _ = _trunk_layer(y_s, c, state_ret[:, layer], state_s5_re[:, layer], state_s5_im[:, layer],
                                    GRID_W, rows, *params)
    new_state_ret = jnp.stack(rets, axis=1)
    new_state_s5_re = jnp.stack(s5rs, axis=1)
    new_state_s5_im = jnp.stack(s5is, axis=1)
    return (y_p, y_s, new_state_ret, new_state_s5_re, new_state_s5_im)
```

```python
import functools

import jax
import jax.numpy as jnp
from jax import lax
from jax.experimental import pallas as pl
from jax.experimental.pallas import tpu as pltpu

D_MODEL = 1024
RET_HEADS = 4
RET_DK = 128
RET_WIDTH = RET_HEADS * RET_DK
S5_CH = 16
S5_GROUPS = 32
S5_STATE = 64
S5_WIDTH = S5_GROUPS * S5_CH
D_FF = 4 * D_MODEL
IN_COLS = 4 * RET_WIDTH + S5_WIDTH
ALPHA = 2.0 ** 0.25
LN_EPS = 1e-5

S5_T = 16
S5_ROW = S5_T * S5_CH
RET_CHUNK = 256
TOKEN_TILE = 512
VMEM_LIMIT = 56 * 1024 * 1024

_BF = jnp.bfloat16
_F32 = jnp.float32


def _norm_rows(x):
    mu = jnp.mean(x, axis=-1, keepdims=True)
    xc = x - mu
    var = jnp.mean(xc * xc, axis=-1, keepdims=True)
    return xc * lax.rsqrt(var + LN_EPS)


def _const_spec(shape):
    nd = len(shape)
    return pl.BlockSpec(shape, lambda *_: (0,) * nd, pipeline_mode=pl.Buffered(1))


def _mod_kernel(cond_ref, w_ref, b_ref, o_ref):
    c = cond_ref[...]
    s = (c * jax.nn.sigmoid(c)).astype(_BF)
    o_ref[...] = jnp.dot(s, w_ref[...].astype(_BF), preferred_element_type=_F32) + b_ref[...]


def _modulation(cond8, w_ada, b_ada):
    n = w_ada.shape[1]
    tn = 1024
    return pl.pallas_call(
        _mod_kernel,
        grid=(n // tn,),
        in_specs=[pl.BlockSpec((8, D_MODEL), lambda j: (0, 0)),
                  pl.BlockSpec((D_MODEL, tn), lambda j: (0, j)),
                  pl.BlockSpec((1, tn), lambda j: (0, j))],
        out_specs=pl.BlockSpec((8, tn), lambda j: (0, j)),
        out_shape=jax.ShapeDtypeStruct((8, n), _F32),
        compiler_params=pltpu.CompilerParams(dimension_semantics=("arbitrary",)),
        name="modulation",
    )(cond8, w_ada, b_ada.reshape(1, n))


def _proj_kernel(x_ref, mod_ref, w_ref, o_ref):
    x = x_ref[0]
    sh = mod_ref[0, 0:1, :]
    sc = mod_ref[0, 1:2, :]
    h = (_norm_rows(x) * (1.0 + sc) + sh).astype(_BF)
    o_ref[0] = jnp.dot(h, w_ref[...], preferred_element_type=_F32).astype(_BF)


def _projection(x, mod, mod_row, w_in_bf):
    b, l, _ = x.shape
    tm = min(TOKEN_TILE, l)
    return pl.pallas_call(
        _proj_kernel,
        grid=(b, l // tm),
        in_specs=[pl.BlockSpec((1, tm, D_MODEL), lambda i, j: (i, j, 0)),
                  pl.BlockSpec((1, 6, D_MODEL), lambda i, j: (mod_row(i), 0, 0)),
                  _const_spec((D_MODEL, IN_COLS))],
        out_specs=pl.BlockSpec((1, tm, IN_COLS), lambda i, j: (i, j, 0)),
        out_shape=jax.ShapeDtypeStruct((b, l, IN_COLS), _BF),
        compiler_params=pltpu.CompilerParams(
            dimension_semantics=("arbitrary", "arbitrary"), vmem_limit_bytes=VMEM_LIMIT),
        name="projection",
    )(x, mod, w_in_bf)


def _ret_kernel(*refs, chunk, n_chunks, has_s0):
    if has_s0:
        dec_ref, q_ref, k_ref, v_ref, g_ref, s0_ref, o_ref, st_ref, kv_scr = refs
    else:
        dec_ref, q_ref, k_ref, v_ref, g_ref, o_ref, st_ref, kv_scr = refs
        s0_ref = None
    c = chunk
    hd = pl.program_id(1)

    def log_gamma(d):
        z = jnp.full((1, 1), dec_ref[d, hd], _F32)
        return jnp.minimum(z, 0.0) - jnp.log(1.0 + jnp.exp(-jnp.abs(z)))

    lg_f, lg_b = log_gamma(0), log_gamma(1)
    scale = RET_DK ** -0.5
    pos = lax.broadcasted_iota(jnp.int32, (c, 1), 0).astype(_F32)
    kdec_f = jnp.exp(lg_f * (c - 1.0 - pos)) * scale
    kdec_b = jnp.exp(lg_b * pos) * scale
    qdec_f = jnp.exp(lg_f * (pos + 1.0))
    qdec_b = jnp.exp(lg_b * (c - pos))
    ri = lax.broadcasted_iota(jnp.int32, (c, c), 0)
    ci = lax.broadcasted_iota(jnp.int32, (c, c), 1)
    rel = (ri - ci).astype(_F32)
    dmat = (jnp.where(ri >= ci, jnp.exp(lg_f * jnp.maximum(rel, 0.0)), 0.0)
            + jnp.where(ci >= ri, jnp.exp(lg_b * jnp.maximum(-rel, 0.0)), 0.0)) * scale
    cdec_f = jnp.exp(lg_f * float(c))
    cdec_b = jnp.exp(lg_b * float(c))

    for n in range(n_chunks):
        kc = k_ref[0, n * c:(n + 1) * c, :].astype(_F32)
        vc = v_ref[0, n * c:(n + 1) * c, :]
        kcat = jnp.concatenate([(kc * kdec_f).astype(_BF), (kc * kdec_b).astype(_BF)], axis=1)
        kv_scr[n] = lax.dot_general(kcat, vc, (((0,), (0,)), ((), ())), preferred_element_type=_F32)

    if has_s0:
        carry_f = s0_ref[0, 0, 0]
        carry_b = s0_ref[0, 1, 0]
    else:
        carry_f = jnp.zeros((RET_DK, RET_DK), _F32)
        carry_b = jnp.zeros((RET_DK, RET_DK), _F32)
    for n in range(n_chunks):
        inc = kv_scr[n, 0:RET_DK, :]
        kv_scr[n, 0:RET_DK, :] = carry_f
        carry_f = cdec_f * carry_f + inc
    for n in range(n_chunks - 1, -1, -1):
        inc = kv_scr[n, RET_DK:2 * RET_DK, :]
        kv_scr[n, RET_DK:2 * RET_DK, :] = carry_b
        carry_b = cdec_b * carry_b + inc
    st_ref[0, 0, 0] = carry_f
    st_ref[0, 1, 0] = carry_b

    for n in range(n_chunks):
        sl = slice(n * c, (n + 1) * c)
        qc = q_ref[0, sl, :]
        kc = k_ref[0, sl, :]
        vc = v_ref[0, sl, :]
        scores = lax.dot_general(qc, kc, (((1,), (1,)), ((), ())), preferred_element_type=_F32)
        o = jnp.dot((scores * dmat).astype(_BF), vc, preferred_element_type=_F32)
        qf = qc.astype(_F32)
        qcat = jnp.concatenate([(qf * qdec_f).astype(_BF), (qf * qdec_b).astype(_BF)], axis=1)
        o = o + jnp.dot(qcat, kv_scr[n].astype(_BF), preferred_element_type=_F32)
        gate = g_ref[0, sl, :].astype(_F32)
        o_ref[0, sl, :] = (_norm_rows(o) * (gate * jax.nn.sigmoid(gate))).astype(_BF)


def _retention(proj, ret_decay, s0):
    b, l, _ = proj.shape
    c = min(RET_CHUNK, l)
    n_chunks = l // c
    has_s0 = s0 is not None
    col = lambda off: pl.BlockSpec((1, l, RET_DK), lambda i, h: (i, 0, off + h))
    in_specs = [pl.BlockSpec(memory_space=pltpu.SMEM), col(0), col(4), col(8), col(12)]
    args = [ret_decay, proj, proj, proj, proj]
    if has_s0:
        in_specs.append(pl.BlockSpec((1, 2, 1, RET_DK, RET_DK), lambda i, h: (i, 0, h, 0, 0)))
        args.append(s0)
    return pl.pallas_call(
        functools.partial(_ret_kernel, chunk=c, n_chunks=n_chunks, has_s0=has_s0),
        grid=(b, RET_HEADS),
        in_specs=in_specs,
        out_specs=[pl.BlockSpec((1, l, RET_DK), lambda i, h: (i, 0, h)),
                   pl.BlockSpec((1, 2, 1, RET_DK, RET_DK), lambda i, h: (i, 0, h, 0, 0))],
        out_shape=[jax.ShapeDtypeStruct((b, l, RET_WIDTH), _BF),
                   jax.ShapeDtypeStruct((b, 2, RET_HEADS, RET_DK, RET_DK), _F32)],
        scratch_shapes=[pltpu.VMEM((n_chunks, 2 * RET_DK, RET_DK), _F32)],
        compiler_params=pltpu.CompilerParams(
            dimension_semantics=("arbitrary", "arbitrary"), vmem_limit_bytes=VMEM_LIMIT),
        name="retention",
    )(*args)


def _swap_halves(z):
    return pltpu.roll(z, S5_STATE, axis=z.ndim - 1)


def _s5_prep_kernel(ar_ref, ai_ref, ldt_ref, bt_ref, cp_ref, wst_ref, cout_ref, kall_ref, apow_ref):
    lane = lax.broadcasted_iota(jnp.int32, (1, 2 * S5_STATE), 1)
    lo = lane < S5_STATE
    sgn = jnp.where(lo, -1.0, 1.0).astype(_F32)
    tau = lax.broadcasted_iota(jnp.int32, (S5_T + 8, 1), 0).astype(_F32)
    bt = bt_ref[0]
    cp = cp_ref[0]
    cc = cp * (-sgn)
    wstacks = []
    for d in range(2):
        ar = ar_ref[0, d:d + 1, :]
        ai = ai_ref[0, d:d + 1, :]
        dt = jnp.exp(ldt_ref[0, d:d + 1, :])
        re_dt = ar * dt
        im_dt = ai * dt
        mag = jnp.exp(tau * re_dt)
        ang = tau * im_dt
        pr2 = mag * jnp.cos(ang)
        pi2 = mag * jnp.sin(ang)
        pi2s = pi2 * sgn
        x2 = pr2[1:2, :] - 1.0
        y2 = pi2[1:2, :]
        den = ar * ar + ai * ai
        coef_re = (x2 * ar + y2 * ai) / den
        coef_im_s = ((y2 * ar - x2 * ai) / den) * sgn
        bb = coef_re * bt + coef_im_s * _swap_halves(bt)
        bbs = _swap_halves(bb)
        cps = _swap_halves(cp)
        w_rows = []
        for m in range(S5_T):
            e = (S5_T - 1 - m) if d == 0 else m
            w = pr2[e:e + 1, :] * bb + pi2s[e:e + 1, :] * bbs
            wst_ref[0, m * S5_CH:(m + 1) * S5_CH, d * 128:(d + 1) * 128] = w.astype(_BF)
            w_rows.append(w)
        for i in range(S5_T):
            e = (i + 1) if d == 0 else (S5_T - i)
            gmat = pr2[e:e + 1, :] * cp + pi2s[e:e + 1, :] * cps
            cout_ref[0, i * S5_CH:(i + 1) * S5_CH, d * 128:(d + 1) * 128] = (gmat * (-sgn)).astype(_BF)
        order = w_rows[::-1] if d == 0 else w_rows
        wstacks.append(jnp.concatenate(order, axis=0))
        apow_ref[0, 2 * d:2 * d + 1, :] = pr2[S5_T:S5_T + 1, :]
        apow_ref[0, 2 * d + 1:2 * d + 2, :] = pi2s[S5_T:S5_T + 1, :]
    for d in range(2):
        kall_ref[0, d] = lax.dot_general(cc, wstacks[d], (((1,), (1,)), ((), ())),
                                         preferred_element_type=_F32,
                                         precision=lax.Precision.HIGHEST)


def _s5_operators(s5_a_re, s5_a_im, s5_log_dt, s5_b_re, s5_b_im, s5_c_re, s5_c_im):
    g = S5_GROUPS
    dup = lambda a: jnp.concatenate([a, a], axis=-1).transpose(1, 0, 2)
    ar2 = dup(s5_a_re)
    ai2 = dup(s5_a_im)
    ldt = s5_log_dt.T.reshape(g, 2, 1)
    btp = jnp.concatenate([s5_b_re.transpose(0, 2, 1), s5_b_im.transpose(0, 2, 1)], axis=-1)
    cpk = jnp.concatenate([s5_c_re, s5_c_im], axis=-1)
    gspec = lambda *shape: pl.BlockSpec((1,) + shape, lambda i: (i,) + (0,) * len(shape))
    wst, cout_t, kall, apow = pl.pallas_call(
        _s5_prep_kernel,
        grid=(g,),
        in_specs=[gspec(2, 128), gspec(2, 128), gspec(2, 1), gspec(S5_CH, 128), gspec(S5_CH, 128)],
        out_specs=[gspec(S5_ROW, 256), gspec(S5_ROW, 256), gspec(2, S5_CH, S5_ROW), gspec(4, 128)],
        out_shape=[jax.ShapeDtypeStruct((g, S5_ROW, 256), _BF),
                   jax.ShapeDtypeStruct((g, S5_ROW, 256), _BF),
                   jax.ShapeDtypeStruct((g, 2, S5_CH, S5_ROW), _F32),
                   jax.ShapeDtypeStruct((g, 4, 128), _F32)],
        compiler_params=pltpu.CompilerParams(dimension_semantics=("arbitrary",)),
        name="s5_operators",
    )(ar2, ai2, ldt, btp, cpk)
    kk = kall.reshape(g, 2, S5_CH, S5_T, S5_CH)
    mi = jnp.arange(S5_T)[:, None]
    ti = jnp.arange(S5_T)[None, :]
    kf = jnp.where((ti >= mi)[None, None, :, :, None],
                   jnp.take(kk[:, 0], jnp.clip(ti - mi, 0, S5_T - 1), axis=2), 0.0)
    kb = jnp.where((mi >= ti)[None, None, :, :, None],
                   jnp.take(kk[:, 1], jnp.clip(mi - ti, 0, S5_T - 1), axis=2), 0.0)
    m_op = (kf + kb).transpose(0, 2, 4, 3, 1).reshape(g, S5_ROW, S5_ROW).astype(_BF)
    return m_op, wst, cout_t, apow


def _s5_kernel(x_ref, m_ref, wst_ref, cout_ref, apow_ref, s0_ref, y_ref, fin_ref, loc_scr, prev_scr,
               *, nb, nk):
    x = x_ref[0]
    loc_scr[...] = jnp.dot(x, wst_ref[0], preferred_element_type=_F32)
    ar_f = apow_ref[0, 0:1, :]
    ai_f = apow_ref[0, 1:2, :]
    ar_b = apow_ref[0, 2:3, :]
    ai_b = apow_ref[0, 3:4, :]
    carry_f = s0_ref[0, :, 0:128]
    carry_b = s0_ref[0, :, 128:256]
    for k in range(nk):
        kb = nk - 1 - k
        rows_f = slice(k * nb, (k + 1) * nb)
        rows_b = slice(kb * nb, (kb + 1) * nb)
        prev_scr[rows_f, 0:128] = carry_f
        prev_scr[rows_b, 128:256] = carry_b
        carry_f = ar_f * carry_f + ai_f * _swap_halves(carry_f) + loc_scr[rows_f, 0:128]
        carry_b = ar_b * carry_b + ai_b * _swap_halves(carry_b) + loc_scr[rows_b, 128:256]
    fin_ref[0, :, 0:128] = carry_f
    fin_ref[0, :, 128:256] = carry_b
    y = jnp.dot(x, m_ref[0], preferred_element_type=_F32)
    y = y + lax.dot_general(prev_scr[...].astype(_BF), cout_ref[0], (((1,), (1,)), ((), ())),
                            preferred_element_type=_F32)
    y_ref[0] = y


def _s5_scan(xg, m_op, wst, cout_t, apow, s0p, nb, nk):
    g = S5_GROUPS
    rows = nk * nb
    gspec = lambda *shape: pl.BlockSpec((1,) + shape, lambda i: (i,) + (0,) * len(shape))
    return pl.pallas_call(
        functools.partial(_s5_kernel, nb=nb, nk=nk),
        grid=(g,),
        in_specs=[gspec(rows, S5_ROW), gspec(S5_ROW, S5_ROW), gspec(S5_ROW, 256), gspec(S5_ROW, 256),
                  gspec(4, 128), gspec(nb, 256)],
        out_specs=[gspec(rows, S5_ROW), gspec(nb, 256)],
        out_shape=[jax.ShapeDtypeStruct((g, rows, S5_ROW), _F32),
                   jax.ShapeDtypeStruct((g, nb, 256), _F32)],
        scratch_shapes=[pltpu.VMEM((rows, 256), _F32), pltpu.VMEM((rows, 256), _F32)],
        compiler_params=pltpu.CompilerParams(
            dimension_semantics=("arbitrary",), vmem_limit_bytes=VMEM_LIMIT),
        name="s5_scan",
    )(xg, m_op, wst, cout_t, apow, s0p)


def _post_kernel(x_ref, o_ref, yr_ref, u_ref, mod_ref, dsk_ref, wglu_ref, bglu_ref, wout_ref,
                 l1g_ref, l1b_ref, w1_ref, b1_ref, w2_ref, b2_ref, l2g_ref, l2b_ref, out_ref):
    x = x_ref[0]
    g1 = mod_ref[0, 2:3, :]
    sh2 = mod_ref[0, 3:4, :]
    sc2 = mod_ref[0, 4:5, :]
    g2 = mod_ref[0, 5:6, :]
    y = yr_ref[0] + dsk_ref[...] * u_ref[0].astype(_F32)
    y = jax.nn.gelu(y)
    y = y * jax.nn.sigmoid(jnp.dot(y.astype(_BF), wglu_ref[...], preferred_element_type=_F32)
                           + bglu_ref[...])
    mix = jnp.dot(o_ref[0], wout_ref[0:RET_WIDTH, :], preferred_element_type=_F32)
    mix = mix + jnp.dot(y.astype(_BF), wout_ref[RET_WIDTH:, :], preferred_element_type=_F32)
    x1 = _norm_rows(ALPHA * x + g1 * mix) * l1g_ref[...] + l1b_ref[...]
    h = (_norm_rows(x1) * (1.0 + sc2) + sh2).astype(_BF)
    acc = jnp.zeros_like(x1)
    fc = 1024
    for j in range(D_FF // fc):
        a = jnp.dot(h, w1_ref[:, j * fc:(j + 1) * fc], preferred_element_type=_F32) + b1_ref[:, j * fc:(j + 1) * fc]
        a = jnp.square(jnp.maximum(a, 0.0)).astype(_BF)
        acc = acc + jnp.dot(a, w2_ref[j * fc:(j + 1) * fc, :], preferred_element_type=_F32)
    f = acc + b2_ref[...]
    out_ref[0] = _norm_rows(ALPHA * x1 + g2 * f) * l2g_ref[...] + l2b_ref[...]


def _post(x, o_ret, y_raw, proj, mod, mod_row, pw):
    b, l, _ = x.shape
    tm = min(TOKEN_TILE, l)
    tok = lambda w: pl.BlockSpec((1, tm, w), lambda i, j: (i, j, 0))
    row = lambda n: _const_spec((1, n))
    return pl.pallas_call(
        _post_kernel,
        grid=(b, l // tm),
        in_specs=[tok(D_MODEL), tok(RET_WIDTH), tok(S5_WIDTH),
                  pl.BlockSpec((1, tm, S5_WIDTH), lambda i, j: (i, j, 4)),
                  pl.BlockSpec((1, 6, D_MODEL), lambda i, j: (mod_row(i), 0, 0)),
                  row(S5_WIDTH), _const_spec((S5_WIDTH, S5_WIDTH)), row(S5_WIDTH),
                  _const_spec((D_MODEL, D_MODEL)), row(D_MODEL), row(D_MODEL),
                  _const_spec((D_MODEL, D_FF)), row(D_FF), _const_spec((D_FF, D_MODEL)), row(D_MODEL),
                  row(D_MODEL), row(D_MODEL)],
        out_specs=tok(D_MODEL),
        out_shape=jax.ShapeDtypeStruct((b, l, D_MODEL), _F32),
        compiler_params=pltpu.CompilerParams(
            dimension_semantics=("arbitrary", "arbitrary"), vmem_limit_bytes=VMEM_LIMIT),
        name="post",
    )(x, o_ret, y_raw, proj, mod, *pw)


def _layer_pass(x, mod, mod_row, ret_s0, s5_s0_re, s5_s0_im, w_in_bf, ret_decay, s5_ops, post_w):
    b, l, _ = x.shape
    nk = l // S5_T
    g = S5_GROUPS
    proj = _projection(x, mod, mod_row, w_in_bf)
    o_ret, ret_state = _retention(proj, ret_decay, ret_s0)
    u = proj[:, :, 4 * RET_WIDTH:]
    xg = u.reshape(b, nk, S5_T, g, S5_CH).transpose(3, 1, 0, 2, 4).reshape(g, nk * b, S5_ROW)
    if s5_s0_re is None:
        s0p = jnp.zeros((g, b, 256), _F32)
    else:
        s0p = jnp.concatenate([s5_s0_re, s5_s0_im], axis=-1)
        s0p = s0p.transpose(2, 0, 1, 3).reshape(g, b, 256)
    yg, fin = _s5_scan(xg, *s5_ops, s0p, b, nk)
    y_raw = yg.reshape(g, nk, b, S5_T, S5_CH).transpose(2, 1, 3, 0, 4).reshape(b, l, S5_WIDTH)
    fin = fin.reshape(g, b, 2, 2, S5_STATE).transpose(1, 2, 0, 3, 4)
    out = _post(x, o_ret, y_raw, proj, mod, mod_row, post_w)
    return out, ret_state, fin[:, :, :, 0, :], fin[:, :, :, 1, :]


def kernel(x_prompt, x_sample, state_ret, state_s5_re, state_s5_im, c, c_ctx, w_ada, b_ada, w_in,
           ret_decay, s5_a_re, s5_a_im, s5_log_dt, s5_b_re, s5_b_im, s5_c_re, s5_c_im, s5_d, w_glu,
           b_glu, w_out, ln1_g, ln1_b, w_ff1, b_ff1, w_ff2, b_ff2, ln2_g, ln2_b):
    depth = w_ada.shape[0]
    bs = x_sample.shape[0]
    y_p, y_s = x_prompt, x_sample
    rets, s5rs, s5is = [], [], []
    for layer in range(depth):
        cond8 = jnp.concatenate(
            [c_ctx[None, :], c, jnp.zeros((8 - 1 - bs, D_MODEL), _F32)], axis=0)
        mod = _modulation(cond8, w_ada[layer], b_ada[layer]).reshape(8, 6, D_MODEL)
        s5_ops = _s5_operators(s5_a_re[layer], s5_a_im[layer], s5_log_dt[layer], s5_b_re[layer],
                               s5_b_im[layer], s5_c_re[layer], s5_c_im[layer])
        w_in_bf = w_in[layer].astype(_BF)
        r2 = lambda a: a[layer].reshape(1, -1)
        post_w = (r2(s5_d), w_glu[layer].astype(_BF), r2(b_glu), w_out[layer].astype(_BF),
                  r2(ln1_g), r2(ln1_b), w_ff1[layer].astype(_BF), r2(b_ff1),
                  w_ff2[layer].astype(_BF), r2(b_ff2), r2(ln2_g), r2(ln2_b))
        y_p, r_st, s5_re, s5_im = _layer_pass(
            y_p, mod, lambda i: 0, None, None, None, w_in_bf, ret_decay[layer], s5_ops, post_w)
        rets.append(r_st)
        s5rs.append(s5_re)
        s5is.append(s5_im)
        y_s, _, _, _ = _layer_pass(
            y_s, mod, lambda i: i + 1, state_ret[:, layer], state_s5_re[:, layer],
            state_s5_im[:, layer], w_in_bf, ret_decay[layer], s5_ops, post_w)
    return (y_p, y_s, jnp.stack(rets, axis=1), jnp.stack(s5rs, axis=1), jnp.stack(s5is, axis=1))
```

```python
import functools

import jax
import jax.numpy as jnp
from jax import lax
from jax.experimental import pallas as pl
from jax.experimental.pallas import tpu as pltpu

D_MODEL = 1024
RET_HEADS = 4
RET_DK = 128
RET_WIDTH = RET_HEADS * RET_DK
S5_CH = 16
S5_GROUPS = 32
S5_STATE = 64
S5_WIDTH = S5_GROUPS * S5_CH
D_FF = 4 * D_MODEL
IN_COLS = 4 * RET_WIDTH + S5_WIDTH
ALPHA = 2.0 ** 0.25
LN_EPS = 1e-5

S5_T = 16
S5_ROW = S5_T * S5_CH
RET_CHUNK = 256
TOKEN_TILE = 512
VMEM_LIMIT = 56 * 1024 * 1024

_BF = jnp.bfloat16
_F32 = jnp.float32


def _norm_rows(x):
    mu = jnp.mean(x, axis=-1, keepdims=True)
    xc = x - mu
    var = jnp.mean(xc * xc, axis=-1, keepdims=True)
    return xc * lax.rsqrt(var + LN_EPS)


def _const_spec(shape):
    nd = len(shape)
    return pl.BlockSpec(shape, lambda *_: (0,) * nd, pipeline_mode=pl.Buffered(1))


def _granule_transpose(v):
    lane = lax.broadcasted_iota(jnp.int32, (1, 128), 1)
    v = list(v)
    for d in (4, 2, 1):
        bit = ((lane // S5_CH) & d) != 0
        nv = list(v)
        for a in range(8):
            if a & d:
                continue
            lo, hi = v[a], v[a + d]
            nv[a] = jnp.where(bit, pltpu.roll(hi, S5_CH * d, axis=1), lo)
            nv[a + d] = jnp.where(bit, hi, pltpu.roll(lo, 128 - S5_CH * d, axis=1))
        v = nv
    return v


def _mod_kernel(cond_ref, w_ref, b_ref, o_ref):
    c = cond_ref[...]
    s = (c * jax.nn.sigmoid(c)).astype(_BF)
    o_ref[...] = jnp.dot(s, w_ref[...].astype(_BF), preferred_element_type=_F32) + b_ref[...]


def _modulation(cond8, w_ada, b_ada):
    n = w_ada.shape[1]
    tn = 1024
    return pl.pallas_call(
        _mod_kernel,
        grid=(n // tn,),
        in_specs=[pl.BlockSpec((8, D_MODEL), lambda j: (0, 0)),
                  pl.BlockSpec((D_MODEL, tn), lambda j: (0, j)),
                  pl.BlockSpec((1, tn), lambda j: (0, j))],
        out_specs=pl.BlockSpec((8, tn), lambda j: (0, j)),
        out_shape=jax.ShapeDtypeStruct((8, n), _F32),
        compiler_params=pltpu.CompilerParams(dimension_semantics=("arbitrary",)),
        name="modulation",
    )(cond8, w_ada, b_ada.reshape(1, n))


def _proj_kernel(x_ref, mod_ref, w_ref, o_ref, xg_ref, u_scr, *, nb, tm):
    x = x_ref[0]
    sh = mod_ref[0, 0:1, :]
    sc = mod_ref[0, 1:2, :]
    h = (_norm_rows(x) * (1.0 + sc) + sh).astype(_BF)
    res = jnp.dot(h, w_ref[...], preferred_element_type=_F32)
    o_ref[0] = res.astype(_BF)
    for j in range(S5_WIDTH // 128):
        u_scr[j] = res[:, 4 * RET_WIDTH + 128 * j:4 * RET_WIDTH + 128 * (j + 1)]
    b = pl.program_id(1)
    for j in range(S5_WIDTH // 128):
        for hf in range(2):
            for q in range(tm // 128):
                v = [u_scr[j, pl.ds(128 * q + 8 * hf + a, 8, stride=S5_T), :] for a in range(8)]
                w = _granule_transpose(v)
                for gg in range(8):
                    xg_ref[8 * j + gg, hf, pl.ds(8 * q * nb + b, 8, stride=nb), :] = w[gg]


def _projection(x, mod, mod_row, w_in_bf):
    b, l, _ = x.shape
    tm = min(TOKEN_TILE, l)
    rows = (tm // S5_T) * b
    return pl.pallas_call(
        functools.partial(_proj_kernel, nb=b, tm=tm),
        grid=(l // tm, b),
        in_specs=[pl.BlockSpec((1, tm, D_MODEL), lambda j, i: (i, j, 0)),
                  pl.BlockSpec((1, 6, D_MODEL), lambda j, i: (mod_row(i), 0, 0)),
                  _const_spec((D_MODEL, IN_COLS))],
        out_specs=[pl.BlockSpec((1, tm, IN_COLS), lambda j, i: (i, j, 0)),
                   pl.BlockSpec((S5_GROUPS, 2, rows, 128), lambda j, i: (0, 0, j, 0))],
        out_shape=[jax.ShapeDtypeStruct((b, l, IN_COLS), _BF),
                   jax.ShapeDtypeStruct((S5_GROUPS, 2, (l // S5_T) * b, 128), _F32)],
        scratch_shapes=[pltpu.VMEM((S5_WIDTH // 128, tm, 128), _F32)],
        compiler_params=pltpu.CompilerParams(
            dimension_semantics=("arbitrary", "arbitrary"), vmem_limit_bytes=VMEM_LIMIT),
        name="projection",
    )(x, mod, w_in_bf)


def _ret_kernel(*refs, chunk, n_chunks, has_s0):
    if has_s0:
        dec_ref, q_ref, k_ref, v_ref, g_ref, s0_ref, o_ref, st_ref, kv_scr = refs
    else:
        dec_ref, q_ref, k_ref, v_ref, g_ref, o_ref, st_ref, kv_scr = refs
        s0_ref = None
    c = chunk
    hd = pl.program_id(1)

    def log_gamma(d):
        z = jnp.full((1, 1), dec_ref[d, hd], _F32)
        return jnp.minimum(z, 0.0) - jnp.log(1.0 + jnp.exp(-jnp.abs(z)))

    lg_f, lg_b = log_gamma(0), log_gamma(1)
    scale = RET_DK ** -0.5
    pos = lax.broadcasted_iota(jnp.int32, (c, 1), 0).astype(_F32)
    kdec_f = jnp.exp(lg_f * (c - 1.0 - pos)) * scale
    kdec_b = jnp.exp(lg_b * pos) * scale
    qdec_f = jnp.exp(lg_f * (pos + 1.0))
    qdec_b = jnp.exp(lg_b * (c - pos))
    ri = lax.broadcasted_iota(jnp.int32, (c, c), 0)
    ci = lax.broadcasted_iota(jnp.int32, (c, c), 1)
    rel = (ri - ci).astype(_F32)
    dmat = (jnp.where(ri >= ci, jnp.exp(lg_f * jnp.maximum(rel, 0.0)), 0.0)
            + jnp.where(ci >= ri, jnp.exp(lg_b * jnp.maximum(-rel, 0.0)), 0.0)) * scale
    cdec_f = jnp.exp(lg_f * float(c))
    cdec_b = jnp.exp(lg_b * float(c))

    for n in range(n_chunks):
        kc = k_ref[0, n * c:(n + 1) * c, :].astype(_F32)
        vc = v_ref[0, n * c:(n + 1) * c, :]
        kcat = jnp.concatenate([(kc * kdec_f).astype(_BF), (kc * kdec_b).astype(_BF)], axis=1)
        kv_scr[n] = lax.dot_general(kcat, vc, (((0,), (0,)), ((), ())), preferred_element_type=_F32)

    if has_s0:
        carry_f = s0_ref[0, 0, 0]
        carry_b = s0_ref[0, 1, 0]
    else:
        carry_f = jnp.zeros((RET_DK, RET_DK), _F32)
        carry_b = jnp.zeros((RET_DK, RET_DK), _F32)
    for n in range(n_chunks):
        inc = kv_scr[n, 0:RET_DK, :]
        kv_scr[n, 0:RET_DK, :] = carry_f
        carry_f = cdec_f * carry_f + inc
    for n in range(n_chunks - 1, -1, -1):
        inc = kv_scr[n, RET_DK:2 * RET_DK, :]
        kv_scr[n, RET_DK:2 * RET_DK, :] = carry_b
        carry_b = cdec_b * carry_b + inc
    st_ref[0, 0, 0] = carry_f
    st_ref[0, 1, 0] = carry_b

    for n in range(n_chunks):
        sl = slice(n * c, (n + 1) * c)
        qc = q_ref[0, sl, :]
        kc = k_ref[0, sl, :]
        vc = v_ref[0, sl, :]
        scores = lax.dot_general(qc, kc, (((1,), (1,)), ((), ())), preferred_element_type=_F32)
        o = jnp.dot((scores * dmat).astype(_BF), vc, preferred_element_type=_F32)
        qf = qc.astype(_F32)
        qcat = jnp.concatenate([(qf * qdec_f).astype(_BF), (qf * qdec_b).astype(_BF)], axis=1)
        o = o + jnp.dot(qcat, kv_scr[n].astype(_BF), preferred_element_type=_F32)
        gate = g_ref[0, sl, :].astype(_F32)
        o_ref[0, sl, :] = (_norm_rows(o) * (gate * jax.nn.sigmoid(gate))).astype(_BF)


def _retention(proj, ret_decay, s0):
    b, l, _ = proj.shape
    c = min(RET_CHUNK, l)
    n_chunks = l // c
    has_s0 = s0 is not None
    col = lambda off: pl.BlockSpec((1, l, RET_DK), lambda i, h: (i, 0, off + h))
    in_specs = [pl.BlockSpec(memory_space=pltpu.SMEM), col(0), col(4), col(8), col(12)]
    args = [ret_decay, proj, proj, proj, proj]
    if has_s0:
        in_specs.append(pl.BlockSpec((1, 2, 1, RET_DK, RET_DK), lambda i, h: (i, 0, h, 0, 0)))
        args.append(s0)
    return pl.pallas_call(
        functools.partial(_ret_kernel, chunk=c, n_chunks=n_chunks, has_s0=has_s0),
        grid=(b, RET_HEADS),
        in_specs=in_specs,
        out_specs=[pl.BlockSpec((1, l, RET_DK), lambda i, h: (i, 0, h)),
                   pl.BlockSpec((1, 2, 1, RET_DK, RET_DK), lambda i, h: (i, 0, h, 0, 0))],
        out_shape=[jax.ShapeDtypeStruct((b, l, RET_WIDTH), _BF),
                   jax.ShapeDtypeStruct((b, 2, RET_HEADS, RET_DK, RET_DK), _F32)],
        scratch_shapes=[pltpu.VMEM((n_chunks, 2 * RET_DK, RET_DK), _F32)],
        compiler_params=pltpu.CompilerParams(
            dimension_semantics=("arbitrary", "arbitrary"), vmem_limit_bytes=VMEM_LIMIT),
        name="retention",
    )(*args)


def _swap_halves(z):
    return pltpu.roll(z, S5_STATE, axis=z.ndim - 1)


def _s5_prep_kernel(ar_ref, ai_ref, ldt_ref, bt_ref, cp_ref, m_ref, wst_ref, cout_ref, apow_ref):
    lane = lax.broadcasted_iota(jnp.int32, (1, 2 * S5_STATE), 1)
    lo = lane < S5_STATE
    sgn = jnp.where(lo, -1.0, 1.0).astype(_F32)
    tau = lax.broadcasted_iota(jnp.int32, (S5_T + 8, 1), 0).astype(_F32)
    bt = bt_ref[0]
    cp = cp_ref[0]
    cc = cp * (-sgn)
    cps = _swap_halves(cp)
    gens = []
    cout_rows = [[], []]
    for d in range(2):
        ar = ar_ref[0, d:d + 1, :]
        ai = ai_ref[0, d:d + 1, :]
        dt = jnp.exp(ldt_ref[0, d:d + 1, :])
        re_dt = ar * dt
        im_dt = ai * dt
        mag = jnp.exp(tau * re_dt)
        ang = tau * im_dt
        pr2 = mag * jnp.cos(ang)
        pi2s = mag * jnp.sin(ang) * sgn
        x2 = pr2[1:2, :] - 1.0
        y2 = pi2s[1:2, :] * sgn
        den = ar * ar + ai * ai
        coef_re = (x2 * ar + y2 * ai) / den
        coef_im_s = ((y2 * ar - x2 * ai) / den) * sgn
        bb = coef_re * bt + coef_im_s * _swap_halves(bt)
        bbs = _swap_halves(bb)
        w_rows = []
        for m in range(S5_T):
            e = (S5_T - 1 - m) if d == 0 else m
            w = pr2[e:e + 1, :] * bb + pi2s[e:e + 1, :] * bbs
            rows = slice(m * S5_CH, (m + 1) * S5_CH)
            wst_ref[0, rows, d * 128:(d + 1) * 128] = w.astype(_BF)
            wst_ref[0, rows, 256 + d * 128:256 + (d + 1) * 128] = _swap_halves(w).astype(_BF)
            w_rows.append(w)
        for i in range(S5_T):
            e = (i + 1) if d == 0 else (S5_T - i)
            gmat = pr2[e:e + 1, :] * cp + pi2s[e:e + 1, :] * cps
            cout_rows[d].append(gmat * (-sgn))
        gens.append(lax.dot_general(cc, jnp.concatenate(w_rows, axis=0), (((1,), (1,)), ((), ())),
                                    preferred_element_type=_F32, precision=lax.Precision.HIGHEST))
        apow_ref[0, 2 * d:2 * d + 1, :] = pr2[S5_T:S5_T + 1, :]
        apow_ref[0, 2 * d + 1:2 * d + 2, :] = pi2s[S5_T:S5_T + 1, :]
    gran = lax.broadcasted_iota(jnp.int32, (1, S5_ROW), 1) // S5_CH
    blocks = []
    for t in range(S5_T):
        sf = (S5_CH * (t + 1)) % S5_ROW
        rf = pltpu.roll(gens[0], sf, axis=1) if sf else gens[0]
        rb = pltpu.roll(gens[1], S5_CH * t, axis=1) if t else gens[1]
        blocks.append(jnp.where(gran <= t, rf, 0.0) + jnp.where(gran >= t, rb, 0.0))
    m_ref[0] = jnp.concatenate(blocks, axis=0).T.astype(_BF)
    cout_t = jnp.concatenate([jnp.concatenate(cout_rows[0], axis=0),
                              jnp.concatenate(cout_rows[1], axis=0)], axis=1)
    cout_ref[0] = cout_t.T.astype(_BF)


def _s5_operators(s5_a_re, s5_a_im, s5_log_dt, s5_b_re, s5_b_im, s5_c_re, s5_c_im):
    g = S5_GROUPS
    dup = lambda a: jnp.concatenate([a, a], axis=-1).transpose(1, 0, 2)
    ar2 = dup(s5_a_re)
    ai2 = dup(s5_a_im)
    ldt = s5_log_dt.T.reshape(g, 2, 1)
    btp = jnp.concatenate([s5_b_re.transpose(0, 2, 1), s5_b_im.transpose(0, 2, 1)], axis=-1)
    cpk = jnp.concatenate([s5_c_re, s5_c_im], axis=-1)
    gspec = lambda *shape: pl.BlockSpec((1,) + shape, lambda i: (i,) + (0,) * len(shape))
    return pl.pallas_call(
        _s5_prep_kernel,
        grid=(g,),
        in_specs=[gspec(2, 128), gspec(2, 128), gspec(2, 1), gspec(S5_CH, 128), gspec(S5_CH, 128)],
        out_specs=[gspec(S5_ROW, S5_ROW), gspec(S5_ROW, 512), gspec(256, S5_ROW), gspec(4, 128)],
        out_shape=[jax.ShapeDtypeStruct((g, S5_ROW, S5_ROW), _BF),
                   jax.ShapeDtypeStruct((g, S5_ROW, 512), _BF),
                   jax.ShapeDtypeStruct((g, 256, S5_ROW), _BF),
                   jax.ShapeDtypeStruct((g, 4, 128), _F32)],
        compiler_params=pltpu.CompilerParams(dimension_semantics=("arbitrary",)),
        name="s5_operators",
    )(ar2, ai2, ldt, btp, cpk)


def _s5_kernel(x_ref, m_ref, wst_ref, cout_ref, apow_ref, s0_ref, y_ref, fin_ref, loc_scr, prev_scr,
               *, nb, nk):
    x = jnp.concatenate([x_ref[0, 0], x_ref[0, 1]], axis=1).astype(_BF)
    loc_scr[...] = jnp.dot(x, wst_ref[0], preferred_element_type=_F32)
    ar_f = apow_ref[0, 0:1, :]
    ai_f = apow_ref[0, 1:2, :]
    ar_b = apow_ref[0, 2:3, :]
    ai_b = apow_ref[0, 3:4, :]
    zf = s0_ref[0, :, 0:128]
    zb = s0_ref[0, :, 128:256]
    zfs = _swap_halves(zf)
    zbs = _swap_halves(zb)
    for k in range(nk):
        kb = nk - 1 - k
        rows_f = slice(k * nb, (k + 1) * nb)
        rows_b = slice(kb * nb, (kb + 1) * nb)
        prev_scr[rows_f, 0:128] = zf
        prev_scr[rows_b, 128:256] = zb
        zf, zfs = (ar_f * zf + ai_f * zfs + loc_scr[rows_f, 0:128],
                   ar_f * zfs - ai_f * zf + loc_scr[rows_f, 256:384])
        zb, zbs = (ar_b * zb + ai_b * zbs + loc_scr[rows_b, 128:256],
                   ar_b * zbs - ai_b * zb + loc_scr[rows_b, 384:512])
    fin_ref[0, :, 0:128] = zf
    fin_ref[0, :, 128:256] = zb
    y = jnp.dot(x, m_ref[0], preferred_element_type=_F32)
    y = y + jnp.dot(prev_scr[...].astype(_BF), cout_ref[0], preferred_element_type=_F32)
    y_ref[0, 0] = y[:, 0:128]
    y_ref[0, 1] = y[:, 128:256]


def _s5_scan(xg, m_op, wst, cout, apow, s0p, nb, nk):
    g = S5_GROUPS
    rows = nk * nb
    gspec = lambda *shape: pl.BlockSpec((1,) + shape, lambda i: (i,) + (0,) * len(shape))
    return pl.pallas_call(
        functools.partial(_s5_kernel, nb=nb, nk=nk),
        grid=(g,),
        in_specs=[gspec(2, rows, 128), gspec(S5_ROW, S5_ROW), gspec(S5_ROW, 512), gspec(256, S5_ROW),
                  gspec(4, 128), gspec(nb, 256)],
        out_specs=[gspec(2, rows, 128), gspec(nb, 256)],
        out_shape=[jax.ShapeDtypeStruct((g, 2, rows, 128), _F32),
                   jax.ShapeDtypeStruct((g, nb, 256), _F32)],
        scratch_shapes=[pltpu.VMEM((rows, 512), _F32), pltpu.VMEM((rows, 256), _F32)],
        compiler_params=pltpu.CompilerParams(
            dimension_semantics=("arbitrary",), vmem_limit_bytes=VMEM_LIMIT),
        name="s5_scan",
    )(xg, m_op, wst, cout, apow, s0p)


def _post_kernel(x_ref, o_ref, yg_ref, u_ref, mod_ref, dsk_ref, wglu_ref, bglu_ref, wout_ref,
                 l1g_ref, l1b_ref, w1_ref, b1_ref, w2_ref, b2_ref, l2g_ref, l2b_ref, out_ref, y_scr,
                 *, nb, tm):
    b = pl.program_id(1)
    for j in range(S5_WIDTH // 128):
        for hf in range(2):
            for q in range(tm // 128):
                w = [yg_ref[8 * j + gg, hf, pl.ds(8 * q * nb + b, 8, stride=nb), :] for gg in range(8)]
                v = _granule_transpose(w)
                for a in range(8):
                    y_scr[j, pl.ds(128 * q + 8 * hf + a, 8, stride=S5_T), :] = v[a]
    x = x_ref[0]
    g1 = mod_ref[0, 2:3, :]
    sh2 = mod_ref[0, 3:4, :]
    sc2 = mod_ref[0, 4:5, :]
    g2 = mod_ref[0, 5:6, :]
    y = jnp.concatenate([y_scr[j] for j in range(S5_WIDTH // 128)], axis=1)
    y = y + dsk_ref[...] * u_ref[0].astype(_F32)
    y = jax.nn.gelu(y)
    y = y * jax.nn.sigmoid(jnp.dot(y.astype(_BF), wglu_ref[...], preferred_element_type=_F32)
                           + bglu_ref[...])
    mix = jnp.dot(o_ref[0], wout_ref[0:RET_WIDTH, :], preferred_element_type=_F32)
    mix = mix + jnp.dot(y.astype(_BF), wout_ref[RET_WIDTH:, :], preferred_element_type=_F32)
    x1 = _norm_rows(ALPHA * x + g1 * mix) * l1g_ref[...] + l1b_ref[...]
    h = (_norm_rows(x1) * (1.0 + sc2) + sh2).astype(_BF)
    acc = jnp.zeros_like(x1)
    fc = 1024
    for j in range(D_FF // fc):
        a = jnp.dot(h, w1_ref[:, j * fc:(j + 1) * fc], preferred_element_type=_F32) + b1_ref[:, j * fc:(j + 1) * fc]
        a = jnp.square(jnp.maximum(a, 0.0)).astype(_BF)
        acc = acc + jnp.dot(a, w2_ref[j * fc:(j + 1) * fc, :], preferred_element_type=_F32)
    f = acc + b2_ref[...]
    out_ref[0] = _norm_rows(ALPHA * x1 + g2 * f) * l2g_ref[...] + l2b_ref[...]


def _post(x, o_ret, yg, proj, mod, mod_row, pw):
    b, l, _ = x.shape
    tm = min(TOKEN_TILE, l)
    rows = (tm // S5_T) * b
    tok = lambda w: pl.BlockSpec((1, tm, w), lambda j, i: (i, j, 0))
    row = lambda n: _const_spec((1, n))
    return pl.pallas_call(
        functools.partial(_post_kernel, nb=b, tm=tm),
        grid=(l // tm, b),
        in_specs=[tok(D_MODEL), tok(RET_WIDTH),
                  pl.BlockSpec((S5_GROUPS, 2, rows, 128), lambda j, i: (0, 0, j, 0)),
                  pl.BlockSpec((1, tm, S5_WIDTH), lambda j, i: (i, j, 4)),
                  pl.BlockSpec((1, 6, D_MODEL), lambda j, i: (mod_row(i), 0, 0)),
                  row(S5_WIDTH), _const_spec((S5_WIDTH, S5_WIDTH)), row(S5_WIDTH),
                  _const_spec((D_MODEL, D_MODEL)), row(D_MODEL), row(D_MODEL),
                  _const_spec((D_MODEL, D_FF)), row(D_FF), _const_spec((D_FF, D_MODEL)), row(D_MODEL),
                  row(D_MODEL), row(D_MODEL)],
        out_specs=tok(D_MODEL),
        out_shape=jax.ShapeDtypeStruct((b, l, D_MODEL), _F32),
        scratch_shapes=[pltpu.VMEM((S5_WIDTH // 128, tm, 128), _F32)],
        compiler_params=pltpu.CompilerParams(
            dimension_semantics=("arbitrary", "arbitrary"), vmem_limit_bytes=VMEM_LIMIT),
        name="post",
    )(x, o_ret, yg, proj, mod, *pw)


def _layer_pass(x, mod, mod_row, ret_s0, s5_s0_re, s5_s0_im, w_in_bf, ret_decay, s5_ops, post_w):
    b, l, _ = x.shape
    nk = l // S5_T
    g = S5_GROUPS
    proj, xg = _projection(x, mod, mod_row, w_in_bf)
    o_ret, ret_state = _retention(proj, ret_decay, ret_s0)
    if s5_s0_re is None:
        s0p = jnp.zeros((g, b, 256), _F32)
    else:
        s0p = jnp.concatenate([s5_s0_re, s5_s0_im], axis=-1)
        s0p = s0p.transpose(2, 0, 1, 3).reshape(g, b, 256)
    yg, fin = _s5_scan(xg, *s5_ops, s0p, b, nk)
    fin = fin.reshape(g, b, 2, 2, S5_STATE).transpose(1, 2, 0, 3, 4)
    out = _post(x, o_ret, yg, proj, mod, mod_row, post_w)
    return out, ret_state, fin[:, :, :, 0, :], fin[:, :, :, 1, :]


def kernel(x_prompt, x_sample, state_ret, state_s5_re, state_s5_im, c, c_ctx, w_ada, b_ada, w_in,
           ret_decay, s5_a_re, s5_a_im, s5_log_dt, s5_b_re, s5_b_im, s5_c_re, s5_c_im, s5_d, w_glu,
           b_glu, w_out, ln1_g, ln1_b, w_ff1, b_ff1, w_ff2, b_ff2, ln2_g, ln2_b):
    depth = w_ada.shape[0]
    bs = x_sample.shape[0]
    y_p, y_s = x_prompt, x_sample
    rets, s5rs, s5is = [], [], []
    for layer in range(depth):
        cond8 = jnp.concatenate(
            [c_ctx[None, :], c, jnp.zeros((8 - 1 - bs, D_MODEL), _F32)], axis=0)
        mod = _modulation(cond8, w_ada[layer], b_ada[layer]).reshape(8, 6, D_MODEL)
        s5_ops = _s5_operators(s5_a_re[layer], s5_a_im[layer], s5_log_dt[layer], s5_b_re[layer],
                               s5_b_im[layer], s5_c_re[layer], s5_c_im[layer])
        w_in_bf = w_in[layer].astype(_BF)
        r2 = lambda a: a[layer].reshape(1, -1)
        post_w = (r2(s5_d), w_glu[layer].astype(_BF), r2(b_glu), w_out[layer].astype(_BF),
                  r2(ln1_g), r2(ln1_b), w_ff1[layer].astype(_BF), r2(b_ff1),
                  w_ff2[layer].astype(_BF), r2(b_ff2), r2(ln2_g), r2(ln2_b))
        y_p, r_st, s5_re, s5_im = _layer_pass(
            y_p, mod, lambda i: 0, None, None, None, w_in_bf, ret_decay[layer], s5_ops, post_w)
        rets.append(r_st)
        s5rs.append(s5_re)
        s5is.append(s5_im)
        y_s, _, _, _ = _layer_pass(
            y_s, mod, lambda i: i + 1, state_ret[:, layer], state_s5_re[:, layer],
            state_s5_im[:, layer], w_in_bf, ret_decay[layer], s5_ops, post_w)
    return (y_p, y_s, jnp.stack(rets, axis=1), jnp.stack(s5rs, axis=1), jnp.stack(s5is, axis=1))
```

```python
import functools

import jax
import jax.numpy as jnp
from jax import lax
from jax.experimental import pallas as pl
from jax.experimental.pallas import tpu as pltpu

D_MODEL = 1024
RET_HEADS = 4
RET_DK = 128
RET_WIDTH = RET_HEADS * RET_DK
S5_CH = 16
S5_GROUPS = 32
S5_STATE = 64
S5_WIDTH = S5_GROUPS * S5_CH
D_FF = 4 * D_MODEL
IN_COLS = 4 * RET_WIDTH + S5_WIDTH
ALPHA = 2.0 ** 0.25
LN_EPS = 1e-5

S5_T = 16
S5_ROW = S5_T * S5_CH
S5_GROUP_TILE = 4
RET_CHUNK = 256
RET_UNIT_TOKENS = 1024
TOKEN_TILE = 512
VMEM_LIMIT = 56 * 1024 * 1024

_BF = jnp.bfloat16
_F32 = jnp.float32


def _norm_rows(x):
    mu = jnp.mean(x, axis=-1, keepdims=True)
    xc = x - mu
    var = jnp.mean(xc * xc, axis=-1, keepdims=True)
    return xc * lax.rsqrt(var + LN_EPS)


def _const_spec(shape):
    nd = len(shape)
    return pl.BlockSpec(shape, lambda *_: (0,) * nd, pipeline_mode=pl.Buffered(1))


def _granule_transpose(v):
    lane = lax.broadcasted_iota(jnp.int32, (1, 128), 1)
    v = list(v)
    for d in (4, 2, 1):
        bit = ((lane // S5_CH) & d) != 0
        nv = list(v)
        for a in range(8):
            if a & d:
                continue
            lo, hi = v[a], v[a + d]
            nv[a] = jnp.where(bit, pltpu.roll(hi, S5_CH * d, axis=1), lo)
            nv[a + d] = jnp.where(bit, hi, pltpu.roll(lo, 128 - S5_CH * d, axis=1))
        v = nv
    return v


def _mod_kernel(cond_ref, w_ref, b_ref, o_ref):
    c = cond_ref[...]
    s = (c * jax.nn.sigmoid(c)).astype(_BF)
    o_ref[...] = jnp.dot(s, w_ref[...].astype(_BF), preferred_element_type=_F32) + b_ref[...]


def _modulation(cond8, w_ada, b_ada):
    n = w_ada.shape[1]
    tn = 1024
    return pl.pallas_call(
        _mod_kernel,
        grid=(n // tn,),
        in_specs=[pl.BlockSpec((8, D_MODEL), lambda j: (0, 0)),
                  pl.BlockSpec((D_MODEL, tn), lambda j: (0, j)),
                  pl.BlockSpec((1, tn), lambda j: (0, j))],
        out_specs=pl.BlockSpec((8, tn), lambda j: (0, j)),
        out_shape=jax.ShapeDtypeStruct((8, n), _F32),
        compiler_params=pltpu.CompilerParams(dimension_semantics=("arbitrary",)),
        name="modulation",
    )(cond8, w_ada, b_ada.reshape(1, n))


def _proj_kernel(x_ref, mod_ref, w_ref, o_ref, xg_ref, u_scr, *, nb, tm):
    x = x_ref[0]
    sh = mod_ref[0, 0:1, :]
    sc = mod_ref[0, 1:2, :]
    h = (_norm_rows(x) * (1.0 + sc) + sh).astype(_BF)
    res = jnp.dot(h, w_ref[...], preferred_element_type=_F32)
    o_ref[0] = res.astype(_BF)
    for j in range(S5_WIDTH // 128):
        u_scr[j] = res[:, 4 * RET_WIDTH + 128 * j:4 * RET_WIDTH + 128 * (j + 1)]
    b = pl.program_id(1)
    for j in range(S5_WIDTH // 128):
        for hf in range(2):
            for q in range(tm // 128):
                v = [u_scr[j, pl.ds(128 * q + 8 * hf + a, 8, stride=S5_T), :] for a in range(8)]
                w = _granule_transpose(v)
                for gg in range(8):
                    xg_ref[8 * j + gg, hf, pl.ds(8 * q * nb + b, 8, stride=nb), :] = w[gg]


def _projection(x, mod, mod_row, w_in_bf):
    b, l, _ = x.shape
    tm = min(TOKEN_TILE, l)
    rows = (tm // S5_T) * b
    return pl.pallas_call(
        functools.partial(_proj_kernel, nb=b, tm=tm),
        grid=(l // tm, b),
        in_specs=[pl.BlockSpec((1, tm, D_MODEL), lambda j, i: (i, j, 0)),
                  pl.BlockSpec((1, 6, D_MODEL), lambda j, i: (mod_row(i), 0, 0)),
                  _const_spec((D_MODEL, IN_COLS))],
        out_specs=[pl.BlockSpec((1, tm, IN_COLS), lambda j, i: (i, j, 0)),
                   pl.BlockSpec((S5_GROUPS, 2, rows, 128), lambda j, i: (0, 0, j, 0))],
        out_shape=[jax.ShapeDtypeStruct((b, l, IN_COLS), _BF),
                   jax.ShapeDtypeStruct((S5_GROUPS, 2, (l // S5_T) * b, 128), _F32)],
        scratch_shapes=[pltpu.VMEM((S5_WIDTH // 128, tm, 128), _F32)],
        compiler_params=pltpu.CompilerParams(
            dimension_semantics=("arbitrary", "arbitrary"), vmem_limit_bytes=VMEM_LIMIT),
        name="projection",
    )(x, mod, w_in_bf)


def _ret_kernel(*refs, chunk, n_chunks, nbb, has_s0):
    if has_s0:
        dec_ref, q_ref, k_ref, v_ref, g_ref, s0_ref, o_ref, st_ref, kv_scr = refs
    else:
        dec_ref, q_ref, k_ref, v_ref, g_ref, o_ref, st_ref, kv_scr = refs
        s0_ref = None
    c = chunk
    scale = RET_DK ** -0.5
    pos = lax.broadcasted_iota(jnp.int32, (c, 1), 0).astype(_F32)
    ri = lax.broadcasted_iota(jnp.int32, (c, c), 0)
    ci = lax.broadcasted_iota(jnp.int32, (c, c), 1)
    rel = (ri - ci).astype(_F32)

    for hd in range(RET_HEADS):
        lanes = slice(hd * RET_DK, (hd + 1) * RET_DK)

        def log_gamma(d):
            z = jnp.full((1, 1), dec_ref[d, hd], _F32)
            return jnp.minimum(z, 0.0) - jnp.log(1.0 + jnp.exp(-jnp.abs(z)))

        lg_f, lg_b = log_gamma(0), log_gamma(1)
        kdec_f = jnp.exp(lg_f * (c - 1.0 - pos)) * scale
        kdec_b = jnp.exp(lg_b * pos) * scale
        qdec_f = jnp.exp(lg_f * (pos + 1.0))
        qdec_b = jnp.exp(lg_b * (c - pos))
        dmat = (jnp.where(ri >= ci, jnp.exp(lg_f * jnp.maximum(rel, 0.0)), 0.0)
                + jnp.where(ci >= ri, jnp.exp(lg_b * jnp.maximum(-rel, 0.0)), 0.0)) * scale
        cdec_f = jnp.exp(lg_f * float(c))
        cdec_b = jnp.exp(lg_b * float(c))

        for bb in range(nbb):
            unit = bb * RET_HEADS + hd
            for n in range(n_chunks):
                kc = k_ref[bb, n * c:(n + 1) * c, lanes].astype(_F32)
                vc = v_ref[bb, n * c:(n + 1) * c, lanes]
                kcat = jnp.concatenate([(kc * kdec_f).astype(_BF), (kc * kdec_b).astype(_BF)], axis=1)
                kv_scr[unit, n] = lax.dot_general(kcat, vc, (((0,), (0,)), ((), ())),
                                                  preferred_element_type=_F32)
            if has_s0:
                carry_f = s0_ref[bb, 0, hd]
                carry_b = s0_ref[bb, 1, hd]
            else:
                carry_f = jnp.zeros((RET_DK, RET_DK), _F32)
                carry_b = jnp.zeros((RET_DK, RET_DK), _F32)
            for n in range(n_chunks):
                inc = kv_scr[unit, n, 0:RET_DK, :]
                kv_scr[unit, n, 0:RET_DK, :] = carry_f
                carry_f = cdec_f * carry_f + inc
            for n in range(n_chunks - 1, -1, -1):
                inc = kv_scr[unit, n, RET_DK:2 * RET_DK, :]
                kv_scr[unit, n, RET_DK:2 * RET_DK, :] = carry_b
                carry_b = cdec_b * carry_b + inc
            st_ref[bb, 0, hd] = carry_f
            st_ref[bb, 1, hd] = carry_b

            for n in range(n_chunks):
                sl = slice(n * c, (n + 1) * c)
                qc = q_ref[bb, sl, lanes]
                kc = k_ref[bb, sl, lanes]
                vc = v_ref[bb, sl, lanes]
                scores = lax.dot_general(qc, kc, (((1,), (1,)), ((), ())), preferred_element_type=_F32)
                o = jnp.dot((scores * dmat).astype(_BF), vc, preferred_element_type=_F32)
                qf = qc.astype(_F32)
                qcat = jnp.concatenate([(qf * qdec_f).astype(_BF), (qf * qdec_b).astype(_BF)], axis=1)
                o = o + jnp.dot(qcat, kv_scr[unit, n].astype(_BF), preferred_element_type=_F32)
                gate = g_ref[bb, sl, lanes].astype(_F32)
                o_ref[bb, sl, lanes] = (_norm_rows(o) * (gate * jax.nn.sigmoid(gate))).astype(_BF)


def _retention(proj, ret_decay, s0):
    b, l, _ = proj.shape
    c = min(RET_CHUNK, l)
    n_chunks = l // c
    nbb = max(1, min(b, RET_UNIT_TOKENS // l))
    has_s0 = s0 is not None
    col = lambda j: pl.BlockSpec((nbb, l, RET_WIDTH), lambda i: (i, 0, j))
    st_spec = pl.BlockSpec((nbb, 2, RET_HEADS, RET_DK, RET_DK), lambda i: (i, 0, 0, 0, 0))
    in_specs = [pl.BlockSpec(memory_space=pltpu.SMEM), col(0), col(1), col(2), col(3)]
    args = [ret_decay, proj, proj, proj, proj]
    if has_s0:
        in_specs.append(st_spec)
        args.append(s0)
    return pl.pallas_call(
        functools.partial(_ret_kernel, chunk=c, n_chunks=n_chunks, nbb=nbb, has_s0=has_s0),
        grid=(b // nbb,),
        in_specs=in_specs,
        out_specs=[pl.BlockSpec((nbb, l, RET_WIDTH), lambda i: (i, 0, 0)), st_spec],
        out_shape=[jax.ShapeDtypeStruct((b, l, RET_WIDTH), _BF),
                   jax.ShapeDtypeStruct((b, 2, RET_HEADS, RET_DK, RET_DK), _F32)],
        scratch_shapes=[pltpu.VMEM((nbb * RET_HEADS, n_chunks, 2 * RET_DK, RET_DK), _F32)],
        compiler_params=pltpu.CompilerParams(
            dimension_semantics=("arbitrary",), vmem_limit_bytes=VMEM_LIMIT),
        name="retention",
    )(*args)


def _swap_halves(z):
    return pltpu.roll(z, S5_STATE, axis=z.ndim - 1)


def _s5_prep_kernel(ar_ref, ai_ref, ldt_ref, bt_ref, cp_ref, m_ref, wst_ref, cout_ref, apow_ref, *, ng):
    lane = lax.broadcasted_iota(jnp.int32, (1, 2 * S5_STATE), 1)
    sgn = jnp.where(lane < S5_STATE, -1.0, 1.0).astype(_F32)
    gran = lax.broadcasted_iota(jnp.int32, (1, S5_ROW), 1) // S5_CH
    ar = ar_ref[...]
    ai = ai_ref[...]
    dt = jnp.exp(ldt_ref[...])
    mag = jnp.exp(ar * dt)
    ang = ai * dt
    pr1 = mag * jnp.cos(ang)
    pi1 = mag * jnp.sin(ang)
    pw_r = [jnp.ones_like(pr1), pr1]
    pw_i = [jnp.zeros_like(pi1), pi1]
    for _ in range(2, S5_T + 1):
        pr, pi = pw_r[-1], pw_i[-1]
        pw_r.append(pr * pr1 - pi * pi1)
        pw_i.append(pr * pi1 + pi * pr1)
    pw_is = [p * sgn for p in pw_i]
    x2 = pr1 - 1.0
    den = ar * ar + ai * ai
    coef_re = (x2 * ar + pi1 * ai) / den
    coef_im_s = ((pi1 * ar - x2 * ai) / den) * sgn
    for gi in range(ng):
        bt = bt_ref[gi]
        cp = cp_ref[gi]
        cc = cp * (-sgn)
        cps = _swap_halves(cp)
        bts = _swap_halves(bt)
        gens = []
        cout_rows = [[], []]
        for d in range(2):
            r = 2 * gi + d
            row = lambda a: a[r:r + 1, :]
            bb = row(coef_re) * bt + row(coef_im_s) * bts
            bbs = _swap_halves(bb)
            w_rows = []
            for m in range(S5_T):
                e = (S5_T - 1 - m) if d == 0 else m
                w = row(pw_r[e]) * bb + row(pw_is[e]) * bbs
                rows = slice(m * S5_CH, (m + 1) * S5_CH)
                wst_ref[gi, rows, d * 128:(d + 1) * 128] = w.astype(_BF)
                wst_ref[gi, rows, 256 + d * 128:256 + (d + 1) * 128] = _swap_halves(w).astype(_BF)
                w_rows.append(w)
            for i in range(S5_T):
                e = (i + 1) if d == 0 else (S5_T - i)
                gmat = row(pw_r[e]) * cp + row(pw_is[e]) * cps
                cout_rows[d].append(gmat * (-sgn))
            gens.append(lax.dot_general(cc, jnp.concatenate(w_rows, axis=0), (((1,), (1,)), ((), ())),
                                        preferred_element_type=_F32, precision=lax.Precision.HIGHEST))
            apow_ref[gi, 2 * d:2 * d + 1, :] = row(pw_r[S5_T])
            apow_ref[gi, 2 * d + 1:2 * d + 2, :] = row(pw_is[S5_T])
        blocks = []
        for t in range(S5_T):
            sf = (S5_CH * (t + 1)) % S5_ROW
            rf = pltpu.roll(gens[0], sf, axis=1) if sf else gens[0]
            rb = pltpu.roll(gens[1], S5_CH * t, axis=1) if t else gens[1]
            blocks.append(jnp.where(gran <= t, rf, 0.0) + jnp.where(gran >= t, rb, 0.0))
        m_ref[gi] = jnp.concatenate(blocks, axis=0).T.astype(_BF)
        cout_t = jnp.concatenate([jnp.concatenate(cout_rows[0], axis=0),
                                  jnp.concatenate(cout_rows[1], axis=0)], axis=1)
        cout_ref[gi] = cout_t.T.astype(_BF)


def _s5_operators(s5_a_re, s5_a_im, s5_log_dt, s5_b_re, s5_b_im, s5_c_re, s5_c_im):
    g = S5_GROUPS
    ng = S5_GROUP_TILE
    dup = lambda a: jnp.concatenate([a, a], axis=-1).transpose(1, 0, 2).reshape(2 * g, 128)
    ar2 = dup(s5_a_re)
    ai2 = dup(s5_a_im)
    ldt = s5_log_dt.T.reshape(2 * g, 1)
    btp = jnp.concatenate([s5_b_re.transpose(0, 2, 1), s5_b_im.transpose(0, 2, 1)], axis=-1)
    cpk = jnp.concatenate([s5_c_re, s5_c_im], axis=-1)
    gspec = lambda *shape: pl.BlockSpec((ng,) + shape, lambda i: (i,) + (0,) * len(shape))
    rspec = lambda w: pl.BlockSpec((2 * ng, w), lambda i: (i, 0))
    return pl.pallas_call(
        functools.partial(_s5_prep_kernel, ng=ng),
        grid=(g // ng,),
        in_specs=[rspec(128), rspec(128), rspec(1), gspec(S5_CH, 128), gspec(S5_CH, 128)],
        out_specs=[gspec(S5_ROW, S5_ROW), gspec(S5_ROW, 512), gspec(256, S5_ROW), gspec(4, 128)],
        out_shape=[jax.ShapeDtypeStruct((g, S5_ROW, S5_ROW), _BF),
                   jax.ShapeDtypeStruct((g, S5_ROW, 512), _BF),
                   jax.ShapeDtypeStruct((g, 256, S5_ROW), _BF),
                   jax.ShapeDtypeStruct((g, 4, 128), _F32)],
        compiler_params=pltpu.CompilerParams(dimension_semantics=("arbitrary",)),
        name="s5_operators",
    )(ar2, ai2, ldt, btp, cpk)


def _s5_kernel(x_ref, m_ref, wst_ref, cout_ref, apow_ref, s0_ref, y_ref, fin_ref, loc_scr, prev_scr,
               *, nb, nk, ng):
    load_x = lambda gi: jnp.concatenate([x_ref[gi, 0], x_ref[gi, 1]], axis=1).astype(_BF)
    for gi in range(ng):
        loc_scr[gi] = jnp.dot(load_x(gi), wst_ref[gi], preferred_element_type=_F32)
    coef, st = [], []
    for gi in range(ng):
        coef.append([apow_ref[gi, r:r + 1, :] for r in range(4)])
        zf = s0_ref[gi, :, 0:128]
        zb = s0_ref[gi, :, 128:256]
        st.append([zf, _swap_halves(zf), zb, _swap_halves(zb)])
    for k in range(nk):
        kb = nk - 1 - k
        rows_f = slice(k * nb, (k + 1) * nb)
        rows_b = slice(kb * nb, (kb + 1) * nb)
        for gi in range(ng):
            ar_f, ai_f, ar_b, ai_b = coef[gi]
            zf, zfs, zb, zbs = st[gi]
            prev_scr[gi, rows_f, 0:128] = zf
            prev_scr[gi, rows_b, 128:256] = zb
            st[gi] = [ar_f * zf + ai_f * zfs + loc_scr[gi, rows_f, 0:128],
                      ar_f * zfs - ai_f * zf + loc_scr[gi, rows_f, 256:384],
                      ar_b * zb + ai_b * zbs + loc_scr[gi, rows_b, 128:256],
                      ar_b * zbs - ai_b * zb + loc_scr[gi, rows_b, 384:512]]
    for gi in range(ng):
        fin_ref[gi, :, 0:128] = st[gi][0]
        fin_ref[gi, :, 128:256] = st[gi][2]
        y = jnp.dot(load_x(gi), m_ref[gi], preferred_element_type=_F32)
        y = y + jnp.dot(prev_scr[gi].astype(_BF), cout_ref[gi], preferred_element_type=_F32)
        y_ref[gi, 0] = y[:, 0:128]
        y_ref[gi, 1] = y[:, 128:256]


def _s5_scan(xg, m_op, wst, cout, apow, s0p, nb, nk):
    g = S5_GROUPS
    ng = S5_GROUP_TILE
    rows = nk * nb
    gspec = lambda *shape: pl.BlockSpec((ng,) + shape, lambda i: (i,) + (0,) * len(shape))
    return pl.pallas_call(
        functools.partial(_s5_kernel, nb=nb, nk=nk, ng=ng),
        grid=(g // ng,),
        in_specs=[gspec(2, rows, 128), gspec(S5_ROW, S5_ROW), gspec(S5_ROW, 512), gspec(256, S5_ROW),
                  gspec(4, 128), gspec(nb, 256)],
        out_specs=[gspec(2, rows, 128), gspec(nb, 256)],
        out_shape=[jax.ShapeDtypeStruct((g, 2, rows, 128), _F32),
                   jax.ShapeDtypeStruct((g, nb, 256), _F32)],
        scratch_shapes=[pltpu.VMEM((ng, rows, 512), _F32), pltpu.VMEM((ng, rows, 256), _F32)],
        compiler_params=pltpu.CompilerParams(
            dimension_semantics=("arbitrary",), vmem_limit_bytes=VMEM_LIMIT),
        name="s5_scan",
    )(xg, m_op, wst, cout, apow, s0p)


def _post_kernel(x_ref, o_ref, yg_ref, u_ref, mod_ref, dsk_ref, wglu_ref, bglu_ref, wout_ref,
                 l1g_ref, l1b_ref, w1_ref, b1_ref, w2_ref, b2_ref, l2g_ref, l2b_ref, out_ref, y_scr,
                 *, nb, tm):
    b = pl.program_id(1)
    for j in range(S5_WIDTH // 128):
        for hf in range(2):
            for q in range(tm // 128):
                w = [yg_ref[8 * j + gg, hf, pl.ds(8 * q * nb + b, 8, stride=nb), :] for gg in range(8)]
                v = _granule_transpose(w)
                for a in range(8):
                    y_scr[j, pl.ds(128 * q + 8 * hf + a, 8, stride=S5_T), :] = v[a]
    x = x_ref[0]
    g1 = mod_ref[0, 2:3, :]
    sh2 = mod_ref[0, 3:4, :]
    sc2 = mod_ref[0, 4:5, :]
    g2 = mod_ref[0, 5:6, :]
    y = jnp.concatenate([y_scr[j] for j in range(S5_WIDTH // 128)], axis=1)
    y = y + dsk_ref[...] * u_ref[0].astype(_F32)
    y = jax.nn.gelu(y)
    y = y * jax.nn.sigmoid(jnp.dot(y.astype(_BF), wglu_ref[...], preferred_element_type=_F32)
                           + bglu_ref[...])
    mix = jnp.dot(o_ref[0], wout_ref[0:RET_WIDTH, :], preferred_element_type=_F32)
    mix = mix + jnp.dot(y.astype(_BF), wout_ref[RET_WIDTH:, :], preferred_element_type=_F32)
    x1 = _norm_rows(ALPHA * x + g1 * mix) * l1g_ref[...] + l1b_ref[...]
    h = (_norm_rows(x1) * (1.0 + sc2) + sh2).astype(_BF)
    acc = jnp.zeros_like(x1)
    fc = 1024
    for j in range(D_FF // fc):
        a = jnp.dot(h, w1_ref[:, j * fc:(j + 1) * fc], preferred_element_type=_F32) + b1_ref[:, j * fc:(j + 1) * fc]
        a = jnp.square(jnp.maximum(a, 0.0)).astype(_BF)
        acc = acc + jnp.dot(a, w2_ref[j * fc:(j + 1) * fc, :], preferred_element_type=_F32)
    f = acc + b2_ref[...]
    out_ref[0] = _norm_rows(ALPHA * x1 + g2 * f) * l2g_ref[...] + l2b_ref[...]


def _post(x, o_ret, yg, proj, mod, mod_row, pw):
    b, l, _ = x.shape
    tm = min(TOKEN_TILE, l)
    rows = (tm // S5_T) * b
    tok = lambda w: pl.BlockSpec((1, tm, w), lambda j, i: (i, j, 0))
    row = lambda n: _const_spec((1, n))
    return pl.pallas_call(
        functools.partial(_post_kernel, nb=b, tm=tm),
        grid=(l // tm, b),
        in_specs=[tok(D_MODEL), tok(RET_WIDTH),
                  pl.BlockSpec((S5_GROUPS, 2, rows, 128), lambda j, i: (0, 0, j, 0)),
                  pl.BlockSpec((1, tm, S5_WIDTH), lambda j, i: (i, j, 4)),
                  pl.BlockSpec((1, 6, D_MODEL), lambda j, i: (mod_row(i), 0, 0)),
                  row(S5_WIDTH), _const_spec((S5_WIDTH, S5_WIDTH)), row(S5_WIDTH),
                  _const_spec((D_MODEL, D_MODEL)), row(D_MODEL), row(D_MODEL),
                  _const_spec((D_MODEL, D_FF)), row(D_FF), _const_spec((D_FF, D_MODEL)), row(D_MODEL),
                  row(D_MODEL), row(D_MODEL)],
        out_specs=tok(D_MODEL),
        out_shape=jax.ShapeDtypeStruct((b, l, D_MODEL), _F32),
        scratch_shapes=[pltpu.VMEM((S5_WIDTH // 128, tm, 128), _F32)],
        compiler_params=pltpu.CompilerParams(
            dimension_semantics=("arbitrary", "arbitrary"), vmem_limit_bytes=VMEM_LIMIT),
        name="post",
    )(x, o_ret, yg, proj, mod, *pw)


def _layer_pass(x, mod, mod_row, ret_s0, s5_s0_re, s5_s0_im, w_in_bf, ret_decay, s5_ops, post_w):
    b, l, _ = x.shape
    nk = l // S5_T
    g = S5_GROUPS
    proj, xg = _projection(x, mod, mod_row, w_in_bf)
    o_ret, ret_state = _retention(proj, ret_decay, ret_s0)
    if s5_s0_re is None:
        s0p = jnp.zeros((g, b, 256), _F32)
    else:
        s0p = jnp.concatenate([s5_s0_re, s5_s0_im], axis=-1)
        s0p = s0p.transpose(2, 0, 1, 3).reshape(g, b, 256)
    yg, fin = _s5_scan(xg, *s5_ops, s0p, b, nk)
    fin = fin.reshape(g, b, 2, 2, S5_STATE).transpose(1, 2, 0, 3, 4)
    out = _post(x, o_ret, yg, proj, mod, mod_row, post_w)
    return out, ret_state, fin[:, :, :, 0, :], fin[:, :, :, 1, :]


def kernel(x_prompt, x_sample, state_ret, state_s5_re, state_s5_im, c, c_ctx, w_ada, b_ada, w_in,
           ret_decay, s5_a_re, s5_a_im, s5_log_dt, s5_b_re, s5_b_im, s5_c_re, s5_c_im, s5_d, w_glu,
           b_glu, w_out, ln1_g, ln1_b, w_ff1, b_ff1, w_ff2, b_ff2, ln2_g, ln2_b):
    depth = w_ada.shape[0]
    bs = x_sample.shape[0]
    y_p, y_s = x_prompt, x_sample
    rets, s5rs, s5is = [], [], []
    for layer in range(depth):
        cond8 = jnp.concatenate(
            [c_ctx[None, :], c, jnp.zeros((8 - 1 - bs, D_MODEL), _F32)], axis=0)
        mod = _modulation(cond8, w_ada[layer], b_ada[layer]).reshape(8, 6, D_MODEL)
        s5_ops = _s5_operators(s5_a_re[layer], s5_a_im[layer], s5_log_dt[layer], s5_b_re[layer],
                               s5_b_im[layer], s5_c_re[layer], s5_c_im[layer])
        w_in_bf = w_in[layer].astype(_BF)
        r2 = lambda a: a[layer].reshape(1, -1)
        post_w = (r2(s5_d), w_glu[layer].astype(_BF), r2(b_glu), w_out[layer].astype(_BF),
                  r2(ln1_g), r2(ln1_b), w_ff1[layer].astype(_BF), r2(b_ff1),
                  w_ff2[layer].astype(_BF), r2(b_ff2), r2(ln2_g), r2(ln2_b))
        y_p, r_st, s5_re, s5_im = _layer_pass(
            y_p, mod, lambda i: 0, None, None, None, w_in_bf, ret_decay[layer], s5_ops, post_w)
        rets.append(r_st)
        s5rs.append(s5_re)
        s5is.append(s5_im)
        y_s, _, _, _ = _layer_pass(
            y_s, mod, lambda i: i + 1, state_ret[:, layer], state_s5_re[:, layer],
            state_s5_im[:, layer], w_in_bf, ret_decay[layer], s5_ops, post_w)
    return (y_p, y_s, jnp.stack(rets, axis=1), jnp.stack(s5rs, axis=1), jnp.stack(s5is, axis=1))
```

```python
import functools

import jax
import jax.numpy as jnp
from jax import lax
from jax.experimental import pallas as pl
from jax.experimental.pallas import tpu as pltpu

D_MODEL = 1024
RET_HEADS = 4
RET_DK = 128
RET_WIDTH = RET_HEADS * RET_DK
S5_CH = 16
S5_GROUPS = 32
S5_STATE = 64
S5_WIDTH = S5_GROUPS * S5_CH
D_FF = 4 * D_MODEL
IN_COLS = 4 * RET_WIDTH + S5_WIDTH
ALPHA = 2.0 ** 0.25
LN_EPS = 1e-5

S5_T = 16
S5_ROW = S5_T * S5_CH
S5_GROUP_TILE = 4
RET_CHUNK = 256
RET_UNIT_TOKENS = 1024
TOKEN_TILE = 512
VMEM_LIMIT = 56 * 1024 * 1024

_BF = jnp.bfloat16
_F32 = jnp.float32


def _norm_rows(x):
    mu = jnp.mean(x, axis=-1, keepdims=True)
    xc = x - mu
    var = jnp.mean(xc * xc, axis=-1, keepdims=True)
    return xc * lax.rsqrt(var + LN_EPS)


def _const_spec(shape):
    nd = len(shape)
    return pl.BlockSpec(shape, lambda *_: (0,) * nd, pipeline_mode=pl.Buffered(1))


def _granule_transpose(v):
    lane = lax.broadcasted_iota(jnp.int32, (1, 128), 1)
    v = list(v)
    for d in (4, 2, 1):
        bit = ((lane // S5_CH) & d) != 0
        nv = list(v)
        for a in range(8):
            if a & d:
                continue
            lo, hi = v[a], v[a + d]
            nv[a] = jnp.where(bit, pltpu.roll(hi, S5_CH * d, axis=1), lo)
            nv[a + d] = jnp.where(bit, hi, pltpu.roll(lo, 128 - S5_CH * d, axis=1))
        v = nv
    return v


def _mod_kernel(cond_ref, w_ref, b_ref, o_ref):
    c = cond_ref[...]
    s = (c * jax.nn.sigmoid(c)).astype(_BF)
    o_ref[...] = jnp.dot(s, w_ref[...].astype(_BF), preferred_element_type=_F32) + b_ref[...]


def _modulation(cond8, w_ada, b_ada):
    n = w_ada.shape[1]
    tn = 1024
    return pl.pallas_call(
        _mod_kernel,
        grid=(n // tn,),
        in_specs=[pl.BlockSpec((8, D_MODEL), lambda j: (0, 0)),
                  pl.BlockSpec((D_MODEL, tn), lambda j: (0, j)),
                  pl.BlockSpec((1, tn), lambda j: (0, j))],
        out_specs=pl.BlockSpec((8, tn), lambda j: (0, j)),
        out_shape=jax.ShapeDtypeStruct((8, n), _F32),
        compiler_params=pltpu.CompilerParams(dimension_semantics=("arbitrary",)),
        name="modulation",
    )(cond8, w_ada, b_ada.reshape(1, n))


def _proj_kernel(x_ref, mod_ref, w_ref, o_ref, xg_ref, u_scr, *, nb, tm):
    x = x_ref[0]
    sh = mod_ref[0, 0:1, :]
    sc = mod_ref[0, 1:2, :]
    h = (_norm_rows(x) * (1.0 + sc) + sh).astype(_BF)
    u = jnp.dot(h, w_ref[:, 4 * RET_WIDTH:], preferred_element_type=_F32)
    o_ref[0, :, 4 * RET_WIDTH:] = u.astype(_BF)
    for j in range(S5_WIDTH // 128):
        u_scr[j] = u[:, 128 * j:128 * (j + 1)]
    o_ref[0, :, 0:4 * RET_WIDTH] = jnp.dot(
        h, w_ref[:, 0:4 * RET_WIDTH], preferred_element_type=_F32).astype(_BF)
    b = pl.program_id(1)
    for j in range(S5_WIDTH // 128):
        for hf in range(2):
            for q in range(tm // 128):
                v = [u_scr[j, pl.ds(128 * q + 8 * hf + a, 8, stride=S5_T), :] for a in range(8)]
                w = _granule_transpose(v)
                for gg in range(8):
                    xg_ref[8 * j + gg, hf, pl.ds(8 * q * nb + b, 8, stride=nb), :] = w[gg]


def _projection(x, mod, mod_row, w_in_bf):
    b, l, _ = x.shape
    tm = min(TOKEN_TILE, l)
    rows = (tm // S5_T) * b
    return pl.pallas_call(
        functools.partial(_proj_kernel, nb=b, tm=tm),
        grid=(l // tm, b),
        in_specs=[pl.BlockSpec((1, tm, D_MODEL), lambda j, i: (i, j, 0)),
                  pl.BlockSpec((1, 6, D_MODEL), lambda j, i: (mod_row(i), 0, 0)),
                  _const_spec((D_MODEL, IN_COLS))],
        out_specs=[pl.BlockSpec((1, tm, IN_COLS), lambda j, i: (i, j, 0)),
                   pl.BlockSpec((S5_GROUPS, 2, rows, 128), lambda j, i: (0, 0, j, 0))],
        out_shape=[jax.ShapeDtypeStruct((b, l, IN_COLS), _BF),
                   jax.ShapeDtypeStruct((S5_GROUPS, 2, (l // S5_T) * b, 128), _F32)],
        scratch_shapes=[pltpu.VMEM((S5_WIDTH // 128, tm, 128), _F32)],
        compiler_params=pltpu.CompilerParams(
            dimension_semantics=("arbitrary", "arbitrary"), vmem_limit_bytes=VMEM_LIMIT),
        name="projection",
    )(x, mod, w_in_bf)


def _ret_kernel(*refs, chunk, n_chunks, nbb, has_s0):
    if has_s0:
        dec_ref, q_ref, k_ref, v_ref, g_ref, s0_ref, o_ref, st_ref, kv_scr = refs
    else:
        dec_ref, q_ref, k_ref, v_ref, g_ref, o_ref, st_ref, kv_scr = refs
        s0_ref = None
    c = chunk
    scale = RET_DK ** -0.5
    pos = lax.broadcasted_iota(jnp.int32, (c, 1), 0).astype(_F32)
    ri = lax.broadcasted_iota(jnp.int32, (c, c), 0)
    ci = lax.broadcasted_iota(jnp.int32, (c, c), 1)
    rel = (ri - ci).astype(_F32)

    for hd in range(RET_HEADS):
        lanes = slice(hd * RET_DK, (hd + 1) * RET_DK)

        def log_gamma(d):
            z = jnp.full((1, 1), dec_ref[d, hd], _F32)
            return jnp.minimum(z, 0.0) - jnp.log(1.0 + jnp.exp(-jnp.abs(z)))

        lg_f, lg_b = log_gamma(0), log_gamma(1)
        kdec_f = jnp.exp(lg_f * (c - 1.0 - pos)) * scale
        kdec_b = jnp.exp(lg_b * pos) * scale
        qdec_f = jnp.exp(lg_f * (pos + 1.0))
        qdec_b = jnp.exp(lg_b * (c - pos))
        dmat = (jnp.where(ri >= ci, jnp.exp(lg_f * jnp.maximum(rel, 0.0)), 0.0)
                + jnp.where(ci >= ri, jnp.exp(lg_b * jnp.maximum(-rel, 0.0)), 0.0)) * scale
        cdec_f = jnp.exp(lg_f * float(c))
        cdec_b = jnp.exp(lg_b * float(c))

        for bb in range(nbb):
            unit = bb * RET_HEADS + hd
            for n in range(n_chunks):
                kc = k_ref[bb, n * c:(n + 1) * c, lanes].astype(_F32)
                vc = v_ref[bb, n * c:(n + 1) * c, lanes]
                kcat = jnp.concatenate([(kc * kdec_f).astype(_BF), (kc * kdec_b).astype(_BF)], axis=1)
                kv_scr[unit, n] = lax.dot_general(kcat, vc, (((0,), (0,)), ((), ())),
                                                  preferred_element_type=_F32)
            if has_s0:
                carry_f = s0_ref[bb, 0, hd]
                carry_b = s0_ref[bb, 1, hd]
            else:
                carry_f = jnp.zeros((RET_DK, RET_DK), _F32)
                carry_b = jnp.zeros((RET_DK, RET_DK), _F32)
            for n in range(n_chunks):
                inc = kv_scr[unit, n, 0:RET_DK, :]
                kv_scr[unit, n, 0:RET_DK, :] = carry_f
                carry_f = cdec_f * carry_f + inc
            for n in range(n_chunks - 1, -1, -1):
                inc = kv_scr[unit, n, RET_DK:2 * RET_DK, :]
                kv_scr[unit, n, RET_DK:2 * RET_DK, :] = carry_b
                carry_b = cdec_b * carry_b + inc
            st_ref[bb, 0, hd] = carry_f
            st_ref[bb, 1, hd] = carry_b

            for n in range(n_chunks):
                sl = slice(n * c, (n + 1) * c)
                qc = q_ref[bb, sl, lanes]
                kc = k_ref[bb, sl, lanes]
                vc = v_ref[bb, sl, lanes]
                scores = lax.dot_general(qc, kc, (((1,), (1,)), ((), ())), preferred_element_type=_F32)
                o = jnp.dot((scores * dmat).astype(_BF), vc, preferred_element_type=_F32)
                qf = qc.astype(_F32)
                qcat = jnp.concatenate([(qf * qdec_f).astype(_BF), (qf * qdec_b).astype(_BF)], axis=1)
                o = o + jnp.dot(qcat, kv_scr[unit, n].astype(_BF), preferred_element_type=_F32)
                gate = g_ref[bb, sl, lanes].astype(_F32)
                o_ref[bb, sl, lanes] = (_norm_rows(o) * (gate * jax.nn.sigmoid(gate))).astype(_BF)


def _retention(proj, ret_decay, s0):
    b, l, _ = proj.shape
    c = min(RET_CHUNK, l)
    n_chunks = l // c
    nbb = max(1, min(b, RET_UNIT_TOKENS // l))
    has_s0 = s0 is not None
    col = lambda j: pl.BlockSpec((nbb, l, RET_WIDTH), lambda i: (i, 0, j))
    st_spec = pl.BlockSpec((nbb, 2, RET_HEADS, RET_DK, RET_DK), lambda i: (i, 0, 0, 0, 0))
    in_specs = [pl.BlockSpec(memory_space=pltpu.SMEM), col(0), col(1), col(2), col(3)]
    args = [ret_decay, proj, proj, proj, proj]
    if has_s0:
        in_specs.append(st_spec)
        args.append(s0)
    return pl.pallas_call(
        functools.partial(_ret_kernel, chunk=c, n_chunks=n_chunks, nbb=nbb, has_s0=has_s0),
        grid=(b // nbb,),
        in_specs=in_specs,
        out_specs=[pl.BlockSpec((nbb, l, RET_WIDTH), lambda i: (i, 0, 0)), st_spec],
        out_shape=[jax.ShapeDtypeStruct((b, l, RET_WIDTH), _BF),
                   jax.ShapeDtypeStruct((b, 2, RET_HEADS, RET_DK, RET_DK), _F32)],
        scratch_shapes=[pltpu.VMEM((nbb * RET_HEADS, n_chunks, 2 * RET_DK, RET_DK), _F32)],
        compiler_params=pltpu.CompilerParams(
            dimension_semantics=("arbitrary",), vmem_limit_bytes=VMEM_LIMIT),
        name="retention",
    )(*args)


def _swap_halves(z):
    return pltpu.roll(z, S5_STATE, axis=z.ndim - 1)


def _s5_prep_kernel(ar_ref, ai_ref, ldt_ref, bt_ref, cp_ref, m_ref, wst_ref, cout_ref, apow_ref, *, ng):
    lane = lax.broadcasted_iota(jnp.int32, (1, 2 * S5_STATE), 1)
    sgn = jnp.where(lane < S5_STATE, -1.0, 1.0).astype(_F32)
    gran = lax.broadcasted_iota(jnp.int32, (1, S5_ROW), 1) // S5_CH
    ar = ar_ref[...]
    ai = ai_ref[...]
    dt = jnp.exp(ldt_ref[...])
    mag = jnp.exp(ar * dt)
    ang = ai * dt
    pr1 = mag * jnp.cos(ang)
    pi1 = mag * jnp.sin(ang)
    pw_r = [jnp.ones_like(pr1), pr1]
    pw_i = [jnp.zeros_like(pi1), pi1]
    for _ in range(2, S5_T + 1):
        pr, pi = pw_r[-1], pw_i[-1]
        pw_r.append(pr * pr1 - pi * pi1)
        pw_i.append(pr * pi1 + pi * pr1)
    pw_is = [p * sgn for p in pw_i]
    x2 = pr1 - 1.0
    den = ar * ar + ai * ai
    coef_re = (x2 * ar + pi1 * ai) / den
    coef_im_s = ((pi1 * ar - x2 * ai) / den) * sgn
    for gi in range(ng):
        bt = bt_ref[gi]
        cp = cp_ref[gi]
        cc = cp * (-sgn)
        cps = _swap_halves(cp)
        bts = _swap_halves(bt)
        gens = []
        cout_rows = [[], []]
        for d in range(2):
            r = 2 * gi + d
            row = lambda a: a[r:r + 1, :]
            bb = row(coef_re) * bt + row(coef_im_s) * bts
            bbs = _swap_halves(bb)
            w_rows = []
            for m in range(S5_T):
                e = (S5_T - 1 - m) if d == 0 else m
                w = row(pw_r[e]) * bb + row(pw_is[e]) * bbs
                rows = slice(m * S5_CH, (m + 1) * S5_CH)
                wst_ref[gi, rows, d * 128:(d + 1) * 128] = w.astype(_BF)
                wst_ref[gi, rows, 256 + d * 128:256 + (d + 1) * 128] = _swap_halves(w).astype(_BF)
                w_rows.append(w)
            for i in range(S5_T):
                e = (i + 1) if d == 0 else (S5_T - i)
                gmat = row(pw_r[e]) * cp + row(pw_is[e]) * cps
                cout_rows[d].append(gmat * (-sgn))
            gens.append(lax.dot_general(cc, jnp.concatenate(w_rows, axis=0), (((1,), (1,)), ((), ())),
                                        preferred_element_type=_F32, precision=lax.Precision.HIGHEST))
            apow_ref[gi, 2 * d:2 * d + 1, :] = row(pw_r[S5_T])
            apow_ref[gi, 2 * d + 1:2 * d + 2, :] = row(pw_is[S5_T])
        blocks = []
        for t in range(S5_T):
            sf = (S5_CH * (t + 1)) % S5_ROW
            rf = pltpu.roll(gens[0], sf, axis=1) if sf else gens[0]
            rb = pltpu.roll(gens[1], S5_CH * t, axis=1) if t else gens[1]
            blocks.append(jnp.where(gran <= t, rf, 0.0) + jnp.where(gran >= t, rb, 0.0))
        m_ref[gi] = jnp.concatenate(blocks, axis=0).T.astype(_BF)
        cout_t = jnp.concatenate([jnp.concatenate(cout_rows[0], axis=0),
                                  jnp.concatenate(cout_rows[1], axis=0)], axis=1)
        cout_ref[gi] = cout_t.T.astype(_BF)


def _s5_operators(s5_a_re, s5_a_im, s5_log_dt, s5_b_re, s5_b_im, s5_c_re, s5_c_im):
    g = S5_GROUPS
    ng = S5_GROUP_TILE
    dup = lambda a: jnp.concatenate([a, a], axis=-1).transpose(1, 0, 2).reshape(2 * g, 128)
    ar2 = dup(s5_a_re)
    ai2 = dup(s5_a_im)
    ldt = s5_log_dt.T.reshape(2 * g, 1)
    btp = jnp.concatenate([s5_b_re.transpose(0, 2, 1), s5_b_im.transpose(0, 2, 1)], axis=-1)
    cpk = jnp.concatenate([s5_c_re, s5_c_im], axis=-1)
    gspec = lambda *shape: pl.BlockSpec((ng,) + shape, lambda i: (i,) + (0,) * len(shape))
    rspec = lambda w: pl.BlockSpec((2 * ng, w), lambda i: (i, 0))
    return pl.pallas_call(
        functools.partial(_s5_prep_kernel, ng=ng),
        grid=(g // ng,),
        in_specs=[rspec(128), rspec(128), rspec(1), gspec(S5_CH, 128), gspec(S5_CH, 128)],
        out_specs=[gspec(S5_ROW, S5_ROW), gspec(S5_ROW, 512), gspec(256, S5_ROW), gspec(4, 128)],
        out_shape=[jax.ShapeDtypeStruct((g, S5_ROW, S5_ROW), _BF),
                   jax.ShapeDtypeStruct((g, S5_ROW, 512), _BF),
                   jax.ShapeDtypeStruct((g, 256, S5_ROW), _BF),
                   jax.ShapeDtypeStruct((g, 4, 128), _F32)],
        compiler_params=pltpu.CompilerParams(dimension_semantics=("arbitrary",)),
        name="s5_operators",
    )(ar2, ai2, ldt, btp, cpk)


def _s5_kernel(x_ref, m_ref, wst_ref, cout_ref, apow_ref, s0_ref, y_ref, fin_ref, loc_scr, prev_scr,
               *, nb, nk, ng):
    load_x = lambda gi: jnp.concatenate([x_ref[gi, 0], x_ref[gi, 1]], axis=1).astype(_BF)
    for gi in range(ng):
        loc_scr[gi] = jnp.dot(load_x(gi), wst_ref[gi], preferred_element_type=_F32)
    coef, st = [], []
    for gi in range(ng):
        coef.append([apow_ref[gi, r:r + 1, :] for r in range(4)])
        zf = s0_ref[gi, :, 0:128]
        zb = s0_ref[gi, :, 128:256]
        st.append([zf, _swap_halves(zf), zb, _swap_halves(zb)])
    for k in range(nk):
        kb = nk - 1 - k
        rows_f = slice(k * nb, (k + 1) * nb)
        rows_b = slice(kb * nb, (kb + 1) * nb)
        for gi in range(ng):
            ar_f, ai_f, ar_b, ai_b = coef[gi]
            zf, zfs, zb, zbs = st[gi]
            prev_scr[gi, rows_f, 0:128] = zf
            prev_scr[gi, rows_b, 128:256] = zb
            st[gi] = [ar_f * zf + ai_f * zfs + loc_scr[gi, rows_f, 0:128],
                      ar_f * zfs - ai_f * zf + loc_scr[gi, rows_f, 256:384],
                      ar_b * zb + ai_b * zbs + loc_scr[gi, rows_b, 128:256],
                      ar_b * zbs - ai_b * zb + loc_scr[gi, rows_b, 384:512]]
    for gi in range(ng):
        fin_ref[gi, :, 0:128] = st[gi][0]
        fin_ref[gi, :, 128:256] = st[gi][2]
        y = jnp.dot(load_x(gi), m_ref[gi], preferred_element_type=_F32)
        y = y + jnp.dot(prev_scr[gi].astype(_BF), cout_ref[gi], preferred_element_type=_F32)
        y_ref[gi, 0] = y[:, 0:128]
        y_ref[gi, 1] = y[:, 128:256]


def _s5_scan(xg, m_op, wst, cout, apow, s0p, nb, nk):
    g = S5_GROUPS
    ng = S5_GROUP_TILE
    rows = nk * nb
    gspec = lambda *shape: pl.BlockSpec((ng,) + shape, lambda i: (i,) + (0,) * len(shape))
    return pl.pallas_call(
        functools.partial(_s5_kernel, nb=nb, nk=nk, ng=ng),
        grid=(g // ng,),
        in_specs=[gspec(2, rows, 128), gspec(S5_ROW, S5_ROW), gspec(S5_ROW, 512), gspec(256, S5_ROW),
                  gspec(4, 128), gspec(nb, 256)],
        out_specs=[gspec(2, rows, 128), gspec(nb, 256)],
        out_shape=[jax.ShapeDtypeStruct((g, 2, rows, 128), _F32),
                   jax.ShapeDtypeStruct((g, nb, 256), _F32)],
        scratch_shapes=[pltpu.VMEM((ng, rows, 512), _F32), pltpu.VMEM((ng, rows, 256), _F32)],
        compiler_params=pltpu.CompilerParams(
            dimension_semantics=("arbitrary",), vmem_limit_bytes=VMEM_LIMIT),
        name="s5_scan",
    )(xg, m_op, wst, cout, apow, s0p)


def _post_kernel(x_ref, o_ref, yg_ref, u_ref, mod_ref, dsk_ref, wglu_ref, bglu_ref, wout_ref,
                 l1g_ref, l1b_ref, w1_ref, b1_ref, w2_ref, b2_ref, l2g_ref, l2b_ref, out_ref, y_scr,
                 *, nb, tm):
    mix = jnp.dot(o_ref[0], wout_ref[0:RET_WIDTH, :], preferred_element_type=_F32)
    b = pl.program_id(1)
    for j in range(S5_WIDTH // 128):
        for hf in range(2):
            for q in range(tm // 128):
                w = [yg_ref[8 * j + gg, hf, pl.ds(8 * q * nb + b, 8, stride=nb), :] for gg in range(8)]
                v = _granule_transpose(w)
                for a in range(8):
                    y_scr[j, pl.ds(128 * q + 8 * hf + a, 8, stride=S5_T), :] = v[a]
    x = x_ref[0]
    g1 = mod_ref[0, 2:3, :]
    sh2 = mod_ref[0, 3:4, :]
    sc2 = mod_ref[0, 4:5, :]
    g2 = mod_ref[0, 5:6, :]
    y = jnp.concatenate([y_scr[j] for j in range(S5_WIDTH // 128)], axis=1)
    y = y + dsk_ref[...] * u_ref[0].astype(_F32)
    y = jax.nn.gelu(y)
    y = y * jax.nn.sigmoid(jnp.dot(y.astype(_BF), wglu_ref[...], preferred_element_type=_F32)
                           + bglu_ref[...])
    mix = mix + jnp.dot(y.astype(_BF), wout_ref[RET_WIDTH:, :], preferred_element_type=_F32)
    x1 = _norm_rows(ALPHA * x + g1 * mix) * l1g_ref[...] + l1b_ref[...]
    h = (_norm_rows(x1) * (1.0 + sc2) + sh2).astype(_BF)
    acc = jnp.zeros_like(x1)
    fc = 1024
    for j in range(D_FF // fc):
        a = jnp.dot(h, w1_ref[:, j * fc:(j + 1) * fc], preferred_element_type=_F32) + b1_ref[:, j * fc:(j + 1) * fc]
        a = jnp.square(jnp.maximum(a, 0.0)).astype(_BF)
        acc = acc + jnp.dot(a, w2_ref[j * fc:(j + 1) * fc, :], preferred_element_type=_F32)
    f = acc + b2_ref[...]
    out_ref[0] = _norm_rows(ALPHA * x1 + g2 * f) * l2g_ref[...] + l2b_ref[...]


def _post(x, o_ret, yg, proj, mod, mod_row, pw):
    b, l, _ = x.shape
    tm = min(TOKEN_TILE, l)
    rows = (tm // S5_T) * b
    tok = lambda w: pl.BlockSpec((1, tm, w), lambda j, i: (i, j, 0))
    row = lambda n: _const_spec((1, n))
    return pl.pallas_call(
        functools.partial(_post_kernel, nb=b, tm=tm),
        grid=(l // tm, b),
        in_specs=[tok(D_MODEL), tok(RET_WIDTH),
                  pl.BlockSpec((S5_GROUPS, 2, rows, 128), lambda j, i: (0, 0, j, 0)),
                  pl.BlockSpec((1, tm, S5_WIDTH), lambda j, i: (i, j, 4)),
                  pl.BlockSpec((1, 6, D_MODEL), lambda j, i: (mod_row(i), 0, 0)),
                  row(S5_WIDTH), _const_spec((S5_WIDTH, S5_WIDTH)), row(S5_WIDTH),
                  _const_spec((D_MODEL, D_MODEL)), row(D_MODEL), row(D_MODEL),
                  _const_spec((D_MODEL, D_FF)), row(D_FF), _const_spec((D_FF, D_MODEL)), row(D_MODEL),
                  row(D_MODEL), row(D_MODEL)],
        out_specs=tok(D_MODEL),
        out_shape=jax.ShapeDtypeStruct((b, l, D_MODEL), _F32),
        scratch_shapes=[pltpu.VMEM((S5_WIDTH // 128, tm, 128), _F32)],
        compiler_params=pltpu.CompilerParams(
            dimension_semantics=("arbitrary", "arbitrary"), vmem_limit_bytes=VMEM_LIMIT),
        name="post",
    )(x, o_ret, yg, proj, mod, *pw)


def _layer_pass(x, mod, mod_row, ret_s0, s5_s0_re, s5_s0_im, w_in_bf, ret_decay, s5_ops, post_w):
    b, l, _ = x.shape
    nk = l // S5_T
    g = S5_GROUPS
    proj, xg = _projection(x, mod, mod_row, w_in_bf)
    o_ret, ret_state = _retention(proj, ret_decay, ret_s0)
    if s5_s0_re is None:
        s0p = jnp.zeros((g, b, 256), _F32)
    else:
        s0p = jnp.concatenate([s5_s0_re, s5_s0_im], axis=-1)
        s0p = s0p.transpose(2, 0, 1, 3).reshape(g, b, 256)
    yg, fin = _s5_scan(xg, *s5_ops, s0p, b, nk)
    fin = fin.reshape(g, b, 2, 2, S5_STATE).transpose(1, 2, 0, 3, 4)
    out = _post(x, o_ret, yg, proj, mod, mod_row, post_w)
    return out, ret_state, fin[:, :, :, 0, :], fin[:, :, :, 1, :]


def kernel(x_prompt, x_sample, state_ret, state_s5_re, state_s5_im, c, c_ctx, w_ada, b_ada, w_in,
           ret_decay, s5_a_re, s5_a_im, s5_log_dt, s5_b_re, s5_b_im, s5_c_re, s5_c_im, s5_d, w_glu,
           b_glu, w_out, ln1_g, ln1_b, w_ff1, b_ff1, w_ff2, b_ff2, ln2_g, ln2_b):
    depth = w_ada.shape[0]
    bs = x_sample.shape[0]
    y_p, y_s = x_prompt, x_sample
    rets, s5rs, s5is = [], [], []
    for layer in range(depth):
        cond8 = jnp.concatenate(
            [c_ctx[None, :], c, jnp.zeros((8 - 1 - bs, D_MODEL), _F32)], axis=0)
        mod = _modulation(cond8, w_ada[layer], b_ada[layer]).reshape(8, 6, D_MODEL)
        s5_ops = _s5_operators(s5_a_re[layer], s5_a_im[layer], s5_log_dt[layer], s5_b_re[layer],
                               s5_b_im[layer], s5_c_re[layer], s5_c_im[layer])
        w_in_bf = w_in[layer].astype(_BF)
        r2 = lambda a: a[layer].reshape(1, -1)
        post_w = (r2(s5_d), w_glu[layer].astype(_BF), r2(b_glu), w_out[layer].astype(_BF),
                  r2(ln1_g), r2(ln1_b), w_ff1[layer].astype(_BF), r2(b_ff1),
                  w_ff2[layer].astype(_BF), r2(b_ff2), r2(ln2_g), r2(ln2_b))
        y_p, r_st, s5_re, s5_im = _layer_pass(
            y_p, mod, lambda i: 0, None, None, None, w_in_bf, ret_decay[layer], s5_ops, post_w)
        rets.append(r_st)
        s5rs.append(s5_re)
        s5is.append(s5_im)
        y_s, _, _, _ = _layer_pass(
            y_s, mod, lambda i: i + 1, state_ret[:, layer], state_s5_re[:, layer],
            state_s5_im[:, layer], w_in_bf, ret_decay[layer], s5_ops, post_w)
    return (y_p, y_s, jnp.stack(rets, axis=1), jnp.stack(s5rs, axis=1), jnp.stack(s5is, axis=1))
```

```python
import functools

import jax
import jax.numpy as jnp
from jax import lax
from jax.experimental import pallas as pl
from jax.experimental.pallas import tpu as pltpu

D_MODEL = 1024
RET_HEADS = 4
RET_DK = 128
RET_WIDTH = RET_HEADS * RET_DK
S5_CH = 16
S5_GROUPS = 32
S5_STATE = 64
S5_WIDTH = S5_GROUPS * S5_CH
D_FF = 4 * D_MODEL
IN_COLS = 4 * RET_WIDTH + S5_WIDTH
ALPHA = 2.0 ** 0.25
LN_EPS = 1e-5

S5_T = 16
S5_ROW = S5_T * S5_CH
S5_GROUP_TILE = 4
RET_CHUNK = 256
RET_UNIT_TOKENS = 1024
TOKEN_TILE = 512
MLP_CHUNK = 1024
VMEM_LIMIT = 56 * 1024 * 1024

_BF = jnp.bfloat16
_F32 = jnp.float32


def _norm_rows(x):
    mu = jnp.mean(x, axis=-1, keepdims=True)
    xc = x - mu
    var = jnp.mean(xc * xc, axis=-1, keepdims=True)
    return xc * lax.rsqrt(var + LN_EPS)


def _const_spec(shape):
    nd = len(shape)
    return pl.BlockSpec(shape, lambda *_: (0,) * nd, pipeline_mode=pl.Buffered(1))


def _granule_transpose(v):
    lane = lax.broadcasted_iota(jnp.int32, (1, 128), 1)
    v = list(v)
    for d in (4, 2, 1):
        bit = ((lane // S5_CH) & d) != 0
        nv = list(v)
        for a in range(8):
            if a & d:
                continue
            lo, hi = v[a], v[a + d]
            nv[a] = jnp.where(bit, pltpu.roll(hi, S5_CH * d, axis=1), lo)
            nv[a + d] = jnp.where(bit, hi, pltpu.roll(lo, 128 - S5_CH * d, axis=1))
        v = nv
    return v


def _mod_kernel(cond_ref, w_ref, b_ref, o_ref):
    c = cond_ref[...]
    s = (c * jax.nn.sigmoid(c)).astype(_BF)
    o_ref[...] = jnp.dot(s, w_ref[...].astype(_BF), preferred_element_type=_F32) + b_ref[...]


def _modulation(cond8, w_ada, b_ada):
    n = w_ada.shape[1]
    tn = 1024
    return pl.pallas_call(
        _mod_kernel,
        grid=(n // tn,),
        in_specs=[pl.BlockSpec((8, D_MODEL), lambda j: (0, 0)),
                  pl.BlockSpec((D_MODEL, tn), lambda j: (0, j)),
                  pl.BlockSpec((1, tn), lambda j: (0, j))],
        out_specs=pl.BlockSpec((8, tn), lambda j: (0, j)),
        out_shape=jax.ShapeDtypeStruct((8, n), _F32),
        compiler_params=pltpu.CompilerParams(dimension_semantics=("arbitrary",)),
        name="modulation",
    )(cond8, w_ada, b_ada.reshape(1, n))


def _proj_kernel(x_ref, mod_ref, w_ref, o_ref, xg_ref, u_scr, *, nb, tm):
    x = x_ref[0]
    sh = mod_ref[0, 0:1, :]
    sc = mod_ref[0, 1:2, :]
    h = (_norm_rows(x) * (1.0 + sc) + sh).astype(_BF)
    u = jnp.dot(h, w_ref[:, 4 * RET_WIDTH:], preferred_element_type=_F32)
    o_ref[0, :, 4 * RET_WIDTH:] = u.astype(_BF)
    for j in range(S5_WIDTH // 128):
        u_scr[j] = u[:, 128 * j:128 * (j + 1)]
    o_ref[0, :, 0:4 * RET_WIDTH] = jnp.dot(
        h, w_ref[:, 0:4 * RET_WIDTH], preferred_element_type=_F32).astype(_BF)
    b = pl.program_id(1)
    for j in range(S5_WIDTH // 128):
        for hf in range(2):
            for q in range(tm // 128):
                v = [u_scr[j, pl.ds(128 * q + 8 * hf + a, 8, stride=S5_T), :] for a in range(8)]
                w = _granule_transpose(v)
                for gg in range(8):
                    xg_ref[8 * j + gg, hf, pl.ds(8 * q * nb + b, 8, stride=nb), :] = w[gg]


def _projection(x, mod, mod_row, w_in_bf):
    b, l, _ = x.shape
    tm = min(TOKEN_TILE, l)
    rows = (tm // S5_T) * b
    return pl.pallas_call(
        functools.partial(_proj_kernel, nb=b, tm=tm),
        grid=(l // tm, b),
        in_specs=[pl.BlockSpec((1, tm, D_MODEL), lambda j, i: (i, j, 0)),
                  pl.BlockSpec((1, 6, D_MODEL), lambda j, i: (mod_row(i), 0, 0)),
                  _const_spec((D_MODEL, IN_COLS))],
        out_specs=[pl.BlockSpec((1, tm, IN_COLS), lambda j, i: (i, j, 0)),
                   pl.BlockSpec((S5_GROUPS, 2, rows, 128), lambda j, i: (0, 0, j, 0))],
        out_shape=[jax.ShapeDtypeStruct((b, l, IN_COLS), _BF),
                   jax.ShapeDtypeStruct((S5_GROUPS, 2, (l // S5_T) * b, 128), _F32)],
        scratch_shapes=[pltpu.VMEM((S5_WIDTH // 128, tm, 128), _F32)],
        compiler_params=pltpu.CompilerParams(
            dimension_semantics=("arbitrary", "arbitrary"), vmem_limit_bytes=VMEM_LIMIT),
        name="projection",
    )(x, mod, w_in_bf)


def _ret_kernel(*refs, chunk, n_chunks, nbb, has_s0):
    if has_s0:
        dec_ref, q_ref, k_ref, v_ref, g_ref, s0_ref, o_ref, st_ref, kv_scr = refs
    else:
        dec_ref, q_ref, k_ref, v_ref, g_ref, o_ref, st_ref, kv_scr = refs
        s0_ref = None
    c = chunk
    scale = RET_DK ** -0.5
    pos = lax.broadcasted_iota(jnp.int32, (c, 1), 0).astype(_F32)
    ri = lax.broadcasted_iota(jnp.int32, (c, c), 0)
    ci = lax.broadcasted_iota(jnp.int32, (c, c), 1)
    rel = (ri - ci).astype(_F32)

    for hd in range(RET_HEADS):
        lanes = slice(hd * RET_DK, (hd + 1) * RET_DK)

        def log_gamma(d):
            z = jnp.full((1, 1), dec_ref[d, hd], _F32)
            return jnp.minimum(z, 0.0) - jnp.log(1.0 + jnp.exp(-jnp.abs(z)))

        lg_f, lg_b = log_gamma(0), log_gamma(1)
        kdec_f = jnp.exp(lg_f * (c - 1.0 - pos)) * scale
        kdec_b = jnp.exp(lg_b * pos) * scale
        qdec_f = jnp.exp(lg_f * (pos + 1.0))
        qdec_b = jnp.exp(lg_b * (c - pos))
        dmat = (jnp.where(ri >= ci, jnp.exp(lg_f * jnp.maximum(rel, 0.0)), 0.0)
                + jnp.where(ci >= ri, jnp.exp(lg_b * jnp.maximum(-rel, 0.0)), 0.0)) * scale
        cdec_f = jnp.exp(lg_f * float(c))
        cdec_b = jnp.exp(lg_b * float(c))

        for bb in range(nbb):
            unit = bb * RET_HEADS + hd
            for n in range(n_chunks):
                kc = k_ref[bb, n * c:(n + 1) * c, lanes].astype(_F32)
                vc = v_ref[bb, n * c:(n + 1) * c, lanes]
                kcat = jnp.concatenate([(kc * kdec_f).astype(_BF), (kc * kdec_b).astype(_BF)], axis=1)
                kv_scr[unit, n] = lax.dot_general(kcat, vc, (((0,), (0,)), ((), ())),
                                                  preferred_element_type=_F32)
            if has_s0:
                carry_f = s0_ref[bb, 0, hd]
                carry_b = s0_ref[bb, 1, hd]
            else:
                carry_f = jnp.zeros((RET_DK, RET_DK), _F32)
                carry_b = jnp.zeros((RET_DK, RET_DK), _F32)
            for n in range(n_chunks):
                inc = kv_scr[unit, n, 0:RET_DK, :]
                kv_scr[unit, n, 0:RET_DK, :] = carry_f
                carry_f = cdec_f * carry_f + inc
            for n in range(n_chunks - 1, -1, -1):
                inc = kv_scr[unit, n, RET_DK:2 * RET_DK, :]
                kv_scr[unit, n, RET_DK:2 * RET_DK, :] = carry_b
                carry_b = cdec_b * carry_b + inc
            st_ref[bb, 0, hd] = carry_f
            st_ref[bb, 1, hd] = carry_b

            for n in range(n_chunks):
                sl = slice(n * c, (n + 1) * c)
                qc = q_ref[bb, sl, lanes]
                kc = k_ref[bb, sl, lanes]
                vc = v_ref[bb, sl, lanes]
                scores = lax.dot_general(qc, kc, (((1,), (1,)), ((), ())), preferred_element_type=_F32)
                o = jnp.dot((scores * dmat).astype(_BF), vc, preferred_element_type=_F32)
                qf = qc.astype(_F32)
                qcat = jnp.concatenate([(qf * qdec_f).astype(_BF), (qf * qdec_b).astype(_BF)], axis=1)
                o = o + jnp.dot(qcat, kv_scr[unit, n].astype(_BF), preferred_element_type=_F32)
                gate = g_ref[bb, sl, lanes].astype(_F32)
                o_ref[bb, sl, lanes] = (_norm_rows(o) * (gate * jax.nn.sigmoid(gate))).astype(_BF)


def _retention(proj, ret_decay, s0):
    b, l, _ = proj.shape
    c = min(RET_CHUNK, l)
    n_chunks = l // c
    nbb = max(1, min(b, RET_UNIT_TOKENS // l))
    has_s0 = s0 is not None
    col = lambda j: pl.BlockSpec((nbb, l, RET_WIDTH), lambda i: (i, 0, j))
    st_spec = pl.BlockSpec((nbb, 2, RET_HEADS, RET_DK, RET_DK), lambda i: (i, 0, 0, 0, 0))
    in_specs = [pl.BlockSpec(memory_space=pltpu.SMEM), col(0), col(1), col(2), col(3)]
    args = [ret_decay, proj, proj, proj, proj]
    if has_s0:
        in_specs.append(st_spec)
        args.append(s0)
    return pl.pallas_call(
        functools.partial(_ret_kernel, chunk=c, n_chunks=n_chunks, nbb=nbb, has_s0=has_s0),
        grid=(b // nbb,),
        in_specs=in_specs,
        out_specs=[pl.BlockSpec((nbb, l, RET_WIDTH), lambda i: (i, 0, 0)), st_spec],
        out_shape=[jax.ShapeDtypeStruct((b, l, RET_WIDTH), _BF),
                   jax.ShapeDtypeStruct((b, 2, RET_HEADS, RET_DK, RET_DK), _F32)],
        scratch_shapes=[pltpu.VMEM((nbb * RET_HEADS, n_chunks, 2 * RET_DK, RET_DK), _F32)],
        compiler_params=pltpu.CompilerParams(
            dimension_semantics=("arbitrary",), vmem_limit_bytes=VMEM_LIMIT),
        name="retention",
    )(*args)


def _swap_halves(z):
    return pltpu.roll(z, S5_STATE, axis=z.ndim - 1)


def _s5_prep_kernel(ar_ref, ai_ref, ldt_ref, bt_ref, cp_ref, m_ref, wst_ref, cout_ref, apow_ref, *, ng):
    lane = lax.broadcasted_iota(jnp.int32, (1, 2 * S5_STATE), 1)
    sgn = jnp.where(lane < S5_STATE, -1.0, 1.0).astype(_F32)
    gran = lax.broadcasted_iota(jnp.int32, (1, S5_ROW), 1) // S5_CH
    ar = ar_ref[...]
    ai = ai_ref[...]
    dt = jnp.exp(ldt_ref[...])
    mag = jnp.exp(ar * dt)
    ang = ai * dt
    pr1 = mag * jnp.cos(ang)
    pi1 = mag * jnp.sin(ang)
    pw_r = [jnp.ones_like(pr1), pr1]
    pw_i = [jnp.zeros_like(pi1), pi1]
    for _ in range(2, S5_T + 1):
        pr, pi = pw_r[-1], pw_i[-1]
        pw_r.append(pr * pr1 - pi * pi1)
        pw_i.append(pr * pi1 + pi * pr1)
    pw_is = [p * sgn for p in pw_i]
    x2 = pr1 - 1.0
    den = ar * ar + ai * ai
    coef_re = (x2 * ar + pi1 * ai) / den
    coef_im_s = ((pi1 * ar - x2 * ai) / den) * sgn
    for gi in range(ng):
        bt = bt_ref[gi]
        cp = cp_ref[gi]
        cc = cp * (-sgn)
        cps = _swap_halves(cp)
        bts = _swap_halves(bt)
        gens = []
        cout_rows = [[], []]
        for d in range(2):
            r = 2 * gi + d
            row = lambda a: a[r:r + 1, :]
            bb = row(coef_re) * bt + row(coef_im_s) * bts
            bbs = _swap_halves(bb)
            w_rows = []
            for m in range(S5_T):
                e = (S5_T - 1 - m) if d == 0 else m
                w = row(pw_r[e]) * bb + row(pw_is[e]) * bbs
                rows = slice(m * S5_CH, (m + 1) * S5_CH)
                wst_ref[gi, rows, d * 128:(d + 1) * 128] = w.astype(_BF)
                wst_ref[gi, rows, 256 + d * 128:256 + (d + 1) * 128] = _swap_halves(w).astype(_BF)
                w_rows.append(w)
            for i in range(S5_T):
                e = (i + 1) if d == 0 else (S5_T - i)
                gmat = row(pw_r[e]) * cp + row(pw_is[e]) * cps
                cout_rows[d].append(gmat * (-sgn))
            gens.append(lax.dot_general(cc, jnp.concatenate(w_rows, axis=0), (((1,), (1,)), ((), ())),
                                        preferred_element_type=_F32, precision=lax.Precision.HIGHEST))
            apow_ref[gi, 2 * d:2 * d + 1, :] = row(pw_r[S5_T])
            apow_ref[gi, 2 * d + 1:2 * d + 2, :] = row(pw_is[S5_T])
        blocks = []
        for t in range(S5_T):
            sf = (S5_CH * (t + 1)) % S5_ROW
            rf = pltpu.roll(gens[0], sf, axis=1) if sf else gens[0]
            rb = pltpu.roll(gens[1], S5_CH * t, axis=1) if t else gens[1]
            blocks.append(jnp.where(gran <= t, rf, 0.0) + jnp.where(gran >= t, rb, 0.0))
        m_ref[gi] = jnp.concatenate(blocks, axis=0).T.astype(_BF)
        cout_t = jnp.concatenate([jnp.concatenate(cout_rows[0], axis=0),
                                  jnp.concatenate(cout_rows[1], axis=0)], axis=1)
        cout_ref[gi] = cout_t.T.astype(_BF)


def _s5_operators(s5_a_re, s5_a_im, s5_log_dt, s5_b_re, s5_b_im, s5_c_re, s5_c_im):
    g = S5_GROUPS
    ng = S5_GROUP_TILE
    dup = lambda a: jnp.concatenate([a, a], axis=-1).transpose(1, 0, 2).reshape(2 * g, 128)
    ar2 = dup(s5_a_re)
    ai2 = dup(s5_a_im)
    ldt = s5_log_dt.T.reshape(2 * g, 1)
    btp = jnp.concatenate([s5_b_re.transpose(0, 2, 1), s5_b_im.transpose(0, 2, 1)], axis=-1)
    cpk = jnp.concatenate([s5_c_re, s5_c_im], axis=-1)
    gspec = lambda *shape: pl.BlockSpec((ng,) + shape, lambda i: (i,) + (0,) * len(shape))
    rspec = lambda w: pl.BlockSpec((2 * ng, w), lambda i: (i, 0))
    return pl.pallas_call(
        functools.partial(_s5_prep_kernel, ng=ng),
        grid=(g // ng,),
        in_specs=[rspec(128), rspec(128), rspec(1), gspec(S5_CH, 128), gspec(S5_CH, 128)],
        out_specs=[gspec(S5_ROW, S5_ROW), gspec(S5_ROW, 512), gspec(256, S5_ROW), gspec(4, 128)],
        out_shape=[jax.ShapeDtypeStruct((g, S5_ROW, S5_ROW), _BF),
                   jax.ShapeDtypeStruct((g, S5_ROW, 512), _BF),
                   jax.ShapeDtypeStruct((g, 256, S5_ROW), _BF),
                   jax.ShapeDtypeStruct((g, 4, 128), _F32)],
        compiler_params=pltpu.CompilerParams(dimension_semantics=("arbitrary",)),
        name="s5_operators",
    )(ar2, ai2, ldt, btp, cpk)


def _s5_kernel(x_ref, m_ref, wst_ref, cout_ref, apow_ref, s0_ref, y_ref, fin_ref, loc_scr, prev_scr,
               *, nb, nk, ng):
    load_x = lambda gi: jnp.concatenate([x_ref[gi, 0], x_ref[gi, 1]], axis=1).astype(_BF)
    for gi in range(ng):
        loc_scr[gi] = jnp.dot(load_x(gi), wst_ref[gi], preferred_element_type=_F32)
    coef, st = [], []
    for gi in range(ng):
        coef.append([apow_ref[gi, r:r + 1, :] for r in range(4)])
        zf = s0_ref[gi, :, 0:128]
        zb = s0_ref[gi, :, 128:256]
        st.append([zf, _swap_halves(zf), zb, _swap_halves(zb)])
    for k in range(nk):
        kb = nk - 1 - k
        rows_f = slice(k * nb, (k + 1) * nb)
        rows_b = slice(kb * nb, (kb + 1) * nb)
        for gi in range(ng):
            ar_f, ai_f, ar_b, ai_b = coef[gi]
            zf, zfs, zb, zbs = st[gi]
            prev_scr[gi, rows_f, 0:128] = zf
            prev_scr[gi, rows_b, 128:256] = zb
            st[gi] = [ar_f * zf + ai_f * zfs + loc_scr[gi, rows_f, 0:128],
                      ar_f * zfs - ai_f * zf + loc_scr[gi, rows_f, 256:384],
                      ar_b * zb + ai_b * zbs + loc_scr[gi, rows_b, 128:256],
                      ar_b * zbs - ai_b * zb + loc_scr[gi, rows_b, 384:512]]
    for gi in range(ng):
        fin_ref[gi, :, 0:128] = st[gi][0]
        fin_ref[gi, :, 128:256] = st[gi][2]
        y = jnp.dot(load_x(gi), m_ref[gi], preferred_element_type=_F32)
        y = y + jnp.dot(prev_scr[gi].astype(_BF), cout_ref[gi], preferred_element_type=_F32)
        y_ref[gi, 0] = y[:, 0:128]
        y_ref[gi, 1] = y[:, 128:256]


def _s5_scan(xg, m_op, wst, cout, apow, s0p, nb, nk):
    g = S5_GROUPS
    ng = S5_GROUP_TILE
    rows = nk * nb
    gspec = lambda *shape: pl.BlockSpec((ng,) + shape, lambda i: (i,) + (0,) * len(shape))
    return pl.pallas_call(
        functools.partial(_s5_kernel, nb=nb, nk=nk, ng=ng),
        grid=(g // ng,),
        in_specs=[gspec(2, rows, 128), gspec(S5_ROW, S5_ROW), gspec(S5_ROW, 512), gspec(256, S5_ROW),
                  gspec(4, 128), gspec(nb, 256)],
        out_specs=[gspec(2, rows, 128), gspec(nb, 256)],
        out_shape=[jax.ShapeDtypeStruct((g, 2, rows, 128), _F32),
                   jax.ShapeDtypeStruct((g, nb, 256), _F32)],
        scratch_shapes=[pltpu.VMEM((ng, rows, 512), _F32), pltpu.VMEM((ng, rows, 256), _F32)],
        compiler_params=pltpu.CompilerParams(
            dimension_semantics=("arbitrary",), vmem_limit_bytes=VMEM_LIMIT),
        name="s5_scan",
    )(xg, m_op, wst, cout, apow, s0p)


def _post_kernel(x_ref, o_ref, yg_ref, u_ref, mod_ref, modp_ref, dsk_ref, wglu_ref, bglu_ref, wout_ref,
                 l1g_ref, l1b_ref, w1_ref, b1_ref, w2_ref, b2_ref, l2g_ref, l2b_ref, out_ref,
                 y_scr, h_next, x1_next, h_cur, x1_cur, *, nb, tm, n_steps):
    i = pl.program_id(0)
    b = jnp.minimum(i, n_steps - 1) % nb
    front = functools.partial(_post_mix_front, o_ref, yg_ref, u_ref, dsk_ref, wglu_ref, bglu_ref,
                              wout_ref, y_scr, nb=nb, tm=tm, b=b)
    back = functools.partial(_post_mix_back, x_ref, mod_ref, wout_ref, l1g_ref, l1b_ref,
                             h_next, x1_next)

    @pl.when(i == 0)
    def _():
        back(*front())

    @pl.when(i > 0)
    def _():
        h_cur[...] = h_next[...]
        x1_cur[...] = x1_next[...]
        h = h_cur[...]
        acc = _mlp_chunk(h, w1_ref, b1_ref, w2_ref, 0)
        y, mix = front()
        acc = acc + _mlp_chunk(h, w1_ref, b1_ref, w2_ref, 1)
        mix = mix + jnp.dot(y, wout_ref[RET_WIDTH:, :], preferred_element_type=_F32)
        back(None, mix)
        acc = acc + _mlp_chunk(h, w1_ref, b1_ref, w2_ref, 2)
        acc = acc + _mlp_chunk(h, w1_ref, b1_ref, w2_ref, 3)
        f = acc + b2_ref[...]
        g2 = modp_ref[0, 5:6, :]
        out_ref[0] = _norm_rows(ALPHA * x1_cur[...] + g2 * f) * l2g_ref[...] + l2b_ref[...]


def _mlp_chunk(h, w1_ref, b1_ref, w2_ref, j):
    cols = slice(j * MLP_CHUNK, (j + 1) * MLP_CHUNK)
    a = jnp.dot(h, w1_ref[:, cols], preferred_element_type=_F32) + b1_ref[:, cols]
    a = jnp.square(jnp.maximum(a, 0.0)).astype(_BF)
    return jnp.dot(a, w2_ref[cols, :], preferred_element_type=_F32)


def _post_mix_front(o_ref, yg_ref, u_ref, dsk_ref, wglu_ref, bglu_ref, wout_ref, y_scr, *, nb, tm, b):
    mix = jnp.dot(o_ref[0], wout_ref[0:RET_WIDTH, :], preferred_element_type=_F32)
    for j in range(S5_WIDTH // 128):
        for hf in range(2):
            for q in range(tm // 128):
                w = [yg_ref[8 * j + gg, hf, pl.ds(8 * q * nb + b, 8, stride=nb), :] for gg in range(8)]
                v = _granule_transpose(w)
                for a in range(8):
                    y_scr[j, pl.ds(128 * q + 8 * hf + a, 8, stride=S5_T), :] = v[a]
    y = jnp.concatenate([y_scr[j] for j in range(S5_WIDTH // 128)], axis=1)
    y = y + dsk_ref[...] * u_ref[0].astype(_F32)
    y = jax.nn.gelu(y)
    y = y * jax.nn.sigmoid(jnp.dot(y.astype(_BF), wglu_ref[...], preferred_element_type=_F32)
                           + bglu_ref[...])
    return y.astype(_BF), mix


def _post_mix_back(x_ref, mod_ref, wout_ref, l1g_ref, l1b_ref, h_out, x1_out, y, mix):
    if y is not None:
        mix = mix + jnp.dot(y, wout_ref[RET_WIDTH:, :], preferred_element_type=_F32)
    g1 = mod_ref[0, 2:3, :]
    sh2 = mod_ref[0, 3:4, :]
    sc2 = mod_ref[0, 4:5, :]
    x1 = _norm_rows(ALPHA * x_ref[0] + g1 * mix) * l1g_ref[...] + l1b_ref[...]
    h_out[...] = (_norm_rows(x1) * (1.0 + sc2) + sh2).astype(_BF)
    x1_out[...] = x1


def _post(x, o_ret, yg, proj, mod, mod_row, pw):
    b, l, _ = x.shape
    tm = min(TOKEN_TILE, l)
    rows = (tm // S5_T) * b
    n_steps = (l // tm) * b
    item_a = lambda i: jnp.minimum(i, n_steps - 1)
    item_b = lambda i: jnp.maximum(i - 1, 0)
    tok = lambda w, item, cb=0: pl.BlockSpec((1, tm, w), lambda i: (item(i) % b, item(i) // b, cb))
    modspec = lambda item: pl.BlockSpec((1, 6, D_MODEL), lambda i: (mod_row(item(i) % b), 0, 0))
    row = lambda n: _const_spec((1, n))
    return pl.pallas_call(
        functools.partial(_post_kernel, nb=b, tm=tm, n_steps=n_steps),
        grid=(n_steps + 1,),
        in_specs=[tok(D_MODEL, item_a), tok(RET_WIDTH, item_a),
                  pl.BlockSpec((S5_GROUPS, 2, rows, 128), lambda i: (0, 0, item_a(i) // b, 0)),
                  tok(S5_WIDTH, item_a, 4),
                  modspec(item_a), modspec(item_b),
                  row(S5_WIDTH), _const_spec((S5_WIDTH, S5_WIDTH)), row(S5_WIDTH),
                  _const_spec((D_MODEL, D_MODEL)), row(D_MODEL), row(D_MODEL),
                  _const_spec((D_MODEL, D_FF)), row(D_FF), _const_spec((D_FF, D_MODEL)), row(D_MODEL),
                  row(D_MODEL), row(D_MODEL)],
        out_specs=tok(D_MODEL, item_b),
        out_shape=jax.ShapeDtypeStruct((b, l, D_MODEL), _F32),
        scratch_shapes=[pltpu.VMEM((S5_WIDTH // 128, tm, 128), _F32),
                        pltpu.VMEM((tm, D_MODEL), _BF), pltpu.VMEM((tm, D_MODEL), _F32),
                        pltpu.VMEM((tm, D_MODEL), _BF), pltpu.VMEM((tm, D_MODEL), _F32)],
        compiler_params=pltpu.CompilerParams(
            dimension_semantics=("arbitrary",), vmem_limit_bytes=VMEM_LIMIT),
        name="post",
    )(x, o_ret, yg, proj, mod, mod, *pw)


def _layer_pass(x, mod, mod_row, ret_s0, s5_s0_re, s5_s0_im, w_in_bf, ret_decay, s5_ops, post_w):
    b, l, _ = x.shape
    nk = l // S5_T
    g = S5_GROUPS
    proj, xg = _projection(x, mod, mod_row, w_in_bf)
    o_ret, ret_state = _retention(proj, ret_decay, ret_s0)
    if s5_s0_re is None:
        s0p = jnp.zeros((g, b, 256), _F32)
    else:
        s0p = jnp.concatenate([s5_s0_re, s5_s0_im], axis=-1)
        s0p = s0p.transpose(2, 0, 1, 3).reshape(g, b, 256)
    yg, fin = _s5_scan(xg, *s5_ops, s0p, b, nk)
    fin = fin.reshape(g, b, 2, 2, S5_STATE).transpose(1, 2, 0, 3, 4)
    out = _post(x, o_ret, yg, proj, mod, mod_row, post_w)
    return out, ret_state, fin[:, :, :, 0, :], fin[:, :, :, 1, :]


def kernel(x_prompt, x_sample, state_ret, state_s5_re, state_s5_im, c, c_ctx, w_ada, b_ada, w_in,
           ret_decay, s5_a_re, s5_a_im, s5_log_dt, s5_b_re, s5_b_im, s5_c_re, s5_c_im, s5_d, w_glu,
           b_glu, w_out, ln1_g, ln1_b, w_ff1, b_ff1, w_ff2, b_ff2, ln2_g, ln2_b):
    depth = w_ada.shape[0]
    bs = x_sample.shape[0]
    y_p, y_s = x_prompt, x_sample
    rets, s5rs, s5is = [], [], []
    for layer in range(depth):
        cond8 = jnp.concatenate(
            [c_ctx[None, :], c, jnp.zeros((8 - 1 - bs, D_MODEL), _F32)], axis=0)
        mod = _modulation(cond8, w_ada[layer], b_ada[layer]).reshape(8, 6, D_MODEL)
        s5_ops = _s5_operators(s5_a_re[layer], s5_a_im[layer], s5_log_dt[layer], s5_b_re[layer],
                               s5_b_im[layer], s5_c_re[layer], s5_c_im[layer])
        w_in_bf = w_in[layer].astype(_BF)
        r2 = lambda a: a[layer].reshape(1, -1)
        post_w = (r2(s5_d), w_glu[layer].astype(_BF), r2(b_glu), w_out[layer].astype(_BF),
                  r2(ln1_g), r2(ln1_b), w_ff1[layer].astype(_BF), r2(b_ff1),
                  w_ff2[layer].astype(_BF), r2(b_ff2), r2(ln2_g), r2(ln2_b))
        y_p, r_st, s5_re, s5_im = _layer_pass(
            y_p, mod, lambda i: 0, None, None, None, w_in_bf, ret_decay[layer], s5_ops, post_w)
        rets.append(r_st)
        s5rs.append(s5_re)
        s5is.append(s5_im)
        y_s, _, _, _ = _layer_pass(
            y_s, mod, lambda i: i + 1, state_ret[:, layer], state_s5_re[:, layer],
            state_s5_im[:, layer], w_in_bf, ret_decay[layer], s5_ops, post_w)
    return (y_p, y_s, jnp.stack(rets, axis=1), jnp.stack(s5rs, axis=1), jnp.stack(s5is, axis=1))
```

```python
import functools

import jax
import jax.numpy as jnp
from jax import lax
from jax.experimental import pallas as pl
from jax.experimental.pallas import tpu as pltpu

D_MODEL = 1024
RET_HEADS = 4
RET_DK = 128
RET_WIDTH = RET_HEADS * RET_DK
S5_CH = 16
S5_GROUPS = 32
S5_STATE = 64
S5_WIDTH = S5_GROUPS * S5_CH
D_FF = 4 * D_MODEL
IN_COLS = 4 * RET_WIDTH + S5_WIDTH
ALPHA = 2.0 ** 0.25
LN_EPS = 1e-5

S5_T = 16
S5_ROW = S5_T * S5_CH
S5_PREP_GROUP_TILE = 4
S5_GROUP_TILE = 8
RET_CHUNK = 256
RET_UNIT_TOKENS = 1024
TOKEN_TILE = 512
MLP_CHUNK = 1024
VMEM_LIMIT = 56 * 1024 * 1024

_BF = jnp.bfloat16
_F32 = jnp.float32


def _norm_rows(x):
    mu = jnp.mean(x, axis=-1, keepdims=True)
    xc = x - mu
    var = jnp.mean(xc * xc, axis=-1, keepdims=True)
    return xc * lax.rsqrt(var + LN_EPS)


def _const_spec(shape):
    nd = len(shape)
    return pl.BlockSpec(shape, lambda *_: (0,) * nd, pipeline_mode=pl.Buffered(1))


def _granule_transpose(v):
    lane = lax.broadcasted_iota(jnp.int32, (1, 128), 1)
    v = list(v)
    for d in (4, 2, 1):
        bit = ((lane // S5_CH) & d) != 0
        nv = list(v)
        for a in range(8):
            if a & d:
                continue
            lo, hi = v[a], v[a + d]
            nv[a] = jnp.where(bit, pltpu.roll(hi, S5_CH * d, axis=1), lo)
            nv[a + d] = jnp.where(bit, hi, pltpu.roll(lo, 128 - S5_CH * d, axis=1))
        v = nv
    return v


def _tile_geometry(b, l):
    spt = max(1, TOKEN_TILE // l)
    return spt, b // spt, l * spt


def _octet_row_start(q, pb, *, nb, spt):
    per_seq = (TOKEN_TILE // spt) // 128
    return 8 * (q % per_seq) * nb + pb * spt + q // per_seq


def _mod_kernel(cond_ref, w_ref, b_ref, o_ref):
    c = cond_ref[...]
    s = (c * jax.nn.sigmoid(c)).astype(_BF)
    o_ref[...] = jnp.dot(s, w_ref[...].astype(_BF), preferred_element_type=_F32) + b_ref[...]


def _modulation(cond8, w_ada, b_ada):
    n = w_ada.shape[1]
    tn = 1024
    return pl.pallas_call(
        _mod_kernel,
        grid=(n // tn,),
        in_specs=[pl.BlockSpec((8, D_MODEL), lambda j: (0, 0)),
                  pl.BlockSpec((D_MODEL, tn), lambda j: (0, j)),
                  pl.BlockSpec((1, tn), lambda j: (0, j))],
        out_specs=pl.BlockSpec((8, tn), lambda j: (0, j)),
        out_shape=jax.ShapeDtypeStruct((8, n), _F32),
        compiler_params=pltpu.CompilerParams(dimension_semantics=("arbitrary",)),
        name="modulation",
    )(cond8, w_ada, b_ada.reshape(1, n))


def _proj_kernel(x_ref, mod_ref, w_ref, o_ref, xg_ref, u_scr, *, nb, spt):
    x = x_ref[0]
    sh = mod_ref[0, 0:1, :]
    sc = mod_ref[0, 1:2, :]
    h = (_norm_rows(x) * (1.0 + sc) + sh).astype(_BF)
    u = jnp.dot(h, w_ref[:, 4 * RET_WIDTH:], preferred_element_type=_F32)
    o_ref[0, :, 4 * RET_WIDTH:] = u.astype(_BF)
    for j in range(S5_WIDTH // 128):
        u_scr[j] = u[:, 128 * j:128 * (j + 1)]
    o_ref[0, :, 0:4 * RET_WIDTH] = jnp.dot(
        h, w_ref[:, 0:4 * RET_WIDTH], preferred_element_type=_F32).astype(_BF)
    pb = pl.program_id(1)
    for j in range(S5_WIDTH // 128):
        for hf in range(2):
            for q in range(TOKEN_TILE // 128):
                v = [u_scr[j, pl.ds(128 * q + 8 * hf + a, 8, stride=S5_T), :] for a in range(8)]
                w = _granule_transpose(v)
                start = _octet_row_start(q, pb, nb=nb, spt=spt)
                for gg in range(8):
                    xg_ref[8 * j + gg, hf, pl.ds(start, 8, stride=nb), :] = w[gg]


def _projection(x, mod, mod_row, w_in_bf):
    b, l, _ = x.shape
    tm = TOKEN_TILE
    spt, nbg, lt = _tile_geometry(b, l)
    rows = (tm // S5_T) * nbg
    proj, xg = pl.pallas_call(
        functools.partial(_proj_kernel, nb=b, spt=spt),
        grid=(lt // tm, nbg),
        in_specs=[pl.BlockSpec((1, tm, D_MODEL), lambda j, i: (i, j, 0)),
                  pl.BlockSpec((1, 6, D_MODEL), lambda j, i: (mod_row(i), 0, 0)),
                  _const_spec((D_MODEL, IN_COLS))],
        out_specs=[pl.BlockSpec((1, tm, IN_COLS), lambda j, i: (i, j, 0)),
                   pl.BlockSpec((S5_GROUPS, 2, rows, 128), lambda j, i: (0, 0, j, 0))],
        out_shape=[jax.ShapeDtypeStruct((nbg, lt, IN_COLS), _BF),
                   jax.ShapeDtypeStruct((S5_GROUPS, 2, (l // S5_T) * b, 128), _F32)],
        scratch_shapes=[pltpu.VMEM((S5_WIDTH // 128, tm, 128), _F32)],
        compiler_params=pltpu.CompilerParams(
            dimension_semantics=("arbitrary", "arbitrary"), vmem_limit_bytes=VMEM_LIMIT),
        name="projection",
    )(x.reshape(nbg, lt, D_MODEL), mod, w_in_bf)
    return proj.reshape(b, l, IN_COLS), xg


def _ret_kernel(*refs, chunk, n_chunks, nbb, has_s0):
    if has_s0:
        dec_ref, q_ref, k_ref, v_ref, g_ref, s0_ref, o_ref, st_ref, kv_scr = refs
    else:
        dec_ref, q_ref, k_ref, v_ref, g_ref, o_ref, st_ref, kv_scr = refs
        s0_ref = None
    c = chunk
    scale = RET_DK ** -0.5
    pos = lax.broadcasted_iota(jnp.int32, (c, 1), 0).astype(_F32)
    ri = lax.broadcasted_iota(jnp.int32, (c, c), 0)
    ci = lax.broadcasted_iota(jnp.int32, (c, c), 1)
    rel = (ri - ci).astype(_F32)

    for hd in range(RET_HEADS):
        lanes = slice(hd * RET_DK, (hd + 1) * RET_DK)

        def log_gamma(d):
            z = jnp.full((1, 1), dec_ref[d, hd], _F32)
            return jnp.minimum(z, 0.0) - jnp.log(1.0 + jnp.exp(-jnp.abs(z)))

        lg_f, lg_b = log_gamma(0), log_gamma(1)
        kdec_f = jnp.exp(lg_f * (c - 1.0 - pos)) * scale
        kdec_b = jnp.exp(lg_b * pos) * scale
        qdec_f = jnp.exp(lg_f * (pos + 1.0))
        qdec_b = jnp.exp(lg_b * (c - pos))
        dmat = (jnp.where(ri >= ci, jnp.exp(lg_f * jnp.maximum(rel, 0.0)), 0.0)
                + jnp.where(ci >= ri, jnp.exp(lg_b * jnp.maximum(-rel, 0.0)), 0.0)) * scale
        cdec_f = jnp.exp(lg_f * float(c))
        cdec_b = jnp.exp(lg_b * float(c))

        for bb in range(nbb):
            unit = bb * RET_HEADS + hd
            for n in range(n_chunks):
                kc = k_ref[bb, n * c:(n + 1) * c, lanes].astype(_F32)
                vc = v_ref[bb, n * c:(n + 1) * c, lanes]
                kcat = jnp.concatenate([(kc * kdec_f).astype(_BF), (kc * kdec_b).astype(_BF)], axis=1)
                kv_scr[unit, n] = lax.dot_general(kcat, vc, (((0,), (0,)), ((), ())),
                                                  preferred_element_type=_F32)
            if has_s0:
                carry_f = s0_ref[bb, 0, hd]
                carry_b = s0_ref[bb, 1, hd]
            else:
                carry_f = jnp.zeros((RET_DK, RET_DK), _F32)
                carry_b = jnp.zeros((RET_DK, RET_DK), _F32)
            for n in range(n_chunks):
                inc = kv_scr[unit, n, 0:RET_DK, :]
                kv_scr[unit, n, 0:RET_DK, :] = carry_f
                carry_f = cdec_f * carry_f + inc
            for n in range(n_chunks - 1, -1, -1):
                inc = kv_scr[unit, n, RET_DK:2 * RET_DK, :]
                kv_scr[unit, n, RET_DK:2 * RET_DK, :] = carry_b
                carry_b = cdec_b * carry_b + inc
            st_ref[bb, 0, hd] = carry_f
            st_ref[bb, 1, hd] = carry_b

            for n in range(n_chunks):
                sl = slice(n * c, (n + 1) * c)
                qc = q_ref[bb, sl, lanes]
                kc = k_ref[bb, sl, lanes]
                vc = v_ref[bb, sl, lanes]
                scores = lax.dot_general(qc, kc, (((1,), (1,)), ((), ())), preferred_element_type=_F32)
                o = jnp.dot((scores * dmat).astype(_BF), vc, preferred_element_type=_F32)
                qf = qc.astype(_F32)
                qcat = jnp.concatenate([(qf * qdec_f).astype(_BF), (qf * qdec_b).astype(_BF)], axis=1)
                o = o + jnp.dot(qcat, kv_scr[unit, n].astype(_BF), preferred_element_type=_F32)
                gate = g_ref[bb, sl, lanes].astype(_F32)
                o_ref[bb, sl, lanes] = (_norm_rows(o) * (gate * jax.nn.sigmoid(gate))).astype(_BF)


def _retention(proj, ret_decay, s0):
    b, l, _ = proj.shape
    c = min(RET_CHUNK, l)
    n_chunks = l // c
    nbb = max(1, min(b, RET_UNIT_TOKENS // l))
    has_s0 = s0 is not None
    col = lambda j: pl.BlockSpec((nbb, l, RET_WIDTH), lambda i: (i, 0, j))
    st_spec = pl.BlockSpec((nbb, 2, RET_HEADS, RET_DK, RET_DK), lambda i: (i, 0, 0, 0, 0))
    in_specs = [pl.BlockSpec(memory_space=pltpu.SMEM), col(0), col(1), col(2), col(3)]
    args = [ret_decay, proj, proj, proj, proj]
    if has_s0:
        in_specs.append(st_spec)
        args.append(s0)
    return pl.pallas_call(
        functools.partial(_ret_kernel, chunk=c, n_chunks=n_chunks, nbb=nbb, has_s0=has_s0),
        grid=(b // nbb,),
        in_specs=in_specs,
        out_specs=[pl.BlockSpec((nbb, l, RET_WIDTH), lambda i: (i, 0, 0)), st_spec],
        out_shape=[jax.ShapeDtypeStruct((b, l, RET_WIDTH), _BF),
                   jax.ShapeDtypeStruct((b, 2, RET_HEADS, RET_DK, RET_DK), _F32)],
        scratch_shapes=[pltpu.VMEM((nbb * RET_HEADS, n_chunks, 2 * RET_DK, RET_DK), _F32)],
        compiler_params=pltpu.CompilerParams(
            dimension_semantics=("arbitrary",), vmem_limit_bytes=VMEM_LIMIT),
        name="retention",
    )(*args)


def _swap_halves(z):
    return pltpu.roll(z, S5_STATE, axis=z.ndim - 1)


def _s5_prep_kernel(ar_ref, ai_ref, ldt_ref, bt_ref, cp_ref, m_ref, wst_ref, cout_ref, apow_ref, *, ng):
    lane = lax.broadcasted_iota(jnp.int32, (1, 2 * S5_STATE), 1)
    sgn = jnp.where(lane < S5_STATE, -1.0, 1.0).astype(_F32)
    gran = lax.broadcasted_iota(jnp.int32, (1, S5_ROW), 1) // S5_CH
    ar = ar_ref[...]
    ai = ai_ref[...]
    dt = jnp.exp(ldt_ref[...])
    mag = jnp.exp(ar * dt)
    ang = ai * dt
    pr1 = mag * jnp.cos(ang)
    pi1 = mag * jnp.sin(ang)
    pw_r = [jnp.ones_like(pr1), pr1]
    pw_i = [jnp.zeros_like(pi1), pi1]
    for _ in range(2, S5_T + 1):
        pr, pi = pw_r[-1], pw_i[-1]
        pw_r.append(pr * pr1 - pi * pi1)
        pw_i.append(pr * pi1 + pi * pr1)
    pw_is = [p * sgn for p in pw_i]
    x2 = pr1 - 1.0
    den = ar * ar + ai * ai
    coef_re = (x2 * ar + pi1 * ai) / den
    coef_im_s = ((pi1 * ar - x2 * ai) / den) * sgn
    for gi in range(ng):
        bt = bt_ref[gi]
        cp = cp_ref[gi]
        cc = cp * (-sgn)
        cps = _swap_halves(cp)
        bts = _swap_halves(bt)
        gens = []
        cout_rows = [[], []]
        for d in range(2):
            r = 2 * gi + d
            row = lambda a: a[r:r + 1, :]
            bb = row(coef_re) * bt + row(coef_im_s) * bts
            bbs = _swap_halves(bb)
            w_rows = []
            for m in range(S5_T):
                e = (S5_T - 1 - m) if d == 0 else m
                w = row(pw_r[e]) * bb + row(pw_is[e]) * bbs
                rows = slice(m * S5_CH, (m + 1) * S5_CH)
                wst_ref[gi, rows, d * 128:(d + 1) * 128] = w.astype(_BF)
                wst_ref[gi, rows, 256 + d * 128:256 + (d + 1) * 128] = _swap_halves(w).astype(_BF)
                w_rows.append(w)
            for i in range(S5_T):
                e = (i + 1) if d == 0 else (S5_T - i)
                gmat = row(pw_r[e]) * cp + row(pw_is[e]) * cps
                cout_rows[d].append(gmat * (-sgn))
            gens.append(lax.dot_general(cc, jnp.concatenate(w_rows, axis=0), (((1,), (1,)), ((), ())),
                                        preferred_element_type=_F32, precision=lax.Precision.HIGHEST))
            apow_ref[gi, 2 * d:2 * d + 1, :] = row(pw_r[S5_T])
            apow_ref[gi, 2 * d + 1:2 * d + 2, :] = row(pw_is[S5_T])
        blocks = []
        for t in range(S5_T):
            sf = (S5_CH * (t + 1)) % S5_ROW
            rf = pltpu.roll(gens[0], sf, axis=1) if sf else gens[0]
            rb = pltpu.roll(gens[1], S5_CH * t, axis=1) if t else gens[1]
            blocks.append(jnp.where(gran <= t, rf, 0.0) + jnp.where(gran >= t, rb, 0.0))
        m_ref[gi] = jnp.concatenate(blocks, axis=0).T.astype(_BF)
        cout_t = jnp.concatenate([jnp.concatenate(cout_rows[0], axis=0),
                                  jnp.concatenate(cout_rows[1], axis=0)], axis=1)
        cout_ref[gi] = cout_t.T.astype(_BF)


def _s5_operators(s5_a_re, s5_a_im, s5_log_dt, s5_b_re, s5_b_im, s5_c_re, s5_c_im):
    g = S5_GROUPS
    ng = S5_PREP_GROUP_TILE
    dup = lambda a: jnp.concatenate([a, a], axis=-1).transpose(1, 0, 2).reshape(2 * g, 128)
    ar2 = dup(s5_a_re)
    ai2 = dup(s5_a_im)
    ldt = s5_log_dt.T.reshape(2 * g, 1)
    btp = jnp.concatenate([s5_b_re.transpose(0, 2, 1), s5_b_im.transpose(0, 2, 1)], axis=-1)
    cpk = jnp.concatenate([s5_c_re, s5_c_im], axis=-1)
    gspec = lambda *shape: pl.BlockSpec((ng,) + shape, lambda i: (i,) + (0,) * len(shape))
    rspec = lambda w: pl.BlockSpec((2 * ng, w), lambda i: (i, 0))
    return pl.pallas_call(
        functools.partial(_s5_prep_kernel, ng=ng),
        grid=(g // ng,),
        in_specs=[rspec(128), rspec(128), rspec(1), gspec(S5_CH, 128), gspec(S5_CH, 128)],
        out_specs=[gspec(S5_ROW, S5_ROW), gspec(S5_ROW, 512), gspec(256, S5_ROW), gspec(4, 128)],
        out_shape=[jax.ShapeDtypeStruct((g, S5_ROW, S5_ROW), _BF),
                   jax.ShapeDtypeStruct((g, S5_ROW, 512), _BF),
                   jax.ShapeDtypeStruct((g, 256, S5_ROW), _BF),
                   jax.ShapeDtypeStruct((g, 4, 128), _F32)],
        compiler_params=pltpu.CompilerParams(dimension_semantics=("arbitrary",)),
        name="s5_operators",
    )(ar2, ai2, ldt, btp, cpk)


def _s5_kernel(x_ref, m_ref, wst_ref, cout_ref, apow_ref, s0_ref, y_ref, fin_ref, loc_scr, prev_scr,
               *, nb, nk, ng):
    load_x = lambda gi: jnp.concatenate([x_ref[gi, 0], x_ref[gi, 1]], axis=1).astype(_BF)
    for gi in range(ng):
        loc_scr[gi] = jnp.dot(load_x(gi), wst_ref[gi], preferred_element_type=_F32)
    coef, st = [], []
    for gi in range(ng):
        coef.append([apow_ref[gi, r:r + 1, :] for r in range(4)])
        zf = s0_ref[gi, :, 0:128]
        zb = s0_ref[gi, :, 128:256]
        st.append([zf, _swap_halves(zf), zb, _swap_halves(zb)])
    for k in range(nk):
        kb = nk - 1 - k
        rows_f = slice(k * nb, (k + 1) * nb)
        rows_b = slice(kb * nb, (kb + 1) * nb)
        for gi in range(ng):
            ar_f, ai_f, ar_b, ai_b = coef[gi]
            zf, zfs, zb, zbs = st[gi]
            prev_scr[gi, rows_f, 0:128] = zf
            prev_scr[gi, rows_b, 128:256] = zb
            st[gi] = [ar_f * zf + ai_f * zfs + loc_scr[gi, rows_f, 0:128],
                      ar_f * zfs - ai_f * zf + loc_scr[gi, rows_f, 256:384],
                      ar_b * zb + ai_b * zbs + loc_scr[gi, rows_b, 128:256],
                      ar_b * zbs - ai_b * zb + loc_scr[gi, rows_b, 384:512]]
    for gi in range(ng):
        fin_ref[gi, :, 0:128] = st[gi][0]
        fin_ref[gi, :, 128:256] = st[gi][2]
        y = jnp.dot(load_x(gi), m_ref[gi], preferred_element_type=_F32)
        y = y + jnp.dot(prev_scr[gi].astype(_BF), cout_ref[gi], preferred_element_type=_F32)
        y_ref[gi, 0] = y[:, 0:128]
        y_ref[gi, 1] = y[:, 128:256]


def _s5_scan(xg, m_op, wst, cout, apow, s0p, nb, nk):
    g = S5_GROUPS
    ng = S5_GROUP_TILE
    rows = nk * nb
    gspec = lambda *shape: pl.BlockSpec((ng,) + shape, lambda i: (i,) + (0,) * len(shape))
    return pl.pallas_call(
        functools.partial(_s5_kernel, nb=nb, nk=nk, ng=ng),
        grid=(g // ng,),
        in_specs=[gspec(2, rows, 128), gspec(S5_ROW, S5_ROW), gspec(S5_ROW, 512), gspec(256, S5_ROW),
                  gspec(4, 128), gspec(nb, 256)],
        out_specs=[gspec(2, rows, 128), gspec(nb, 256)],
        out_shape=[jax.ShapeDtypeStruct((g, 2, rows, 128), _F32),
                   jax.ShapeDtypeStruct((g, nb, 256), _F32)],
        scratch_shapes=[pltpu.VMEM((ng, rows, 512), _F32), pltpu.VMEM((ng, rows, 256), _F32)],
        compiler_params=pltpu.CompilerParams(
            dimension_semantics=("arbitrary",), vmem_limit_bytes=VMEM_LIMIT),
        name="s5_scan",
    )(xg, m_op, wst, cout, apow, s0p)


def _post_kernel(x_ref, o_ref, yg_ref, u_ref, mod_ref, modp_ref, dsk_ref, wglu_ref, bglu_ref, wout_ref,
                 l1g_ref, l1b_ref, w1_ref, b1_ref, w2_ref, b2_ref, l2g_ref, l2b_ref, out_ref,
                 y_scr, h_next, x1_next, h_cur, x1_cur, *, nb, spt, nbg, n_steps):
    i = pl.program_id(0)
    pb = jnp.minimum(i, n_steps - 1) % nbg
    front = functools.partial(_post_mix_front, o_ref, yg_ref, u_ref, dsk_ref, wglu_ref, bglu_ref,
                              wout_ref, y_scr, nb=nb, spt=spt, pb=pb)
    back = functools.partial(_post_mix_back, x_ref, mod_ref, wout_ref, l1g_ref, l1b_ref,
                             h_next, x1_next)

    @pl.when(i == 0)
    def _():
        back(*front())

    @pl.when(i > 0)
    def _():
        h_cur[...] = h_next[...]
        x1_cur[...] = x1_next[...]
        h = h_cur[...]
        acc = _mlp_chunk(h, w1_ref, b1_ref, w2_ref, 0)
        y, mix = front()
        acc = acc + _mlp_chunk(h, w1_ref, b1_ref, w2_ref, 1)
        mix = mix + jnp.dot(y, wout_ref[RET_WIDTH:, :], preferred_element_type=_F32)
        back(None, mix)
        acc = acc + _mlp_chunk(h, w1_ref, b1_ref, w2_ref, 2)
        acc = acc + _mlp_chunk(h, w1_ref, b1_ref, w2_ref, 3)
        f = acc + b2_ref[...]
        g2 = modp_ref[0, 5:6, :]
        out_ref[0] = _norm_rows(ALPHA * x1_cur[...] + g2 * f) * l2g_ref[...] + l2b_ref[...]


def _mlp_chunk(h, w1_ref, b1_ref, w2_ref, j):
    cols = slice(j * MLP_CHUNK, (j + 1) * MLP_CHUNK)
    a = jnp.dot(h, w1_ref[:, cols], preferred_element_type=_F32) + b1_ref[:, cols]
    a = jnp.square(jnp.maximum(a, 0.0)).astype(_BF)
    return jnp.dot(a, w2_ref[cols, :], preferred_element_type=_F32)


def _post_mix_front(o_ref, yg_ref, u_ref, dsk_ref, wglu_ref, bglu_ref, wout_ref, y_scr, *, nb, spt, pb):
    mix = jnp.dot(o_ref[0], wout_ref[0:RET_WIDTH, :], preferred_element_type=_F32)
    for j in range(S5_WIDTH // 128):
        for hf in range(2):
            for q in range(TOKEN_TILE // 128):
                start = _octet_row_start(q, pb, nb=nb, spt=spt)
                w = [yg_ref[8 * j + gg, hf, pl.ds(start, 8, stride=nb), :] for gg in range(8)]
                v = _granule_transpose(w)
                for a in range(8):
                    y_scr[j, pl.ds(128 * q + 8 * hf + a, 8, stride=S5_T), :] = v[a]
    y = jnp.concatenate([y_scr[j] for j in range(S5_WIDTH // 128)], axis=1)
    y = y + dsk_ref[...] * u_ref[0].astype(_F32)
    y = jax.nn.gelu(y)
    y = y * jax.nn.sigmoid(jnp.dot(y.astype(_BF), wglu_ref[...], preferred_element_type=_F32)
                           + bglu_ref[...])
    return y.astype(_BF), mix


def _post_mix_back(x_ref, mod_ref, wout_ref, l1g_ref, l1b_ref, h_out, x1_out, y, mix):
    if y is not None:
        mix = mix + jnp.dot(y, wout_ref[RET_WIDTH:, :], preferred_element_type=_F32)
    g1 = mod_ref[0, 2:3, :]
    sh2 = mod_ref[0, 3:4, :]
    sc2 = mod_ref[0, 4:5, :]
    x1 = _norm_rows(ALPHA * x_ref[0] + g1 * mix) * l1g_ref[...] + l1b_ref[...]
    h_out[...] = (_norm_rows(x1) * (1.0 + sc2) + sh2).astype(_BF)
    x1_out[...] = x1


def _post(x, o_ret, yg, proj, mod, mod_row, pw):
    b, l, _ = x.shape
    tm = TOKEN_TILE
    spt, nbg, lt = _tile_geometry(b, l)
    rows = (tm // S5_T) * nbg
    n_steps = (lt // tm) * nbg
    item_a = lambda i: jnp.minimum(i, n_steps - 1)
    item_b = lambda i: jnp.maximum(i - 1, 0)
    tok = lambda w, item, cb=0: pl.BlockSpec((1, tm, w), lambda i: (item(i) % nbg, item(i) // nbg, cb))
    modspec = lambda item: pl.BlockSpec((1, 6, D_MODEL), lambda i: (mod_row(item(i) % nbg), 0, 0))
    row = lambda n: _const_spec((1, n))
    return pl.pallas_call(
        functools.partial(_post_kernel, nb=b, spt=spt, nbg=nbg, n_steps=n_steps),
        grid=(n_steps + 1,),
        in_specs=[tok(D_MODEL, item_a), tok(RET_WIDTH, item_a),
                  pl.BlockSpec((S5_GROUPS, 2, rows, 128), lambda i: (0, 0, item_a(i) // nbg, 0)),
                  tok(S5_WIDTH, item_a, 4),
                  modspec(item_a), modspec(item_b),
                  row(S5_WIDTH), _const_spec((S5_WIDTH, S5_WIDTH)), row(S5_WIDTH),
                  _const_spec((D_MODEL, D_MODEL)), row(D_MODEL), row(D_MODEL),
                  _const_spec((D_MODEL, D_FF)), row(D_FF), _const_spec((D_FF, D_MODEL)), row(D_MODEL),
                  row(D_MODEL), row(D_MODEL)],
        out_specs=tok(D_MODEL, item_b),
        out_shape=jax.ShapeDtypeStruct((nbg, lt, D_MODEL), _F32),
        scratch_shapes=[pltpu.VMEM((S5_WIDTH // 128, tm, 128), _F32),
                        pltpu.VMEM((tm, D_MODEL), _BF), pltpu.VMEM((tm, D_MODEL), _F32),
                        pltpu.VMEM((tm, D_MODEL), _BF), pltpu.VMEM((tm, D_MODEL), _F32)],
        compiler_params=pltpu.CompilerParams(
            dimension_semantics=("arbitrary",), vmem_limit_bytes=VMEM_LIMIT),
        name="post",
    )(x.reshape(nbg, lt, D_MODEL), o_ret.reshape(nbg, lt, RET_WIDTH), yg,
      proj.reshape(nbg, lt, IN_COLS), mod, mod, *pw).reshape(b, l, D_MODEL)


def _layer_pass(x, mod, mod_row, ret_s0, s5_s0_re, s5_s0_im, w_in_bf, ret_decay, s5_ops, post_w):
    b, l, _ = x.shape
    nk = l // S5_T
    g = S5_GROUPS
    proj, xg = _projection(x, mod, mod_row, w_in_bf)
    o_ret, ret_state = _retention(proj, ret_decay, ret_s0)
    if s5_s0_re is None:
        s0p = jnp.zeros((g, b, 256), _F32)
    else:
        s0p = jnp.concatenate([s5_s0_re, s5_s0_im], axis=-1)
        s0p = s0p.transpose(2, 0, 1, 3).reshape(g, b, 256)
    yg, fin = _s5_scan(xg, *s5_ops, s0p, b, nk)
    fin = fin.reshape(g, b, 2, 2, S5_STATE).transpose(1, 2, 0, 3, 4)
    out = _post(x, o_ret, yg, proj, mod, mod_row, post_w)
    return out, ret_state, fin[:, :, :, 0, :], fin[:, :, :, 1, :]


def kernel(x_prompt, x_sample, state_ret, state_s5_re, state_s5_im, c, c_ctx, w_ada, b_ada, w_in,
           ret_decay, s5_a_re, s5_a_im, s5_log_dt, s5_b_re, s5_b_im, s5_c_re, s5_c_im, s5_d, w_glu,
           b_glu, w_out, ln1_g, ln1_b, w_ff1, b_ff1, w_ff2, b_ff2, ln2_g, ln2_b):
    depth = w_ada.shape[0]
    bs = x_sample.shape[0]
    y_p, y_s = x_prompt, x_sample
    rets, s5rs, s5is = [], [], []
    for layer in range(depth):
        cond8 = jnp.concatenate(
            [c_ctx[None, :], c, jnp.zeros((8 - 1 - bs, D_MODEL), _F32)], axis=0)
        mod = _modulation(cond8, w_ada[layer], b_ada[layer]).reshape(8, 6, D_MODEL)
        s5_ops = _s5_operators(s5_a_re[layer], s5_a_im[layer], s5_log_dt[layer], s5_b_re[layer],
                               s5_b_im[layer], s5_c_re[layer], s5_c_im[layer])
        w_in_bf = w_in[layer].astype(_BF)
        r2 = lambda a: a[layer].reshape(1, -1)
        post_w = (r2(s5_d), w_glu[layer].astype(_BF), r2(b_glu), w_out[layer].astype(_BF),
                  r2(ln1_g), r2(ln1_b), w_ff1[layer].astype(_BF), r2(b_ff1),
                  w_ff2[layer].astype(_BF), r2(b_ff2), r2(ln2_g), r2(ln2_b))
        y_p, r_st, s5_re, s5_im = _layer_pass(
            y_p, mod, lambda i: 0, None, None, None, w_in_bf, ret_decay[layer], s5_ops, post_w)
        rets.append(r_st)
        s5rs.append(s5_re)
        s5is.append(s5_im)
        y_s, _, _, _ = _layer_pass(
            y_s, mod, lambda i: i + 1, state_ret[:, layer], state_s5_re[:, layer],
            state_s5_im[:, layer], w_in_bf, ret_decay[layer], s5_ops, post_w)
    return (y_p, y_s, jnp.stack(rets, axis=1), jnp.stack(s5rs, axis=1), jnp.stack(s5is, axis=1))
```

```python
import functools

import jax
import jax.numpy as jnp
from jax import lax
from jax.experimental import pallas as pl
from jax.experimental.pallas import tpu as pltpu

D_MODEL = 1024
RET_HEADS = 4
RET_DK = 128
RET_WIDTH = RET_HEADS * RET_DK
S5_CH = 16
S5_GROUPS = 32
S5_STATE = 64
S5_WIDTH = S5_GROUPS * S5_CH
D_FF = 4 * D_MODEL
IN_COLS = 4 * RET_WIDTH + S5_WIDTH
ALPHA = 2.0 ** 0.25
LN_EPS = 1e-5

S5_T = 16
S5_ROW = S5_T * S5_CH
S5_PREP_GROUP_TILE = 4
S5_GROUP_TILE = 8
RET_CHUNK = 256
RET_UNIT_TOKENS = 1024
TOKEN_TILE = 512
MLP_CHUNK = 1024
VMEM_LIMIT = 56 * 1024 * 1024

_BF = jnp.bfloat16
_F32 = jnp.float32


def _norm_rows(x):
    mu = jnp.mean(x, axis=-1, keepdims=True)
    xc = x - mu
    var = jnp.mean(xc * xc, axis=-1, keepdims=True)
    return xc * lax.rsqrt(var + LN_EPS)


def _const_spec(shape):
    nd = len(shape)
    return pl.BlockSpec(shape, lambda *_: (0,) * nd, pipeline_mode=pl.Buffered(1))


def _granule_transpose(v):
    lane = lax.broadcasted_iota(jnp.int32, (1, 128), 1)
    v = list(v)
    for d in (4, 2, 1):
        bit = ((lane // S5_CH) & d) != 0
        nv = list(v)
        for a in range(8):
            if a & d:
                continue
            lo, hi = v[a], v[a + d]
            nv[a] = jnp.where(bit, pltpu.roll(hi, S5_CH * d, axis=1), lo)
            nv[a + d] = jnp.where(bit, hi, pltpu.roll(lo, 128 - S5_CH * d, axis=1))
        v = nv
    return v


def _tile_geometry(b, l):
    spt = max(1, TOKEN_TILE // l)
    return spt, b // spt, l * spt


def _octet_row_start(q, pb, *, nb, spt):
    per_seq = (TOKEN_TILE // spt) // 128
    return 8 * (q % per_seq) * nb + pb * spt + q // per_seq


def _mod_kernel(cond_ref, w_ref, b_ref, o_ref):
    c = cond_ref[...]
    s = (c * jax.nn.sigmoid(c)).astype(_BF)
    o_ref[...] = jnp.dot(s, w_ref[...].astype(_BF), preferred_element_type=_F32) + b_ref[...]


def _modulation(cond8, w_ada, b_ada):
    n = w_ada.shape[1]
    tn = 1024
    return pl.pallas_call(
        _mod_kernel,
        grid=(n // tn,),
        in_specs=[pl.BlockSpec((8, D_MODEL), lambda j: (0, 0)),
                  pl.BlockSpec((D_MODEL, tn), lambda j: (0, j)),
                  pl.BlockSpec((1, tn), lambda j: (0, j))],
        out_specs=pl.BlockSpec((8, tn), lambda j: (0, j)),
        out_shape=jax.ShapeDtypeStruct((8, n), _F32),
        compiler_params=pltpu.CompilerParams(dimension_semantics=("arbitrary",)),
        name="modulation",
    )(cond8, w_ada, b_ada.reshape(1, n))


def _proj_kernel(x_ref, mod_ref, w_ref, o_ref, xg_ref, u_scr, *, nb, spt):
    x = x_ref[0]
    sh = mod_ref[0, 0:1, :]
    sc = mod_ref[0, 1:2, :]
    h = (_norm_rows(x) * (1.0 + sc) + sh).astype(_BF)
    u = jnp.dot(h, w_ref[:, 4 * RET_WIDTH:], preferred_element_type=_F32)
    o_ref[0, :, 4 * RET_WIDTH:] = u.astype(_BF)
    for j in range(S5_WIDTH // 128):
        u_scr[j] = u[:, 128 * j:128 * (j + 1)]
    o_ref[0, :, 0:4 * RET_WIDTH] = jnp.dot(
        h, w_ref[:, 0:4 * RET_WIDTH], preferred_element_type=_F32).astype(_BF)
    pb = pl.program_id(1)
    for j in range(S5_WIDTH // 128):
        for hf in range(2):
            for q in range(TOKEN_TILE // 128):
                v = [u_scr[j, pl.ds(128 * q + 8 * hf + a, 8, stride=S5_T), :] for a in range(8)]
                w = _granule_transpose(v)
                start = _octet_row_start(q, pb, nb=nb, spt=spt)
                for gg in range(8):
                    xg_ref[8 * j + gg, hf, pl.ds(start, 8, stride=nb), :] = w[gg]


def _projection(x, mod, mod_row, w_in_bf):
    b, l, _ = x.shape
    tm = TOKEN_TILE
    spt, nbg, lt = _tile_geometry(b, l)
    rows = (tm // S5_T) * nbg
    proj, xg = pl.pallas_call(
        functools.partial(_proj_kernel, nb=b, spt=spt),
        grid=(lt // tm, nbg),
        in_specs=[pl.BlockSpec((1, tm, D_MODEL), lambda j, i: (i, j, 0)),
                  pl.BlockSpec((1, 6, D_MODEL), lambda j, i: (mod_row(i), 0, 0)),
                  _const_spec((D_MODEL, IN_COLS))],
        out_specs=[pl.BlockSpec((1, tm, IN_COLS), lambda j, i: (i, j, 0)),
                   pl.BlockSpec((S5_GROUPS, 2, rows, 128), lambda j, i: (0, 0, j, 0))],
        out_shape=[jax.ShapeDtypeStruct((nbg, lt, IN_COLS), _BF),
                   jax.ShapeDtypeStruct((S5_GROUPS, 2, (l // S5_T) * b, 128), _F32)],
        scratch_shapes=[pltpu.VMEM((S5_WIDTH // 128, tm, 128), _F32)],
        compiler_params=pltpu.CompilerParams(
            dimension_semantics=("arbitrary", "arbitrary"), vmem_limit_bytes=VMEM_LIMIT),
        name="projection",
    )(x.reshape(nbg, lt, D_MODEL), mod, w_in_bf)
    return proj.reshape(b, l, IN_COLS), xg


def _ret_kernel(*refs, chunk, n_chunks, nbb, has_s0):
    if has_s0:
        dec_ref, q_ref, k_ref, v_ref, g_ref, s0_ref, o_ref, st_ref, kv_scr = refs
    else:
        dec_ref, q_ref, k_ref, v_ref, g_ref, o_ref, st_ref, kv_scr = refs
        s0_ref = None
    c = chunk
    scale = RET_DK ** -0.5
    pos = lax.broadcasted_iota(jnp.int32, (c, 1), 0).astype(_F32)
    ri = lax.broadcasted_iota(jnp.int32, (c, c), 0)
    ci = lax.broadcasted_iota(jnp.int32, (c, c), 1)
    rel = (ri - ci).astype(_F32)

    for hd in range(RET_HEADS):
        lanes = slice(hd * RET_DK, (hd + 1) * RET_DK)

        def log_gamma(d):
            z = jnp.full((1, 1), dec_ref[d, hd], _F32)
            return jnp.minimum(z, 0.0) - jnp.log(1.0 + jnp.exp(-jnp.abs(z)))

        lg_f, lg_b = log_gamma(0), log_gamma(1)
        kdec_f = jnp.exp(lg_f * (c - 1.0 - pos)) * scale
        kdec_b = jnp.exp(lg_b * pos) * scale
        qdec_f = jnp.exp(lg_f * (pos + 1.0))
        qdec_b = jnp.exp(lg_b * (c - pos))
        dmat = (jnp.where(ri >= ci, jnp.exp(lg_f * jnp.maximum(rel, 0.0)), 0.0)
                + jnp.where(ci >= ri, jnp.exp(lg_b * jnp.maximum(-rel, 0.0)), 0.0)) * scale
        cdec_f = jnp.exp(lg_f * float(c))
        cdec_b = jnp.exp(lg_b * float(c))

        for bb in range(nbb):
            unit = bb * RET_HEADS + hd
            for n in range(n_chunks):
                kc = k_ref[bb, n * c:(n + 1) * c, lanes].astype(_F32)
                vc = v_ref[bb, n * c:(n + 1) * c, lanes]
                kcat = jnp.concatenate([(kc * kdec_f).astype(_BF), (kc * kdec_b).astype(_BF)], axis=1)
                kv_scr[unit, n] = lax.dot_general(kcat, vc, (((0,), (0,)), ((), ())),
                                                  preferred_element_type=_F32)
            if has_s0:
                carry_f = s0_ref[bb, 0, hd]
                carry_b = s0_ref[bb, 1, hd]
            else:
                carry_f = jnp.zeros((RET_DK, RET_DK), _F32)
                carry_b = jnp.zeros((RET_DK, RET_DK), _F32)
            for n in range(n_chunks):
                inc = kv_scr[unit, n, 0:RET_DK, :]
                kv_scr[unit, n, 0:RET_DK, :] = carry_f
                carry_f = cdec_f * carry_f + inc
            for n in range(n_chunks - 1, -1, -1):
                inc = kv_scr[unit, n, RET_DK:2 * RET_DK, :]
                kv_scr[unit, n, RET_DK:2 * RET_DK, :] = carry_b
                carry_b = cdec_b * carry_b + inc
            st_ref[bb, 0, hd] = carry_f
            st_ref[bb, 1, hd] = carry_b

            for n in range(n_chunks):
                sl = slice(n * c, (n + 1) * c)
                qc = q_ref[bb, sl, lanes]
                kc = k_ref[bb, sl, lanes]
                vc = v_ref[bb, sl, lanes]
                scores = lax.dot_general(qc, kc, (((1,), (1,)), ((), ())), preferred_element_type=_F32)
                o = jnp.dot((scores * dmat).astype(_BF), vc, preferred_element_type=_F32)
                qf = qc.astype(_F32)
                qcat = jnp.concatenate([(qf * qdec_f).astype(_BF), (qf * qdec_b).astype(_BF)], axis=1)
                o = o + jnp.dot(qcat, kv_scr[unit, n].astype(_BF), preferred_element_type=_F32)
                gate = g_ref[bb, sl, lanes].astype(_F32)
                o_ref[bb, sl, lanes] = (_norm_rows(o) * (gate * jax.nn.sigmoid(gate))).astype(_BF)


def _retention(proj, ret_decay, s0):
    b, l, _ = proj.shape
    c = min(RET_CHUNK, l)
    n_chunks = l // c
    nbb = max(1, min(b, RET_UNIT_TOKENS // l))
    has_s0 = s0 is not None
    col = lambda j: pl.BlockSpec((nbb, l, RET_WIDTH), lambda i: (i, 0, j))
    st_spec = pl.BlockSpec((nbb, 2, RET_HEADS, RET_DK, RET_DK), lambda i: (i, 0, 0, 0, 0))
    in_specs = [pl.BlockSpec(memory_space=pltpu.SMEM), col(0), col(1), col(2), col(3)]
    args = [ret_decay, proj, proj, proj, proj]
    if has_s0:
        in_specs.append(st_spec)
        args.append(s0)
    return pl.pallas_call(
        functools.partial(_ret_kernel, chunk=c, n_chunks=n_chunks, nbb=nbb, has_s0=has_s0),
        grid=(b // nbb,),
        in_specs=in_specs,
        out_specs=[pl.BlockSpec((nbb, l, RET_WIDTH), lambda i: (i, 0, 0)), st_spec],
        out_shape=[jax.ShapeDtypeStruct((b, l, RET_WIDTH), _BF),
                   jax.ShapeDtypeStruct((b, 2, RET_HEADS, RET_DK, RET_DK), _F32)],
        scratch_shapes=[pltpu.VMEM((nbb * RET_HEADS, n_chunks, 2 * RET_DK, RET_DK), _F32)],
        compiler_params=pltpu.CompilerParams(
            dimension_semantics=("arbitrary",), vmem_limit_bytes=VMEM_LIMIT),
        name="retention",
    )(*args)


def _swap_halves(z):
    return pltpu.roll(z, S5_STATE, axis=z.ndim - 1)


def _s5_prep_kernel(ar_ref, ai_ref, ldt_ref, bt_ref, cp_ref, m_ref, wst_ref, cout_ref, apow_ref, *, ng):
    lane = lax.broadcasted_iota(jnp.int32, (1, 2 * S5_STATE), 1)
    sgn = jnp.where(lane < S5_STATE, -1.0, 1.0).astype(_F32)
    gran = lax.broadcasted_iota(jnp.int32, (1, S5_ROW), 1) // S5_CH
    ar = ar_ref[...]
    ai = ai_ref[...]
    dt = jnp.exp(ldt_ref[...])
    mag = jnp.exp(ar * dt)
    ang = ai * dt
    pr1 = mag * jnp.cos(ang)
    pi1 = mag * jnp.sin(ang)
    pw_r = [jnp.ones_like(pr1), pr1]
    pw_i = [jnp.zeros_like(pi1), pi1]
    for _ in range(2, S5_T + 1):
        pr, pi = pw_r[-1], pw_i[-1]
        pw_r.append(pr * pr1 - pi * pi1)
        pw_i.append(pr * pi1 + pi * pr1)
    pw_is = [p * sgn for p in pw_i]
    x2 = pr1 - 1.0
    den = ar * ar + ai * ai
    coef_re = (x2 * ar + pi1 * ai) / den
    coef_im_s = ((pi1 * ar - x2 * ai) / den) * sgn
    for gi in range(ng):
        bt = bt_ref[gi]
        cp = cp_ref[gi]
        cc = cp * (-sgn)
        cps = _swap_halves(cp)
        bts = _swap_halves(bt)
        gens = []
        cout_rows = [[], []]
        for d in range(2):
            r = 2 * gi + d
            row = lambda a: a[r:r + 1, :]
            bb = row(coef_re) * bt + row(coef_im_s) * bts
            bbs = _swap_halves(bb)
            w_rows = []
            for m in range(S5_T):
                e = (S5_T - 1 - m) if d == 0 else m
                w = row(pw_r[e]) * bb + row(pw_is[e]) * bbs
                rows = slice(m * S5_CH, (m + 1) * S5_CH)
                wst_ref[gi, rows, d * 128:(d + 1) * 128] = w.astype(_BF)
                wst_ref[gi, rows, 256 + d * 128:256 + (d + 1) * 128] = _swap_halves(w).astype(_BF)
                w_rows.append(w)
            for i in range(S5_T):
                e = (i + 1) if d == 0 else (S5_T - i)
                gmat = row(pw_r[e]) * cp + row(pw_is[e]) * cps
                cout_rows[d].append(gmat * (-sgn))
            gens.append(lax.dot_general(cc, jnp.concatenate(w_rows, axis=0), (((1,), (1,)), ((), ())),
                                        preferred_element_type=_F32, precision=lax.Precision.HIGHEST))
            apow_ref[gi, 2 * d:2 * d + 1, :] = row(pw_r[S5_T])
            apow_ref[gi, 2 * d + 1:2 * d + 2, :] = row(pw_is[S5_T])
        blocks = []
        for t in range(S5_T):
            sf = (S5_CH * (t + 1)) % S5_ROW
            rf = pltpu.roll(gens[0], sf, axis=1) if sf else gens[0]
            rb = pltpu.roll(gens[1], S5_CH * t, axis=1) if t else gens[1]
            blocks.append(jnp.where(gran <= t, rf, 0.0) + jnp.where(gran >= t, rb, 0.0))
        m_ref[gi] = jnp.concatenate(blocks, axis=0).T.astype(_BF)
        cout_t = jnp.concatenate([jnp.concatenate(cout_rows[0], axis=0),
                                  jnp.concatenate(cout_rows[1], axis=0)], axis=1)
        cout_ref[gi] = cout_t.T.astype(_BF)


def _s5_operators(s5_a_re, s5_a_im, s5_log_dt, s5_b_re, s5_b_im, s5_c_re, s5_c_im):
    g = S5_GROUPS
    ng = S5_PREP_GROUP_TILE
    dup = lambda a: jnp.concatenate([a, a], axis=-1).transpose(1, 0, 2).reshape(2 * g, 128)
    ar2 = dup(s5_a_re)
    ai2 = dup(s5_a_im)
    ldt = s5_log_dt.T.reshape(2 * g, 1)
    btp = jnp.concatenate([s5_b_re.transpose(0, 2, 1), s5_b_im.transpose(0, 2, 1)], axis=-1)
    cpk = jnp.concatenate([s5_c_re, s5_c_im], axis=-1)
    gspec = lambda *shape: pl.BlockSpec((ng,) + shape, lambda i: (i,) + (0,) * len(shape))
    rspec = lambda w: pl.BlockSpec((2 * ng, w), lambda i: (i, 0))
    return pl.pallas_call(
        functools.partial(_s5_prep_kernel, ng=ng),
        grid=(g // ng,),
        in_specs=[rspec(128), rspec(128), rspec(1), gspec(S5_CH, 128), gspec(S5_CH, 128)],
        out_specs=[gspec(S5_ROW, S5_ROW), gspec(S5_ROW, 512), gspec(256, S5_ROW), gspec(4, 128)],
        out_shape=[jax.ShapeDtypeStruct((g, S5_ROW, S5_ROW), _BF),
                   jax.ShapeDtypeStruct((g, S5_ROW, 512), _BF),
                   jax.ShapeDtypeStruct((g, 256, S5_ROW), _BF),
                   jax.ShapeDtypeStruct((g, 4, 128), _F32)],
        compiler_params=pltpu.CompilerParams(dimension_semantics=("arbitrary",)),
        name="s5_operators",
    )(ar2, ai2, ldt, btp, cpk)


def _s5_kernel(x_ref, m_ref, wst_ref, cout_ref, apow_ref, s0_ref, y_ref, fin_ref, loc_scr, prev_scr,
               *, nb, nk, ng):
    load_x = lambda gi: jnp.concatenate([x_ref[gi, 0], x_ref[gi, 1]], axis=1).astype(_BF)
    for gi in range(ng):
        loc_scr[gi] = jnp.dot(load_x(gi), wst_ref[gi], preferred_element_type=_F32)
    coef, st = [], []
    for gi in range(ng):
        coef.append([apow_ref[gi, r:r + 1, :] for r in range(4)])
        zf = s0_ref[gi, :, 0:128]
        zb = s0_ref[gi, :, 128:256]
        st.append([zf, _swap_halves(zf), zb, _swap_halves(zb)])
    for k in range(nk):
        kb = nk - 1 - k
        rows_f = slice(k * nb, (k + 1) * nb)
        rows_b = slice(kb * nb, (kb + 1) * nb)
        for gi in range(ng):
            ar_f, ai_f, ar_b, ai_b = coef[gi]
            zf, zfs, zb, zbs = st[gi]
            prev_scr[gi, rows_f, 0:128] = zf
            prev_scr[gi, rows_b, 128:256] = zb
            st[gi] = [ar_f * zf + ai_f * zfs + loc_scr[gi, rows_f, 0:128],
                      ar_f * zfs - ai_f * zf + loc_scr[gi, rows_f, 256:384],
                      ar_b * zb + ai_b * zbs + loc_scr[gi, rows_b, 128:256],
                      ar_b * zbs - ai_b * zb + loc_scr[gi, rows_b, 384:512]]
    for gi in range(ng):
        fin_ref[gi, :, 0:128] = st[gi][0]
        fin_ref[gi, :, 128:256] = st[gi][2]
        y = jnp.dot(load_x(gi), m_ref[gi], preferred_element_type=_F32)
        y = y + jnp.dot(prev_scr[gi].astype(_BF), cout_ref[gi], preferred_element_type=_F32)
        y_ref[gi, 0] = y[:, 0:128]
        y_ref[gi, 1] = y[:, 128:256]


def _s5_scan(xg, m_op, wst, cout, apow, s0p, nb, nk):
    g = S5_GROUPS
    ng = S5_GROUP_TILE
    rows = nk * nb
    gspec = lambda *shape: pl.BlockSpec((ng,) + shape, lambda i: (i,) + (0,) * len(shape))
    return pl.pallas_call(
        functools.partial(_s5_kernel, nb=nb, nk=nk, ng=ng),
        grid=(g // ng,),
        in_specs=[gspec(2, rows, 128), gspec(S5_ROW, S5_ROW), gspec(S5_ROW, 512), gspec(256, S5_ROW),
                  gspec(4, 128), gspec(nb, 256)],
        out_specs=[gspec(2, rows, 128), gspec(nb, 256)],
        out_shape=[jax.ShapeDtypeStruct((g, 2, rows, 128), _F32),
                   jax.ShapeDtypeStruct((g, nb, 256), _F32)],
        scratch_shapes=[pltpu.VMEM((ng, rows, 512), _F32), pltpu.VMEM((ng, rows, 256), _F32)],
        compiler_params=pltpu.CompilerParams(
            dimension_semantics=("arbitrary",), vmem_limit_bytes=VMEM_LIMIT),
        name="s5_scan",
    )(xg, m_op, wst, cout, apow, s0p)


def _post_kernel(x_ref, o_ref, yg_ref, u_ref, mod_ref, modp_ref, dsk_ref, wglu_ref, bglu_ref, wout_ref,
                 l1g_ref, l1b_ref, w1_ref, b1_ref, w2_ref, b2_ref, l2g_ref, l2b_ref, out_ref,
                 y_scr, h_next, x1_next, h_cur, x1_cur, *, nb, spt, nbg, n_steps):
    i = pl.program_id(0)
    pb = jnp.minimum(i, n_steps - 1) % nbg
    front = functools.partial(_post_mix_front, o_ref, yg_ref, u_ref, dsk_ref, wglu_ref, bglu_ref,
                              wout_ref, y_scr, nb=nb, spt=spt, pb=pb)
    back = functools.partial(_post_mix_back, x_ref, mod_ref, wout_ref, l1g_ref, l1b_ref,
                             h_next, x1_next)

    @pl.when(i == 0)
    def _():
        back(*front())

    @pl.when(i > 0)
    def _():
        h_cur[...] = h_next[...]
        x1_cur[...] = x1_next[...]
        h = h_cur[...]
        acc = _mlp_chunk(h, w1_ref, b1_ref, w2_ref, 0)
        y, mix = front()
        acc = acc + _mlp_chunk(h, w1_ref, b1_ref, w2_ref, 1)
        mix = mix + jnp.dot(y, wout_ref[RET_WIDTH:, :], preferred_element_type=_F32)
        back(None, mix)
        acc = acc + _mlp_chunk(h, w1_ref, b1_ref, w2_ref, 2)
        acc = acc + _mlp_chunk(h, w1_ref, b1_ref, w2_ref, 3)
        f = acc + b2_ref[...]
        g2 = modp_ref[0, 5:6, :]
        out_ref[0] = _norm_rows(ALPHA * x1_cur[...] + g2 * f) * l2g_ref[...] + l2b_ref[...]


def _mlp_chunk(h, w1_ref, b1_ref, w2_ref, j):
    cols = slice(j * MLP_CHUNK, (j + 1) * MLP_CHUNK)
    a = jnp.dot(h, w1_ref[:, cols], preferred_element_type=_F32) + b1_ref[:, cols]
    a = jnp.square(jnp.maximum(a, 0.0)).astype(_BF)
    return jnp.dot(a, w2_ref[cols, :], preferred_element_type=_F32)


def _post_mix_front(o_ref, yg_ref, u_ref, dsk_ref, wglu_ref, bglu_ref, wout_ref, y_scr, *, nb, spt, pb):
    mix = jnp.dot(o_ref[0], wout_ref[0:RET_WIDTH, :], preferred_element_type=_F32)
    for j in range(S5_WIDTH // 128):
        for hf in range(2):
            for q in range(TOKEN_TILE // 128):
                start = _octet_row_start(q, pb, nb=nb, spt=spt)
                w = [yg_ref[8 * j + gg, hf, pl.ds(start, 8, stride=nb), :] for gg in range(8)]
                v = _granule_transpose(w)
                for a in range(8):
                    y_scr[j, pl.ds(128 * q + 8 * hf + a, 8, stride=S5_T), :] = v[a]
    y = jnp.concatenate([y_scr[j] for j in range(S5_WIDTH // 128)], axis=1)
    y = y + dsk_ref[...] * u_ref[0].astype(_F32)
    y = jax.nn.gelu(y)
    y = y * jax.nn.sigmoid(jnp.dot(y.astype(_BF), wglu_ref[...], preferred_element_type=_F32)
                           + bglu_ref[...])
    return y.astype(_BF), mix


def _post_mix_back(x_ref, mod_ref, wout_ref, l1g_ref, l1b_ref, h_out, x1_out, y, mix):
    if y is not None:
        mix = mix + jnp.dot(y, wout_ref[RET_WIDTH:, :], preferred_element_type=_F32)
    g1 = mod_ref[0, 2:3, :]
    sh2 = mod_ref[0, 3:4, :]
    sc2 = mod_ref[0, 4:5, :]
    x1 = _norm_rows(ALPHA * x_ref[0] + g1 * mix) * l1g_ref[...] + l1b_ref[...]
    h_out[...] = (_norm_rows(x1) * (1.0 + sc2) + sh2).astype(_BF)
    x1_out[...] = x1


def _post(x, o_ret, yg, proj, mod, mod_row, pw):
    b, l, _ = x.shape
    tm = TOKEN_TILE
    spt, nbg, lt = _tile_geometry(b, l)
    rows = (tm // S5_T) * nbg
    n_steps = (lt // tm) * nbg
    item_a = lambda i: jnp.minimum(i, n_steps - 1)
    item_b = lambda i: jnp.maximum(i - 1, 0)
    tok = lambda w, item, cb=0: pl.BlockSpec((1, tm, w), lambda i: (item(i) % nbg, item(i) // nbg, cb))
    modspec = lambda item: pl.BlockSpec((1, 6, D_MODEL), lambda i: (mod_row(item(i) % nbg), 0, 0))
    row = lambda n: _const_spec((1, n))
    return pl.pallas_call(
        functools.partial(_post_kernel, nb=b, spt=spt, nbg=nbg, n_steps=n_steps),
        grid=(n_steps + 1,),
        in_specs=[tok(D_MODEL, item_a), tok(RET_WIDTH, item_a),
                  (_const_spec((S5_GROUPS, 2, rows, 128)) if lt == tm else
                   pl.BlockSpec((S5_GROUPS, 2, rows, 128), lambda i: (0, 0, item_a(i) // nbg, 0))),
                  tok(S5_WIDTH, item_a, 4),
                  modspec(item_a), modspec(item_b),
                  row(S5_WIDTH), _const_spec((S5_WIDTH, S5_WIDTH)), row(S5_WIDTH),
                  _const_spec((D_MODEL, D_MODEL)), row(D_MODEL), row(D_MODEL),
                  _const_spec((D_MODEL, D_FF)), row(D_FF), _const_spec((D_FF, D_MODEL)), row(D_MODEL),
                  row(D_MODEL), row(D_MODEL)],
        out_specs=tok(D_MODEL, item_b),
        out_shape=jax.ShapeDtypeStruct((nbg, lt, D_MODEL), _F32),
        scratch_shapes=[pltpu.VMEM((S5_WIDTH // 128, tm, 128), _F32),
                        pltpu.VMEM((tm, D_MODEL), _BF), pltpu.VMEM((tm, D_MODEL), _F32),
                        pltpu.VMEM((tm, D_MODEL), _BF), pltpu.VMEM((tm, D_MODEL), _F32)],
        compiler_params=pltpu.CompilerParams(
            dimension_semantics=("arbitrary",), vmem_limit_bytes=VMEM_LIMIT),
        name="post",
    )(x.reshape(nbg, lt, D_MODEL), o_ret.reshape(nbg, lt, RET_WIDTH), yg,
      proj.reshape(nbg, lt, IN_COLS), mod, mod, *pw).reshape(b, l, D_MODEL)


def _layer_pass(x, mod, mod_row, ret_s0, s5_s0_re, s5_s0_im, w_in_bf, ret_decay, s5_ops, post_w):
    b, l, _ = x.shape
    nk = l // S5_T
    g = S5_GROUPS
    proj, xg = _projection(x, mod, mod_row, w_in_bf)
    o_ret, ret_state = _retention(proj, ret_decay, ret_s0)
    if s5_s0_re is None:
        s0p = jnp.zeros((g, b, 256), _F32)
    else:
        s0p = jnp.concatenate([s5_s0_re, s5_s0_im], axis=-1)
        s0p = s0p.transpose(2, 0, 1, 3).reshape(g, b, 256)
    yg, fin = _s5_scan(xg, *s5_ops, s0p, b, nk)
    fin = fin.reshape(g, b, 2, 2, S5_STATE).transpose(1, 2, 0, 3, 4)
    out = _post(x, o_ret, yg, proj, mod, mod_row, post_w)
    return out, ret_state, fin[:, :, :, 0, :], fin[:, :, :, 1, :]


def kernel(x_prompt, x_sample, state_ret, state_s5_re, state_s5_im, c, c_ctx, w_ada, b_ada, w_in,
           ret_decay, s5_a_re, s5_a_im, s5_log_dt, s5_b_re, s5_b_im, s5_c_re, s5_c_im, s5_d, w_glu,
           b_glu, w_out, ln1_g, ln1_b, w_ff1, b_ff1, w_ff2, b_ff2, ln2_g, ln2_b):
    depth = w_ada.shape[0]
    bs = x_sample.shape[0]
    y_p, y_s = x_prompt, x_sample
    rets, s5rs, s5is = [], [], []
    for layer in range(depth):
        cond8 = jnp.concatenate(
            [c_ctx[None, :], c, jnp.zeros((8 - 1 - bs, D_MODEL), _F32)], axis=0)
        mod = _modulation(cond8, w_ada[layer], b_ada[layer]).reshape(8, 6, D_MODEL)
        s5_ops = _s5_operators(s5_a_re[layer], s5_a_im[layer], s5_log_dt[layer], s5_b_re[layer],
                               s5_b_im[layer], s5_c_re[layer], s5_c_im[layer])
        w_in_bf = w_in[layer].astype(_BF)
        r2 = lambda a: a[layer].reshape(1, -1)
        post_w = (r2(s5_d), w_glu[layer].astype(_BF), r2(b_glu), w_out[layer].astype(_BF),
                  r2(ln1_g), r2(ln1_b), w_ff1[layer].astype(_BF), r2(b_ff1),
                  w_ff2[layer].astype(_BF), r2(b_ff2), r2(ln2_g), r2(ln2_b))
        y_p, r_st, s5_re, s5_im = _layer_pass(
            y_p, mod, lambda i: 0, None, None, None, w_in_bf, ret_decay[layer], s5_ops, post_w)
        rets.append(r_st)
        s5rs.append(s5_re)
        s5is.append(s5_im)
        y_s, _, _, _ = _layer_pass(
            y_s, mod, lambda i: i + 1, state_ret[:, layer], state_s5_re[:, layer],
            state_s5_im[:, layer], w_in_bf, ret_decay[layer], s5_ops, post_w)
    return (y_p, y_s, jnp.stack(rets, axis=1), jnp.stack(s5rs, axis=1), jnp.stack(s5is, axis=1))
```

```python
import functools

import jax
import jax.numpy as jnp
from jax import lax
from jax.experimental import pallas as pl
from jax.experimental.pallas import tpu as pltpu

D_MODEL = 1024
RET_HEADS = 4
RET_DK = 128
RET_WIDTH = RET_HEADS * RET_DK
S5_CH = 16
S5_GROUPS = 32
S5_STATE = 64
S5_WIDTH = S5_GROUPS * S5_CH
D_FF = 4 * D_MODEL
IN_COLS = 4 * RET_WIDTH + S5_WIDTH
ALPHA = 2.0 ** 0.25
LN_EPS = 1e-5

S5_T = 16
S5_ROW = S5_T * S5_CH
S5_PREP_GROUP_TILE = 4
S5_GROUP_TILE = 8
RET_CHUNK = 256
RET_UNIT_TOKENS = 1024
TOKEN_TILE = 512
MLP_CHUNK = 1024
VMEM_LIMIT = 56 * 1024 * 1024

_BF = jnp.bfloat16
_F32 = jnp.float32


def _norm_rows(x):
    mu = jnp.mean(x, axis=-1, keepdims=True)
    xc = x - mu
    var = jnp.mean(xc * xc, axis=-1, keepdims=True)
    return xc * lax.rsqrt(var + LN_EPS)


def _const_spec(shape):
    nd = len(shape)
    return pl.BlockSpec(shape, lambda *_: (0,) * nd, pipeline_mode=pl.Buffered(1))


def _granule_transpose(v):
    lane = lax.broadcasted_iota(jnp.int32, (1, 128), 1)
    v = list(v)
    for d in (4, 2, 1):
        bit = ((lane // S5_CH) & d) != 0
        nv = list(v)
        for a in range(8):
            if a & d:
                continue
            lo, hi = v[a], v[a + d]
            nv[a] = jnp.where(bit, pltpu.roll(hi, S5_CH * d, axis=1), lo)
            nv[a + d] = jnp.where(bit, hi, pltpu.roll(lo, 128 - S5_CH * d, axis=1))
        v = nv
    return v


def _tile_geometry(b, l, tm):
    spt = max(1, tm // l)
    return spt, b // spt, l * spt


def _octet_row_start(q, pb, *, nb, spt, tm):
    per_seq = (tm // spt) // 128
    return 8 * (q % per_seq) * nb + pb * spt + q // per_seq


def _mod_kernel(cond_ref, w_ref, b_ref, o_ref):
    c = cond_ref[...]
    s = (c * jax.nn.sigmoid(c)).astype(_BF)
    o_ref[...] = jnp.dot(s, w_ref[...].astype(_BF), preferred_element_type=_F32) + b_ref[...]


def _modulation(cond8, w_ada, b_ada):
    n = w_ada.shape[1]
    tn = 1024
    return pl.pallas_call(
        _mod_kernel,
        grid=(n // tn,),
        in_specs=[pl.BlockSpec((8, D_MODEL), lambda j: (0, 0)),
                  pl.BlockSpec((D_MODEL, tn), lambda j: (0, j)),
                  pl.BlockSpec((1, tn), lambda j: (0, j))],
        out_specs=pl.BlockSpec((8, tn), lambda j: (0, j)),
        out_shape=jax.ShapeDtypeStruct((8, n), _F32),
        compiler_params=pltpu.CompilerParams(dimension_semantics=("arbitrary",)),
        name="modulation",
    )(cond8, w_ada, b_ada.reshape(1, n))


def _proj_kernel(x_ref, mod_ref, w_ref, o_ref, xg_ref, u_scr, *, nb, spt, tm):
    x = x_ref[0]
    sh = mod_ref[0, 0:1, :]
    sc = mod_ref[0, 1:2, :]
    h = (_norm_rows(x) * (1.0 + sc) + sh).astype(_BF)
    u = jnp.dot(h, w_ref[:, 4 * RET_WIDTH:], preferred_element_type=_F32)
    o_ref[0, :, 4 * RET_WIDTH:] = u.astype(_BF)
    for j in range(S5_WIDTH // 128):
        u_scr[j] = u[:, 128 * j:128 * (j + 1)]
    o_ref[0, :, 0:4 * RET_WIDTH] = jnp.dot(
        h, w_ref[:, 0:4 * RET_WIDTH], preferred_element_type=_F32).astype(_BF)
    pb = pl.program_id(1)
    for j in range(S5_WIDTH // 128):
        for hf in range(2):
            for q in range(tm // 128):
                v = [u_scr[j, pl.ds(128 * q + 8 * hf + a, 8, stride=S5_T), :] for a in range(8)]
                w = _granule_transpose(v)
                start = _octet_row_start(q, pb, nb=nb, spt=spt, tm=tm)
                for gg in range(8):
                    xg_ref[8 * j + gg, hf, pl.ds(start, 8, stride=nb), :] = w[gg]


def _projection(x, mod, mod_row, w_in_bf):
    b, l, _ = x.shape
    tm = TOKEN_TILE
    spt, nbg, lt = _tile_geometry(b, l, tm)
    rows = (tm // S5_T) * nbg
    proj, xg = pl.pallas_call(
        functools.partial(_proj_kernel, nb=b, spt=spt, tm=tm),
        grid=(lt // tm, nbg),
        in_specs=[pl.BlockSpec((1, tm, D_MODEL), lambda j, i: (i, j, 0)),
                  pl.BlockSpec((1, 6, D_MODEL), lambda j, i: (mod_row(i), 0, 0)),
                  _const_spec((D_MODEL, IN_COLS))],
        out_specs=[pl.BlockSpec((1, tm, IN_COLS), lambda j, i: (i, j, 0)),
                   pl.BlockSpec((S5_GROUPS, 2, rows, 128), lambda j, i: (0, 0, j, 0))],
        out_shape=[jax.ShapeDtypeStruct((nbg, lt, IN_COLS), _BF),
                   jax.ShapeDtypeStruct((S5_GROUPS, 2, (l // S5_T) * b, 128), _F32)],
        scratch_shapes=[pltpu.VMEM((S5_WIDTH // 128, tm, 128), _F32)],
        compiler_params=pltpu.CompilerParams(
            dimension_semantics=("arbitrary", "arbitrary"), vmem_limit_bytes=VMEM_LIMIT),
        name="projection",
    )(x.reshape(nbg, lt, D_MODEL), mod, w_in_bf)
    return proj.reshape(b, l, IN_COLS), xg


def _ret_kernel(*refs, chunk, n_chunks, nbb, has_s0):
    if has_s0:
        dec_ref, q_ref, k_ref, v_ref, g_ref, s0_ref, o_ref, st_ref, kv_scr = refs
    else:
        dec_ref, q_ref, k_ref, v_ref, g_ref, o_ref, st_ref, kv_scr = refs
        s0_ref = None
    c = chunk
    scale = RET_DK ** -0.5
    pos = lax.broadcasted_iota(jnp.int32, (c, 1), 0).astype(_F32)
    ri = lax.broadcasted_iota(jnp.int32, (c, c), 0)
    ci = lax.broadcasted_iota(jnp.int32, (c, c), 1)
    rel = (ri - ci).astype(_F32)

    for hd in range(RET_HEADS):
        lanes = slice(hd * RET_DK, (hd + 1) * RET_DK)

        def log_gamma(d):
            z = jnp.full((1, 1), dec_ref[d, hd], _F32)
            return jnp.minimum(z, 0.0) - jnp.log(1.0 + jnp.exp(-jnp.abs(z)))

        lg_f, lg_b = log_gamma(0), log_gamma(1)
        kdec_f = jnp.exp(lg_f * (c - 1.0 - pos)) * scale
        kdec_b = jnp.exp(lg_b * pos) * scale
        qdec_f = jnp.exp(lg_f * (pos + 1.0))
        qdec_b = jnp.exp(lg_b * (c - pos))
        dmat = (jnp.where(ri >= ci, jnp.exp(lg_f * jnp.maximum(rel, 0.0)), 0.0)
                + jnp.where(ci >= ri, jnp.exp(lg_b * jnp.maximum(-rel, 0.0)), 0.0)) * scale
        cdec_f = jnp.exp(lg_f * float(c))
        cdec_b = jnp.exp(lg_b * float(c))

        for bb in range(nbb):
            unit = bb * RET_HEADS + hd
            for n in range(n_chunks):
                kc = k_ref[bb, n * c:(n + 1) * c, lanes].astype(_F32)
                vc = v_ref[bb, n * c:(n + 1) * c, lanes]
                kcat = jnp.concatenate([(kc * kdec_f).astype(_BF), (kc * kdec_b).astype(_BF)], axis=1)
                kv_scr[unit, n] = lax.dot_general(kcat, vc, (((0,), (0,)), ((), ())),
                                                  preferred_element_type=_F32)
            if has_s0:
                carry_f = s0_ref[bb, 0, hd]
                carry_b = s0_ref[bb, 1, hd]
            else:
                carry_f = jnp.zeros((RET_DK, RET_DK), _F32)
                carry_b = jnp.zeros((RET_DK, RET_DK), _F32)
            for n in range(n_chunks):
                inc = kv_scr[unit, n, 0:RET_DK, :]
                kv_scr[unit, n, 0:RET_DK, :] = carry_f
                carry_f = cdec_f * carry_f + inc
            for n in range(n_chunks - 1, -1, -1):
                inc = kv_scr[unit, n, RET_DK:2 * RET_DK, :]
                kv_scr[unit, n, RET_DK:2 * RET_DK, :] = carry_b
                carry_b = cdec_b * carry_b + inc
            st_ref[bb, 0, hd] = carry_f
            st_ref[bb, 1, hd] = carry_b

            for n in range(n_chunks):
                sl = slice(n * c, (n + 1) * c)
                qc = q_ref[bb, sl, lanes]
                kc = k_ref[bb, sl, lanes]
                vc = v_ref[bb, sl, lanes]
                scores = lax.dot_general(qc, kc, (((1,), (1,)), ((), ())), preferred_element_type=_F32)
                o = jnp.dot((scores * dmat).astype(_BF), vc, preferred_element_type=_F32)
                qf = qc.astype(_F32)
                qcat = jnp.concatenate([(qf * qdec_f).astype(_BF), (qf * qdec_b).astype(_BF)], axis=1)
                o = o + jnp.dot(qcat, kv_scr[unit, n].astype(_BF), preferred_element_type=_F32)
                gate = g_ref[bb, sl, lanes].astype(_F32)
                o_ref[bb, sl, lanes] = (_norm_rows(o) * (gate * jax.nn.sigmoid(gate))).astype(_BF)


def _retention(proj, ret_decay, s0):
    b, l, _ = proj.shape
    c = min(RET_CHUNK, l)
    n_chunks = l // c
    nbb = max(1, min(b, RET_UNIT_TOKENS // l))
    has_s0 = s0 is not None
    col = lambda j: pl.BlockSpec((nbb, l, RET_WIDTH), lambda i: (i, 0, j))
    st_spec = pl.BlockSpec((nbb, 2, RET_HEADS, RET_DK, RET_DK), lambda i: (i, 0, 0, 0, 0))
    in_specs = [pl.BlockSpec(memory_space=pltpu.SMEM), col(0), col(1), col(2), col(3)]
    args = [ret_decay, proj, proj, proj, proj]
    if has_s0:
        in_specs.append(st_spec)
        args.append(s0)
    return pl.pallas_call(
        functools.partial(_ret_kernel, chunk=c, n_chunks=n_chunks, nbb=nbb, has_s0=has_s0),
        grid=(b // nbb,),
        in_specs=in_specs,
        out_specs=[pl.BlockSpec((nbb, l, RET_WIDTH), lambda i: (i, 0, 0)), st_spec],
        out_shape=[jax.ShapeDtypeStruct((b, l, RET_WIDTH), _BF),
                   jax.ShapeDtypeStruct((b, 2, RET_HEADS, RET_DK, RET_DK), _F32)],
        scratch_shapes=[pltpu.VMEM((nbb * RET_HEADS, n_chunks, 2 * RET_DK, RET_DK), _F32)],
        compiler_params=pltpu.CompilerParams(
            dimension_semantics=("arbitrary",), vmem_limit_bytes=VMEM_LIMIT),
        name="retention",
    )(*args)


def _swap_halves(z):
    return pltpu.roll(z, S5_STATE, axis=z.ndim - 1)


def _s5_prep_kernel(ar_ref, ai_ref, ldt_ref, bt_ref, cp_ref, m_ref, wst_ref, cout_ref, apow_ref, *, ng):
    lane = lax.broadcasted_iota(jnp.int32, (1, 2 * S5_STATE), 1)
    sgn = jnp.where(lane < S5_STATE, -1.0, 1.0).astype(_F32)
    gran = lax.broadcasted_iota(jnp.int32, (1, S5_ROW), 1) // S5_CH
    ar = ar_ref[...]
    ai = ai_ref[...]
    dt = jnp.exp(ldt_ref[...])
    mag = jnp.exp(ar * dt)
    ang = ai * dt
    pr1 = mag * jnp.cos(ang)
    pi1 = mag * jnp.sin(ang)
    pw_r = [jnp.ones_like(pr1), pr1]
    pw_i = [jnp.zeros_like(pi1), pi1]
    for _ in range(2, S5_T + 1):
        pr, pi = pw_r[-1], pw_i[-1]
        pw_r.append(pr * pr1 - pi * pi1)
        pw_i.append(pr * pi1 + pi * pr1)
    pw_is = [p * sgn for p in pw_i]
    x2 = pr1 - 1.0
    den = ar * ar + ai * ai
    coef_re = (x2 * ar + pi1 * ai) / den
    coef_im_s = ((pi1 * ar - x2 * ai) / den) * sgn
    for gi in range(ng):
        bt = bt_ref[gi]
        cp = cp_ref[gi]
        cc = cp * (-sgn)
        cps = _swap_halves(cp)
        bts = _swap_halves(bt)
        gens = []
        cout_rows = [[], []]
        for d in range(2):
            r = 2 * gi + d
            row = lambda a: a[r:r + 1, :]
            bb = row(coef_re) * bt + row(coef_im_s) * bts
            bbs = _swap_halves(bb)
            w_rows = []
            for m in range(S5_T):
                e = (S5_T - 1 - m) if d == 0 else m
                w = row(pw_r[e]) * bb + row(pw_is[e]) * bbs
                rows = slice(m * S5_CH, (m + 1) * S5_CH)
                wst_ref[gi, rows, d * 128:(d + 1) * 128] = w.astype(_BF)
                wst_ref[gi, rows, 256 + d * 128:256 + (d + 1) * 128] = _swap_halves(w).astype(_BF)
                w_rows.append(w)
            for i in range(S5_T):
                e = (i + 1) if d == 0 else (S5_T - i)
                gmat = row(pw_r[e]) * cp + row(pw_is[e]) * cps
                cout_rows[d].append(gmat * (-sgn))
            gens.append(lax.dot_general(cc, jnp.concatenate(w_rows, axis=0), (((1,), (1,)), ((), ())),
                                        preferred_element_type=_F32, precision=lax.Precision.HIGHEST))
            apow_ref[gi, 2 * d:2 * d + 1, :] = row(pw_r[S5_T])
            apow_ref[gi, 2 * d + 1:2 * d + 2, :] = row(pw_is[S5_T])
        blocks = []
        for t in range(S5_T):
            sf = (S5_CH * (t + 1)) % S5_ROW
            rf = pltpu.roll(gens[0], sf, axis=1) if sf else gens[0]
            rb = pltpu.roll(gens[1], S5_CH * t, axis=1) if t else gens[1]
            blocks.append(jnp.where(gran <= t, rf, 0.0) + jnp.where(gran >= t, rb, 0.0))
        m_ref[gi] = jnp.concatenate(blocks, axis=0).T.astype(_BF)
        cout_t = jnp.concatenate([jnp.concatenate(cout_rows[0], axis=0),
                                  jnp.concatenate(cout_rows[1], axis=0)], axis=1)
        cout_ref[gi] = cout_t.T.astype(_BF)


def _s5_operators(s5_a_re, s5_a_im, s5_log_dt, s5_b_re, s5_b_im, s5_c_re, s5_c_im):
    g = S5_GROUPS
    ng = S5_PREP_GROUP_TILE
    dup = lambda a: jnp.concatenate([a, a], axis=-1).transpose(1, 0, 2).reshape(2 * g, 128)
    ar2 = dup(s5_a_re)
    ai2 = dup(s5_a_im)
    ldt = s5_log_dt.T.reshape(2 * g, 1)
    btp = jnp.concatenate([s5_b_re.transpose(0, 2, 1), s5_b_im.transpose(0, 2, 1)], axis=-1)
    cpk = jnp.concatenate([s5_c_re, s5_c_im], axis=-1)
    gspec = lambda *shape: pl.BlockSpec((ng,) + shape, lambda i: (i,) + (0,) * len(shape))
    rspec = lambda w: pl.BlockSpec((2 * ng, w), lambda i: (i, 0))
    return pl.pallas_call(
        functools.partial(_s5_prep_kernel, ng=ng),
        grid=(g // ng,),
        in_specs=[rspec(128), rspec(128), rspec(1), gspec(S5_CH, 128), gspec(S5_CH, 128)],
        out_specs=[gspec(S5_ROW, S5_ROW), gspec(S5_ROW, 512), gspec(256, S5_ROW), gspec(4, 128)],
        out_shape=[jax.ShapeDtypeStruct((g, S5_ROW, S5_ROW), _BF),
                   jax.ShapeDtypeStruct((g, S5_ROW, 512), _BF),
                   jax.ShapeDtypeStruct((g, 256, S5_ROW), _BF),
                   jax.ShapeDtypeStruct((g, 4, 128), _F32)],
        compiler_params=pltpu.CompilerParams(dimension_semantics=("arbitrary",)),
        name="s5_operators",
    )(ar2, ai2, ldt, btp, cpk)


def _s5_kernel(x_ref, m_ref, wst_ref, cout_ref, apow_ref, s0_ref, y_ref, fin_ref, loc_scr, prev_scr,
               *, nb, nk, ng):
    load_x = lambda gi: jnp.concatenate([x_ref[gi, 0], x_ref[gi, 1]], axis=1).astype(_BF)
    for gi in range(ng):
        loc_scr[gi] = jnp.dot(load_x(gi), wst_ref[gi], preferred_element_type=_F32)
    coef, st = [], []
    for gi in range(ng):
        coef.append([apow_ref[gi, r:r + 1, :] for r in range(4)])
        zf = s0_ref[gi, :, 0:128]
        zb = s0_ref[gi, :, 128:256]
        st.append([zf, _swap_halves(zf), zb, _swap_halves(zb)])
    for k in range(nk):
        kb = nk - 1 - k
        rows_f = slice(k * nb, (k + 1) * nb)
        rows_b = slice(kb * nb, (kb + 1) * nb)
        for gi in range(ng):
            ar_f, ai_f, ar_b, ai_b = coef[gi]
            zf, zfs, zb, zbs = st[gi]
            prev_scr[gi, rows_f, 0:128] = zf
            prev_scr[gi, rows_b, 128:256] = zb
            st[gi] = [ar_f * zf + ai_f * zfs + loc_scr[gi, rows_f, 0:128],
                      ar_f * zfs - ai_f * zf + loc_scr[gi, rows_f, 256:384],
                      ar_b * zb + ai_b * zbs + loc_scr[gi, rows_b, 128:256],
                      ar_b * zbs - ai_b * zb + loc_scr[gi, rows_b, 384:512]]
    for gi in range(ng):
        fin_ref[gi, :, 0:128] = st[gi][0]
        fin_ref[gi, :, 128:256] = st[gi][2]
        y = jnp.dot(load_x(gi), m_ref[gi], preferred_element_type=_F32)
        y = y + jnp.dot(prev_scr[gi].astype(_BF), cout_ref[gi], preferred_element_type=_F32)
        y_ref[gi, 0] = y[:, 0:128]
        y_ref[gi, 1] = y[:, 128:256]


def _s5_scan(xg, m_op, wst, cout, apow, s0p, nb, nk):
    g = S5_GROUPS
    ng = S5_GROUP_TILE
    rows = nk * nb
    gspec = lambda *shape: pl.BlockSpec((ng,) + shape, lambda i: (i,) + (0,) * len(shape))
    return pl.pallas_call(
        functools.partial(_s5_kernel, nb=nb, nk=nk, ng=ng),
        grid=(g // ng,),
        in_specs=[gspec(2, rows, 128), gspec(S5_ROW, S5_ROW), gspec(S5_ROW, 512), gspec(256, S5_ROW),
                  gspec(4, 128), gspec(nb, 256)],
        out_specs=[gspec(2, rows, 128), gspec(nb, 256)],
        out_shape=[jax.ShapeDtypeStruct((g, 2, rows, 128), _F32),
                   jax.ShapeDtypeStruct((g, nb, 256), _F32)],
        scratch_shapes=[pltpu.VMEM((ng, rows, 512), _F32), pltpu.VMEM((ng, rows, 256), _F32)],
        compiler_params=pltpu.CompilerParams(
            dimension_semantics=("arbitrary",), vmem_limit_bytes=VMEM_LIMIT),
        name="s5_scan",
    )(xg, m_op, wst, cout, apow, s0p)


def _post_kernel(x_ref, o_ref, yg_ref, u_ref, mod_ref, modp_ref, dsk_ref, wglu_ref, bglu_ref, wout_ref,
                 l1g_ref, l1b_ref, w1_ref, b1_ref, w2_ref, b2_ref, l2g_ref, l2b_ref, out_ref,
                 y_scr, h_next, x1_next, h_cur, x1_cur, *, nb, spt, tm, nbg, n_steps):
    i = pl.program_id(0)
    pb = jnp.minimum(i, n_steps - 1) % nbg
    front = functools.partial(_post_mix_front, o_ref, yg_ref, u_ref, dsk_ref, wglu_ref, bglu_ref,
                              wout_ref, y_scr, nb=nb, spt=spt, tm=tm, pb=pb)
    back = functools.partial(_post_mix_back, x_ref, mod_ref, wout_ref, l1g_ref, l1b_ref,
                             h_next, x1_next)

    @pl.when(i == 0)
    def _():
        back(*front())

    @pl.when(i > 0)
    def _():
        h_cur[...] = h_next[...]
        x1_cur[...] = x1_next[...]
        h = h_cur[...]
        up = functools.partial(_mlp_up, h, w1_ref, b1_ref)
        down = functools.partial(_mlp_down, w2_ref)
        a0 = up(0)
        a1 = up(1)
        acc = down(a0, 0)
        y, mix = front()
        a2 = up(2)
        acc = acc + down(a1, 1)
        mix = mix + jnp.dot(y, wout_ref[RET_WIDTH:, :], preferred_element_type=_F32)
        a3 = up(3)
        acc = acc + down(a2, 2)
        back(None, mix)
        acc = acc + down(a3, 3)
        f = acc + b2_ref[...]
        g2 = modp_ref[0, 5:6, :]
        out_ref[0] = _norm_rows(ALPHA * x1_cur[...] + g2 * f) * l2g_ref[...] + l2b_ref[...]


def _mlp_up(h, w1_ref, b1_ref, j):
    cols = slice(j * MLP_CHUNK, (j + 1) * MLP_CHUNK)
    a = jnp.dot(h, w1_ref[:, cols], preferred_element_type=_F32) + b1_ref[:, cols]
    return jnp.square(jnp.maximum(a, 0.0)).astype(_BF)


def _mlp_down(w2_ref, a, j):
    return jnp.dot(a, w2_ref[j * MLP_CHUNK:(j + 1) * MLP_CHUNK, :], preferred_element_type=_F32)


def _post_mix_front(o_ref, yg_ref, u_ref, dsk_ref, wglu_ref, bglu_ref, wout_ref, y_scr, *, nb, spt, tm, pb):
    mix = jnp.dot(o_ref[0], wout_ref[0:RET_WIDTH, :], preferred_element_type=_F32)
    for j in range(S5_WIDTH // 128):
        for hf in range(2):
            for q in range(tm // 128):
                start = _octet_row_start(q, pb, nb=nb, spt=spt, tm=tm)
                w = [yg_ref[8 * j + gg, hf, pl.ds(start, 8, stride=nb), :] for gg in range(8)]
                v = _granule_transpose(w)
                for a in range(8):
                    y_scr[j, pl.ds(128 * q + 8 * hf + a, 8, stride=S5_T), :] = v[a]
    y = jnp.concatenate([y_scr[j] for j in range(S5_WIDTH // 128)], axis=1)
    y = y + dsk_ref[...] * u_ref[0].astype(_F32)
    y = jax.nn.gelu(y)
    y = y * jax.nn.sigmoid(jnp.dot(y.astype(_BF), wglu_ref[...], preferred_element_type=_F32)
                           + bglu_ref[...])
    return y.astype(_BF), mix


def _post_mix_back(x_ref, mod_ref, wout_ref, l1g_ref, l1b_ref, h_out, x1_out, y, mix):
    if y is not None:
        mix = mix + jnp.dot(y, wout_ref[RET_WIDTH:, :], preferred_element_type=_F32)
    g1 = mod_ref[0, 2:3, :]
    sh2 = mod_ref[0, 3:4, :]
    sc2 = mod_ref[0, 4:5, :]
    x1 = _norm_rows(ALPHA * x_ref[0] + g1 * mix) * l1g_ref[...] + l1b_ref[...]
    h_out[...] = (_norm_rows(x1) * (1.0 + sc2) + sh2).astype(_BF)
    x1_out[...] = x1


def _post(x, o_ret, yg, proj, mod, mod_row, pw):
    b, l, _ = x.shape
    tm = min(TOKEN_TILE, l)
    spt, nbg, lt = _tile_geometry(b, l, tm)
    rows = (tm // S5_T) * nbg
    n_steps = (lt // tm) * nbg
    item_a = lambda i: jnp.minimum(i, n_steps - 1)
    item_b = lambda i: jnp.maximum(i - 1, 0)
    tok = lambda w, item, cb=0: pl.BlockSpec((1, tm, w), lambda i: (item(i) % nbg, item(i) // nbg, cb))
    modspec = lambda item: pl.BlockSpec((1, 6, D_MODEL), lambda i: (mod_row(item(i) % nbg), 0, 0))
    row = lambda n: _const_spec((1, n))
    return pl.pallas_call(
        functools.partial(_post_kernel, nb=b, spt=spt, tm=tm, nbg=nbg, n_steps=n_steps),
        grid=(n_steps + 1,),
        in_specs=[tok(D_MODEL, item_a), tok(RET_WIDTH, item_a),
                  (_const_spec((S5_GROUPS, 2, rows, 128)) if lt == tm else
                   pl.BlockSpec((S5_GROUPS, 2, rows, 128), lambda i: (0, 0, item_a(i) // nbg, 0))),
                  tok(S5_WIDTH, item_a, 4),
                  modspec(item_a), modspec(item_b),
                  row(S5_WIDTH), _const_spec((S5_WIDTH, S5_WIDTH)), row(S5_WIDTH),
                  _const_spec((D_MODEL, D_MODEL)), row(D_MODEL), row(D_MODEL),
                  _const_spec((D_MODEL, D_FF)), row(D_FF), _const_spec((D_FF, D_MODEL)), row(D_MODEL),
                  row(D_MODEL), row(D_MODEL)],
        out_specs=tok(D_MODEL, item_b),
        out_shape=jax.ShapeDtypeStruct((nbg, lt, D_MODEL), _F32),
        scratch_shapes=[pltpu.VMEM((S5_WIDTH // 128, tm, 128), _F32),
                        pltpu.VMEM((tm, D_MODEL), _BF), pltpu.VMEM((tm, D_MODEL), _F32),
                        pltpu.VMEM((tm, D_MODEL), _BF), pltpu.VMEM((tm, D_MODEL), _F32)],
        compiler_params=pltpu.CompilerParams(
            dimension_semantics=("arbitrary",), vmem_limit_bytes=VMEM_LIMIT),
        name="post",
    )(x.reshape(nbg, lt, D_MODEL), o_ret.reshape(nbg, lt, RET_WIDTH), yg,
      proj.reshape(nbg, lt, IN_COLS), mod, mod, *pw).reshape(b, l, D_MODEL)


def _layer_pass(x, mod, mod_row, ret_s0, s5_s0_re, s5_s0_im, w_in_bf, ret_decay, s5_ops, post_w):
    b, l, _ = x.shape
    nk = l // S5_T
    g = S5_GROUPS
    proj, xg = _projection(x, mod, mod_row, w_in_bf)
    o_ret, ret_state = _retention(proj, ret_decay, ret_s0)
    if s5_s0_re is None:
        s0p = jnp.zeros((g, b, 256), _F32)
    else:
        s0p = jnp.concatenate([s5_s0_re, s5_s0_im], axis=-1)
        s0p = s0p.transpose(2, 0, 1, 3).reshape(g, b, 256)
    yg, fin = _s5_scan(xg, *s5_ops, s0p, b, nk)
    fin = fin.reshape(g, b, 2, 2, S5_STATE).transpose(1, 2, 0, 3, 4)
    out = _post(x, o_ret, yg, proj, mod, mod_row, post_w)
    return out, ret_state, fin[:, :, :, 0, :], fin[:, :, :, 1, :]


def kernel(x_prompt, x_sample, state_ret, state_s5_re, state_s5_im, c, c_ctx, w_ada, b_ada, w_in,
           ret_decay, s5_a_re, s5_a_im, s5_log_dt, s5_b_re, s5_b_im, s5_c_re, s5_c_im, s5_d, w_glu,
           b_glu, w_out, ln1_g, ln1_b, w_ff1, b_ff1, w_ff2, b_ff2, ln2_g, ln2_b):
    depth = w_ada.shape[0]
    bs = x_sample.shape[0]
    y_p, y_s = x_prompt, x_sample
    rets, s5rs, s5is = [], [], []
    for layer in range(depth):
        cond8 = jnp.concatenate(
            [c_ctx[None, :], c, jnp.zeros((8 - 1 - bs, D_MODEL), _F32)], axis=0)
        mod = _modulation(cond8, w_ada[layer], b_ada[layer]).reshape(8, 6, D_MODEL)
        s5_ops = _s5_operators(s5_a_re[layer], s5_a_im[layer], s5_log_dt[layer], s5_b_re[layer],
                               s5_b_im[layer], s5_c_re[layer], s5_c_im[layer])
        w_in_bf = w_in[layer].astype(_BF)
        r2 = lambda a: a[layer].reshape(1, -1)
        post_w = (r2(s5_d), w_glu[layer].astype(_BF), r2(b_glu), w_out[layer].astype(_BF),
                  r2(ln1_g), r2(ln1_b), w_ff1[layer].astype(_BF), r2(b_ff1),
                  w_ff2[layer].astype(_BF), r2(b_ff2), r2(ln2_g), r2(ln2_b))
        y_p, r_st, s5_re, s5_im = _layer_pass(
            y_p, mod, lambda i: 0, None, None, None, w_in_bf, ret_decay[layer], s5_ops, post_w)
        rets.append(r_st)
        s5rs.append(s5_re)
        s5is.append(s5_im)
        y_s, _, _, _ = _layer_pass(
            y_s, mod, lambda i: i + 1, state_ret[:, layer], state_s5_re[:, layer],
            state_s5_im[:, layer], w_in_bf, ret_decay[layer], s5_ops, post_w)
    return (y_p, y_s, jnp.stack(rets, axis=1), jnp.stack(s5rs, axis=1), jnp.stack(s5is, axis=1))
```

```python
import functools

import jax
import jax.numpy as jnp
from jax import lax
from jax.experimental import pallas as pl
from jax.experimental.pallas import tpu as pltpu

D_MODEL = 1024
RET_HEADS = 4
RET_DK = 128
RET_WIDTH = RET_HEADS * RET_DK
S5_CH = 16
S5_GROUPS = 32
S5_STATE = 64
S5_WIDTH = S5_GROUPS * S5_CH
D_FF = 4 * D_MODEL
IN_COLS = 4 * RET_WIDTH + S5_WIDTH
ALPHA = 2.0 ** 0.25
LN_EPS = 1e-5

S5_T = 16
S5_ROW = S5_T * S5_CH
S5_PREP_GROUP_TILE = 4
S5_GROUP_TILE = 8
RET_CHUNK = 256
RET_UNIT_TOKENS = 1024
TOKEN_TILE = 512
MLP_CHUNK = 1024
VMEM_LIMIT = 56 * 1024 * 1024

_BF = jnp.bfloat16
_F32 = jnp.float32


def _norm_rows(x):
    mu = jnp.mean(x, axis=-1, keepdims=True)
    xc = x - mu
    var = jnp.mean(xc * xc, axis=-1, keepdims=True)
    return xc * lax.rsqrt(var + LN_EPS)


def _const_spec(shape):
    nd = len(shape)
    return pl.BlockSpec(shape, lambda *_: (0,) * nd, pipeline_mode=pl.Buffered(1))


def _granule_transpose(v):
    lane = lax.broadcasted_iota(jnp.int32, (1, 128), 1)
    v = list(v)
    for d in (4, 2, 1):
        bit = ((lane // S5_CH) & d) != 0
        nv = list(v)
        for a in range(8):
            if a & d:
                continue
            lo, hi = v[a], v[a + d]
            nv[a] = jnp.where(bit, pltpu.roll(hi, S5_CH * d, axis=1), lo)
            nv[a + d] = jnp.where(bit, hi, pltpu.roll(lo, 128 - S5_CH * d, axis=1))
        v = nv
    return v


def _tile_geometry(b, l, tm):
    spt = max(1, tm // l)
    return spt, b // spt, l * spt


def _octet_row_start(q, pb, *, nb, spt, tm):
    per_seq = (tm // spt) // 128
    return 8 * (q % per_seq) * nb + pb * spt + q // per_seq


def _mod_kernel(cond_ref, w_ref, b_ref, o_ref):
    c = cond_ref[...]
    s = (c * jax.nn.sigmoid(c)).astype(_BF)
    o_ref[...] = jnp.dot(s, w_ref[...].astype(_BF), preferred_element_type=_F32) + b_ref[...]


def _modulation(cond8, w_ada, b_ada):
    n = w_ada.shape[1]
    tn = 1024
    return pl.pallas_call(
        _mod_kernel,
        grid=(n // tn,),
        in_specs=[pl.BlockSpec((8, D_MODEL), lambda j: (0, 0)),
                  pl.BlockSpec((D_MODEL, tn), lambda j: (0, j)),
                  pl.BlockSpec((1, tn), lambda j: (0, j))],
        out_specs=pl.BlockSpec((8, tn), lambda j: (0, j)),
        out_shape=jax.ShapeDtypeStruct((8, n), _F32),
        compiler_params=pltpu.CompilerParams(dimension_semantics=("arbitrary",)),
        name="modulation",
    )(cond8, w_ada, b_ada.reshape(1, n))


def _proj_kernel(x_ref, mod_ref, w_ref, o_ref, xg_ref, u_scr, *, nb, spt, tm):
    x = x_ref[0]
    sh = mod_ref[0, 0:1, :]
    sc = mod_ref[0, 1:2, :]
    h = (_norm_rows(x) * (1.0 + sc) + sh).astype(_BF)
    u = jnp.dot(h, w_ref[:, 4 * RET_WIDTH:], preferred_element_type=_F32)
    o_ref[0, :, 4 * RET_WIDTH:] = u.astype(_BF)
    for j in range(S5_WIDTH // 128):
        u_scr[j] = u[:, 128 * j:128 * (j + 1)]
    o_ref[0, :, 0:4 * RET_WIDTH] = jnp.dot(
        h, w_ref[:, 0:4 * RET_WIDTH], preferred_element_type=_F32).astype(_BF)
    pb = pl.program_id(1)
    for j in range(S5_WIDTH // 128):
        for hf in range(2):
            for q in range(tm // 128):
                v = [u_scr[j, pl.ds(128 * q + 8 * hf + a, 8, stride=S5_T), :] for a in range(8)]
                w = _granule_transpose(v)
                start = _octet_row_start(q, pb, nb=nb, spt=spt, tm=tm)
                for gg in range(8):
                    xg_ref[8 * j + gg, hf, pl.ds(start, 8, stride=nb), :] = w[gg]


def _projection(x, mod, mod_row, w_in_bf):
    b, l, _ = x.shape
    tm = TOKEN_TILE
    spt, nbg, lt = _tile_geometry(b, l, tm)
    rows = (tm // S5_T) * nbg
    proj, xg = pl.pallas_call(
        functools.partial(_proj_kernel, nb=b, spt=spt, tm=tm),
        grid=(lt // tm, nbg),
        in_specs=[pl.BlockSpec((1, tm, D_MODEL), lambda j, i: (i, j, 0)),
                  pl.BlockSpec((1, 6, D_MODEL), lambda j, i: (mod_row(i), 0, 0)),
                  _const_spec((D_MODEL, IN_COLS))],
        out_specs=[pl.BlockSpec((1, tm, IN_COLS), lambda j, i: (i, j, 0)),
                   pl.BlockSpec((S5_GROUPS, 2, rows, 128), lambda j, i: (0, 0, j, 0))],
        out_shape=[jax.ShapeDtypeStruct((nbg, lt, IN_COLS), _BF),
                   jax.ShapeDtypeStruct((S5_GROUPS, 2, (l // S5_T) * b, 128), _F32)],
        scratch_shapes=[pltpu.VMEM((S5_WIDTH // 128, tm, 128), _F32)],
        compiler_params=pltpu.CompilerParams(
            dimension_semantics=("arbitrary", "arbitrary"), vmem_limit_bytes=VMEM_LIMIT),
        name="projection",
    )(x.reshape(nbg, lt, D_MODEL), mod, w_in_bf)
    return proj.reshape(b, l, IN_COLS), xg


def _ret_kernel(*refs, chunk, n_chunks, nbb, has_s0):
    if has_s0:
        dec_ref, q_ref, k_ref, v_ref, g_ref, s0_ref, o_ref, st_ref, kv_scr, dec_scr = refs
    else:
        dec_ref, q_ref, k_ref, v_ref, g_ref, o_ref, st_ref, kv_scr, dec_scr = refs
        s0_ref = None
    c = chunk
    scale = RET_DK ** -0.5
    pos = lax.broadcasted_iota(jnp.int32, (c, 1), 0).astype(_F32)
    ri = lax.broadcasted_iota(jnp.int32, (c, c), 0)
    ci = lax.broadcasted_iota(jnp.int32, (c, c), 1)
    rel = (ri - ci).astype(_F32)

    for hd in range(RET_HEADS):
        lanes = slice(hd * RET_DK, (hd + 1) * RET_DK)

        def log_gamma(d):
            z = jnp.full((1, 1), dec_ref[d, hd], _F32)
            return jnp.minimum(z, 0.0) - jnp.log(1.0 + jnp.exp(-jnp.abs(z)))

        lg_f, lg_b = log_gamma(0), log_gamma(1)
        wide = lambda col: jnp.broadcast_to(col, (c, RET_DK))
        dec_scr[hd, 0] = wide(jnp.exp(lg_f * (c - 1.0 - pos)) * scale)
        dec_scr[hd, 1] = wide(jnp.exp(lg_b * pos) * scale)
        dec_scr[hd, 2] = wide(jnp.exp(lg_f * (pos + 1.0)))
        dec_scr[hd, 3] = wide(jnp.exp(lg_b * (c - pos)))
        dmat = (jnp.where(ri >= ci, jnp.exp(lg_f * jnp.maximum(rel, 0.0)), 0.0)
                + jnp.where(ci >= ri, jnp.exp(lg_b * jnp.maximum(-rel, 0.0)), 0.0)) * scale
        cdec_f = jnp.exp(lg_f * float(c))
        cdec_b = jnp.exp(lg_b * float(c))

        for bb in range(nbb):
            unit = bb * RET_HEADS + hd
            for n in range(n_chunks):
                kc = k_ref[bb, n * c:(n + 1) * c, lanes]
                vc = v_ref[bb, n * c:(n + 1) * c, lanes].astype(_F32)
                vcat = jnp.concatenate([(vc * dec_scr[hd, 0]).astype(_BF), (vc * dec_scr[hd, 1]).astype(_BF)], axis=1)
                kv_scr[unit, n] = lax.dot_general(kc, vcat, (((0,), (0,)), ((), ())),
                                                  preferred_element_type=_F32)
            if has_s0:
                carry_f = s0_ref[bb, 0, hd]
                carry_b = s0_ref[bb, 1, hd]
            else:
                carry_f = jnp.zeros((RET_DK, RET_DK), _F32)
                carry_b = jnp.zeros((RET_DK, RET_DK), _F32)
            for n in range(n_chunks):
                inc = kv_scr[unit, n, :, 0:RET_DK]
                kv_scr[unit, n, :, 0:RET_DK] = carry_f
                carry_f = cdec_f * carry_f + inc
            for n in range(n_chunks - 1, -1, -1):
                inc = kv_scr[unit, n, :, RET_DK:2 * RET_DK]
                kv_scr[unit, n, :, RET_DK:2 * RET_DK] = carry_b
                carry_b = cdec_b * carry_b + inc
            st_ref[bb, 0, hd] = carry_f
            st_ref[bb, 1, hd] = carry_b

            for n in range(n_chunks):
                sl = slice(n * c, (n + 1) * c)
                qc = q_ref[bb, sl, lanes]
                kc = k_ref[bb, sl, lanes]
                vc = v_ref[bb, sl, lanes]
                scores = lax.dot_general(qc, kc, (((1,), (1,)), ((), ())), preferred_element_type=_F32)
                o = jnp.dot((scores * dmat).astype(_BF), vc, preferred_element_type=_F32)
                oi = jnp.dot(qc, kv_scr[unit, n].astype(_BF), preferred_element_type=_F32)
                o = o + oi[:, 0:RET_DK] * dec_scr[hd, 2] + oi[:, RET_DK:2 * RET_DK] * dec_scr[hd, 3]
                gate = g_ref[bb, sl, lanes].astype(_F32)
                o_ref[bb, sl, lanes] = (_norm_rows(o) * (gate * jax.nn.sigmoid(gate))).astype(_BF)


def _retention(proj, ret_decay, s0):
    b, l, _ = proj.shape
    c = min(RET_CHUNK, l)
    n_chunks = l // c
    nbb = max(1, min(b, RET_UNIT_TOKENS // l))
    has_s0 = s0 is not None
    col = lambda j: pl.BlockSpec((nbb, l, RET_WIDTH), lambda i: (i, 0, j))
    st_spec = pl.BlockSpec((nbb, 2, RET_HEADS, RET_DK, RET_DK), lambda i: (i, 0, 0, 0, 0))
    in_specs = [pl.BlockSpec(memory_space=pltpu.SMEM), col(0), col(1), col(2), col(3)]
    args = [ret_decay, proj, proj, proj, proj]
    if has_s0:
        in_specs.append(st_spec)
        args.append(s0)
    return pl.pallas_call(
        functools.partial(_ret_kernel, chunk=c, n_chunks=n_chunks, nbb=nbb, has_s0=has_s0),
        grid=(b // nbb,),
        in_specs=in_specs,
        out_specs=[pl.BlockSpec((nbb, l, RET_WIDTH), lambda i: (i, 0, 0)), st_spec],
        out_shape=[jax.ShapeDtypeStruct((b, l, RET_WIDTH), _BF),
                   jax.ShapeDtypeStruct((b, 2, RET_HEADS, RET_DK, RET_DK), _F32)],
        scratch_shapes=[pltpu.VMEM((nbb * RET_HEADS, n_chunks, RET_DK, 2 * RET_DK), _F32),
                        pltpu.VMEM((RET_HEADS, 4, c, RET_DK), _F32)],
        compiler_params=pltpu.CompilerParams(
            dimension_semantics=("arbitrary",), vmem_limit_bytes=VMEM_LIMIT),
        name="retention",
    )(*args)


def _swap_halves(z):
    return pltpu.roll(z, S5_STATE, axis=z.ndim - 1)


def _s5_prep_kernel(ar_ref, ai_ref, ldt_ref, bt_ref, cp_ref, m_ref, wst_ref, cout_ref, apow_ref, *, ng):
    lane = lax.broadcasted_iota(jnp.int32, (1, 2 * S5_STATE), 1)
    sgn = jnp.where(lane < S5_STATE, -1.0, 1.0).astype(_F32)
    gran = lax.broadcasted_iota(jnp.int32, (1, S5_ROW), 1) // S5_CH
    ar = ar_ref[...]
    ai = ai_ref[...]
    dt = jnp.exp(ldt_ref[...])
    mag = jnp.exp(ar * dt)
    ang = ai * dt
    pr1 = mag * jnp.cos(ang)
    pi1 = mag * jnp.sin(ang)
    pw_r = [jnp.ones_like(pr1), pr1]
    pw_i = [jnp.zeros_like(pi1), pi1]
    for _ in range(2, S5_T + 1):
        pr, pi = pw_r[-1], pw_i[-1]
        pw_r.append(pr * pr1 - pi * pi1)
        pw_i.append(pr * pi1 + pi * pr1)
    pw_is = [p * sgn for p in pw_i]
    x2 = pr1 - 1.0
    den = ar * ar + ai * ai
    coef_re = (x2 * ar + pi1 * ai) / den
    coef_im_s = ((pi1 * ar - x2 * ai) / den) * sgn
    for gi in range(ng):
        bt = bt_ref[gi]
        cp = cp_ref[gi]
        cc = cp * (-sgn)
        cps = _swap_halves(cp)
        bts = _swap_halves(bt)
        gens = []
        cout_rows = [[], []]
        for d in range(2):
            r = 2 * gi + d
            row = lambda a: a[r:r + 1, :]
            bb = row(coef_re) * bt + row(coef_im_s) * bts
            bbs = _swap_halves(bb)
            w_rows = []
            for m in range(S5_T):
                e = (S5_T - 1 - m) if d == 0 else m
                w = row(pw_r[e]) * bb + row(pw_is[e]) * bbs
                rows = slice(m * S5_CH, (m + 1) * S5_CH)
                wst_ref[gi, rows, d * 128:(d + 1) * 128] = w.astype(_BF)
                wst_ref[gi, rows, 256 + d * 128:256 + (d + 1) * 128] = _swap_halves(w).astype(_BF)
                w_rows.append(w)
            for i in range(S5_T):
                e = (i + 1) if d == 0 else (S5_T - i)
                gmat = row(pw_r[e]) * cp + row(pw_is[e]) * cps
                cout_rows[d].append(gmat * (-sgn))
            gens.append(lax.dot_general(cc, jnp.concatenate(w_rows, axis=0), (((1,), (1,)), ((), ())),
                                        preferred_element_type=_F32, precision=lax.Precision.HIGHEST))
            apow_ref[gi, 2 * d:2 * d + 1, :] = row(pw_r[S5_T])
            apow_ref[gi, 2 * d + 1:2 * d + 2, :] = row(pw_is[S5_T])
        blocks = []
        for t in range(S5_T):
            sf = (S5_CH * (t + 1)) % S5_ROW
            rf = pltpu.roll(gens[0], sf, axis=1) if sf else gens[0]
            rb = pltpu.roll(gens[1], S5_CH * t, axis=1) if t else gens[1]
            blocks.append(jnp.where(gran <= t, rf, 0.0) + jnp.where(gran >= t, rb, 0.0))
        m_ref[gi] = jnp.concatenate(blocks, axis=0).T.astype(_BF)
        cout_t = jnp.concatenate([jnp.concatenate(cout_rows[0], axis=0),
                                  jnp.concatenate(cout_rows[1], axis=0)], axis=1)
        cout_ref[gi] = cout_t.T.astype(_BF)


def _s5_operators(s5_a_re, s5_a_im, s5_log_dt, s5_b_re, s5_b_im, s5_c_re, s5_c_im):
    g = S5_GROUPS
    ng = S5_PREP_GROUP_TILE
    dup = lambda a: jnp.concatenate([a, a], axis=-1).transpose(1, 0, 2).reshape(2 * g, 128)
    ar2 = dup(s5_a_re)
    ai2 = dup(s5_a_im)
    ldt = s5_log_dt.T.reshape(2 * g, 1)
    btp = jnp.concatenate([s5_b_re.transpose(0, 2, 1), s5_b_im.transpose(0, 2, 1)], axis=-1)
    cpk = jnp.concatenate([s5_c_re, s5_c_im], axis=-1)
    gspec = lambda *shape: pl.BlockSpec((ng,) + shape, lambda i: (i,) + (0,) * len(shape))
    rspec = lambda w: pl.BlockSpec((2 * ng, w), lambda i: (i, 0))
    return pl.pallas_call(
        functools.partial(_s5_prep_kernel, ng=ng),
        grid=(g // ng,),
        in_specs=[rspec(128), rspec(128), rspec(1), gspec(S5_CH, 128), gspec(S5_CH, 128)],
        out_specs=[gspec(S5_ROW, S5_ROW), gspec(S5_ROW, 512), gspec(256, S5_ROW), gspec(4, 128)],
        out_shape=[jax.ShapeDtypeStruct((g, S5_ROW, S5_ROW), _BF),
                   jax.ShapeDtypeStruct((g, S5_ROW, 512), _BF),
                   jax.ShapeDtypeStruct((g, 256, S5_ROW), _BF),
                   jax.ShapeDtypeStruct((g, 4, 128), _F32)],
        compiler_params=pltpu.CompilerParams(dimension_semantics=("arbitrary",)),
        name="s5_operators",
    )(ar2, ai2, ldt, btp, cpk)


def _s5_kernel(x_ref, m_ref, wst_ref, cout_ref, apow_ref, s0_ref, y_ref, fin_ref, loc_scr, prev_scr,
               *, nb, nk, ng):
    load_x = lambda gi: jnp.concatenate([x_ref[gi, 0], x_ref[gi, 1]], axis=1).astype(_BF)
    for gi in range(ng):
        loc_scr[gi] = jnp.dot(load_x(gi), wst_ref[gi], preferred_element_type=_F32)
    coef, st = [], []
    for gi in range(ng):
        coef.append([apow_ref[gi, r:r + 1, :] for r in range(4)])
        zf = s0_ref[gi, :, 0:128]
        zb = s0_ref[gi, :, 128:256]
        st.append([zf, _swap_halves(zf), zb, _swap_halves(zb)])
    for k in range(nk):
        kb = nk - 1 - k
        rows_f = slice(k * nb, (k + 1) * nb)
        rows_b = slice(kb * nb, (kb + 1) * nb)
        for gi in range(ng):
            ar_f, ai_f, ar_b, ai_b = coef[gi]
            zf, zfs, zb, zbs = st[gi]
            prev_scr[gi, rows_f, 0:128] = zf
            prev_scr[gi, rows_b, 128:256] = zb
            st[gi] = [ar_f * zf + ai_f * zfs + loc_scr[gi, rows_f, 0:128],
                      ar_f * zfs - ai_f * zf + loc_scr[gi, rows_f, 256:384],
                      ar_b * zb + ai_b * zbs + loc_scr[gi, rows_b, 128:256],
                      ar_b * zbs - ai_b * zb + loc_scr[gi, rows_b, 384:512]]
    for gi in range(ng):
        fin_ref[gi, :, 0:128] = st[gi][0]
        fin_ref[gi, :, 128:256] = st[gi][2]
        y = jnp.dot(load_x(gi), m_ref[gi], preferred_element_type=_F32)
        y = y + jnp.dot(prev_scr[gi].astype(_BF), cout_ref[gi], preferred_element_type=_F32)
        y_ref[gi, 0] = y[:, 0:128]
        y_ref[gi, 1] = y[:, 128:256]


def _s5_scan(xg, m_op, wst, cout, apow, s0p, nb, nk):
    g = S5_GROUPS
    ng = S5_GROUP_TILE
    rows = nk * nb
    gspec = lambda *shape: pl.BlockSpec((ng,) + shape, lambda i: (i,) + (0,) * len(shape))
    return pl.pallas_call(
        functools.partial(_s5_kernel, nb=nb, nk=nk, ng=ng),
        grid=(g // ng,),
        in_specs=[gspec(2, rows, 128), gspec(S5_ROW, S5_ROW), gspec(S5_ROW, 512), gspec(256, S5_ROW),
                  gspec(4, 128), gspec(nb, 256)],
        out_specs=[gspec(2, rows, 128), gspec(nb, 256)],
        out_shape=[jax.ShapeDtypeStruct((g, 2, rows, 128), _F32),
                   jax.ShapeDtypeStruct((g, nb, 256), _F32)],
        scratch_shapes=[pltpu.VMEM((ng, rows, 512), _F32), pltpu.VMEM((ng, rows, 256), _F32)],
        compiler_params=pltpu.CompilerParams(
            dimension_semantics=("arbitrary",), vmem_limit_bytes=VMEM_LIMIT),
        name="s5_scan",
    )(xg, m_op, wst, cout, apow, s0p)


def _post_kernel(x_ref, o_ref, yg_ref, u_ref, mod_ref, modp_ref, dsk_ref, wglu_ref, bglu_ref, wout_ref,
                 l1g_ref, l1b_ref, w1_ref, b1_ref, w2_ref, b2_ref, l2g_ref, l2b_ref, out_ref,
                 y_scr, h_next, x1_next, h_cur, x1_cur, *, nb, spt, tm, nbg, n_steps):
    i = pl.program_id(0)
    pb = jnp.minimum(i, n_steps - 1) % nbg
    front = functools.partial(_post_mix_front, o_ref, yg_ref, u_ref, dsk_ref, wglu_ref, bglu_ref,
                              wout_ref, y_scr, nb=nb, spt=spt, tm=tm, pb=pb)
    back = functools.partial(_post_mix_back, x_ref, mod_ref, wout_ref, l1g_ref, l1b_ref,
                             h_next, x1_next)

    @pl.when(i == 0)
    def _():
        back(*front())

    @pl.when(i > 0)
    def _():
        h_cur[...] = h_next[...]
        x1_cur[...] = x1_next[...]
        h = h_cur[...]
        up = functools.partial(_mlp_up, h, w1_ref, b1_ref)
        down = functools.partial(_mlp_down, w2_ref)
        a0 = up(0)
        a1 = up(1)
        acc = down(a0, 0)
        y, mix = front()
        a2 = up(2)
        acc = acc + down(a1, 1)
        mix = mix + jnp.dot(y, wout_ref[RET_WIDTH:, :], preferred_element_type=_F32)
        a3 = up(3)
        acc = acc + down(a2, 2)
        back(None, mix)
        acc = acc + down(a3, 3)
        f = acc + b2_ref[...]
        g2 = modp_ref[0, 5:6, :]
        out_ref[0] = _norm_rows(ALPHA * x1_cur[...] + g2 * f) * l2g_ref[...] + l2b_ref[...]


def _mlp_up(h, w1_ref, b1_ref, j):
    cols = slice(j * MLP_CHUNK, (j + 1) * MLP_CHUNK)
    a = jnp.dot(h, w1_ref[:, cols], preferred_element_type=_F32) + b1_ref[:, cols]
    return jnp.square(jnp.maximum(a, 0.0)).astype(_BF)


def _mlp_down(w2_ref, a, j):
    return jnp.dot(a, w2_ref[j * MLP_CHUNK:(j + 1) * MLP_CHUNK, :], preferred_element_type=_F32)


def _post_mix_front(o_ref, yg_ref, u_ref, dsk_ref, wglu_ref, bglu_ref, wout_ref, y_scr, *, nb, spt, tm, pb):
    mix = jnp.dot(o_ref[0], wout_ref[0:RET_WIDTH, :], preferred_element_type=_F32)
    for j in range(S5_WIDTH // 128):
        for hf in range(2):
            for q in range(tm // 128):
                start = _octet_row_start(q, pb, nb=nb, spt=spt, tm=tm)
                w = [yg_ref[8 * j + gg, hf, pl.ds(start, 8, stride=nb), :] for gg in range(8)]
                v = _granule_transpose(w)
                for a in range(8):
                    y_scr[j, pl.ds(128 * q + 8 * hf + a, 8, stride=S5_T), :] = v[a]
    y = jnp.concatenate([y_scr[j] for j in range(S5_WIDTH // 128)], axis=1)
    y = y + dsk_ref[...] * u_ref[0].astype(_F32)
    y = jax.nn.gelu(y)
    y = y * jax.nn.sigmoid(jnp.dot(y.astype(_BF), wglu_ref[...], preferred_element_type=_F32)
                           + bglu_ref[...])
    return y.astype(_BF), mix


def _post_mix_back(x_ref, mod_ref, wout_ref, l1g_ref, l1b_ref, h_out, x1_out, y, mix):
    if y is not None:
        mix = mix + jnp.dot(y, wout_ref[RET_WIDTH:, :], preferred_element_type=_F32)
    g1 = mod_ref[0, 2:3, :]
    sh2 = mod_ref[0, 3:4, :]
    sc2 = mod_ref[0, 4:5, :]
    x1 = _norm_rows(ALPHA * x_ref[0] + g1 * mix) * l1g_ref[...] + l1b_ref[...]
    h_out[...] = (_norm_rows(x1) * (1.0 + sc2) + sh2).astype(_BF)
    x1_out[...] = x1


def _post(x, o_ret, yg, proj, mod, mod_row, pw):
    b, l, _ = x.shape
    tm = min(TOKEN_TILE, l)
    spt, nbg, lt = _tile_geometry(b, l, tm)
    rows = (tm // S5_T) * nbg
    n_steps = (lt // tm) * nbg
    item_a = lambda i: jnp.minimum(i, n_steps - 1)
    item_b = lambda i: jnp.maximum(i - 1, 0)
    tok = lambda w, item, cb=0: pl.BlockSpec((1, tm, w), lambda i: (item(i) % nbg, item(i) // nbg, cb))
    modspec = lambda item: pl.BlockSpec((1, 6, D_MODEL), lambda i: (mod_row(item(i) % nbg), 0, 0))
    row = lambda n: _const_spec((1, n))
    return pl.pallas_call(
        functools.partial(_post_kernel, nb=b, spt=spt, tm=tm, nbg=nbg, n_steps=n_steps),
        grid=(n_steps + 1,),
        in_specs=[tok(D_MODEL, item_a), tok(RET_WIDTH, item_a),
                  (_const_spec((S5_GROUPS, 2, rows, 128)) if lt == tm else
                   pl.BlockSpec((S5_GROUPS, 2, rows, 128), lambda i: (0, 0, item_a(i) // nbg, 0))),
                  tok(S5_WIDTH, item_a, 4),
                  modspec(item_a), modspec(item_b),
                  row(S5_WIDTH), _const_spec((S5_WIDTH, S5_WIDTH)), row(S5_WIDTH),
                  _const_spec((D_MODEL, D_MODEL)), row(D_MODEL), row(D_MODEL),
                  _const_spec((D_MODEL, D_FF)), row(D_FF), _const_spec((D_FF, D_MODEL)), row(D_MODEL),
                  row(D_MODEL), row(D_MODEL)],
        out_specs=tok(D_MODEL, item_b),
        out_shape=jax.ShapeDtypeStruct((nbg, lt, D_MODEL), _F32),
        scratch_shapes=[pltpu.VMEM((S5_WIDTH // 128, tm, 128), _F32),
                        pltpu.VMEM((tm, D_MODEL), _BF), pltpu.VMEM((tm, D_MODEL), _F32),
                        pltpu.VMEM((tm, D_MODEL), _BF), pltpu.VMEM((tm, D_MODEL), _F32)],
        compiler_params=pltpu.CompilerParams(
            dimension_semantics=("arbitrary",), vmem_limit_bytes=VMEM_LIMIT),
        name="post",
    )(x.reshape(nbg, lt, D_MODEL), o_ret.reshape(nbg, lt, RET_WIDTH), yg,
      proj.reshape(nbg, lt, IN_COLS), mod, mod, *pw).reshape(b, l, D_MODEL)


def _layer_pass(x, mod, mod_row, ret_s0, s5_s0_re, s5_s0_im, w_in_bf, ret_decay, s5_ops, post_w):
    b, l, _ = x.shape
    nk = l // S5_T
    g = S5_GROUPS
    proj, xg = _projection(x, mod, mod_row, w_in_bf)
    o_ret, ret_state = _retention(proj, ret_decay, ret_s0)
    if s5_s0_re is None:
        s0p = jnp.zeros((g, b, 256), _F32)
    else:
        s0p = jnp.concatenate([s5_s0_re, s5_s0_im], axis=-1)
        s0p = s0p.transpose(2, 0, 1, 3).reshape(g, b, 256)
    yg, fin = _s5_scan(xg, *s5_ops, s0p, b, nk)
    fin = fin.reshape(g, b, 2, 2, S5_STATE).transpose(1, 2, 0, 3, 4)
    out = _post(x, o_ret, yg, proj, mod, mod_row, post_w)
    return out, ret_state, fin[:, :, :, 0, :], fin[:, :, :, 1, :]


def kernel(x_prompt, x_sample, state_ret, state_s5_re, state_s5_im, c, c_ctx, w_ada, b_ada, w_in,
           ret_decay, s5_a_re, s5_a_im, s5_log_dt, s5_b_re, s5_b_im, s5_c_re, s5_c_im, s5_d, w_glu,
           b_glu, w_out, ln1_g, ln1_b, w_ff1, b_ff1, w_ff2, b_ff2, ln2_g, ln2_b):
    depth = w_ada.shape[0]
    bs = x_sample.shape[0]
    y_p, y_s = x_prompt, x_sample
    rets, s5rs, s5is = [], [], []
    for layer in range(depth):
        cond8 = jnp.concatenate(
            [c_ctx[None, :], c, jnp.zeros((8 - 1 - bs, D_MODEL), _F32)], axis=0)
        mod = _modulation(cond8, w_ada[layer], b_ada[layer]).reshape(8, 6, D_MODEL)
        s5_ops = _s5_operators(s5_a_re[layer], s5_a_im[layer], s5_log_dt[layer], s5_b_re[layer],
                               s5_b_im[layer], s5_c_re[layer], s5_c_im[layer])
        w_in_bf = w_in[layer].astype(_BF)
        r2 = lambda a: a[layer].reshape(1, -1)
        post_w = (r2(s5_d), w_glu[layer].astype(_BF), r2(b_glu), w_out[layer].astype(_BF),
                  r2(ln1_g), r2(ln1_b), w_ff1[layer].astype(_BF), r2(b_ff1),
                  w_ff2[layer].astype(_BF), r2(b_ff2), r2(ln2_g), r2(ln2_b))
        y_p, r_st, s5_re, s5_im = _layer_pass(
            y_p, mod, lambda i: 0, None, None, None, w_in_bf, ret_decay[layer], s5_ops, post_w)
        rets.append(r_st)
        s5rs.append(s5_re)
        s5is.append(s5_im)
        y_s, _, _, _ = _layer_pass(
            y_s, mod, lambda i: i + 1, state_ret[:, layer], state_s5_re[:, layer],
            state_s5_im[:, layer], w_in_bf, ret_decay[layer], s5_ops, post_w)
    return (y_p, y_s, jnp.stack(rets, axis=1), jnp.stack(s5rs, axis=1), jnp.stack(s5is, axis=1))
```

```python
import functools

import jax
import jax.numpy as jnp
from jax import lax
from jax.experimental import pallas as pl
from jax.experimental.pallas import tpu as pltpu

D_MODEL = 1024
RET_HEADS = 4
RET_DK = 128
RET_WIDTH = RET_HEADS * RET_DK
S5_CH = 16
S5_GROUPS = 32
S5_STATE = 64
S5_WIDTH = S5_GROUPS * S5_CH
D_FF = 4 * D_MODEL
IN_COLS = 4 * RET_WIDTH + S5_WIDTH
ALPHA = 2.0 ** 0.25
LN_EPS = 1e-5

S5_T = 16
S5_ROW = S5_T * S5_CH
S5_PREP_GROUP_TILE = 4
S5_GROUP_TILE = 8
RET_CHUNK = 256
RET_UNIT_TOKENS = 1024
TOKEN_TILE = 512
MLP_CHUNK = 1024
VMEM_LIMIT = 56 * 1024 * 1024

_BF = jnp.bfloat16
_F32 = jnp.float32


def _norm_rows(x):
    mu = jnp.mean(x, axis=-1, keepdims=True)
    xc = x - mu
    var = jnp.mean(xc * xc, axis=-1, keepdims=True)
    return xc * lax.rsqrt(var + LN_EPS)


def _const_spec(shape):
    nd = len(shape)
    return pl.BlockSpec(shape, lambda *_: (0,) * nd, pipeline_mode=pl.Buffered(1))


def _granule_transpose(v):
    lane = lax.broadcasted_iota(jnp.int32, (1, 128), 1)
    v = list(v)
    for d in (4, 2, 1):
        bit = ((lane // S5_CH) & d) != 0
        nv = list(v)
        for a in range(8):
            if a & d:
                continue
            lo, hi = v[a], v[a + d]
            nv[a] = jnp.where(bit, pltpu.roll(hi, S5_CH * d, axis=1), lo)
            nv[a + d] = jnp.where(bit, hi, pltpu.roll(lo, 128 - S5_CH * d, axis=1))
        v = nv
    return v


def _tile_geometry(b, l, tm):
    spt = max(1, tm // l)
    return spt, b // spt, l * spt


def _octet_row_start(q, pb, *, nb, spt, tm):
    per_seq = (tm // spt) // 128
    return 8 * (q % per_seq) * nb + pb * spt + q // per_seq


def _mod_kernel(cond_ref, w_ref, b_ref, o_ref):
    c = cond_ref[...]
    s = (c * jax.nn.sigmoid(c)).astype(_BF)
    o_ref[...] = jnp.dot(s, w_ref[...].astype(_BF), preferred_element_type=_F32) + b_ref[...]


def _modulation(cond8, w_ada, b_ada):
    n = w_ada.shape[1]
    tn = 1024
    return pl.pallas_call(
        _mod_kernel,
        grid=(n // tn,),
        in_specs=[pl.BlockSpec((8, D_MODEL), lambda j: (0, 0)),
                  pl.BlockSpec((D_MODEL, tn), lambda j: (0, j)),
                  pl.BlockSpec((1, tn), lambda j: (0, j))],
        out_specs=pl.BlockSpec((8, tn), lambda j: (0, j)),
        out_shape=jax.ShapeDtypeStruct((8, n), _F32),
        compiler_params=pltpu.CompilerParams(dimension_semantics=("arbitrary",)),
        name="modulation",
    )(cond8, w_ada, b_ada.reshape(1, n))


def _proj_kernel(*refs, nb, spt, tm, n_cast):
    x_ref, mod_ref, w_ref = refs[:3]
    cast_in = refs[3:3 + n_cast]
    o_ref, xg_ref = refs[3 + n_cast:5 + n_cast]
    cast_out = refs[5 + n_cast:5 + 2 * n_cast]
    u_scr = refs[5 + 2 * n_cast]
    for src, dst in zip(cast_in, cast_out):
        dst[...] = src[...].astype(_BF)
    x = x_ref[0]
    sh = mod_ref[0, 0:1, :]
    sc = mod_ref[0, 1:2, :]
    h = (_norm_rows(x) * (1.0 + sc) + sh).astype(_BF)
    u = jnp.dot(h, w_ref[:, 4 * RET_WIDTH:], preferred_element_type=_F32)
    o_ref[0, :, 4 * RET_WIDTH:] = u.astype(_BF)
    for j in range(S5_WIDTH // 128):
        u_scr[j] = u[:, 128 * j:128 * (j + 1)]
    o_ref[0, :, 0:4 * RET_WIDTH] = jnp.dot(
        h, w_ref[:, 0:4 * RET_WIDTH], preferred_element_type=_F32).astype(_BF)
    pb = pl.program_id(1)
    for j in range(S5_WIDTH // 128):
        for hf in range(2):
            for q in range(tm // 128):
                v = [u_scr[j, pl.ds(128 * q + 8 * hf + a, 8, stride=S5_T), :] for a in range(8)]
                w = _granule_transpose(v)
                start = _octet_row_start(q, pb, nb=nb, spt=spt, tm=tm)
                for gg in range(8):
                    xg_ref[8 * j + gg, hf, pl.ds(start, 8, stride=nb), :] = w[gg]


def _projection(x, mod, mod_row, w_in_bf, cast_weights=()):
    b, l, _ = x.shape
    tm = TOKEN_TILE
    spt, nbg, lt = _tile_geometry(b, l, tm)
    rows = (tm // S5_T) * nbg
    n_steps = (lt // tm) * nbg
    cast_specs = [pl.BlockSpec((w.shape[0] // n_steps, w.shape[1]), lambda j, i: (j * nbg + i, 0))
                  for w in cast_weights]
    outs = pl.pallas_call(
        functools.partial(_proj_kernel, nb=b, spt=spt, tm=tm, n_cast=len(cast_weights)),
        grid=(lt // tm, nbg),
        in_specs=[pl.BlockSpec((1, tm, D_MODEL), lambda j, i: (i, j, 0)),
                  pl.BlockSpec((1, 6, D_MODEL), lambda j, i: (mod_row(i), 0, 0)),
                  _const_spec((D_MODEL, IN_COLS))] + cast_specs,
        out_specs=[pl.BlockSpec((1, tm, IN_COLS), lambda j, i: (i, j, 0)),
                   pl.BlockSpec((S5_GROUPS, 2, rows, 128), lambda j, i: (0, 0, j, 0))] + cast_specs,
        out_shape=[jax.ShapeDtypeStruct((nbg, lt, IN_COLS), _BF),
                   jax.ShapeDtypeStruct((S5_GROUPS, 2, (l // S5_T) * b, 128), _F32)]
                  + [jax.ShapeDtypeStruct(w.shape, _BF) for w in cast_weights],
        scratch_shapes=[pltpu.VMEM((S5_WIDTH // 128, tm, 128), _F32)],
        compiler_params=pltpu.CompilerParams(
            dimension_semantics=("arbitrary", "arbitrary"), vmem_limit_bytes=VMEM_LIMIT),
        name="projection",
    )(x.reshape(nbg, lt, D_MODEL), mod, w_in_bf, *cast_weights)
    return outs[0].reshape(b, l, IN_COLS), outs[1], tuple(outs[2:])


def _ret_kernel(*refs, chunk, n_chunks, nbb, has_s0):
    if has_s0:
        dec_ref, q_ref, k_ref, v_ref, g_ref, s0_ref, o_ref, st_ref, kv_scr, dec_scr = refs
    else:
        dec_ref, q_ref, k_ref, v_ref, g_ref, o_ref, st_ref, kv_scr, dec_scr = refs
        s0_ref = None
    c = chunk
    scale = RET_DK ** -0.5
    pos = lax.broadcasted_iota(jnp.int32, (c, 1), 0).astype(_F32)
    ri = lax.broadcasted_iota(jnp.int32, (c, c), 0)
    ci = lax.broadcasted_iota(jnp.int32, (c, c), 1)
    rel = (ri - ci).astype(_F32)

    for hd in range(RET_HEADS):
        lanes = slice(hd * RET_DK, (hd + 1) * RET_DK)

        def log_gamma(d):
            z = jnp.full((1, 1), dec_ref[d, hd], _F32)
            return jnp.minimum(z, 0.0) - jnp.log(1.0 + jnp.exp(-jnp.abs(z)))

        lg_f, lg_b = log_gamma(0), log_gamma(1)
        wide = lambda col: jnp.broadcast_to(col, (c, RET_DK))
        dec_scr[hd, 0] = wide(jnp.exp(lg_f * (c - 1.0 - pos)) * scale)
        dec_scr[hd, 1] = wide(jnp.exp(lg_b * pos) * scale)
        dec_scr[hd, 2] = wide(jnp.exp(lg_f * (pos + 1.0)))
        dec_scr[hd, 3] = wide(jnp.exp(lg_b * (c - pos)))
        dmat = (jnp.where(ri >= ci, jnp.exp(lg_f * jnp.maximum(rel, 0.0)), 0.0)
                + jnp.where(ci >= ri, jnp.exp(lg_b * jnp.maximum(-rel, 0.0)), 0.0)) * scale
        cdec_f = jnp.exp(lg_f * float(c))
        cdec_b = jnp.exp(lg_b * float(c))

        for bb in range(nbb):
            unit = bb * RET_HEADS + hd
            for n in range(n_chunks):
                kc = k_ref[bb, n * c:(n + 1) * c, lanes]
                vc = v_ref[bb, n * c:(n + 1) * c, lanes].astype(_F32)
                vcat = jnp.concatenate([(vc * dec_scr[hd, 0]).astype(_BF), (vc * dec_scr[hd, 1]).astype(_BF)], axis=1)
                kv_scr[unit, n] = lax.dot_general(kc, vcat, (((0,), (0,)), ((), ())),
                                                  preferred_element_type=_F32)
            if has_s0:
                carry_f = s0_ref[bb, 0, hd]
                carry_b = s0_ref[bb, 1, hd]
            else:
                carry_f = jnp.zeros((RET_DK, RET_DK), _F32)
                carry_b = jnp.zeros((RET_DK, RET_DK), _F32)
            for n in range(n_chunks):
                inc = kv_scr[unit, n, :, 0:RET_DK]
                kv_scr[unit, n, :, 0:RET_DK] = carry_f
                carry_f = cdec_f * carry_f + inc
            for n in range(n_chunks - 1, -1, -1):
                inc = kv_scr[unit, n, :, RET_DK:2 * RET_DK]
                kv_scr[unit, n, :, RET_DK:2 * RET_DK] = carry_b
                carry_b = cdec_b * carry_b + inc
            st_ref[bb, 0, hd] = carry_f
            st_ref[bb, 1, hd] = carry_b

            for n in range(n_chunks):
                sl = slice(n * c, (n + 1) * c)
                qc = q_ref[bb, sl, lanes]
                kc = k_ref[bb, sl, lanes]
                vc = v_ref[bb, sl, lanes]
                scores = lax.dot_general(qc, kc, (((1,), (1,)), ((), ())), preferred_element_type=_F32)
                o = jnp.dot((scores * dmat).astype(_BF), vc, preferred_element_type=_F32)
                oi = jnp.dot(qc, kv_scr[unit, n].astype(_BF), preferred_element_type=_F32)
                o = o + oi[:, 0:RET_DK] * dec_scr[hd, 2] + oi[:, RET_DK:2 * RET_DK] * dec_scr[hd, 3]
                gate = g_ref[bb, sl, lanes].astype(_F32)
                o_ref[bb, sl, lanes] = (_norm_rows(o) * (gate * jax.nn.sigmoid(gate))).astype(_BF)


def _retention(proj, ret_decay, s0):
    b, l, _ = proj.shape
    c = min(RET_CHUNK, l)
    n_chunks = l // c
    nbb = max(1, min(b, RET_UNIT_TOKENS // l))
    has_s0 = s0 is not None
    col = lambda j: pl.BlockSpec((nbb, l, RET_WIDTH), lambda i: (i, 0, j))
    st_spec = pl.BlockSpec((nbb, 2, RET_HEADS, RET_DK, RET_DK), lambda i: (i, 0, 0, 0, 0))
    in_specs = [pl.BlockSpec(memory_space=pltpu.SMEM), col(0), col(1), col(2), col(3)]
    args = [ret_decay, proj, proj, proj, proj]
    if has_s0:
        in_specs.append(st_spec)
        args.append(s0)
    return pl.pallas_call(
        functools.partial(_ret_kernel, chunk=c, n_chunks=n_chunks, nbb=nbb, has_s0=has_s0),
        grid=(b // nbb,),
        in_specs=in_specs,
        out_specs=[pl.BlockSpec((nbb, l, RET_WIDTH), lambda i: (i, 0, 0)), st_spec],
        out_shape=[jax.ShapeDtypeStruct((b, l, RET_WIDTH), _BF),
                   jax.ShapeDtypeStruct((b, 2, RET_HEADS, RET_DK, RET_DK), _F32)],
        scratch_shapes=[pltpu.VMEM((nbb * RET_HEADS, n_chunks, RET_DK, 2 * RET_DK), _F32),
                        pltpu.VMEM((RET_HEADS, 4, c, RET_DK), _F32)],
        compiler_params=pltpu.CompilerParams(
            dimension_semantics=("arbitrary",), vmem_limit_bytes=VMEM_LIMIT),
        name="retention",
    )(*args)


def _swap_halves(z):
    return pltpu.roll(z, S5_STATE, axis=z.ndim - 1)


def _s5_prep_kernel(ar_ref, ai_ref, ldt_ref, bt_ref, cp_ref, m_ref, wst_ref, cout_ref, apow_ref, *, ng):
    lane = lax.broadcasted_iota(jnp.int32, (1, 2 * S5_STATE), 1)
    sgn = jnp.where(lane < S5_STATE, -1.0, 1.0).astype(_F32)
    gran = lax.broadcasted_iota(jnp.int32, (1, S5_ROW), 1) // S5_CH
    ar = ar_ref[...]
    ai = ai_ref[...]
    dt = jnp.exp(ldt_ref[...])
    mag = jnp.exp(ar * dt)
    ang = ai * dt
    pr1 = mag * jnp.cos(ang)
    pi1 = mag * jnp.sin(ang)
    pw_r = [jnp.ones_like(pr1), pr1]
    pw_i = [jnp.zeros_like(pi1), pi1]
    for _ in range(2, S5_T + 1):
        pr, pi = pw_r[-1], pw_i[-1]
        pw_r.append(pr * pr1 - pi * pi1)
        pw_i.append(pr * pi1 + pi * pr1)
    pw_is = [p * sgn for p in pw_i]
    x2 = pr1 - 1.0
    den = ar * ar + ai * ai
    coef_re = (x2 * ar + pi1 * ai) / den
    coef_im_s = ((pi1 * ar - x2 * ai) / den) * sgn
    for gi in range(ng):
        bt = bt_ref[gi]
        cp = cp_ref[gi]
        cc = cp * (-sgn)
        cps = _swap_halves(cp)
        bts = _swap_halves(bt)
        gens = []
        cout_rows = [[], []]
        for d in range(2):
            r = 2 * gi + d
            row = lambda a: a[r:r + 1, :]
            bb = row(coef_re) * bt + row(coef_im_s) * bts
            bbs = _swap_halves(bb)
            w_rows = []
            for m in range(S5_T):
                e = (S5_T - 1 - m) if d == 0 else m
                w = row(pw_r[e]) * bb + row(pw_is[e]) * bbs
                rows = slice(m * S5_CH, (m + 1) * S5_CH)
                wst_ref[gi, rows, d * 128:(d + 1) * 128] = w.astype(_BF)
                wst_ref[gi, rows, 256 + d * 128:256 + (d + 1) * 128] = _swap_halves(w).astype(_BF)
                w_rows.append(w)
            for i in range(S5_T):
                e = (i + 1) if d == 0 else (S5_T - i)
                gmat = row(pw_r[e]) * cp + row(pw_is[e]) * cps
                cout_rows[d].append(gmat * (-sgn))
            gens.append(lax.dot_general(cc, jnp.concatenate(w_rows, axis=0), (((1,), (1,)), ((), ())),
                                        preferred_element_type=_F32, precision=lax.Precision.HIGHEST))
            apow_ref[gi, 2 * d:2 * d + 1, :] = row(pw_r[S5_T])
            apow_ref[gi, 2 * d + 1:2 * d + 2, :] = row(pw_is[S5_T])
        blocks = []
        for t in range(S5_T):
            sf = (S5_CH * (t + 1)) % S5_ROW
            rf = pltpu.roll(gens[0], sf, axis=1) if sf else gens[0]
            rb = pltpu.roll(gens[1], S5_CH * t, axis=1) if t else gens[1]
            blocks.append(jnp.where(gran <= t, rf, 0.0) + jnp.where(gran >= t, rb, 0.0))
        m_ref[gi] = jnp.concatenate(blocks, axis=0).T.astype(_BF)
        cout_t = jnp.concatenate([jnp.concatenate(cout_rows[0], axis=0),
                                  jnp.concatenate(cout_rows[1], axis=0)], axis=1)
        cout_ref[gi] = cout_t.T.astype(_BF)


def _s5_operators(s5_a_re, s5_a_im, s5_log_dt, s5_b_re, s5_b_im, s5_c_re, s5_c_im):
    g = S5_GROUPS
    ng = S5_PREP_GROUP_TILE
    dup = lambda a: jnp.concatenate([a, a], axis=-1).transpose(1, 0, 2).reshape(2 * g, 128)
    ar2 = dup(s5_a_re)
    ai2 = dup(s5_a_im)
    ldt = s5_log_dt.T.reshape(2 * g, 1)
    btp = jnp.concatenate([s5_b_re.transpose(0, 2, 1), s5_b_im.transpose(0, 2, 1)], axis=-1)
    cpk = jnp.concatenate([s5_c_re, s5_c_im], axis=-1)
    gspec = lambda *shape: pl.BlockSpec((ng,) + shape, lambda i: (i,) + (0,) * len(shape))
    rspec = lambda w: pl.BlockSpec((2 * ng, w), lambda i: (i, 0))
    return pl.pallas_call(
        functools.partial(_s5_prep_kernel, ng=ng),
        grid=(g // ng,),
        in_specs=[rspec(128), rspec(128), rspec(1), gspec(S5_CH, 128), gspec(S5_CH, 128)],
        out_specs=[gspec(S5_ROW, S5_ROW), gspec(S5_ROW, 512), gspec(256, S5_ROW), gspec(4, 128)],
        out_shape=[jax.ShapeDtypeStruct((g, S5_ROW, S5_ROW), _BF),
                   jax.ShapeDtypeStruct((g, S5_ROW, 512), _BF),
                   jax.ShapeDtypeStruct((g, 256, S5_ROW), _BF),
                   jax.ShapeDtypeStruct((g, 4, 128), _F32)],
        compiler_params=pltpu.CompilerParams(dimension_semantics=("arbitrary",)),
        name="s5_operators",
    )(ar2, ai2, ldt, btp, cpk)


def _s5_kernel(x_ref, m_ref, wst_ref, cout_ref, apow_ref, s0_ref, y_ref, fin_ref, loc_scr, prev_scr,
               *, nb, nk, ng):
    load_x = lambda gi: jnp.concatenate([x_ref[gi, 0], x_ref[gi, 1]], axis=1).astype(_BF)
    for gi in range(ng):
        loc_scr[gi] = jnp.dot(load_x(gi), wst_ref[gi], preferred_element_type=_F32)
    coef, st = [], []
    for gi in range(ng):
        coef.append([apow_ref[gi, r:r + 1, :] for r in range(4)])
        zf = s0_ref[gi, :, 0:128]
        zb = s0_ref[gi, :, 128:256]
        st.append([zf, _swap_halves(zf), zb, _swap_halves(zb)])
    for k in range(nk):
        kb = nk - 1 - k
        rows_f = slice(k * nb, (k + 1) * nb)
        rows_b = slice(kb * nb, (kb + 1) * nb)
        for gi in range(ng):
            ar_f, ai_f, ar_b, ai_b = coef[gi]
            zf, zfs, zb, zbs = st[gi]
            prev_scr[gi, rows_f, 0:128] = zf
            prev_scr[gi, rows_b, 128:256] = zb
            st[gi] = [ar_f * zf + ai_f * zfs + loc_scr[gi, rows_f, 0:128],
                      ar_f * zfs - ai_f * zf + loc_scr[gi, rows_f, 256:384],
                      ar_b * zb + ai_b * zbs + loc_scr[gi, rows_b, 128:256],
                      ar_b * zbs - ai_b * zb + loc_scr[gi, rows_b, 384:512]]
    for gi in range(ng):
        fin_ref[gi, :, 0:128] = st[gi][0]
        fin_ref[gi, :, 128:256] = st[gi][2]
        y = jnp.dot(load_x(gi), m_ref[gi], preferred_element_type=_F32)
        y = y + jnp.dot(prev_scr[gi].astype(_BF), cout_ref[gi], preferred_element_type=_F32)
        y_ref[gi, 0] = y[:, 0:128]
        y_ref[gi, 1] = y[:, 128:256]


def _s5_scan(xg, m_op, wst, cout, apow, s0p, nb, nk):
    g = S5_GROUPS
    ng = S5_GROUP_TILE
    rows = nk * nb
    gspec = lambda *shape: pl.BlockSpec((ng,) + shape, lambda i: (i,) + (0,) * len(shape))
    return pl.pallas_call(
        functools.partial(_s5_kernel, nb=nb, nk=nk, ng=ng),
        grid=(g // ng,),
        in_specs=[gspec(2, rows, 128), gspec(S5_ROW, S5_ROW), gspec(S5_ROW, 512), gspec(256, S5_ROW),
                  gspec(4, 128), gspec(nb, 256)],
        out_specs=[gspec(2, rows, 128), gspec(nb, 256)],
        out_shape=[jax.ShapeDtypeStruct((g, 2, rows, 128), _F32),
                   jax.ShapeDtypeStruct((g, nb, 256), _F32)],
        scratch_shapes=[pltpu.VMEM((ng, rows, 512), _F32), pltpu.VMEM((ng, rows, 256), _F32)],
        compiler_params=pltpu.CompilerParams(
            dimension_semantics=("arbitrary",), vmem_limit_bytes=VMEM_LIMIT),
        name="s5_scan",
    )(xg, m_op, wst, cout, apow, s0p)


def _post_kernel(x_ref, o_ref, yg_ref, u_ref, mod_ref, modp_ref, dsk_ref, wglu_ref, bglu_ref, wout_ref,
                 l1g_ref, l1b_ref, w1_ref, b1_ref, w2_ref, b2_ref, l2g_ref, l2b_ref, out_ref,
                 y_scr, h_next, x1_next, h_cur, x1_cur, *, nb, spt, tm, nbg, n_steps):
    i = pl.program_id(0)
    pb = jnp.minimum(i, n_steps - 1) % nbg
    front = functools.partial(_post_mix_front, o_ref, yg_ref, u_ref, dsk_ref, wglu_ref, bglu_ref,
                              wout_ref, y_scr, nb=nb, spt=spt, tm=tm, pb=pb)
    back = functools.partial(_post_mix_back, x_ref, mod_ref, wout_ref, l1g_ref, l1b_ref,
                             h_next, x1_next)

    @pl.when(i == 0)
    def _():
        back(*front())

    @pl.when(i > 0)
    def _():
        h_cur[...] = h_next[...]
        x1_cur[...] = x1_next[...]
        h = h_cur[...]
        up = functools.partial(_mlp_up, h, w1_ref, b1_ref)
        down = functools.partial(_mlp_down, w2_ref)
        a0 = up(0)
        a1 = up(1)
        acc = down(a0, 0)
        y, mix = front()
        a2 = up(2)
        acc = acc + down(a1, 1)
        mix = mix + jnp.dot(y, wout_ref[RET_WIDTH:, :], preferred_element_type=_F32)
        a3 = up(3)
        acc = acc + down(a2, 2)
        back(None, mix)
        acc = acc + down(a3, 3)
        f = acc + b2_ref[...]
        g2 = modp_ref[0, 5:6, :]
        out_ref[0] = _norm_rows(ALPHA * x1_cur[...] + g2 * f) * l2g_ref[...] + l2b_ref[...]


def _mlp_up(h, w1_ref, b1_ref, j):
    cols = slice(j * MLP_CHUNK, (j + 1) * MLP_CHUNK)
    a = jnp.dot(h, w1_ref[:, cols], preferred_element_type=_F32) + b1_ref[:, cols]
    return jnp.square(jnp.maximum(a, 0.0)).astype(_BF)


def _mlp_down(w2_ref, a, j):
    return jnp.dot(a, w2_ref[j * MLP_CHUNK:(j + 1) * MLP_CHUNK, :], preferred_element_type=_F32)


def _post_mix_front(o_ref, yg_ref, u_ref, dsk_ref, wglu_ref, bglu_ref, wout_ref, y_scr, *, nb, spt, tm, pb):
    mix = jnp.dot(o_ref[0], wout_ref[0:RET_WIDTH, :], preferred_element_type=_F32)
    for j in range(S5_WIDTH // 128):
        for hf in range(2):
            for q in range(tm // 128):
                start = _octet_row_start(q, pb, nb=nb, spt=spt, tm=tm)
                w = [yg_ref[8 * j + gg, hf, pl.ds(start, 8, stride=nb), :] for gg in range(8)]
                v = _granule_transpose(w)
                for a in range(8):
                    y_scr[j, pl.ds(128 * q + 8 * hf + a, 8, stride=S5_T), :] = v[a]
    y = jnp.concatenate([y_scr[j] for j in range(S5_WIDTH // 128)], axis=1)
    y = y + dsk_ref[...] * u_ref[0].astype(_F32)
    y = jax.nn.gelu(y)
    y = y * jax.nn.sigmoid(jnp.dot(y.astype(_BF), wglu_ref[...], preferred_element_type=_F32)
                           + bglu_ref[...])
    return y.astype(_BF), mix


def _post_mix_back(x_ref, mod_ref, wout_ref, l1g_ref, l1b_ref, h_out, x1_out, y, mix):
    if y is not None:
        mix = mix + jnp.dot(y, wout_ref[RET_WIDTH:, :], preferred_element_type=_F32)
    g1 = mod_ref[0, 2:3, :]
    sh2 = mod_ref[0, 3:4, :]
    sc2 = mod_ref[0, 4:5, :]
    x1 = _norm_rows(ALPHA * x_ref[0] + g1 * mix) * l1g_ref[...] + l1b_ref[...]
    h_out[...] = (_norm_rows(x1) * (1.0 + sc2) + sh2).astype(_BF)
    x1_out[...] = x1


def _post(x, o_ret, yg, proj, mod, mod_row, pw):
    b, l, _ = x.shape
    tm = min(TOKEN_TILE, l)
    spt, nbg, lt = _tile_geometry(b, l, tm)
    rows = (tm // S5_T) * nbg
    n_steps = (lt // tm) * nbg
    item_a = lambda i: jnp.minimum(i, n_steps - 1)
    item_b = lambda i: jnp.maximum(i - 1, 0)
    tok = lambda w, item, cb=0: pl.BlockSpec((1, tm, w), lambda i: (item(i) % nbg, item(i) // nbg, cb))
    modspec = lambda item: pl.BlockSpec((1, 6, D_MODEL), lambda i: (mod_row(item(i) % nbg), 0, 0))
    row = lambda n: _const_spec((1, n))
    return pl.pallas_call(
        functools.partial(_post_kernel, nb=b, spt=spt, tm=tm, nbg=nbg, n_steps=n_steps),
        grid=(n_steps + 1,),
        in_specs=[tok(D_MODEL, item_a), tok(RET_WIDTH, item_a),
                  (_const_spec((S5_GROUPS, 2, rows, 128)) if lt == tm else
                   pl.BlockSpec((S5_GROUPS, 2, rows, 128), lambda i: (0, 0, item_a(i) // nbg, 0))),
                  tok(S5_WIDTH, item_a, 4),
                  modspec(item_a), modspec(item_b),
                  row(S5_WIDTH), _const_spec((S5_WIDTH, S5_WIDTH)), row(S5_WIDTH),
                  _const_spec((D_MODEL, D_MODEL)), row(D_MODEL), row(D_MODEL),
                  _const_spec((D_MODEL, D_FF)), row(D_FF), _const_spec((D_FF, D_MODEL)), row(D_MODEL),
                  row(D_MODEL), row(D_MODEL)],
        out_specs=tok(D_MODEL, item_b),
        out_shape=jax.ShapeDtypeStruct((nbg, lt, D_MODEL), _F32),
        scratch_shapes=[pltpu.VMEM((S5_WIDTH // 128, tm, 128), _F32),
                        pltpu.VMEM((tm, D_MODEL), _BF), pltpu.VMEM((tm, D_MODEL), _F32),
                        pltpu.VMEM((tm, D_MODEL), _BF), pltpu.VMEM((tm, D_MODEL), _F32)],
        compiler_params=pltpu.CompilerParams(
            dimension_semantics=("arbitrary",), vmem_limit_bytes=VMEM_LIMIT),
        name="post",
    )(x.reshape(nbg, lt, D_MODEL), o_ret.reshape(nbg, lt, RET_WIDTH), yg,
      proj.reshape(nbg, lt, IN_COLS), mod, mod, *pw).reshape(b, l, D_MODEL)


def _layer_pass(x, mod, mod_row, ret_s0, s5_s0_re, s5_s0_im, w_in_bf, ret_decay, s5_ops, post_w):
    b, l, _ = x.shape
    nk = l // S5_T
    g = S5_GROUPS
    mats = ("w_glu", "w_out", "w_ff1", "w_ff2")
    to_cast = [k for k in mats if post_w[k].dtype != _BF]
    proj, xg, cast = _projection(x, mod, mod_row, w_in_bf, tuple(post_w[k] for k in to_cast))
    post_w = dict(post_w, **dict(zip(to_cast, cast)))
    o_ret, ret_state = _retention(proj, ret_decay, ret_s0)
    if s5_s0_re is None:
        s0p = jnp.zeros((g, b, 256), _F32)
    else:
        s0p = jnp.concatenate([s5_s0_re, s5_s0_im], axis=-1)
        s0p = s0p.transpose(2, 0, 1, 3).reshape(g, b, 256)
    yg, fin = _s5_scan(xg, *s5_ops, s0p, b, nk)
    fin = fin.reshape(g, b, 2, 2, S5_STATE).transpose(1, 2, 0, 3, 4)
    pw = tuple(post_w[k] for k in ("s5_d", "w_glu", "b_glu", "w_out", "ln1_g", "ln1_b", "w_ff1", "b_ff1",
                                   "w_ff2", "b_ff2", "ln2_g", "ln2_b"))
    out = _post(x, o_ret, yg, proj, mod, mod_row, pw)
    return out, ret_state, fin[:, :, :, 0, :], fin[:, :, :, 1, :], post_w


def kernel(x_prompt, x_sample, state_ret, state_s5_re, state_s5_im, c, c_ctx, w_ada, b_ada, w_in,
           ret_decay, s5_a_re, s5_a_im, s5_log_dt, s5_b_re, s5_b_im, s5_c_re, s5_c_im, s5_d, w_glu,
           b_glu, w_out, ln1_g, ln1_b, w_ff1, b_ff1, w_ff2, b_ff2, ln2_g, ln2_b):
    depth = w_ada.shape[0]
    bs = x_sample.shape[0]
    y_p, y_s = x_prompt, x_sample
    rets, s5rs, s5is = [], [], []
    for layer in range(depth):
        cond8 = jnp.concatenate(
            [c_ctx[None, :], c, jnp.zeros((8 - 1 - bs, D_MODEL), _F32)], axis=0)
        mod = _modulation(cond8, w_ada[layer], b_ada[layer]).reshape(8, 6, D_MODEL)
        s5_ops = _s5_operators(s5_a_re[layer], s5_a_im[layer], s5_log_dt[layer], s5_b_re[layer],
                               s5_b_im[layer], s5_c_re[layer], s5_c_im[layer])
        w_in_bf = w_in[layer].astype(_BF)
        r2 = lambda a: a[layer].reshape(1, -1)
        post_w = dict(s5_d=r2(s5_d), w_glu=w_glu[layer], b_glu=r2(b_glu), w_out=w_out[layer],
                      ln1_g=r2(ln1_g), ln1_b=r2(ln1_b), w_ff1=w_ff1[layer], b_ff1=r2(b_ff1),
                      w_ff2=w_ff2[layer], b_ff2=r2(b_ff2), ln2_g=r2(ln2_g), ln2_b=r2(ln2_b))
        y_p, r_st, s5_re, s5_im, post_w = _layer_pass(
            y_p, mod, lambda i: 0, None, None, None, w_in_bf, ret_decay[layer], s5_ops, post_w)
        rets.append(r_st)
        s5rs.append(s5_re)
        s5is.append(s5_im)
        y_s, _, _, _, _ = _layer_pass(
            y_s, mod, lambda i: i + 1, state_ret[:, layer], state_s5_re[:, layer],
            state_s5_im[:, layer], w_in_bf, ret_decay[layer], s5_ops, post_w)
    return (y_p, y_s, jnp.stack(rets, axis=1), jnp.stack(s5rs, axis=1), jnp.stack(s5is, axis=1))
```

```python
import functools

import jax
import jax.numpy as jnp
from jax import lax
from jax.experimental import pallas as pl
from jax.experimental.pallas import tpu as pltpu

D_MODEL = 1024
RET_HEADS = 4
RET_DK = 128
RET_WIDTH = RET_HEADS * RET_DK
S5_CH = 16
S5_GROUPS = 32
S5_STATE = 64
S5_WIDTH = S5_GROUPS * S5_CH
D_FF = 4 * D_MODEL
IN_COLS = 4 * RET_WIDTH + S5_WIDTH
ALPHA = 2.0 ** 0.25
LN_EPS = 1e-5

S5_T = 16
S5_ROW = S5_T * S5_CH
S5_PREP_GROUP_TILE = 4
S5_GROUP_TILE = 8
RET_CHUNK = 256
RET_UNIT_TOKENS = 1024
TOKEN_TILE = 512
MLP_CHUNK = 1024
VMEM_LIMIT = 56 * 1024 * 1024

_BF = jnp.bfloat16
_F32 = jnp.float32


def _norm_rows(x):
    mu = jnp.mean(x, axis=-1, keepdims=True)
    xc = x - mu
    var = jnp.mean(xc * xc, axis=-1, keepdims=True)
    return xc * lax.rsqrt(var + LN_EPS)


def _const_spec(shape):
    nd = len(shape)
    return pl.BlockSpec(shape, lambda *_: (0,) * nd, pipeline_mode=pl.Buffered(1))


def _granule_transpose(v):
    lane = lax.broadcasted_iota(jnp.int32, (1, 128), 1)
    v = list(v)
    for d in (4, 2, 1):
        bit = ((lane // S5_CH) & d) != 0
        nv = list(v)
        for a in range(8):
            if a & d:
                continue
            lo, hi = v[a], v[a + d]
            nv[a] = jnp.where(bit, pltpu.roll(hi, S5_CH * d, axis=1), lo)
            nv[a + d] = jnp.where(bit, hi, pltpu.roll(lo, 128 - S5_CH * d, axis=1))
        v = nv
    return v


def _tile_geometry(b, l, tm):
    spt = max(1, tm // l)
    return spt, b // spt, l * spt


def _octet_row_start(q, pb, *, nb, spt, tm):
    per_seq = (tm // spt) // 128
    return 8 * (q % per_seq) * nb + pb * spt + q // per_seq


def _proj_kernel(*refs, nb, spt, tm, n_cast):
    x_ref, mod_ref, w_ref = refs[:3]
    cast_in = refs[3:3 + n_cast]
    o_ref, xg_ref = refs[3 + n_cast:5 + n_cast]
    cast_out = refs[5 + n_cast:5 + 2 * n_cast]
    u_scr = refs[5 + 2 * n_cast]
    for src, dst in zip(cast_in, cast_out):
        dst[...] = src[...].astype(_BF)
    x = x_ref[0]
    sh = mod_ref[0, 0:1, :]
    sc = mod_ref[0, 1:2, :]
    h = (_norm_rows(x) * (1.0 + sc) + sh).astype(_BF)
    u = jnp.dot(h, w_ref[:, 4 * RET_WIDTH:], preferred_element_type=_F32)
    o_ref[0, :, 4 * RET_WIDTH:] = u.astype(_BF)
    for j in range(S5_WIDTH // 128):
        u_scr[j] = u[:, 128 * j:128 * (j + 1)]
    o_ref[0, :, 0:4 * RET_WIDTH] = jnp.dot(
        h, w_ref[:, 0:4 * RET_WIDTH], preferred_element_type=_F32).astype(_BF)
    pb = pl.program_id(1)
    for j in range(S5_WIDTH // 128):
        for hf in range(2):
            for q in range(tm // 128):
                v = [u_scr[j, pl.ds(128 * q + 8 * hf + a, 8, stride=S5_T), :] for a in range(8)]
                w = _granule_transpose(v)
                start = _octet_row_start(q, pb, nb=nb, spt=spt, tm=tm)
                for gg in range(8):
                    xg_ref[8 * j + gg, hf, pl.ds(start, 8, stride=nb), :] = w[gg]


def _projection(x, mod, mod_row, w_in_bf, cast_weights=()):
    b, l, _ = x.shape
    tm = TOKEN_TILE
    spt, nbg, lt = _tile_geometry(b, l, tm)
    rows = (tm // S5_T) * nbg
    n_steps = (lt // tm) * nbg
    cast_specs = [pl.BlockSpec((w.shape[0] // n_steps, w.shape[1]), lambda j, i: (j * nbg + i, 0))
                  for w in cast_weights]
    outs = pl.pallas_call(
        functools.partial(_proj_kernel, nb=b, spt=spt, tm=tm, n_cast=len(cast_weights)),
        grid=(lt // tm, nbg),
        in_specs=[pl.BlockSpec((1, tm, D_MODEL), lambda j, i: (i, j, 0)),
                  pl.BlockSpec((1, 6, D_MODEL), lambda j, i: (mod_row(i), 0, 0)),
                  _const_spec((D_MODEL, IN_COLS))] + cast_specs,
        out_specs=[pl.BlockSpec((1, tm, IN_COLS), lambda j, i: (i, j, 0)),
                   pl.BlockSpec((S5_GROUPS, 2, rows, 128), lambda j, i: (0, 0, j, 0))] + cast_specs,
        out_shape=[jax.ShapeDtypeStruct((nbg, lt, IN_COLS), _BF),
                   jax.ShapeDtypeStruct((S5_GROUPS, 2, (l // S5_T) * b, 128), _F32)]
                  + [jax.ShapeDtypeStruct(w.shape, _BF) for w in cast_weights],
        scratch_shapes=[pltpu.VMEM((S5_WIDTH // 128, tm, 128), _F32)],
        compiler_params=pltpu.CompilerParams(
            dimension_semantics=("arbitrary", "arbitrary"), vmem_limit_bytes=VMEM_LIMIT),
        name="projection",
    )(x.reshape(nbg, lt, D_MODEL), mod, w_in_bf, *cast_weights)
    return outs[0].reshape(b, l, IN_COLS), outs[1], tuple(outs[2:])


def _ret_kernel(*refs, chunk, n_chunks, nbb, has_s0):
    if has_s0:
        dec_ref, q_ref, k_ref, v_ref, g_ref, s0_ref, o_ref, st_ref, kv_scr, dec_scr = refs
    else:
        dec_ref, q_ref, k_ref, v_ref, g_ref, o_ref, st_ref, kv_scr, dec_scr = refs
        s0_ref = None
    c = chunk
    scale = RET_DK ** -0.5
    pos = lax.broadcasted_iota(jnp.int32, (c, 1), 0).astype(_F32)
    ri = lax.broadcasted_iota(jnp.int32, (c, c), 0)
    ci = lax.broadcasted_iota(jnp.int32, (c, c), 1)
    rel = (ri - ci).astype(_F32)

    for hd in range(RET_HEADS):
        lanes = slice(hd * RET_DK, (hd + 1) * RET_DK)

        def log_gamma(d):
            z = jnp.full((1, 1), dec_ref[d, hd], _F32)
            return jnp.minimum(z, 0.0) - jnp.log(1.0 + jnp.exp(-jnp.abs(z)))

        lg_f, lg_b = log_gamma(0), log_gamma(1)
        wide = lambda col: jnp.broadcast_to(col, (c, RET_DK))
        dec_scr[hd, 0] = wide(jnp.exp(lg_f * (c - 1.0 - pos)) * scale)
        dec_scr[hd, 1] = wide(jnp.exp(lg_b * pos) * scale)
        dec_scr[hd, 2] = wide(jnp.exp(lg_f * (pos + 1.0)))
        dec_scr[hd, 3] = wide(jnp.exp(lg_b * (c - pos)))
        dmat = (jnp.where(ri >= ci, jnp.exp(lg_f * jnp.maximum(rel, 0.0)), 0.0)
                + jnp.where(ci >= ri, jnp.exp(lg_b * jnp.maximum(-rel, 0.0)), 0.0)) * scale
        cdec_f = jnp.exp(lg_f * float(c))
        cdec_b = jnp.exp(lg_b * float(c))

        for bb in range(nbb):
            unit = bb * RET_HEADS + hd
            for n in range(n_chunks):
                kc = k_ref[bb, n * c:(n + 1) * c, lanes]
                vc = v_ref[bb, n * c:(n + 1) * c, lanes].astype(_F32)
                vcat = jnp.concatenate([(vc * dec_scr[hd, 0]).astype(_BF), (vc * dec_scr[hd, 1]).astype(_BF)], axis=1)
                kv_scr[unit, n] = lax.dot_general(kc, vcat, (((0,), (0,)), ((), ())),
                                                  preferred_element_type=_F32)
            if has_s0:
                carry_f = s0_ref[bb, 0, hd]
                carry_b = s0_ref[bb, 1, hd]
            else:
                carry_f = jnp.zeros((RET_DK, RET_DK), _F32)
                carry_b = jnp.zeros((RET_DK, RET_DK), _F32)
            for n in range(n_chunks):
                inc = kv_scr[unit, n, :, 0:RET_DK]
                kv_scr[unit, n, :, 0:RET_DK] = carry_f
                carry_f = cdec_f * carry_f + inc
            for n in range(n_chunks - 1, -1, -1):
                inc = kv_scr[unit, n, :, RET_DK:2 * RET_DK]
                kv_scr[unit, n, :, RET_DK:2 * RET_DK] = carry_b
                carry_b = cdec_b * carry_b + inc
            st_ref[bb, 0, hd] = carry_f
            st_ref[bb, 1, hd] = carry_b

            for n in range(n_chunks):
                sl = slice(n * c, (n + 1) * c)
                qc = q_ref[bb, sl, lanes]
                kc = k_ref[bb, sl, lanes]
                vc = v_ref[bb, sl, lanes]
                scores = lax.dot_general(qc, kc, (((1,), (1,)), ((), ())), preferred_element_type=_F32)
                o = jnp.dot((scores * dmat).astype(_BF), vc, preferred_element_type=_F32)
                oi = jnp.dot(qc, kv_scr[unit, n].astype(_BF), preferred_element_type=_F32)
                o = o + oi[:, 0:RET_DK] * dec_scr[hd, 2] + oi[:, RET_DK:2 * RET_DK] * dec_scr[hd, 3]
                gate = g_ref[bb, sl, lanes].astype(_F32)
                o_ref[bb, sl, lanes] = (_norm_rows(o) * (gate * jax.nn.sigmoid(gate))).astype(_BF)


def _retention(proj, ret_decay, s0):
    b, l, _ = proj.shape
    c = min(RET_CHUNK, l)
    n_chunks = l // c
    nbb = max(1, min(b, RET_UNIT_TOKENS // l))
    has_s0 = s0 is not None
    col = lambda j: pl.BlockSpec((nbb, l, RET_WIDTH), lambda i: (i, 0, j))
    st_spec = pl.BlockSpec((nbb, 2, RET_HEADS, RET_DK, RET_DK), lambda i: (i, 0, 0, 0, 0))
    in_specs = [pl.BlockSpec(memory_space=pltpu.SMEM), col(0), col(1), col(2), col(3)]
    args = [ret_decay, proj, proj, proj, proj]
    if has_s0:
        in_specs.append(st_spec)
        args.append(s0)
    return pl.pallas_call(
        functools.partial(_ret_kernel, chunk=c, n_chunks=n_chunks, nbb=nbb, has_s0=has_s0),
        grid=(b // nbb,),
        in_specs=in_specs,
        out_specs=[pl.BlockSpec((nbb, l, RET_WIDTH), lambda i: (i, 0, 0)), st_spec],
        out_shape=[jax.ShapeDtypeStruct((b, l, RET_WIDTH), _BF),
                   jax.ShapeDtypeStruct((b, 2, RET_HEADS, RET_DK, RET_DK), _F32)],
        scratch_shapes=[pltpu.VMEM((nbb * RET_HEADS, n_chunks, RET_DK, 2 * RET_DK), _F32),
                        pltpu.VMEM((RET_HEADS, 4, c, RET_DK), _F32)],
        compiler_params=pltpu.CompilerParams(
            dimension_semantics=("arbitrary",), vmem_limit_bytes=VMEM_LIMIT),
        name="retention",
    )(*args)


def _swap_halves(z):
    return pltpu.roll(z, S5_STATE, axis=z.ndim - 1)


def _s5_prep_kernel(ar_ref, ai_ref, ldt_ref, bt_ref, cp_ref, win_ref, cond_ref, wada_ref, bada_ref,
                    m_ref, wst_ref, cout_ref, apow_ref, winb_ref, mod_ref, *, ng):
    winb_ref[...] = win_ref[...].astype(_BF)
    cond = cond_ref[...]
    mod_ref[...] = jnp.dot((cond * jax.nn.sigmoid(cond)).astype(_BF), wada_ref[...].astype(_BF),
                           preferred_element_type=_F32) + bada_ref[...]
    lane = lax.broadcasted_iota(jnp.int32, (1, 2 * S5_STATE), 1)
    sgn = jnp.where(lane < S5_STATE, -1.0, 1.0).astype(_F32)
    gran = lax.broadcasted_iota(jnp.int32, (1, S5_ROW), 1) // S5_CH
    ar = ar_ref[...]
    ai = ai_ref[...]
    dt = jnp.exp(ldt_ref[...])
    mag = jnp.exp(ar * dt)
    ang = ai * dt
    pr1 = mag * jnp.cos(ang)
    pi1 = mag * jnp.sin(ang)
    pw_r = [jnp.ones_like(pr1), pr1]
    pw_i = [jnp.zeros_like(pi1), pi1]
    for _ in range(2, S5_T + 1):
        pr, pi = pw_r[-1], pw_i[-1]
        pw_r.append(pr * pr1 - pi * pi1)
        pw_i.append(pr * pi1 + pi * pr1)
    pw_is = [p * sgn for p in pw_i]
    x2 = pr1 - 1.0
    den = ar * ar + ai * ai
    coef_re = (x2 * ar + pi1 * ai) / den
    coef_im_s = ((pi1 * ar - x2 * ai) / den) * sgn
    for gi in range(ng):
        bt = bt_ref[gi]
        cp = cp_ref[gi]
        cc = cp * (-sgn)
        cps = _swap_halves(cp)
        bts = _swap_halves(bt)
        gens = []
        cout_rows = [[], []]
        for d in range(2):
            r = 2 * gi + d
            row = lambda a: a[r:r + 1, :]
            bb = row(coef_re) * bt + row(coef_im_s) * bts
            bbs = _swap_halves(bb)
            w_rows = []
            for m in range(S5_T):
                e = (S5_T - 1 - m) if d == 0 else m
                w = row(pw_r[e]) * bb + row(pw_is[e]) * bbs
                rows = slice(m * S5_CH, (m + 1) * S5_CH)
                wst_ref[gi, rows, d * 128:(d + 1) * 128] = w.astype(_BF)
                wst_ref[gi, rows, 256 + d * 128:256 + (d + 1) * 128] = _swap_halves(w).astype(_BF)
                w_rows.append(w)
            for i in range(S5_T):
                e = (i + 1) if d == 0 else (S5_T - i)
                gmat = row(pw_r[e]) * cp + row(pw_is[e]) * cps
                cout_rows[d].append(gmat * (-sgn))
            gens.append(lax.dot_general(cc, jnp.concatenate(w_rows, axis=0), (((1,), (1,)), ((), ())),
                                        preferred_element_type=_F32, precision=lax.Precision.HIGHEST))
            apow_ref[gi, 2 * d:2 * d + 1, :] = row(pw_r[S5_T])
            apow_ref[gi, 2 * d + 1:2 * d + 2, :] = row(pw_is[S5_T])
        blocks = []
        for t in range(S5_T):
            sf = (S5_CH * (t + 1)) % S5_ROW
            rf = pltpu.roll(gens[0], sf, axis=1) if sf else gens[0]
            rb = pltpu.roll(gens[1], S5_CH * t, axis=1) if t else gens[1]
            blocks.append(jnp.where(gran <= t, rf, 0.0) + jnp.where(gran >= t, rb, 0.0))
        m_ref[gi] = jnp.concatenate(blocks, axis=0).T.astype(_BF)
        cout_t = jnp.concatenate([jnp.concatenate(cout_rows[0], axis=0),
                                  jnp.concatenate(cout_rows[1], axis=0)], axis=1)
        cout_ref[gi] = cout_t.T.astype(_BF)


def _s5_operators(s5_a_re, s5_a_im, s5_log_dt, s5_b_re, s5_b_im, s5_c_re, s5_c_im, w_in, cond8, w_ada, b_ada):
    g = S5_GROUPS
    ng = S5_PREP_GROUP_TILE
    steps = g // ng
    wspec = pl.BlockSpec((w_in.shape[0] // steps, w_in.shape[1]), lambda i: (i, 0))
    n_mod = w_ada.shape[1]
    tn = n_mod // steps
    dup = lambda a: jnp.concatenate([a, a], axis=-1).transpose(1, 0, 2).reshape(2 * g, 128)
    ar2 = dup(s5_a_re)
    ai2 = dup(s5_a_im)
    ldt = s5_log_dt.T.reshape(2 * g, 1)
    btp = jnp.concatenate([s5_b_re.transpose(0, 2, 1), s5_b_im.transpose(0, 2, 1)], axis=-1)
    cpk = jnp.concatenate([s5_c_re, s5_c_im], axis=-1)
    gspec = lambda *shape: pl.BlockSpec((ng,) + shape, lambda i: (i,) + (0,) * len(shape))
    rspec = lambda w: pl.BlockSpec((2 * ng, w), lambda i: (i, 0))
    return pl.pallas_call(
        functools.partial(_s5_prep_kernel, ng=ng),
        grid=(g // ng,),
        in_specs=[rspec(128), rspec(128), rspec(1), gspec(S5_CH, 128), gspec(S5_CH, 128), wspec,
                  pl.BlockSpec((8, D_MODEL), lambda i: (0, 0)),
                  pl.BlockSpec((D_MODEL, tn), lambda i: (0, i)),
                  pl.BlockSpec((1, tn), lambda i: (0, i))],
        out_specs=[gspec(S5_ROW, S5_ROW), gspec(S5_ROW, 512), gspec(256, S5_ROW), gspec(4, 128), wspec,
                   pl.BlockSpec((8, tn), lambda i: (0, i))],
        out_shape=[jax.ShapeDtypeStruct((g, S5_ROW, S5_ROW), _BF),
                   jax.ShapeDtypeStruct((g, S5_ROW, 512), _BF),
                   jax.ShapeDtypeStruct((g, 256, S5_ROW), _BF),
                   jax.ShapeDtypeStruct((g, 4, 128), _F32),
                   jax.ShapeDtypeStruct(w_in.shape, _BF),
                   jax.ShapeDtypeStruct((8, n_mod), _F32)],
        compiler_params=pltpu.CompilerParams(dimension_semantics=("arbitrary",)),
        name="s5_operators",
    )(ar2, ai2, ldt, btp, cpk, w_in, cond8, w_ada, b_ada.reshape(1, n_mod))


def _s5_kernel(x_ref, m_ref, wst_ref, cout_ref, apow_ref, s0_ref, y_ref, fin_ref, loc_scr, prev_scr,
               *, nb, nk, ng):
    load_x = lambda gi: jnp.concatenate([x_ref[gi, 0], x_ref[gi, 1]], axis=1).astype(_BF)
    for gi in range(ng):
        loc_scr[gi] = jnp.dot(load_x(gi), wst_ref[gi], preferred_element_type=_F32)
    coef, st = [], []
    for gi in range(ng):
        coef.append([apow_ref[gi, r:r + 1, :] for r in range(4)])
        zf = s0_ref[gi, :, 0:128]
        zb = s0_ref[gi, :, 128:256]
        st.append([zf, _swap_halves(zf), zb, _swap_halves(zb)])
    for k in range(nk):
        kb = nk - 1 - k
        rows_f = slice(k * nb, (k + 1) * nb)
        rows_b = slice(kb * nb, (kb + 1) * nb)
        for gi in range(ng):
            ar_f, ai_f, ar_b, ai_b = coef[gi]
            zf, zfs, zb, zbs = st[gi]
            prev_scr[gi, rows_f, 0:128] = zf
            prev_scr[gi, rows_b, 128:256] = zb
            st[gi] = [ar_f * zf + ai_f * zfs + loc_scr[gi, rows_f, 0:128],
                      ar_f * zfs - ai_f * zf + loc_scr[gi, rows_f, 256:384],
                      ar_b * zb + ai_b * zbs + loc_scr[gi, rows_b, 128:256],
                      ar_b * zbs - ai_b * zb + loc_scr[gi, rows_b, 384:512]]
    for gi in range(ng):
        fin_ref[gi, :, 0:128] = st[gi][0]
        fin_ref[gi, :, 128:256] = st[gi][2]
        y = jnp.dot(load_x(gi), m_ref[gi], preferred_element_type=_F32)
        y = y + jnp.dot(prev_scr[gi].astype(_BF), cout_ref[gi], preferred_element_type=_F32)
        y_ref[gi, 0] = y[:, 0:128]
        y_ref[gi, 1] = y[:, 128:256]


def _s5_scan(xg, m_op, wst, cout, apow, s0p, nb, nk):
    g = S5_GROUPS
    ng = S5_GROUP_TILE
    rows = nk * nb
    gspec = lambda *shape: pl.BlockSpec((ng,) + shape, lambda i: (i,) + (0,) * len(shape))
    return pl.pallas_call(
        functools.partial(_s5_kernel, nb=nb, nk=nk, ng=ng),
        grid=(g // ng,),
        in_specs=[gspec(2, rows, 128), gspec(S5_ROW, S5_ROW), gspec(S5_ROW, 512), gspec(256, S5_ROW),
                  gspec(4, 128), gspec(nb, 256)],
        out_specs=[gspec(2, rows, 128), gspec(nb, 256)],
        out_shape=[jax.ShapeDtypeStruct((g, 2, rows, 128), _F32),
                   jax.ShapeDtypeStruct((g, nb, 256), _F32)],
        scratch_shapes=[pltpu.VMEM((ng, rows, 512), _F32), pltpu.VMEM((ng, rows, 256), _F32)],
        compiler_params=pltpu.CompilerParams(
            dimension_semantics=("arbitrary",), vmem_limit_bytes=VMEM_LIMIT),
        name="s5_scan",
    )(xg, m_op, wst, cout, apow, s0p)


def _post_kernel(x_ref, o_ref, yg_ref, u_ref, mod_ref, modp_ref, dsk_ref, wglu_ref, bglu_ref, wout_ref,
                 l1g_ref, l1b_ref, w1_ref, b1_ref, w2_ref, b2_ref, l2g_ref, l2b_ref, out_ref,
                 y_scr, h_next, x1_next, h_cur, x1_cur, *, nb, spt, tm, nbg, n_steps):
    i = pl.program_id(0)
    pb = jnp.minimum(i, n_steps - 1) % nbg
    front = functools.partial(_post_mix_front, o_ref, yg_ref, u_ref, dsk_ref, wglu_ref, bglu_ref,
                              wout_ref, y_scr, nb=nb, spt=spt, tm=tm, pb=pb)
    back = functools.partial(_post_mix_back, x_ref, mod_ref, wout_ref, l1g_ref, l1b_ref,
                             h_next, x1_next)

    @pl.when(i == 0)
    def _():
        back(*front())

    @pl.when(i > 0)
    def _():
        h_cur[...] = h_next[...]
        x1_cur[...] = x1_next[...]
        h = h_cur[...]
        up = functools.partial(_mlp_up, h, w1_ref, b1_ref)
        down = functools.partial(_mlp_down, w2_ref)
        a0 = up(0)
        a1 = up(1)
        acc = down(a0, 0)
        y, mix = front()
        a2 = up(2)
        acc = acc + down(a1, 1)
        mix = mix + jnp.dot(y, wout_ref[RET_WIDTH:, :], preferred_element_type=_F32)
        a3 = up(3)
        acc = acc + down(a2, 2)
        back(None, mix)
        acc = acc + down(a3, 3)
        f = acc + b2_ref[...]
        g2 = modp_ref[0, 5:6, :]
        out_ref[0] = _norm_rows(ALPHA * x1_cur[...] + g2 * f) * l2g_ref[...] + l2b_ref[...]


def _mlp_up(h, w1_ref, b1_ref, j):
    cols = slice(j * MLP_CHUNK, (j + 1) * MLP_CHUNK)
    a = jnp.dot(h, w1_ref[:, cols], preferred_element_type=_F32) + b1_ref[:, cols]
    return jnp.square(jnp.maximum(a, 0.0)).astype(_BF)


def _mlp_down(w2_ref, a, j):
    return jnp.dot(a, w2_ref[j * MLP_CHUNK:(j + 1) * MLP_CHUNK, :], preferred_element_type=_F32)


def _post_mix_front(o_ref, yg_ref, u_ref, dsk_ref, wglu_ref, bglu_ref, wout_ref, y_scr, *, nb, spt, tm, pb):
    mix = jnp.dot(o_ref[0], wout_ref[0:RET_WIDTH, :], preferred_element_type=_F32)
    for j in range(S5_WIDTH // 128):
        for hf in range(2):
            for q in range(tm // 128):
                start = _octet_row_start(q, pb, nb=nb, spt=spt, tm=tm)
                w = [yg_ref[8 * j + gg, hf, pl.ds(start, 8, stride=nb), :] for gg in range(8)]
                v = _granule_transpose(w)
                for a in range(8):
                    y_scr[j, pl.ds(128 * q + 8 * hf + a, 8, stride=S5_T), :] = v[a]
    y = jnp.concatenate([y_scr[j] for j in range(S5_WIDTH // 128)], axis=1)
    y = y + dsk_ref[...] * u_ref[0].astype(_F32)
    y = jax.nn.gelu(y)
    y = y * jax.nn.sigmoid(jnp.dot(y.astype(_BF), wglu_ref[...], preferred_element_type=_F32)
                           + bglu_ref[...])
    return y.astype(_BF), mix


def _post_mix_back(x_ref, mod_ref, wout_ref, l1g_ref, l1b_ref, h_out, x1_out, y, mix):
    if y is not None:
        mix = mix + jnp.dot(y, wout_ref[RET_WIDTH:, :], preferred_element_type=_F32)
    g1 = mod_ref[0, 2:3, :]
    sh2 = mod_ref[0, 3:4, :]
    sc2 = mod_ref[0, 4:5, :]
    x1 = _norm_rows(ALPHA * x_ref[0] + g1 * mix) * l1g_ref[...] + l1b_ref[...]
    h_out[...] = (_norm_rows(x1) * (1.0 + sc2) + sh2).astype(_BF)
    x1_out[...] = x1


def _post(x, o_ret, yg, proj, mod, mod_row, pw):
    b, l, _ = x.shape
    tm = min(TOKEN_TILE, l)
    spt, nbg, lt = _tile_geometry(b, l, tm)
    rows = (tm // S5_T) * nbg
    n_steps = (lt // tm) * nbg
    item_a = lambda i: jnp.minimum(i, n_steps - 1)
    item_b = lambda i: jnp.maximum(i - 1, 0)
    tok = lambda w, item, cb=0: pl.BlockSpec((1, tm, w), lambda i: (item(i) % nbg, item(i) // nbg, cb))
    modspec = lambda item: pl.BlockSpec((1, 6, D_MODEL), lambda i: (mod_row(item(i) % nbg), 0, 0))
    row = lambda n: _const_spec((1, n))
    return pl.pallas_call(
        functools.partial(_post_kernel, nb=b, spt=spt, tm=tm, nbg=nbg, n_steps=n_steps),
        grid=(n_steps + 1,),
        in_specs=[tok(D_MODEL, item_a), tok(RET_WIDTH, item_a),
                  (_const_spec((S5_GROUPS, 2, rows, 128)) if lt == tm else
                   pl.BlockSpec((S5_GROUPS, 2, rows, 128), lambda i: (0, 0, item_a(i) // nbg, 0))),
                  tok(S5_WIDTH, item_a, 4),
                  modspec(item_a), modspec(item_b),
                  row(S5_WIDTH), _const_spec((S5_WIDTH, S5_WIDTH)), row(S5_WIDTH),
                  _const_spec((D_MODEL, D_MODEL)), row(D_MODEL), row(D_MODEL),
                  _const_spec((D_MODEL, D_FF)), row(D_FF), _const_spec((D_FF, D_MODEL)), row(D_MODEL),
                  row(D_MODEL), row(D_MODEL)],
        out_specs=tok(D_MODEL, item_b),
        out_shape=jax.ShapeDtypeStruct((nbg, lt, D_MODEL), _F32),
        scratch_shapes=[pltpu.VMEM((S5_WIDTH // 128, tm, 128), _F32),
                        pltpu.VMEM((tm, D_MODEL), _BF), pltpu.VMEM((tm, D_MODEL), _F32),
                        pltpu.VMEM((tm, D_MODEL), _BF), pltpu.VMEM((tm, D_MODEL), _F32)],
        compiler_params=pltpu.CompilerParams(
            dimension_semantics=("arbitrary",), vmem_limit_bytes=VMEM_LIMIT),
        name="post",
    )(x.reshape(nbg, lt, D_MODEL), o_ret.reshape(nbg, lt, RET_WIDTH), yg,
      proj.reshape(nbg, lt, IN_COLS), mod, mod, *pw).reshape(b, l, D_MODEL)


def _layer_pass(x, mod, mod_row, ret_s0, s5_s0_re, s5_s0_im, w_in_bf, ret_decay, s5_ops, post_w):
    b, l, _ = x.shape
    nk = l // S5_T
    g = S5_GROUPS
    mats = ("w_glu", "w_out", "w_ff1", "w_ff2")
    to_cast = [k for k in mats if post_w[k].dtype != _BF]
    proj, xg, cast = _projection(x, mod, mod_row, w_in_bf, tuple(post_w[k] for k in to_cast))
    post_w = dict(post_w, **dict(zip(to_cast, cast)))
    o_ret, ret_state = _retention(proj, ret_decay, ret_s0)
    if s5_s0_re is None:
        s0p = jnp.zeros((g, b, 256), _F32)
    else:
        s0p = jnp.concatenate([s5_s0_re, s5_s0_im], axis=-1)
        s0p = s0p.transpose(2, 0, 1, 3).reshape(g, b, 256)
    yg, fin = _s5_scan(xg, *s5_ops, s0p, b, nk)
    fin = fin.reshape(g, b, 2, 2, S5_STATE).transpose(1, 2, 0, 3, 4)
    pw = tuple(post_w[k] for k in ("s5_d", "w_glu", "b_glu", "w_out", "ln1_g", "ln1_b", "w_ff1", "b_ff1",
                                   "w_ff2", "b_ff2", "ln2_g", "ln2_b"))
    out = _post(x, o_ret, yg, proj, mod, mod_row, pw)
    return out, ret_state, fin[:, :, :, 0, :], fin[:, :, :, 1, :], post_w


def kernel(x_prompt, x_sample, state_ret, state_s5_re, state_s5_im, c, c_ctx, w_ada, b_ada, w_in,
           ret_decay, s5_a_re, s5_a_im, s5_log_dt, s5_b_re, s5_b_im, s5_c_re, s5_c_im, s5_d, w_glu,
           b_glu, w_out, ln1_g, ln1_b, w_ff1, b_ff1, w_ff2, b_ff2, ln2_g, ln2_b):
    depth = w_ada.shape[0]
    bs = x_sample.shape[0]
    y_p, y_s = x_prompt, x_sample
    rets, s5rs, s5is = [], [], []
    for layer in range(depth):
        cond8 = jnp.concatenate(
            [c_ctx[None, :], c, jnp.zeros((8 - 1 - bs, D_MODEL), _F32)], axis=0)
        *s5_ops, w_in_bf, mod = _s5_operators(
            s5_a_re[layer], s5_a_im[layer], s5_log_dt[layer], s5_b_re[layer], s5_b_im[layer],
            s5_c_re[layer], s5_c_im[layer], w_in[layer], cond8, w_ada[layer], b_ada[layer])
        mod = mod.reshape(8, 6, D_MODEL)
        r2 = lambda a: a[layer].reshape(1, -1)
        post_w = dict(s5_d=r2(s5_d), w_glu=w_glu[layer], b_glu=r2(b_glu), w_out=w_out[layer],
                      ln1_g=r2(ln1_g), ln1_b=r2(ln1_b), w_ff1=w_ff1[layer], b_ff1=r2(b_ff1),
                      w_ff2=w_ff2[layer], b_ff2=r2(b_ff2), ln2_g=r2(ln2_g), ln2_b=r2(ln2_b))
        y_p, r_st, s5_re, s5_im, post_w = _layer_pass(
            y_p, mod, lambda i: 0, None, None, None, w_in_bf, ret_decay[layer], s5_ops, post_w)
        rets.append(r_st)
        s5rs.append(s5_re)
        s5is.append(s5_im)
        y_s, _, _, _, _ = _layer_pass(
            y_s, mod, lambda i: i + 1, state_ret[:, layer], state_s5_re[:, layer],
            state_s5_im[:, layer], w_in_bf, ret_decay[layer], s5_ops, post_w)
    return (y_p, y_s, jnp.stack(rets, axis=1), jnp.stack(s5rs, axis=1), jnp.stack(s5is, axis=1))
```

```python
import functools

import jax
import jax.numpy as jnp
from jax import lax
from jax.experimental import pallas as pl
from jax.experimental.pallas import tpu as pltpu

D_MODEL = 1024
RET_HEADS = 4
RET_DK = 128
RET_WIDTH = RET_HEADS * RET_DK
S5_CH = 16
S5_GROUPS = 32
S5_STATE = 64
S5_WIDTH = S5_GROUPS * S5_CH
D_FF = 4 * D_MODEL
IN_COLS = 4 * RET_WIDTH + S5_WIDTH
ALPHA = 2.0 ** 0.25
LN_EPS = 1e-5

S5_T = 16
S5_ROW = S5_T * S5_CH
S5_PREP_GROUP_TILE = 4
S5_GROUP_TILE = 8
RET_CHUNK = 256
RET_UNIT_TOKENS = 1024
TOKEN_TILE = 512
MLP_CHUNK = 1024
VMEM_LIMIT = 56 * 1024 * 1024

_BF = jnp.bfloat16
_F32 = jnp.float32


def _norm_rows(x):
    mu = jnp.mean(x, axis=-1, keepdims=True)
    xc = x - mu
    var = jnp.mean(xc * xc, axis=-1, keepdims=True)
    return xc * lax.rsqrt(var + LN_EPS)


def _const_spec(shape):
    nd = len(shape)
    return pl.BlockSpec(shape, lambda *_: (0,) * nd, pipeline_mode=pl.Buffered(1))


def _granule_transpose(v):
    lane = lax.broadcasted_iota(jnp.int32, (1, 128), 1)
    v = list(v)
    for d in (4, 2, 1):
        bit = ((lane // S5_CH) & d) != 0
        nv = list(v)
        for a in range(8):
            if a & d:
                continue
            lo, hi = v[a], v[a + d]
            nv[a] = jnp.where(bit, pltpu.roll(hi, S5_CH * d, axis=1), lo)
            nv[a + d] = jnp.where(bit, hi, pltpu.roll(lo, 128 - S5_CH * d, axis=1))
        v = nv
    return v


def _tile_geometry(b, l, tm):
    spt = max(1, tm // l)
    return spt, b // spt, l * spt


def _octet_row_start(q, pb, *, nb, spt, tm):
    per_seq = (tm // spt) // 128
    return 8 * (q % per_seq) * nb + pb * spt + q // per_seq


def _proj_kernel(*refs, nb, spt, tm, n_cast):
    x_ref, mod_ref, w_ref = refs[:3]
    cast_in = refs[3:3 + n_cast]
    o_ref, xg_ref = refs[3 + n_cast:5 + n_cast]
    cast_out = refs[5 + n_cast:5 + 2 * n_cast]
    u_scr = refs[5 + 2 * n_cast]
    for src, dst in zip(cast_in, cast_out):
        dst[...] = src[...].astype(_BF)
    x = x_ref[0]
    sh = mod_ref[0, 0:1, :]
    sc = mod_ref[0, 1:2, :]
    h = (_norm_rows(x) * (1.0 + sc) + sh).astype(_BF)
    u = jnp.dot(h, w_ref[:, 4 * RET_WIDTH:], preferred_element_type=_F32)
    o_ref[0, :, 4 * RET_WIDTH:] = u.astype(_BF)
    for j in range(S5_WIDTH // 128):
        u_scr[j] = u[:, 128 * j:128 * (j + 1)]
    o_ref[0, :, 0:4 * RET_WIDTH] = jnp.dot(
        h, w_ref[:, 0:4 * RET_WIDTH], preferred_element_type=_F32).astype(_BF)
    pb = pl.program_id(1)
    for j in range(S5_WIDTH // 128):
        for hf in range(2):
            for q in range(tm // 128):
                v = [u_scr[j, pl.ds(128 * q + 8 * hf + a, 8, stride=S5_T), :] for a in range(8)]
                w = _granule_transpose(v)
                start = _octet_row_start(q, pb, nb=nb, spt=spt, tm=tm)
                for gg in range(8):
                    xg_ref[8 * j + gg, hf, pl.ds(start, 8, stride=nb), :] = w[gg]


def _projection(x, mod, mod_row, w_in_bf, cast_weights=()):
    b, l, _ = x.shape
    tm = TOKEN_TILE
    spt, nbg, lt = _tile_geometry(b, l, tm)
    rows = (tm // S5_T) * nbg
    n_steps = (lt // tm) * nbg
    cast_specs = [pl.BlockSpec((w.shape[0] // n_steps, w.shape[1]), lambda j, i: (j * nbg + i, 0))
                  for w in cast_weights]
    outs = pl.pallas_call(
        functools.partial(_proj_kernel, nb=b, spt=spt, tm=tm, n_cast=len(cast_weights)),
        grid=(lt // tm, nbg),
        in_specs=[pl.BlockSpec((1, tm, D_MODEL), lambda j, i: (i, j, 0)),
                  pl.BlockSpec((1, 6, D_MODEL), lambda j, i: (mod_row(i), 0, 0)),
                  _const_spec((D_MODEL, IN_COLS))] + cast_specs,
        out_specs=[pl.BlockSpec((1, tm, IN_COLS), lambda j, i: (i, j, 0)),
                   pl.BlockSpec((S5_GROUPS, 2, rows, 128), lambda j, i: (0, 0, j, 0))] + cast_specs,
        out_shape=[jax.ShapeDtypeStruct((nbg, lt, IN_COLS), _BF),
                   jax.ShapeDtypeStruct((S5_GROUPS, 2, (l // S5_T) * b, 128), _F32)]
                  + [jax.ShapeDtypeStruct(w.shape, _BF) for w in cast_weights],
        scratch_shapes=[pltpu.VMEM((S5_WIDTH // 128, tm, 128), _F32)],
        compiler_params=pltpu.CompilerParams(
            dimension_semantics=("arbitrary", "arbitrary"), vmem_limit_bytes=VMEM_LIMIT),
        name="projection",
    )(x.reshape(nbg, lt, D_MODEL), mod, w_in_bf, *cast_weights)
    return outs[0].reshape(b, l, IN_COLS), outs[1], tuple(outs[2:])


def _ret_kernel(*refs, chunk, n_chunks, nbb, has_s0):
    if has_s0:
        dec_ref, q_ref, k_ref, v_ref, g_ref, s0_ref, o_ref, st_ref, kv_scr, dec_scr = refs
    else:
        dec_ref, q_ref, k_ref, v_ref, g_ref, o_ref, st_ref, kv_scr, dec_scr = refs
        s0_ref = None
    c = chunk
    scale = RET_DK ** -0.5
    pos = lax.broadcasted_iota(jnp.int32, (c, 1), 0).astype(_F32)
    ri = lax.broadcasted_iota(jnp.int32, (c, c), 0)
    ci = lax.broadcasted_iota(jnp.int32, (c, c), 1)
    rel = (ri - ci).astype(_F32)

    for hd in range(RET_HEADS):
        lanes = slice(hd * RET_DK, (hd + 1) * RET_DK)

        def log_gamma(d):
            z = jnp.full((1, 1), dec_ref[d, hd], _F32)
            return jnp.minimum(z, 0.0) - jnp.log(1.0 + jnp.exp(-jnp.abs(z)))

        lg_f, lg_b = log_gamma(0), log_gamma(1)
        wide = lambda col: jnp.broadcast_to(col, (c, RET_DK))
        dec_scr[hd, 0] = wide(jnp.exp(lg_f * (c - 1.0 - pos)) * scale)
        dec_scr[hd, 1] = wide(jnp.exp(lg_b * pos) * scale)
        dec_scr[hd, 2] = wide(jnp.exp(lg_f * (pos + 1.0)))
        dec_scr[hd, 3] = wide(jnp.exp(lg_b * (c - pos)))
        dmat = (jnp.where(ri >= ci, jnp.exp(lg_f * jnp.maximum(rel, 0.0)), 0.0)
                + jnp.where(ci >= ri, jnp.exp(lg_b * jnp.maximum(-rel, 0.0)), 0.0)) * scale
        cdec_f = jnp.exp(lg_f * float(c))
        cdec_b = jnp.exp(lg_b * float(c))

        for bb in range(nbb):
            unit = bb * RET_HEADS + hd
            for n in range(n_chunks):
                kc = k_ref[bb, n * c:(n + 1) * c, lanes]
                vc = v_ref[bb, n * c:(n + 1) * c, lanes].astype(_F32)
                vcat = jnp.concatenate([(vc * dec_scr[hd, 0]).astype(_BF), (vc * dec_scr[hd, 1]).astype(_BF)], axis=1)
                kv_scr[unit, n] = lax.dot_general(kc, vcat, (((0,), (0,)), ((), ())),
                                                  preferred_element_type=_F32)
            if has_s0:
                carry_f = s0_ref[bb, 0, hd]
                carry_b = s0_ref[bb, 1, hd]
            else:
                carry_f = jnp.zeros((RET_DK, RET_DK), _F32)
                carry_b = jnp.zeros((RET_DK, RET_DK), _F32)
            for n in range(n_chunks):
                inc = kv_scr[unit, n, :, 0:RET_DK]
                kv_scr[unit, n, :, 0:RET_DK] = carry_f
                carry_f = cdec_f * carry_f + inc
            for n in range(n_chunks - 1, -1, -1):
                inc = kv_scr[unit, n, :, RET_DK:2 * RET_DK]
                kv_scr[unit, n, :, RET_DK:2 * RET_DK] = carry_b
                carry_b = cdec_b * carry_b + inc
            st_ref[bb, 0, hd] = carry_f
            st_ref[bb, 1, hd] = carry_b

            for n in range(n_chunks):
                sl = slice(n * c, (n + 1) * c)
                qc = q_ref[bb, sl, lanes]
                kc = k_ref[bb, sl, lanes]
                vc = v_ref[bb, sl, lanes]
                scores = lax.dot_general(qc, kc, (((1,), (1,)), ((), ())), preferred_element_type=_F32)
                o = jnp.dot((scores * dmat).astype(_BF), vc, preferred_element_type=_F32)
                oi = jnp.dot(qc, kv_scr[unit, n].astype(_BF), preferred_element_type=_F32)
                o = o + oi[:, 0:RET_DK] * dec_scr[hd, 2] + oi[:, RET_DK:2 * RET_DK] * dec_scr[hd, 3]
                gate = g_ref[bb, sl, lanes].astype(_F32)
                o_ref[bb, sl, lanes] = (_norm_rows(o) * (gate * jax.nn.sigmoid(gate))).astype(_BF)


def _retention(proj, ret_decay, s0):
    b, l, _ = proj.shape
    c = min(RET_CHUNK, l)
    n_chunks = l // c
    nbb = max(1, min(b, RET_UNIT_TOKENS // l))
    has_s0 = s0 is not None
    col = lambda j: pl.BlockSpec((nbb, l, RET_WIDTH), lambda i: (i, 0, j))
    st_spec = pl.BlockSpec((nbb, 2, RET_HEADS, RET_DK, RET_DK), lambda i: (i, 0, 0, 0, 0))
    in_specs = [pl.BlockSpec(memory_space=pltpu.SMEM), col(0), col(1), col(2), col(3)]
    args = [ret_decay, proj, proj, proj, proj]
    if has_s0:
        in_specs.append(st_spec)
        args.append(s0)
    return pl.pallas_call(
        functools.partial(_ret_kernel, chunk=c, n_chunks=n_chunks, nbb=nbb, has_s0=has_s0),
        grid=(b // nbb,),
        in_specs=in_specs,
        out_specs=[pl.BlockSpec((nbb, l, RET_WIDTH), lambda i: (i, 0, 0)), st_spec],
        out_shape=[jax.ShapeDtypeStruct((b, l, RET_WIDTH), _BF),
                   jax.ShapeDtypeStruct((b, 2, RET_HEADS, RET_DK, RET_DK), _F32)],
        scratch_shapes=[pltpu.VMEM((nbb * RET_HEADS, n_chunks, RET_DK, 2 * RET_DK), _F32),
                        pltpu.VMEM((RET_HEADS, 4, c, RET_DK), _F32)],
        compiler_params=pltpu.CompilerParams(
            dimension_semantics=("arbitrary",), vmem_limit_bytes=VMEM_LIMIT),
        name="retention",
    )(*args)


def _swap_halves(z):
    return pltpu.roll(z, S5_STATE, axis=z.ndim - 1)


def _s5_prep_kernel(ar_ref, ai_ref, ldt_ref, bt_ref, cp_ref, win_ref, cond_ref, wada_ref, bada_ref,
                    m_ref, wst_ref, cout_ref, apow_ref, winb_ref, mod_ref, *, ng):
    winb_ref[...] = win_ref[...].astype(_BF)
    cond = cond_ref[...]
    mod_ref[...] = jnp.dot((cond * jax.nn.sigmoid(cond)).astype(_BF), wada_ref[...].astype(_BF),
                           preferred_element_type=_F32) + bada_ref[...]
    lane = lax.broadcasted_iota(jnp.int32, (1, 2 * S5_STATE), 1)
    sgn = jnp.where(lane < S5_STATE, -1.0, 1.0).astype(_F32)
    gran = lax.broadcasted_iota(jnp.int32, (1, S5_ROW), 1) // S5_CH
    ar = ar_ref[...]
    ai = ai_ref[...]
    dt = jnp.exp(ldt_ref[...])
    mag = jnp.exp(ar * dt)
    ang = ai * dt
    pr1 = mag * jnp.cos(ang)
    pi1 = mag * jnp.sin(ang)
    pw_r = [jnp.ones_like(pr1), pr1]
    pw_i = [jnp.zeros_like(pi1), pi1]
    for _ in range(2, S5_T + 1):
        pr, pi = pw_r[-1], pw_i[-1]
        pw_r.append(pr * pr1 - pi * pi1)
        pw_i.append(pr * pi1 + pi * pr1)
    pw_is = [p * sgn for p in pw_i]
    x2 = pr1 - 1.0
    den = ar * ar + ai * ai
    coef_re = (x2 * ar + pi1 * ai) / den
    coef_im_s = ((pi1 * ar - x2 * ai) / den) * sgn
    for gi in range(ng):
        bt = bt_ref[gi]
        cp = cp_ref[gi]
        cc = cp * (-sgn)
        cps = _swap_halves(cp)
        bts = _swap_halves(bt)
        gens = []
        cout_rows = [[], []]
        for d in range(2):
            r = 2 * gi + d
            row = lambda a: a[r:r + 1, :]
            bb = row(coef_re) * bt + row(coef_im_s) * bts
            bbs = _swap_halves(bb)
            w_rows = []
            for m in range(S5_T):
                e = (S5_T - 1 - m) if d == 0 else m
                w = row(pw_r[e]) * bb + row(pw_is[e]) * bbs
                rows = slice(m * S5_CH, (m + 1) * S5_CH)
                wst_ref[gi, rows, d * 128:(d + 1) * 128] = w.astype(_BF)
                wst_ref[gi, rows, 256 + d * 128:256 + (d + 1) * 128] = _swap_halves(w).astype(_BF)
                w_rows.append(w)
            for i in range(S5_T):
                e = (i + 1) if d == 0 else (S5_T - i)
                gmat = row(pw_r[e]) * cp + row(pw_is[e]) * cps
                cout_rows[d].append(gmat * (-sgn))
            gens.append(lax.dot_general(cc, jnp.concatenate(w_rows, axis=0), (((1,), (1,)), ((), ())),
                                        preferred_element_type=_F32, precision=lax.Precision.HIGHEST))
            apow_ref[gi, 2 * d:2 * d + 1, :] = row(pw_r[S5_T])
            apow_ref[gi, 2 * d + 1:2 * d + 2, :] = row(pw_is[S5_T])
        blocks = []
        for t in range(S5_T):
            sf = (S5_CH * (t + 1)) % S5_ROW
            rf = pltpu.roll(gens[0], sf, axis=1) if sf else gens[0]
            rb = pltpu.roll(gens[1], S5_CH * t, axis=1) if t else gens[1]
            blocks.append(jnp.where(gran <= t, rf, 0.0) + jnp.where(gran >= t, rb, 0.0))
        m_ref[gi] = jnp.concatenate(blocks, axis=0).T.astype(_BF)
        cout_t = jnp.concatenate([jnp.concatenate(cout_rows[0], axis=0),
                                  jnp.concatenate(cout_rows[1], axis=0)], axis=1)
        cout_ref[gi] = cout_t.T.astype(_BF)


def _s5_operators(s5_a_re, s5_a_im, s5_log_dt, s5_b_re, s5_b_im, s5_c_re, s5_c_im, w_in, cond8, w_ada, b_ada):
    g = S5_GROUPS
    ng = S5_PREP_GROUP_TILE
    steps = g // ng
    wspec = pl.BlockSpec((w_in.shape[0] // steps, w_in.shape[1]), lambda i: (i, 0))
    n_mod = w_ada.shape[1]
    tn = n_mod // steps
    dup = lambda a: jnp.concatenate([a, a], axis=-1).transpose(1, 0, 2).reshape(2 * g, 128)
    ar2 = dup(s5_a_re)
    ai2 = dup(s5_a_im)
    ldt = s5_log_dt.T.reshape(2 * g, 1)
    btp = jnp.concatenate([s5_b_re.transpose(0, 2, 1), s5_b_im.transpose(0, 2, 1)], axis=-1)
    cpk = jnp.concatenate([s5_c_re, s5_c_im], axis=-1)
    gspec = lambda *shape: pl.BlockSpec((ng,) + shape, lambda i: (i,) + (0,) * len(shape))
    rspec = lambda w: pl.BlockSpec((2 * ng, w), lambda i: (i, 0))
    return pl.pallas_call(
        functools.partial(_s5_prep_kernel, ng=ng),
        grid=(g // ng,),
        in_specs=[rspec(128), rspec(128), rspec(1), gspec(S5_CH, 128), gspec(S5_CH, 128), wspec,
                  pl.BlockSpec((8, D_MODEL), lambda i: (0, 0)),
                  pl.BlockSpec((D_MODEL, tn), lambda i: (0, i)),
                  pl.BlockSpec((1, tn), lambda i: (0, i))],
        out_specs=[gspec(S5_ROW, S5_ROW), gspec(S5_ROW, 512), gspec(256, S5_ROW), gspec(4, 128), wspec,
                   pl.BlockSpec((8, tn), lambda i: (0, i))],
        out_shape=[jax.ShapeDtypeStruct((g, S5_ROW, S5_ROW), _BF),
                   jax.ShapeDtypeStruct((g, S5_ROW, 512), _BF),
                   jax.ShapeDtypeStruct((g, 256, S5_ROW), _BF),
                   jax.ShapeDtypeStruct((g, 4, 128), _F32),
                   jax.ShapeDtypeStruct(w_in.shape, _BF),
                   jax.ShapeDtypeStruct((8, n_mod), _F32)],
        compiler_params=pltpu.CompilerParams(dimension_semantics=("arbitrary",)),
        name="s5_operators",
    )(ar2, ai2, ldt, btp, cpk, w_in, cond8, w_ada, b_ada.reshape(1, n_mod))


def _s5_kernel(*refs, segs, ng):
    n = len(segs)
    x_refs = refs[0:n]
    m_ref, wst_ref, cout_ref, apow_ref = refs[n:n + 4]
    s0_refs = refs[n + 4:2 * n + 4]
    y_refs = refs[2 * n + 4:3 * n + 4]
    fin_refs = refs[3 * n + 4:4 * n + 4]
    scr = refs[4 * n + 4:]
    coef = [[apow_ref[gi, r:r + 1, :] for r in range(4)] for gi in range(ng)]
    for si, (nb, nk) in enumerate(segs):
        x_ref, s0_ref, y_ref, fin_ref = x_refs[si], s0_refs[si], y_refs[si], fin_refs[si]
        loc_scr, prev_scr = scr[2 * si], scr[2 * si + 1]
        load_x = lambda gi: jnp.concatenate([x_ref[gi, 0], x_ref[gi, 1]], axis=1).astype(_BF)
        for gi in range(ng):
            loc_scr[gi] = jnp.dot(load_x(gi), wst_ref[gi], preferred_element_type=_F32)
        st = []
        for gi in range(ng):
            zf = s0_ref[gi, :, 0:128]
            zb = s0_ref[gi, :, 128:256]
            st.append([zf, _swap_halves(zf), zb, _swap_halves(zb)])
        for k in range(nk):
            kb = nk - 1 - k
            rows_f = slice(k * nb, (k + 1) * nb)
            rows_b = slice(kb * nb, (kb + 1) * nb)
            for gi in range(ng):
                ar_f, ai_f, ar_b, ai_b = coef[gi]
                zf, zfs, zb, zbs = st[gi]
                prev_scr[gi, rows_f, 0:128] = zf
                prev_scr[gi, rows_b, 128:256] = zb
                st[gi] = [ar_f * zf + ai_f * zfs + loc_scr[gi, rows_f, 0:128],
                          ar_f * zfs - ai_f * zf + loc_scr[gi, rows_f, 256:384],
                          ar_b * zb + ai_b * zbs + loc_scr[gi, rows_b, 128:256],
                          ar_b * zbs - ai_b * zb + loc_scr[gi, rows_b, 384:512]]
        for gi in range(ng):
            fin_ref[gi, :, 0:128] = st[gi][0]
            fin_ref[gi, :, 128:256] = st[gi][2]
            y = jnp.dot(load_x(gi), m_ref[gi], preferred_element_type=_F32)
            y = y + jnp.dot(prev_scr[gi].astype(_BF), cout_ref[gi], preferred_element_type=_F32)
            y_ref[gi, 0] = y[:, 0:128]
            y_ref[gi, 1] = y[:, 128:256]


def _s5_scan(xgs, m_op, wst, cout, apow, s0ps, segs):
    g = S5_GROUPS
    ng = S5_GROUP_TILE
    n = len(segs)
    gspec = lambda *shape: pl.BlockSpec((ng,) + shape, lambda i: (i,) + (0,) * len(shape))
    rows = [nb * nk for nb, nk in segs]
    outs = pl.pallas_call(
        functools.partial(_s5_kernel, segs=tuple(segs), ng=ng),
        grid=(g // ng,),
        in_specs=[gspec(2, r, 128) for r in rows]
                 + [gspec(S5_ROW, S5_ROW), gspec(S5_ROW, 512), gspec(256, S5_ROW), gspec(4, 128)]
                 + [gspec(nb, 256) for nb, _ in segs],
        out_specs=[gspec(2, r, 128) for r in rows] + [gspec(nb, 256) for nb, _ in segs],
        out_shape=[jax.ShapeDtypeStruct((g, 2, r, 128), _F32) for r in rows]
                  + [jax.ShapeDtypeStruct((g, nb, 256), _F32) for nb, _ in segs],
        scratch_shapes=[buf for r in rows
                        for buf in (pltpu.VMEM((ng, r, 512), _F32), pltpu.VMEM((ng, r, 256), _F32))],
        compiler_params=pltpu.CompilerParams(
            dimension_semantics=("arbitrary",), vmem_limit_bytes=VMEM_LIMIT),
        name="s5_scan",
    )(*xgs, m_op, wst, cout, apow, *s0ps)
    return outs[:n], outs[n:]


def _post_kernel(x_ref, o_ref, yg_ref, u_ref, mod_ref, modp_ref, dsk_ref, wglu_ref, bglu_ref, wout_ref,
                 l1g_ref, l1b_ref, w1_ref, b1_ref, w2_ref, b2_ref, l2g_ref, l2b_ref, out_ref,
                 y_scr, h_next, x1_next, h_cur, x1_cur, *, nb, spt, tm, nbg, n_steps):
    i = pl.program_id(0)
    pb = jnp.minimum(i, n_steps - 1) % nbg
    front = functools.partial(_post_mix_front, o_ref, yg_ref, u_ref, dsk_ref, wglu_ref, bglu_ref,
                              wout_ref, y_scr, nb=nb, spt=spt, tm=tm, pb=pb)
    back = functools.partial(_post_mix_back, x_ref, mod_ref, wout_ref, l1g_ref, l1b_ref,
                             h_next, x1_next)

    @pl.when(i == 0)
    def _():
        back(*front())

    @pl.when(i > 0)
    def _():
        h_cur[...] = h_next[...]
        x1_cur[...] = x1_next[...]
        h = h_cur[...]
        up = functools.partial(_mlp_up, h, w1_ref, b1_ref)
        down = functools.partial(_mlp_down, w2_ref)
        a0 = up(0)
        a1 = up(1)
        acc = down(a0, 0)
        y, mix = front()
        a2 = up(2)
        acc = acc + down(a1, 1)
        mix = mix + jnp.dot(y, wout_ref[RET_WIDTH:, :], preferred_element_type=_F32)
        a3 = up(3)
        acc = acc + down(a2, 2)
        back(None, mix)
        acc = acc + down(a3, 3)
        f = acc + b2_ref[...]
        g2 = modp_ref[0, 5:6, :]
        out_ref[0] = _norm_rows(ALPHA * x1_cur[...] + g2 * f) * l2g_ref[...] + l2b_ref[...]


def _mlp_up(h, w1_ref, b1_ref, j):
    cols = slice(j * MLP_CHUNK, (j + 1) * MLP_CHUNK)
    a = jnp.dot(h, w1_ref[:, cols], preferred_element_type=_F32) + b1_ref[:, cols]
    return jnp.square(jnp.maximum(a, 0.0)).astype(_BF)


def _mlp_down(w2_ref, a, j):
    return jnp.dot(a, w2_ref[j * MLP_CHUNK:(j + 1) * MLP_CHUNK, :], preferred_element_type=_F32)


def _post_mix_front(o_ref, yg_ref, u_ref, dsk_ref, wglu_ref, bglu_ref, wout_ref, y_scr, *, nb, spt, tm, pb):
    mix = jnp.dot(o_ref[0], wout_ref[0:RET_WIDTH, :], preferred_element_type=_F32)
    for j in range(S5_WIDTH // 128):
        for hf in range(2):
            for q in range(tm // 128):
                start = _octet_row_start(q, pb, nb=nb, spt=spt, tm=tm)
                w = [yg_ref[8 * j + gg, hf, pl.ds(start, 8, stride=nb), :] for gg in range(8)]
                v = _granule_transpose(w)
                for a in range(8):
                    y_scr[j, pl.ds(128 * q + 8 * hf + a, 8, stride=S5_T), :] = v[a]
    y = jnp.concatenate([y_scr[j] for j in range(S5_WIDTH // 128)], axis=1)
    y = y + dsk_ref[...] * u_ref[0].astype(_F32)
    y = jax.nn.gelu(y)
    y = y * jax.nn.sigmoid(jnp.dot(y.astype(_BF), wglu_ref[...], preferred_element_type=_F32)
                           + bglu_ref[...])
    return y.astype(_BF), mix


def _post_mix_back(x_ref, mod_ref, wout_ref, l1g_ref, l1b_ref, h_out, x1_out, y, mix):
    if y is not None:
        mix = mix + jnp.dot(y, wout_ref[RET_WIDTH:, :], preferred_element_type=_F32)
    g1 = mod_ref[0, 2:3, :]
    sh2 = mod_ref[0, 3:4, :]
    sc2 = mod_ref[0, 4:5, :]
    x1 = _norm_rows(ALPHA * x_ref[0] + g1 * mix) * l1g_ref[...] + l1b_ref[...]
    h_out[...] = (_norm_rows(x1) * (1.0 + sc2) + sh2).astype(_BF)
    x1_out[...] = x1


def _post(x, o_ret, yg, proj, mod, mod_row, pw):
    b, l, _ = x.shape
    tm = min(TOKEN_TILE, l)
    spt, nbg, lt = _tile_geometry(b, l, tm)
    rows = (tm // S5_T) * nbg
    n_steps = (lt // tm) * nbg
    item_a = lambda i: jnp.minimum(i, n_steps - 1)
    item_b = lambda i: jnp.maximum(i - 1, 0)
    tok = lambda w, item, cb=0: pl.BlockSpec((1, tm, w), lambda i: (item(i) % nbg, item(i) // nbg, cb))
    modspec = lambda item: pl.BlockSpec((1, 6, D_MODEL), lambda i: (mod_row(item(i) % nbg), 0, 0))
    row = lambda n: _const_spec((1, n))
    return pl.pallas_call(
        functools.partial(_post_kernel, nb=b, spt=spt, tm=tm, nbg=nbg, n_steps=n_steps),
        grid=(n_steps + 1,),
        in_specs=[tok(D_MODEL, item_a), tok(RET_WIDTH, item_a),
                  (_const_spec((S5_GROUPS, 2, rows, 128)) if lt == tm else
                   pl.BlockSpec((S5_GROUPS, 2, rows, 128), lambda i: (0, 0, item_a(i) // nbg, 0))),
                  tok(S5_WIDTH, item_a, 4),
                  modspec(item_a), modspec(item_b),
                  row(S5_WIDTH), _const_spec((S5_WIDTH, S5_WIDTH)), row(S5_WIDTH),
                  _const_spec((D_MODEL, D_MODEL)), row(D_MODEL), row(D_MODEL),
                  _const_spec((D_MODEL, D_FF)), row(D_FF), _const_spec((D_FF, D_MODEL)), row(D_MODEL),
                  row(D_MODEL), row(D_MODEL)],
        out_specs=tok(D_MODEL, item_b),
        out_shape=jax.ShapeDtypeStruct((nbg, lt, D_MODEL), _F32),
        scratch_shapes=[pltpu.VMEM((S5_WIDTH // 128, tm, 128), _F32),
                        pltpu.VMEM((tm, D_MODEL), _BF), pltpu.VMEM((tm, D_MODEL), _F32),
                        pltpu.VMEM((tm, D_MODEL), _BF), pltpu.VMEM((tm, D_MODEL), _F32)],
        compiler_params=pltpu.CompilerParams(
            dimension_semantics=("arbitrary",), vmem_limit_bytes=VMEM_LIMIT),
        name="post",
    )(x.reshape(nbg, lt, D_MODEL), o_ret.reshape(nbg, lt, RET_WIDTH), yg,
      proj.reshape(nbg, lt, IN_COLS), mod, mod, *pw).reshape(b, l, D_MODEL)


def _pack_s5_state(b, s5_s0_re, s5_s0_im):
    g = S5_GROUPS
    if s5_s0_re is None:
        return jnp.zeros((g, b, 256), _F32)
    s0p = jnp.concatenate([s5_s0_re, s5_s0_im], axis=-1)
    return s0p.transpose(2, 0, 1, 3).reshape(g, b, 256)


def kernel(x_prompt, x_sample, state_ret, state_s5_re, state_s5_im, c, c_ctx, w_ada, b_ada, w_in,
           ret_decay, s5_a_re, s5_a_im, s5_log_dt, s5_b_re, s5_b_im, s5_c_re, s5_c_im, s5_d, w_glu,
           b_glu, w_out, ln1_g, ln1_b, w_ff1, b_ff1, w_ff2, b_ff2, ln2_g, ln2_b):
    depth = w_ada.shape[0]
    bs = x_sample.shape[0]
    g = S5_GROUPS
    y_p, y_s = x_prompt, x_sample
    rets, s5rs, s5is = [], [], []
    for layer in range(depth):
        cond8 = jnp.concatenate(
            [c_ctx[None, :], c, jnp.zeros((8 - 1 - bs, D_MODEL), _F32)], axis=0)
        *s5_ops, w_in_bf, mod = _s5_operators(
            s5_a_re[layer], s5_a_im[layer], s5_log_dt[layer], s5_b_re[layer], s5_b_im[layer],
            s5_c_re[layer], s5_c_im[layer], w_in[layer], cond8, w_ada[layer], b_ada[layer])
        mod = mod.reshape(8, 6, D_MODEL)
        r2 = lambda a: a[layer].reshape(1, -1)
        mats = (w_glu[layer], w_out[layer], w_ff1[layer], w_ff2[layer])
        rows_p, rows_s = (lambda i: 0), (lambda i: i + 1)
        proj_p, xg_p, (w_glu_bf, w_out_bf, w_ff1_bf, w_ff2_bf) = _projection(y_p, mod, rows_p, w_in_bf, mats)
        proj_s, xg_s, _ = _projection(y_s, mod, rows_s, w_in_bf)
        o_p, r_st = _retention(proj_p, ret_decay[layer], None)
        o_s, _ = _retention(proj_s, ret_decay[layer], state_ret[:, layer])
        bp, lp, _ = y_p.shape
        segs = ((bp, lp // S5_T), (bs, y_s.shape[1] // S5_T))
        (yg_p, yg_s), (fin_p, _) = _s5_scan(
            (xg_p, xg_s), *s5_ops,
            (_pack_s5_state(bp, None, None), _pack_s5_state(bs, state_s5_re[:, layer], state_s5_im[:, layer])),
            segs)
        fin_p = fin_p.reshape(g, bp, 2, 2, S5_STATE).transpose(1, 2, 0, 3, 4)
        pw = (r2(s5_d), w_glu_bf, r2(b_glu), w_out_bf, r2(ln1_g), r2(ln1_b), w_ff1_bf, r2(b_ff1),
              w_ff2_bf, r2(b_ff2), r2(ln2_g), r2(ln2_b))
        y_p = _post(y_p, o_p, yg_p, proj_p, mod, rows_p, pw)
        y_s = _post(y_s, o_s, yg_s, proj_s, mod, rows_s, pw)
        rets.append(r_st)
        s5rs.append(fin_p[:, :, :, 0, :])
        s5is.append(fin_p[:, :, :, 1, :])
    return (y_p, y_s, jnp.stack(rets, axis=1), jnp.stack(s5rs, axis=1), jnp.stack(s5is, axis=1))
```

```python
import functools

import jax
import jax.numpy as jnp
from jax import lax
from jax.experimental import pallas as pl
from jax.experimental.pallas import tpu as pltpu

D_MODEL = 1024
RET_HEADS = 4
RET_DK = 128
RET_WIDTH = RET_HEADS * RET_DK
S5_CH = 16
S5_GROUPS = 32
S5_STATE = 64
S5_WIDTH = S5_GROUPS * S5_CH
D_FF = 4 * D_MODEL
IN_COLS = 4 * RET_WIDTH + S5_WIDTH
ALPHA = 2.0 ** 0.25
LN_EPS = 1e-5

S5_T = 16
S5_ROW = S5_T * S5_CH
S5_PREP_GROUP_TILE = 4
S5_GROUP_TILE = 8
RET_CHUNK = 256
RET_UNIT_TOKENS = 1024
TOKEN_TILE = 512
MLP_CHUNK = 2048
VMEM_LIMIT = 56 * 1024 * 1024

_BF = jnp.bfloat16
_F32 = jnp.float32


def _norm_rows(x):
    mu = jnp.mean(x, axis=-1, keepdims=True)
    xc = x - mu
    var = jnp.mean(xc * xc, axis=-1, keepdims=True)
    return xc * lax.rsqrt(var + LN_EPS)


def _const_spec(shape):
    nd = len(shape)
    return pl.BlockSpec(shape, lambda *_: (0,) * nd, pipeline_mode=pl.Buffered(1))


def _granule_transpose(v):
    lane = lax.broadcasted_iota(jnp.int32, (1, 128), 1)
    v = list(v)
    for d in (4, 2, 1):
        bit = ((lane // S5_CH) & d) != 0
        nv = list(v)
        for a in range(8):
            if a & d:
                continue
            lo, hi = v[a], v[a + d]
            nv[a] = jnp.where(bit, pltpu.roll(hi, S5_CH * d, axis=1), lo)
            nv[a + d] = jnp.where(bit, hi, pltpu.roll(lo, 128 - S5_CH * d, axis=1))
        v = nv
    return v


def _tile_geometry(b, l, tm):
    spt = max(1, tm // l)
    return spt, b // spt, l * spt


def _octet_row_start(q, pb, *, nb, spt, tm):
    per_seq = (tm // spt) // 128
    return 8 * (q % per_seq) * nb + pb * spt + q // per_seq


def _proj_kernel(*refs, nb, spt, tm, n_cast):
    x_ref, mod_ref, w_ref = refs[:3]
    cast_in = refs[3:3 + n_cast]
    o_ref, xg_ref = refs[3 + n_cast:5 + n_cast]
    cast_out = refs[5 + n_cast:5 + 2 * n_cast]
    u_scr = refs[5 + 2 * n_cast]
    for src, dst in zip(cast_in, cast_out):
        dst[...] = src[...].astype(_BF)
    x = x_ref[0]
    sh = mod_ref[0, 0:1, :]
    sc = mod_ref[0, 1:2, :]
    h = (_norm_rows(x) * (1.0 + sc) + sh).astype(_BF)
    u = jnp.dot(h, w_ref[:, 4 * RET_WIDTH:], preferred_element_type=_F32)
    o_ref[0, :, 4 * RET_WIDTH:] = u.astype(_BF)
    for j in range(S5_WIDTH // 128):
        u_scr[j] = u[:, 128 * j:128 * (j + 1)]
    o_ref[0, :, 0:4 * RET_WIDTH] = jnp.dot(
        h, w_ref[:, 0:4 * RET_WIDTH], preferred_element_type=_F32).astype(_BF)
    pb = pl.program_id(1)
    for j in range(S5_WIDTH // 128):
        for hf in range(2):
            for q in range(tm // 128):
                v = [u_scr[j, pl.ds(128 * q + 8 * hf + a, 8, stride=S5_T), :] for a in range(8)]
                w = _granule_transpose(v)
                start = _octet_row_start(q, pb, nb=nb, spt=spt, tm=tm)
                for gg in range(8):
                    xg_ref[8 * j + gg, hf, pl.ds(start, 8, stride=nb), :] = w[gg]


def _projection(x, mod, mod_row, w_in_bf, cast_weights=()):
    b, l, _ = x.shape
    tm = TOKEN_TILE
    spt, nbg, lt = _tile_geometry(b, l, tm)
    rows = (tm // S5_T) * nbg
    n_steps = (lt // tm) * nbg
    cast_specs = [pl.BlockSpec((w.shape[0] // n_steps, w.shape[1]), lambda j, i: (j * nbg + i, 0))
                  for w in cast_weights]
    outs = pl.pallas_call(
        functools.partial(_proj_kernel, nb=b, spt=spt, tm=tm, n_cast=len(cast_weights)),
        grid=(lt // tm, nbg),
        in_specs=[pl.BlockSpec((1, tm, D_MODEL), lambda j, i: (i, j, 0)),
                  pl.BlockSpec((1, 6, D_MODEL), lambda j, i: (mod_row(i), 0, 0)),
                  _const_spec((D_MODEL, IN_COLS))] + cast_specs,
        out_specs=[pl.BlockSpec((1, tm, IN_COLS), lambda j, i: (i, j, 0)),
                   pl.BlockSpec((S5_GROUPS, 2, rows, 128), lambda j, i: (0, 0, j, 0))] + cast_specs,
        out_shape=[jax.ShapeDtypeStruct((nbg, lt, IN_COLS), _BF),
                   jax.ShapeDtypeStruct((S5_GROUPS, 2, (l // S5_T) * b, 128), _F32)]
                  + [jax.ShapeDtypeStruct(w.shape, _BF) for w in cast_weights],
        scratch_shapes=[pltpu.VMEM((S5_WIDTH // 128, tm, 128), _F32)],
        compiler_params=pltpu.CompilerParams(
            dimension_semantics=("arbitrary", "arbitrary"), vmem_limit_bytes=VMEM_LIMIT),
        name="projection",
    )(x.reshape(nbg, lt, D_MODEL), mod, w_in_bf, *cast_weights)
    return outs[0].reshape(b, l, IN_COLS), outs[1], tuple(outs[2:])


def _ret_kernel(*refs, chunk, n_chunks, nbb, has_s0):
    if has_s0:
        dec_ref, q_ref, k_ref, v_ref, g_ref, s0_ref, o_ref, st_ref, kv_scr, dec_scr = refs
    else:
        dec_ref, q_ref, k_ref, v_ref, g_ref, o_ref, st_ref, kv_scr, dec_scr = refs
        s0_ref = None
    c = chunk
    scale = RET_DK ** -0.5
    pos = lax.broadcasted_iota(jnp.int32, (c, 1), 0).astype(_F32)
    ri = lax.broadcasted_iota(jnp.int32, (c, c), 0)
    ci = lax.broadcasted_iota(jnp.int32, (c, c), 1)
    rel = (ri - ci).astype(_F32)

    for hd in range(RET_HEADS):
        lanes = slice(hd * RET_DK, (hd + 1) * RET_DK)

        def log_gamma(d):
            z = jnp.full((1, 1), dec_ref[d, hd], _F32)
            return jnp.minimum(z, 0.0) - jnp.log(1.0 + jnp.exp(-jnp.abs(z)))

        lg_f, lg_b = log_gamma(0), log_gamma(1)
        wide = lambda col: jnp.broadcast_to(col, (c, RET_DK))
        dec_scr[hd, 0] = wide(jnp.exp(lg_f * (c - 1.0 - pos)) * scale)
        dec_scr[hd, 1] = wide(jnp.exp(lg_b * pos) * scale)
        dec_scr[hd, 2] = wide(jnp.exp(lg_f * (pos + 1.0)))
        dec_scr[hd, 3] = wide(jnp.exp(lg_b * (c - pos)))
        dmat = (jnp.where(ri >= ci, jnp.exp(lg_f * jnp.maximum(rel, 0.0)), 0.0)
                + jnp.where(ci >= ri, jnp.exp(lg_b * jnp.maximum(-rel, 0.0)), 0.0)) * scale
        cdec_f = jnp.exp(lg_f * float(c))
        cdec_b = jnp.exp(lg_b * float(c))

        for bb in range(nbb):
            unit = bb * RET_HEADS + hd
            for n in range(n_chunks):
                kc = k_ref[bb, n * c:(n + 1) * c, lanes]
                vc = v_ref[bb, n * c:(n + 1) * c, lanes].astype(_F32)
                vcat = jnp.concatenate([(vc * dec_scr[hd, 0]).astype(_BF), (vc * dec_scr[hd, 1]).astype(_BF)], axis=1)
                kv_scr[unit, n] = lax.dot_general(kc, vcat, (((0,), (0,)), ((), ())),
                                                  preferred_element_type=_F32)
            if has_s0:
                carry_f = s0_ref[bb, 0, hd]
                carry_b = s0_ref[bb, 1, hd]
            else:
                carry_f = jnp.zeros((RET_DK, RET_DK), _F32)
                carry_b = jnp.zeros((RET_DK, RET_DK), _F32)
            for n in range(n_chunks):
                inc = kv_scr[unit, n, :, 0:RET_DK]
                kv_scr[unit, n, :, 0:RET_DK] = carry_f
                carry_f = cdec_f * carry_f + inc
            for n in range(n_chunks - 1, -1, -1):
                inc = kv_scr[unit, n, :, RET_DK:2 * RET_DK]
                kv_scr[unit, n, :, RET_DK:2 * RET_DK] = carry_b
                carry_b = cdec_b * carry_b + inc
            st_ref[bb, 0, hd] = carry_f
            st_ref[bb, 1, hd] = carry_b

            for n in range(n_chunks):
                sl = slice(n * c, (n + 1) * c)
                qc = q_ref[bb, sl, lanes]
                kc = k_ref[bb, sl, lanes]
                vc = v_ref[bb, sl, lanes]
                scores = lax.dot_general(qc, kc, (((1,), (1,)), ((), ())), preferred_element_type=_F32)
                o = jnp.dot((scores * dmat).astype(_BF), vc, preferred_element_type=_F32)
                oi = jnp.dot(qc, kv_scr[unit, n].astype(_BF), preferred_element_type=_F32)
                o = o + oi[:, 0:RET_DK] * dec_scr[hd, 2] + oi[:, RET_DK:2 * RET_DK] * dec_scr[hd, 3]
                gate = g_ref[bb, sl, lanes].astype(_F32)
                o_ref[bb, sl, lanes] = (_norm_rows(o) * (gate * jax.nn.sigmoid(gate))).astype(_BF)


def _retention(proj, ret_decay, s0):
    b, l, _ = proj.shape
    c = min(RET_CHUNK, l)
    n_chunks = l // c
    nbb = max(1, min(b, RET_UNIT_TOKENS // l))
    has_s0 = s0 is not None
    col = lambda j: pl.BlockSpec((nbb, l, RET_WIDTH), lambda i: (i, 0, j))
    st_spec = pl.BlockSpec((nbb, 2, RET_HEADS, RET_DK, RET_DK), lambda i: (i, 0, 0, 0, 0))
    in_specs = [pl.BlockSpec(memory_space=pltpu.SMEM), col(0), col(1), col(2), col(3)]
    args = [ret_decay, proj, proj, proj, proj]
    if has_s0:
        in_specs.append(st_spec)
        args.append(s0)
    return pl.pallas_call(
        functools.partial(_ret_kernel, chunk=c, n_chunks=n_chunks, nbb=nbb, has_s0=has_s0),
        grid=(b // nbb,),
        in_specs=in_specs,
        out_specs=[pl.BlockSpec((nbb, l, RET_WIDTH), lambda i: (i, 0, 0)), st_spec],
        out_shape=[jax.ShapeDtypeStruct((b, l, RET_WIDTH), _BF),
                   jax.ShapeDtypeStruct((b, 2, RET_HEADS, RET_DK, RET_DK), _F32)],
        scratch_shapes=[pltpu.VMEM((nbb * RET_HEADS, n_chunks, RET_DK, 2 * RET_DK), _F32),
                        pltpu.VMEM((RET_HEADS, 4, c, RET_DK), _F32)],
        compiler_params=pltpu.CompilerParams(
            dimension_semantics=("arbitrary",), vmem_limit_bytes=VMEM_LIMIT),
        name="retention",
    )(*args)


def _swap_halves(z):
    return pltpu.roll(z, S5_STATE, axis=z.ndim - 1)


def _s5_prep_kernel(ar_ref, ai_ref, ldt_ref, bt_ref, cp_ref, win_ref, cond_ref, wada_ref, bada_ref,
                    m_ref, wst_ref, cout_ref, apow_ref, winb_ref, mod_ref, *, ng):
    winb_ref[...] = win_ref[...].astype(_BF)
    cond = cond_ref[...]
    mod_ref[...] = jnp.dot((cond * jax.nn.sigmoid(cond)).astype(_BF), wada_ref[...].astype(_BF),
                           preferred_element_type=_F32) + bada_ref[...]
    lane = lax.broadcasted_iota(jnp.int32, (1, 2 * S5_STATE), 1)
    sgn = jnp.where(lane < S5_STATE, -1.0, 1.0).astype(_F32)
    gran = lax.broadcasted_iota(jnp.int32, (1, S5_ROW), 1) // S5_CH
    ar = ar_ref[...]
    ai = ai_ref[...]
    dt = jnp.exp(ldt_ref[...])
    mag = jnp.exp(ar * dt)
    ang = ai * dt
    pr1 = mag * jnp.cos(ang)
    pi1 = mag * jnp.sin(ang)
    pw_r = [jnp.ones_like(pr1), pr1]
    pw_i = [jnp.zeros_like(pi1), pi1]
    for _ in range(2, S5_T + 1):
        pr, pi = pw_r[-1], pw_i[-1]
        pw_r.append(pr * pr1 - pi * pi1)
        pw_i.append(pr * pi1 + pi * pr1)
    pw_is = [p * sgn for p in pw_i]
    x2 = pr1 - 1.0
    den = ar * ar + ai * ai
    coef_re = (x2 * ar + pi1 * ai) / den
    coef_im_s = ((pi1 * ar - x2 * ai) / den) * sgn
    for gi in range(ng):
        bt = bt_ref[gi]
        cp = cp_ref[gi]
        cc = cp * (-sgn)
        cps = _swap_halves(cp)
        bts = _swap_halves(bt)
        gens = []
        cout_rows = [[], []]
        for d in range(2):
            r = 2 * gi + d
            row = lambda a: a[r:r + 1, :]
            bb = row(coef_re) * bt + row(coef_im_s) * bts
            bbs = _swap_halves(bb)
            w_rows = []
            for m in range(S5_T):
                e = (S5_T - 1 - m) if d == 0 else m
                w = row(pw_r[e]) * bb + row(pw_is[e]) * bbs
                rows = slice(m * S5_CH, (m + 1) * S5_CH)
                wst_ref[gi, rows, d * 128:(d + 1) * 128] = w.astype(_BF)
                wst_ref[gi, rows, 256 + d * 128:256 + (d + 1) * 128] = _swap_halves(w).astype(_BF)
                w_rows.append(w)
            for i in range(S5_T):
                e = (i + 1) if d == 0 else (S5_T - i)
                gmat = row(pw_r[e]) * cp + row(pw_is[e]) * cps
                cout_rows[d].append(gmat * (-sgn))
            gens.append(lax.dot_general(cc, jnp.concatenate(w_rows, axis=0), (((1,), (1,)), ((), ())),
                                        preferred_element_type=_F32, precision=lax.Precision.HIGHEST))
            apow_ref[gi, 2 * d:2 * d + 1, :] = row(pw_r[S5_T])
            apow_ref[gi, 2 * d + 1:2 * d + 2, :] = row(pw_is[S5_T])
        blocks = []
        for t in range(S5_T):
            sf = (S5_CH * (t + 1)) % S5_ROW
            rf = pltpu.roll(gens[0], sf, axis=1) if sf else gens[0]
            rb = pltpu.roll(gens[1], S5_CH * t, axis=1) if t else gens[1]
            blocks.append(jnp.where(gran <= t, rf, 0.0) + jnp.where(gran >= t, rb, 0.0))
        m_ref[gi] = jnp.concatenate(blocks, axis=0).T.astype(_BF)
        cout_t = jnp.concatenate([jnp.concatenate(cout_rows[0], axis=0),
                                  jnp.concatenate(cout_rows[1], axis=0)], axis=1)
        cout_ref[gi] = cout_t.T.astype(_BF)


def _s5_operators(s5_a_re, s5_a_im, s5_log_dt, s5_b_re, s5_b_im, s5_c_re, s5_c_im, w_in, cond8, w_ada, b_ada):
    g = S5_GROUPS
    ng = S5_PREP_GROUP_TILE
    steps = g // ng
    wspec = pl.BlockSpec((w_in.shape[0] // steps, w_in.shape[1]), lambda i: (i, 0))
    n_mod = w_ada.shape[1]
    tn = n_mod // steps
    dup = lambda a: jnp.concatenate([a, a], axis=-1).transpose(1, 0, 2).reshape(2 * g, 128)
    ar2 = dup(s5_a_re)
    ai2 = dup(s5_a_im)
    ldt = s5_log_dt.T.reshape(2 * g, 1)
    btp = jnp.concatenate([s5_b_re.transpose(0, 2, 1), s5_b_im.transpose(0, 2, 1)], axis=-1)
    cpk = jnp.concatenate([s5_c_re, s5_c_im], axis=-1)
    gspec = lambda *shape: pl.BlockSpec((ng,) + shape, lambda i: (i,) + (0,) * len(shape))
    rspec = lambda w: pl.BlockSpec((2 * ng, w), lambda i: (i, 0))
    return pl.pallas_call(
        functools.partial(_s5_prep_kernel, ng=ng),
        grid=(g // ng,),
        in_specs=[rspec(128), rspec(128), rspec(1), gspec(S5_CH, 128), gspec(S5_CH, 128), wspec,
                  pl.BlockSpec((8, D_MODEL), lambda i: (0, 0)),
                  pl.BlockSpec((D_MODEL, tn), lambda i: (0, i)),
                  pl.BlockSpec((1, tn), lambda i: (0, i))],
        out_specs=[gspec(S5_ROW, S5_ROW), gspec(S5_ROW, 512), gspec(256, S5_ROW), gspec(4, 128), wspec,
                   pl.BlockSpec((8, tn), lambda i: (0, i))],
        out_shape=[jax.ShapeDtypeStruct((g, S5_ROW, S5_ROW), _BF),
                   jax.ShapeDtypeStruct((g, S5_ROW, 512), _BF),
                   jax.ShapeDtypeStruct((g, 256, S5_ROW), _BF),
                   jax.ShapeDtypeStruct((g, 4, 128), _F32),
                   jax.ShapeDtypeStruct(w_in.shape, _BF),
                   jax.ShapeDtypeStruct((8, n_mod), _F32)],
        compiler_params=pltpu.CompilerParams(dimension_semantics=("arbitrary",)),
        name="s5_operators",
    )(ar2, ai2, ldt, btp, cpk, w_in, cond8, w_ada, b_ada.reshape(1, n_mod))


def _s5_kernel(*refs, segs, ng):
    n = len(segs)
    x_refs = refs[0:n]
    m_ref, wst_ref, cout_ref, apow_ref = refs[n:n + 4]
    s0_refs = refs[n + 4:2 * n + 4]
    y_refs = refs[2 * n + 4:3 * n + 4]
    fin_refs = refs[3 * n + 4:4 * n + 4]
    scr = refs[4 * n + 4:]
    coef = [[apow_ref[gi, r:r + 1, :] for r in range(4)] for gi in range(ng)]
    for si, (nb, nk) in enumerate(segs):
        x_ref, s0_ref, y_ref, fin_ref = x_refs[si], s0_refs[si], y_refs[si], fin_refs[si]
        loc_scr, prev_scr = scr[2 * si], scr[2 * si + 1]
        load_x = lambda gi: jnp.concatenate([x_ref[gi, 0], x_ref[gi, 1]], axis=1).astype(_BF)
        for gi in range(ng):
            loc_scr[gi] = jnp.dot(load_x(gi), wst_ref[gi], preferred_element_type=_F32)
        st = []
        for gi in range(ng):
            zf = s0_ref[gi, :, 0:128]
            zb = s0_ref[gi, :, 128:256]
            st.append([zf, _swap_halves(zf), zb, _swap_halves(zb)])
        for k in range(nk):
            kb = nk - 1 - k
            rows_f = slice(k * nb, (k + 1) * nb)
            rows_b = slice(kb * nb, (kb + 1) * nb)
            for gi in range(ng):
                ar_f, ai_f, ar_b, ai_b = coef[gi]
                zf, zfs, zb, zbs = st[gi]
                prev_scr[gi, rows_f, 0:128] = zf
                prev_scr[gi, rows_b, 128:256] = zb
                st[gi] = [ar_f * zf + ai_f * zfs + loc_scr[gi, rows_f, 0:128],
                          ar_f * zfs - ai_f * zf + loc_scr[gi, rows_f, 256:384],
                          ar_b * zb + ai_b * zbs + loc_scr[gi, rows_b, 128:256],
                          ar_b * zbs - ai_b * zb + loc_scr[gi, rows_b, 384:512]]
        for gi in range(ng):
            fin_ref[gi, :, 0:128] = st[gi][0]
            fin_ref[gi, :, 128:256] = st[gi][2]
            y = jnp.dot(load_x(gi), m_ref[gi], preferred_element_type=_F32)
            y = y + jnp.dot(prev_scr[gi].astype(_BF), cout_ref[gi], preferred_element_type=_F32)
            y_ref[gi, 0] = y[:, 0:128]
            y_ref[gi, 1] = y[:, 128:256]


def _s5_scan(xgs, m_op, wst, cout, apow, s0ps, segs):
    g = S5_GROUPS
    ng = S5_GROUP_TILE
    n = len(segs)
    gspec = lambda *shape: pl.BlockSpec((ng,) + shape, lambda i: (i,) + (0,) * len(shape))
    rows = [nb * nk for nb, nk in segs]
    outs = pl.pallas_call(
        functools.partial(_s5_kernel, segs=tuple(segs), ng=ng),
        grid=(g // ng,),
        in_specs=[gspec(2, r, 128) for r in rows]
                 + [gspec(S5_ROW, S5_ROW), gspec(S5_ROW, 512), gspec(256, S5_ROW), gspec(4, 128)]
                 + [gspec(nb, 256) for nb, _ in segs],
        out_specs=[gspec(2, r, 128) for r in rows] + [gspec(nb, 256) for nb, _ in segs],
        out_shape=[jax.ShapeDtypeStruct((g, 2, r, 128), _F32) for r in rows]
                  + [jax.ShapeDtypeStruct((g, nb, 256), _F32) for nb, _ in segs],
        scratch_shapes=[buf for r in rows
                        for buf in (pltpu.VMEM((ng, r, 512), _F32), pltpu.VMEM((ng, r, 256), _F32))],
        compiler_params=pltpu.CompilerParams(
            dimension_semantics=("arbitrary",), vmem_limit_bytes=VMEM_LIMIT),
        name="s5_scan",
    )(*xgs, m_op, wst, cout, apow, *s0ps)
    return outs[:n], outs[n:]


def _post_kernel(x_ref, o_ref, yg_ref, u_ref, mod_ref, modp_ref, dsk_ref, wglu_ref, bglu_ref, wout_ref,
                 l1g_ref, l1b_ref, w1_ref, b1_ref, w2_ref, b2_ref, l2g_ref, l2b_ref, out_ref,
                 y_scr, h_next, x1_next, h_cur, x1_cur, *, nb, spt, tm, nbg, n_steps):
    i = pl.program_id(0)
    pb = jnp.minimum(i, n_steps - 1) % nbg
    front = functools.partial(_post_mix_front, o_ref, yg_ref, u_ref, dsk_ref, wglu_ref, bglu_ref,
                              wout_ref, y_scr, nb=nb, spt=spt, tm=tm, pb=pb)
    back = functools.partial(_post_mix_back, x_ref, mod_ref, wout_ref, l1g_ref, l1b_ref,
                             h_next, x1_next)

    @pl.when(i == 0)
    def _():
        back(*front())

    @pl.when(i > 0)
    def _():
        h_cur[...] = h_next[...]
        x1_cur[...] = x1_next[...]
        h = h_cur[...]
        up = functools.partial(_mlp_up, h, w1_ref, b1_ref)
        down = functools.partial(_mlp_down, w2_ref)
        a0 = up(0)
        y, mix = front()
        a1 = up(1)
        mix = mix + jnp.dot(y, wout_ref[RET_WIDTH:, :], preferred_element_type=_F32)
        back(None, mix)
        acc = down(a0, 0) + down(a1, 1)
        f = acc + b2_ref[...]
        g2 = modp_ref[0, 5:6, :]
        out_ref[0] = _norm_rows(ALPHA * x1_cur[...] + g2 * f) * l2g_ref[...] + l2b_ref[...]


def _mlp_up(h, w1_ref, b1_ref, j):
    cols = slice(j * MLP_CHUNK, (j + 1) * MLP_CHUNK)
    a = jnp.dot(h, w1_ref[:, cols], preferred_element_type=_F32) + b1_ref[:, cols]
    return jnp.square(jnp.maximum(a, 0.0)).astype(_BF)


def _mlp_down(w2_ref, a, j):
    return jnp.dot(a, w2_ref[j * MLP_CHUNK:(j + 1) * MLP_CHUNK, :], preferred_element_type=_F32)


def _post_mix_front(o_ref, yg_ref, u_ref, dsk_ref, wglu_ref, bglu_ref, wout_ref, y_scr, *, nb, spt, tm, pb):
    mix = jnp.dot(o_ref[0], wout_ref[0:RET_WIDTH, :], preferred_element_type=_F32)
    for j in range(S5_WIDTH // 128):
        for hf in range(2):
            for q in range(tm // 128):
                start = _octet_row_start(q, pb, nb=nb, spt=spt, tm=tm)
                w = [yg_ref[8 * j + gg, hf, pl.ds(start, 8, stride=nb), :] for gg in range(8)]
                v = _granule_transpose(w)
                for a in range(8):
                    y_scr[j, pl.ds(128 * q + 8 * hf + a, 8, stride=S5_T), :] = v[a]
    y = jnp.concatenate([y_scr[j] for j in range(S5_WIDTH // 128)], axis=1)
    y = y + dsk_ref[...] * u_ref[0].astype(_F32)
    y = jax.nn.gelu(y)
    y = y * jax.nn.sigmoid(jnp.dot(y.astype(_BF), wglu_ref[...], preferred_element_type=_F32)
                           + bglu_ref[...])
    return y.astype(_BF), mix


def _post_mix_back(x_ref, mod_ref, wout_ref, l1g_ref, l1b_ref, h_out, x1_out, y, mix):
    if y is not None:
        mix = mix + jnp.dot(y, wout_ref[RET_WIDTH:, :], preferred_element_type=_F32)
    g1 = mod_ref[0, 2:3, :]
    sh2 = mod_ref[0, 3:4, :]
    sc2 = mod_ref[0, 4:5, :]
    x1 = _norm_rows(ALPHA * x_ref[0] + g1 * mix) * l1g_ref[...] + l1b_ref[...]
    h_out[...] = (_norm_rows(x1) * (1.0 + sc2) + sh2).astype(_BF)
    x1_out[...] = x1


def _post(x, o_ret, yg, proj, mod, mod_row, pw):
    b, l, _ = x.shape
    tm = min(TOKEN_TILE, l)
    spt, nbg, lt = _tile_geometry(b, l, tm)
    rows = (tm // S5_T) * nbg
    n_steps = (lt // tm) * nbg
    item_a = lambda i: jnp.minimum(i, n_steps - 1)
    item_b = lambda i: jnp.maximum(i - 1, 0)
    tok = lambda w, item, cb=0: pl.BlockSpec((1, tm, w), lambda i: (item(i) % nbg, item(i) // nbg, cb))
    modspec = lambda item: pl.BlockSpec((1, 6, D_MODEL), lambda i: (mod_row(item(i) % nbg), 0, 0))
    row = lambda n: _const_spec((1, n))
    return pl.pallas_call(
        functools.partial(_post_kernel, nb=b, spt=spt, tm=tm, nbg=nbg, n_steps=n_steps),
        grid=(n_steps + 1,),
        in_specs=[tok(D_MODEL, item_a), tok(RET_WIDTH, item_a),
                  (_const_spec((S5_GROUPS, 2, rows, 128)) if lt == tm else
                   pl.BlockSpec((S5_GROUPS, 2, rows, 128), lambda i: (0, 0, item_a(i) // nbg, 0))),
                  tok(S5_WIDTH, item_a, 4),
                  modspec(item_a), modspec(item_b),
                  row(S5_WIDTH), _const_spec((S5_WIDTH, S5_WIDTH)), row(S5_WIDTH),
                  _const_spec((D_MODEL, D_MODEL)), row(D_MODEL), row(D_MODEL),
                  _const_spec((D_MODEL, D_FF)), row(D_FF), _const_spec((D_FF, D_MODEL)), row(D_MODEL),
                  row(D_MODEL), row(D_MODEL)],
        out_specs=tok(D_MODEL, item_b),
        out_shape=jax.ShapeDtypeStruct((nbg, lt, D_MODEL), _F32),
        scratch_shapes=[pltpu.VMEM((S5_WIDTH // 128, tm, 128), _F32),
                        pltpu.VMEM((tm, D_MODEL), _BF), pltpu.VMEM((tm, D_MODEL), _F32),
                        pltpu.VMEM((tm, D_MODEL), _BF), pltpu.VMEM((tm, D_MODEL), _F32)],
        compiler_params=pltpu.CompilerParams(
            dimension_semantics=("arbitrary",), vmem_limit_bytes=VMEM_LIMIT),
        name="post",
    )(x.reshape(nbg, lt, D_MODEL), o_ret.reshape(nbg, lt, RET_WIDTH), yg,
      proj.reshape(nbg, lt, IN_COLS), mod, mod, *pw).reshape(b, l, D_MODEL)


def _pack_s5_state(b, s5_s0_re, s5_s0_im):
    g = S5_GROUPS
    if s5_s0_re is None:
        return jnp.zeros((g, b, 256), _F32)
    s0p = jnp.concatenate([s5_s0_re, s5_s0_im], axis=-1)
    return s0p.transpose(2, 0, 1, 3).reshape(g, b, 256)


def kernel(x_prompt, x_sample, state_ret, state_s5_re, state_s5_im, c, c_ctx, w_ada, b_ada, w_in,
           ret_decay, s5_a_re, s5_a_im, s5_log_dt, s5_b_re, s5_b_im, s5_c_re, s5_c_im, s5_d, w_glu,
           b_glu, w_out, ln1_g, ln1_b, w_ff1, b_ff1, w_ff2, b_ff2, ln2_g, ln2_b):
    depth = w_ada.shape[0]
    bs = x_sample.shape[0]
    g = S5_GROUPS
    y_p, y_s = x_prompt, x_sample
    rets, s5rs, s5is = [], [], []
    for layer in range(depth):
        cond8 = jnp.concatenate(
            [c_ctx[None, :], c, jnp.zeros((8 - 1 - bs, D_MODEL), _F32)], axis=0)
        *s5_ops, w_in_bf, mod = _s5_operators(
            s5_a_re[layer], s5_a_im[layer], s5_log_dt[layer], s5_b_re[layer], s5_b_im[layer],
            s5_c_re[layer], s5_c_im[layer], w_in[layer], cond8, w_ada[layer], b_ada[layer])
        mod = mod.reshape(8, 6, D_MODEL)
        r2 = lambda a: a[layer].reshape(1, -1)
        mats = (w_glu[layer], w_out[layer], w_ff1[layer], w_ff2[layer])
        rows_p, rows_s = (lambda i: 0), (lambda i: i + 1)
        proj_p, xg_p, (w_glu_bf, w_out_bf, w_ff1_bf, w_ff2_bf) = _projection(y_p, mod, rows_p, w_in_bf, mats)
        proj_s, xg_s, _ = _projection(y_s, mod, rows_s, w_in_bf)
        o_p, r_st = _retention(proj_p, ret_decay[layer], None)
        o_s, _ = _retention(proj_s, ret_decay[layer], state_ret[:, layer])
        bp, lp, _ = y_p.shape
        segs = ((bp, lp // S5_T), (bs, y_s.shape[1] // S5_T))
        (yg_p, yg_s), (fin_p, _) = _s5_scan(
            (xg_p, xg_s), *s5_ops,
            (_pack_s5_state(bp, None, None), _pack_s5_state(bs, state_s5_re[:, layer], state_s5_im[:, layer])),
            segs)
        fin_p = fin_p.reshape(g, bp, 2, 2, S5_STATE).transpose(1, 2, 0, 3, 4)
        pw = (r2(s5_d), w_glu_bf, r2(b_glu), w_out_bf, r2(ln1_g), r2(ln1_b), w_ff1_bf, r2(b_ff1),
              w_ff2_bf, r2(b_ff2), r2(ln2_g), r2(ln2_b))
        y_p = _post(y_p, o_p, yg_p, proj_p, mod, rows_p, pw)
        y_s = _post(y_s, o_s, yg_s, proj_s, mod, rows_s, pw)
        rets.append(r_st)
        s5rs.append(fin_p[:, :, :, 0, :])
        s5is.append(fin_p[:, :, :, 1, :])
    return (y_p, y_s, jnp.stack(rets, axis=1), jnp.stack(s5rs, axis=1), jnp.stack(s5is, axis=1))
```

```python
import functools

import jax
import jax.numpy as jnp
from jax import lax
from jax.experimental import pallas as pl
from jax.experimental.pallas import tpu as pltpu

D_MODEL = 1024
RET_HEADS = 4
RET_DK = 128
RET_WIDTH = RET_HEADS * RET_DK
S5_CH = 16
S5_GROUPS = 32
S5_STATE = 64
S5_WIDTH = S5_GROUPS * S5_CH
D_FF = 4 * D_MODEL
IN_COLS = 4 * RET_WIDTH + S5_WIDTH
ALPHA = 2.0 ** 0.25
LN_EPS = 1e-5

S5_T = 16
S5_ROW = S5_T * S5_CH
S5_PREP_GROUP_TILE = 4
S5_GROUP_TILE = 8
RET_CHUNK = 256
RET_UNIT_TOKENS = 1024
TOKEN_TILE = 512
MLP_CHUNK = 2048
VMEM_LIMIT = 56 * 1024 * 1024

_BF = jnp.bfloat16
_F32 = jnp.float32


def _norm_rows(x):
    mu = jnp.mean(x, axis=-1, keepdims=True)
    xc = x - mu
    var = jnp.mean(xc * xc, axis=-1, keepdims=True)
    return xc * lax.rsqrt(var + LN_EPS)


def _const_spec(shape):
    nd = len(shape)
    return pl.BlockSpec(shape, lambda *_: (0,) * nd, pipeline_mode=pl.Buffered(1))


def _granule_transpose(v):
    lane = lax.broadcasted_iota(jnp.int32, (1, 128), 1)
    v = list(v)
    for d in (4, 2, 1):
        bit = ((lane // S5_CH) & d) != 0
        nv = list(v)
        for a in range(8):
            if a & d:
                continue
            lo, hi = v[a], v[a + d]
            nv[a] = jnp.where(bit, pltpu.roll(hi, S5_CH * d, axis=1), lo)
            nv[a + d] = jnp.where(bit, hi, pltpu.roll(lo, 128 - S5_CH * d, axis=1))
        v = nv
    return v


def _tile_geometry(b, l, tm):
    spt = max(1, tm // l)
    return spt, b // spt, l * spt


def _octet_row_start(q, pb, *, nb, spt, tm):
    per_seq = (tm // spt) // 128
    return 8 * (q % per_seq) * nb + pb * spt + q // per_seq


def _proj_kernel(*refs, nb, spt, tm, n_cast):
    x_ref, mod_ref, w_ref = refs[:3]
    cast_in = refs[3:3 + n_cast]
    o_ref, xg_ref = refs[3 + n_cast:5 + n_cast]
    cast_out = refs[5 + n_cast:5 + 2 * n_cast]
    u_scr = refs[5 + 2 * n_cast]
    for src, dst in zip(cast_in, cast_out):
        dst[...] = src[...].astype(_BF)
    x = x_ref[0]
    sh = mod_ref[0, 0:1, :]
    sc = mod_ref[0, 1:2, :]
    h = (_norm_rows(x) * (1.0 + sc) + sh).astype(_BF)
    u = jnp.dot(h, w_ref[:, 4 * RET_WIDTH:], preferred_element_type=_F32)
    o_ref[0, :, 4 * RET_WIDTH:] = u.astype(_BF)
    for j in range(S5_WIDTH // 128):
        u_scr[j] = u[:, 128 * j:128 * (j + 1)]
    o_ref[0, :, 0:4 * RET_WIDTH] = jnp.dot(
        h, w_ref[:, 0:4 * RET_WIDTH], preferred_element_type=_F32).astype(_BF)
    pb = pl.program_id(1)
    for j in range(S5_WIDTH // 128):
        for hf in range(2):
            for q in range(tm // 128):
                v = [u_scr[j, pl.ds(128 * q + 8 * hf + a, 8, stride=S5_T), :] for a in range(8)]
                w = _granule_transpose(v)
                start = _octet_row_start(q, pb, nb=nb, spt=spt, tm=tm)
                for gg in range(8):
                    xg_ref[8 * j + gg, hf, pl.ds(start, 8, stride=nb), :] = w[gg]


def _projection(x, mod, mod_row, w_in_bf, cast_weights=()):
    b, l, _ = x.shape
    tm = TOKEN_TILE
    spt, nbg, lt = _tile_geometry(b, l, tm)
    rows = (tm // S5_T) * nbg
    n_steps = (lt // tm) * nbg
    cast_specs = [pl.BlockSpec((w.shape[0] // n_steps, w.shape[1]), lambda j, i: (j * nbg + i, 0))
                  for w in cast_weights]
    outs = pl.pallas_call(
        functools.partial(_proj_kernel, nb=b, spt=spt, tm=tm, n_cast=len(cast_weights)),
        grid=(lt // tm, nbg),
        in_specs=[pl.BlockSpec((1, tm, D_MODEL), lambda j, i: (i, j, 0)),
                  pl.BlockSpec((1, 6, D_MODEL), lambda j, i: (mod_row(i), 0, 0)),
                  _const_spec((D_MODEL, IN_COLS))] + cast_specs,
        out_specs=[pl.BlockSpec((1, tm, IN_COLS), lambda j, i: (i, j, 0)),
                   pl.BlockSpec((S5_GROUPS, 2, rows, 128), lambda j, i: (0, 0, j, 0))] + cast_specs,
        out_shape=[jax.ShapeDtypeStruct((nbg, lt, IN_COLS), _BF),
                   jax.ShapeDtypeStruct((S5_GROUPS, 2, (l // S5_T) * b, 128), _F32)]
                  + [jax.ShapeDtypeStruct(w.shape, _BF) for w in cast_weights],
        scratch_shapes=[pltpu.VMEM((S5_WIDTH // 128, tm, 128), _F32)],
        compiler_params=pltpu.CompilerParams(
            dimension_semantics=("arbitrary", "arbitrary"), vmem_limit_bytes=VMEM_LIMIT),
        name="projection",
    )(x.reshape(nbg, lt, D_MODEL), mod, w_in_bf, *cast_weights)
    return outs[0].reshape(b, l, IN_COLS), outs[1], tuple(outs[2:])


def _ret_kernel(*refs, chunk, n_chunks, nbb, has_s0):
    if has_s0:
        dec_ref, q_ref, k_ref, v_ref, g_ref, s0_ref, o_ref, st_ref, kv_scr, dec_scr = refs
    else:
        dec_ref, q_ref, k_ref, v_ref, g_ref, o_ref, st_ref, kv_scr, dec_scr = refs
        s0_ref = None
    c = chunk
    scale = RET_DK ** -0.5
    pos = lax.broadcasted_iota(jnp.int32, (c, 1), 0).astype(_F32)
    ri = lax.broadcasted_iota(jnp.int32, (c, c), 0)
    ci = lax.broadcasted_iota(jnp.int32, (c, c), 1)
    rel = (ri - ci).astype(_F32)

    for hd in range(RET_HEADS):
        lanes = slice(hd * RET_DK, (hd + 1) * RET_DK)

        def log_gamma(d):
            z = jnp.full((1, 1), dec_ref[d, hd], _F32)
            return jnp.minimum(z, 0.0) - jnp.log(1.0 + jnp.exp(-jnp.abs(z)))

        lg_f, lg_b = log_gamma(0), log_gamma(1)
        wide = lambda col: jnp.broadcast_to(col, (c, RET_DK))
        dec_scr[hd, 0] = wide(jnp.exp(lg_f * (c - 1.0 - pos)) * scale)
        dec_scr[hd, 1] = wide(jnp.exp(lg_b * pos) * scale)
        dec_scr[hd, 2] = wide(jnp.exp(lg_f * (pos + 1.0)))
        dec_scr[hd, 3] = wide(jnp.exp(lg_b * (c - pos)))
        dmat = (jnp.where(ri >= ci, jnp.exp(lg_f * jnp.maximum(rel, 0.0)), 0.0)
                + jnp.where(ci >= ri, jnp.exp(lg_b * jnp.maximum(-rel, 0.0)), 0.0)) * scale
        cdec_f = jnp.exp(lg_f * float(c))
        cdec_b = jnp.exp(lg_b * float(c))

        for bb in range(nbb):
            unit = bb * RET_HEADS + hd
            for n in range(n_chunks):
                kc = k_ref[bb, n * c:(n + 1) * c, lanes]
                vc = v_ref[bb, n * c:(n + 1) * c, lanes].astype(_F32)
                vcat = jnp.concatenate([(vc * dec_scr[hd, 0]).astype(_BF), (vc * dec_scr[hd, 1]).astype(_BF)], axis=1)
                kv_scr[unit, n] = lax.dot_general(kc, vcat, (((0,), (0,)), ((), ())),
                                                  preferred_element_type=_F32)
            if has_s0:
                carry_f = s0_ref[bb, 0, hd]
                carry_b = s0_ref[bb, 1, hd]
            else:
                carry_f = jnp.zeros((RET_DK, RET_DK), _F32)
                carry_b = jnp.zeros((RET_DK, RET_DK), _F32)
            for n in range(n_chunks):
                inc = kv_scr[unit, n, :, 0:RET_DK]
                kv_scr[unit, n, :, 0:RET_DK] = carry_f
                carry_f = cdec_f * carry_f + inc
            for n in range(n_chunks - 1, -1, -1):
                inc = kv_scr[unit, n, :, RET_DK:2 * RET_DK]
                kv_scr[unit, n, :, RET_DK:2 * RET_DK] = carry_b
                carry_b = cdec_b * carry_b + inc
            st_ref[bb, 0, hd] = carry_f
            st_ref[bb, 1, hd] = carry_b

            for n in range(n_chunks):
                sl = slice(n * c, (n + 1) * c)
                qc = q_ref[bb, sl, lanes]
                kc = k_ref[bb, sl, lanes]
                vc = v_ref[bb, sl, lanes]
                scores = lax.dot_general(qc, kc, (((1,), (1,)), ((), ())), preferred_element_type=_F32)
                o = jnp.dot((scores * dmat).astype(_BF), vc, preferred_element_type=_F32)
                oi = jnp.dot(qc, kv_scr[unit, n].astype(_BF), preferred_element_type=_F32)
                o = o + oi[:, 0:RET_DK] * dec_scr[hd, 2] + oi[:, RET_DK:2 * RET_DK] * dec_scr[hd, 3]
                gate = g_ref[bb, sl, lanes].astype(_F32)
                o_ref[bb, sl, lanes] = (_norm_rows(o) * (gate * jax.nn.sigmoid(gate))).astype(_BF)


def _retention(proj, ret_decay, s0):
    b, l, _ = proj.shape
    c = min(RET_CHUNK, l)
    n_chunks = l // c
    nbb = max(1, min(b, RET_UNIT_TOKENS // l))
    has_s0 = s0 is not None
    col = lambda j: pl.BlockSpec((nbb, l, RET_WIDTH), lambda i: (i, 0, j))
    st_spec = pl.BlockSpec((nbb, 2, RET_HEADS, RET_DK, RET_DK), lambda i: (i, 0, 0, 0, 0))
    in_specs = [pl.BlockSpec(memory_space=pltpu.SMEM), col(0), col(1), col(2), col(3)]
    args = [ret_decay, proj, proj, proj, proj]
    if has_s0:
        in_specs.append(st_spec)
        args.append(s0)
    return pl.pallas_call(
        functools.partial(_ret_kernel, chunk=c, n_chunks=n_chunks, nbb=nbb, has_s0=has_s0),
        grid=(b // nbb,),
        in_specs=in_specs,
        out_specs=[pl.BlockSpec((nbb, l, RET_WIDTH), lambda i: (i, 0, 0)), st_spec],
        out_shape=[jax.ShapeDtypeStruct((b, l, RET_WIDTH), _BF),
                   jax.ShapeDtypeStruct((b, 2, RET_HEADS, RET_DK, RET_DK), _F32)],
        scratch_shapes=[pltpu.VMEM((nbb * RET_HEADS, n_chunks, RET_DK, 2 * RET_DK), _F32),
                        pltpu.VMEM((RET_HEADS, 4, c, RET_DK), _F32)],
        compiler_params=pltpu.CompilerParams(
            dimension_semantics=("arbitrary",), vmem_limit_bytes=VMEM_LIMIT),
        name="retention",
    )(*args)


def _swap_halves(z):
    return pltpu.roll(z, S5_STATE, axis=z.ndim - 1)


def _s5_prep_kernel(ar_ref, ai_ref, ldt_ref, bt_ref, cp_ref, win_ref, cond_ref, wada_ref, bada_ref,
                    m_ref, wst_ref, cout_ref, apow_ref, winb_ref, mod_ref, *, ng):
    winb_ref[...] = win_ref[...].astype(_BF)
    cond = cond_ref[...]
    mod_ref[...] = jnp.dot((cond * jax.nn.sigmoid(cond)).astype(_BF), wada_ref[...].astype(_BF),
                           preferred_element_type=_F32) + bada_ref[...]
    lane = lax.broadcasted_iota(jnp.int32, (1, 2 * S5_STATE), 1)
    sgn = jnp.where(lane < S5_STATE, -1.0, 1.0).astype(_F32)
    gran = lax.broadcasted_iota(jnp.int32, (1, S5_ROW), 1) // S5_CH
    ar = ar_ref[...]
    ai = ai_ref[...]
    dt = jnp.exp(ldt_ref[...])
    mag = jnp.exp(ar * dt)
    ang = ai * dt
    pr1 = mag * jnp.cos(ang)
    pi1 = mag * jnp.sin(ang)
    pw_r = [jnp.ones_like(pr1), pr1]
    pw_i = [jnp.zeros_like(pi1), pi1]
    for _ in range(2, S5_T + 1):
        pr, pi = pw_r[-1], pw_i[-1]
        pw_r.append(pr * pr1 - pi * pi1)
        pw_i.append(pr * pi1 + pi * pr1)
    pw_is = [p * sgn for p in pw_i]
    x2 = pr1 - 1.0
    den = ar * ar + ai * ai
    coef_re = (x2 * ar + pi1 * ai) / den
    coef_im_s = ((pi1 * ar - x2 * ai) / den) * sgn
    for gi in range(ng):
        bt = bt_ref[gi]
        cp = cp_ref[gi]
        cc = cp * (-sgn)
        cps = _swap_halves(cp)
        bts = _swap_halves(bt)
        gens = []
        cout_rows = [[], []]
        for d in range(2):
            r = 2 * gi + d
            row = lambda a: a[r:r + 1, :]
            bb = row(coef_re) * bt + row(coef_im_s) * bts
            bbs = _swap_halves(bb)
            w_rows = []
            for m in range(S5_T):
                e = (S5_T - 1 - m) if d == 0 else m
                w = row(pw_r[e]) * bb + row(pw_is[e]) * bbs
                rows = slice(m * S5_CH, (m + 1) * S5_CH)
                wst_ref[gi, rows, d * 128:(d + 1) * 128] = w.astype(_BF)
                wst_ref[gi, rows, 256 + d * 128:256 + (d + 1) * 128] = _swap_halves(w).astype(_BF)
                w_rows.append(w)
            for i in range(S5_T):
                e = (i + 1) if d == 0 else (S5_T - i)
                gmat = row(pw_r[e]) * cp + row(pw_is[e]) * cps
                cout_rows[d].append(gmat * (-sgn))
            gens.append(lax.dot_general(cc, jnp.concatenate(w_rows, axis=0), (((1,), (1,)), ((), ())),
                                        preferred_element_type=_F32, precision=lax.Precision.HIGHEST))
            apow_ref[gi, 2 * d:2 * d + 1, :] = row(pw_r[S5_T])
            apow_ref[gi, 2 * d + 1:2 * d + 2, :] = row(pw_is[S5_T])
        blocks = []
        for t in range(S5_T):
            sf = (S5_CH * (t + 1)) % S5_ROW
            rf = pltpu.roll(gens[0], sf, axis=1) if sf else gens[0]
            rb = pltpu.roll(gens[1], S5_CH * t, axis=1) if t else gens[1]
            blocks.append(jnp.where(gran <= t, rf, 0.0) + jnp.where(gran >= t, rb, 0.0))
        m_ref[gi] = jnp.concatenate(blocks, axis=0).T.astype(_BF)
        cout_t = jnp.concatenate([jnp.concatenate(cout_rows[0], axis=0),
                                  jnp.concatenate(cout_rows[1], axis=0)], axis=1)
        cout_ref[gi] = cout_t.T.astype(_BF)


def _s5_operators(s5_a_re, s5_a_im, s5_log_dt, s5_b_re, s5_b_im, s5_c_re, s5_c_im, w_in, cond8, w_ada, b_ada):
    g = S5_GROUPS
    ng = S5_PREP_GROUP_TILE
    steps = g // ng
    wspec = pl.BlockSpec((w_in.shape[0] // steps, w_in.shape[1]), lambda i: (i, 0))
    n_mod = w_ada.shape[1]
    tn = n_mod // steps
    dup = lambda a: jnp.concatenate([a, a], axis=-1).transpose(1, 0, 2).reshape(2 * g, 128)
    ar2 = dup(s5_a_re)
    ai2 = dup(s5_a_im)
    ldt = s5_log_dt.T.reshape(2 * g, 1)
    btp = jnp.concatenate([s5_b_re.transpose(0, 2, 1), s5_b_im.transpose(0, 2, 1)], axis=-1)
    cpk = jnp.concatenate([s5_c_re, s5_c_im], axis=-1)
    gspec = lambda *shape: pl.BlockSpec((ng,) + shape, lambda i: (i,) + (0,) * len(shape))
    rspec = lambda w: pl.BlockSpec((2 * ng, w), lambda i: (i, 0))
    return pl.pallas_call(
        functools.partial(_s5_prep_kernel, ng=ng),
        grid=(g // ng,),
        in_specs=[rspec(128), rspec(128), rspec(1), gspec(S5_CH, 128), gspec(S5_CH, 128), wspec,
                  pl.BlockSpec((8, D_MODEL), lambda i: (0, 0)),
                  pl.BlockSpec((D_MODEL, tn), lambda i: (0, i)),
                  pl.BlockSpec((1, tn), lambda i: (0, i))],
        out_specs=[gspec(S5_ROW, S5_ROW), gspec(S5_ROW, 512), gspec(256, S5_ROW), gspec(4, 128), wspec,
                   pl.BlockSpec((8, tn), lambda i: (0, i))],
        out_shape=[jax.ShapeDtypeStruct((g, S5_ROW, S5_ROW), _BF),
                   jax.ShapeDtypeStruct((g, S5_ROW, 512), _BF),
                   jax.ShapeDtypeStruct((g, 256, S5_ROW), _BF),
                   jax.ShapeDtypeStruct((g, 4, 128), _F32),
                   jax.ShapeDtypeStruct(w_in.shape, _BF),
                   jax.ShapeDtypeStruct((8, n_mod), _F32)],
        compiler_params=pltpu.CompilerParams(dimension_semantics=("arbitrary",)),
        name="s5_operators",
    )(ar2, ai2, ldt, btp, cpk, w_in, cond8, w_ada, b_ada.reshape(1, n_mod))


def _s5_kernel(*refs, segs, ng):
    n = len(segs)
    x_refs = refs[0:n]
    m_ref, wst_ref, cout_ref, apow_ref = refs[n:n + 4]
    s0_refs = refs[n + 4:2 * n + 4]
    y_refs = refs[2 * n + 4:3 * n + 4]
    fin_refs = refs[3 * n + 4:4 * n + 4]
    scr = refs[4 * n + 4:]
    coef = [[apow_ref[gi, r:r + 1, :] for r in range(4)] for gi in range(ng)]
    for si, (nb, nk) in enumerate(segs):
        x_ref, s0_ref, y_ref, fin_ref = x_refs[si], s0_refs[si], y_refs[si], fin_refs[si]
        loc_scr, prev_scr = scr[2 * si], scr[2 * si + 1]
        load_x = lambda gi: jnp.concatenate([x_ref[gi, 0], x_ref[gi, 1]], axis=1).astype(_BF)
        for gi in range(ng):
            loc_scr[gi] = jnp.dot(load_x(gi), wst_ref[gi], preferred_element_type=_F32)
        st = []
        for gi in range(ng):
            zf = s0_ref[gi, :, 0:128]
            zb = s0_ref[gi, :, 128:256]
            st.append([zf, _swap_halves(zf), zb, _swap_halves(zb)])
        for k in range(nk):
            kb = nk - 1 - k
            rows_f = slice(k * nb, (k + 1) * nb)
            rows_b = slice(kb * nb, (kb + 1) * nb)
            for gi in range(ng):
                ar_f, ai_f, ar_b, ai_b = coef[gi]
                zf, zfs, zb, zbs = st[gi]
                prev_scr[gi, rows_f, 0:128] = zf
                prev_scr[gi, rows_b, 128:256] = zb
                st[gi] = [ar_f * zf + ai_f * zfs + loc_scr[gi, rows_f, 0:128],
                          ar_f * zfs - ai_f * zf + loc_scr[gi, rows_f, 256:384],
                          ar_b * zb + ai_b * zbs + loc_scr[gi, rows_b, 128:256],
                          ar_b * zbs - ai_b * zb + loc_scr[gi, rows_b, 384:512]]
        for gi in range(ng):
            fin_ref[gi, :, 0:128] = st[gi][0]
            fin_ref[gi, :, 128:256] = st[gi][2]
            y = jnp.dot(load_x(gi), m_ref[gi], preferred_element_type=_F32)
            y = y + jnp.dot(prev_scr[gi].astype(_BF), cout_ref[gi], preferred_element_type=_F32)
            y_ref[gi, 0] = y[:, 0:128]
            y_ref[gi, 1] = y[:, 128:256]


def _s5_scan(xgs, m_op, wst, cout, apow, s0ps, segs):
    g = S5_GROUPS
    ng = S5_GROUP_TILE
    n = len(segs)
    gspec = lambda *shape: pl.BlockSpec((ng,) + shape, lambda i: (i,) + (0,) * len(shape))
    rows = [nb * nk for nb, nk in segs]
    outs = pl.pallas_call(
        functools.partial(_s5_kernel, segs=tuple(segs), ng=ng),
        grid=(g // ng,),
        in_specs=[gspec(2, r, 128) for r in rows]
                 + [gspec(S5_ROW, S5_ROW), gspec(S5_ROW, 512), gspec(256, S5_ROW), gspec(4, 128)]
                 + [gspec(nb, 256) for nb, _ in segs],
        out_specs=[gspec(2, r, 128) for r in rows] + [gspec(nb, 256) for nb, _ in segs],
        out_shape=[jax.ShapeDtypeStruct((g, 2, r, 128), _F32) for r in rows]
                  + [jax.ShapeDtypeStruct((g, nb, 256), _F32) for nb, _ in segs],
        scratch_shapes=[buf for r in rows
                        for buf in (pltpu.VMEM((ng, r, 512), _F32), pltpu.VMEM((ng, r, 256), _F32))],
        compiler_params=pltpu.CompilerParams(
            dimension_semantics=("arbitrary",), vmem_limit_bytes=VMEM_LIMIT),
        name="s5_scan",
    )(*xgs, m_op, wst, cout, apow, *s0ps)
    return outs[:n], outs[n:]


def _post_kernel(x_ref, o_ref, yg_ref, u_ref, mod_ref, modp_ref, dsk_ref, wglu_ref, bglu_ref, wout_ref,
                 l1g_ref, l1b_ref, w1_ref, b1_ref, w2_ref, b2_ref, l2g_ref, l2b_ref, out_ref,
                 y_scr, h_next, x1_next, h_cur, x1_cur, *, nb, spt, tm, nbg, n_steps):
    i = pl.program_id(0)
    pb = jnp.minimum(i, n_steps - 1) % nbg
    front = functools.partial(_post_mix_front, o_ref, yg_ref, u_ref, dsk_ref, wglu_ref, bglu_ref,
                              wout_ref, y_scr, nb=nb, spt=spt, tm=tm, pb=pb)
    back = functools.partial(_post_mix_back, x_ref, mod_ref, wout_ref, l1g_ref, l1b_ref,
                             h_next, x1_next)

    @pl.when(i == 0)
    def _():
        back(*front())

    @pl.when(i > 0)
    def _():
        h_cur[...] = h_next[...]
        x1_cur[...] = x1_next[...]
        h = h_cur[...]
        up = functools.partial(_mlp_up, h, w1_ref, b1_ref)
        down = functools.partial(_mlp_down, w2_ref)
        acts = [up(0)]
        y, mix = front()
        acts += [up(j) for j in range(1, D_FF // MLP_CHUNK)]
        mix = mix + jnp.dot(y, wout_ref[RET_WIDTH:, :], preferred_element_type=_F32)
        back(None, mix)
        acc = down(acts[0], 0)
        for j in range(1, len(acts)):
            acc = acc + down(acts[j], j)
        f = acc + b2_ref[...]
        g2 = modp_ref[0, 5:6, :]
        out_ref[0] = _norm_rows(ALPHA * x1_cur[...] + g2 * f) * l2g_ref[...] + l2b_ref[...]


def _mlp_up(h, w1_ref, b1_ref, j):
    cols = slice(j * MLP_CHUNK, (j + 1) * MLP_CHUNK)
    a = jnp.dot(h, w1_ref[:, cols], preferred_element_type=_F32) + b1_ref[:, cols]
    return jnp.square(jnp.maximum(a, 0.0)).astype(_BF)


def _mlp_down(w2_ref, a, j):
    return jnp.dot(a, w2_ref[j * MLP_CHUNK:(j + 1) * MLP_CHUNK, :], preferred_element_type=_F32)


def _post_mix_front(o_ref, yg_ref, u_ref, dsk_ref, wglu_ref, bglu_ref, wout_ref, y_scr, *, nb, spt, tm, pb):
    mix = jnp.dot(o_ref[0], wout_ref[0:RET_WIDTH, :], preferred_element_type=_F32)
    for j in range(S5_WIDTH // 128):
        for hf in range(2):
            for q in range(tm // 128):
                start = _octet_row_start(q, pb, nb=nb, spt=spt, tm=tm)
                w = [yg_ref[8 * j + gg, hf, pl.ds(start, 8, stride=nb), :] for gg in range(8)]
                v = _granule_transpose(w)
                for a in range(8):
                    y_scr[j, pl.ds(128 * q + 8 * hf + a, 8, stride=S5_T), :] = v[a]
    y = jnp.concatenate([y_scr[j] for j in range(S5_WIDTH // 128)], axis=1)
    y = y + dsk_ref[...] * u_ref[0].astype(_F32)
    y = jax.nn.gelu(y)
    y = y * jax.nn.sigmoid(jnp.dot(y.astype(_BF), wglu_ref[...], preferred_element_type=_F32)
                           + bglu_ref[...])
    return y.astype(_BF), mix


def _post_mix_back(x_ref, mod_ref, wout_ref, l1g_ref, l1b_ref, h_out, x1_out, y, mix):
    if y is not None:
        mix = mix + jnp.dot(y, wout_ref[RET_WIDTH:, :], preferred_element_type=_F32)
    g1 = mod_ref[0, 2:3, :]
    sh2 = mod_ref[0, 3:4, :]
    sc2 = mod_ref[0, 4:5, :]
    x1 = _norm_rows(ALPHA * x_ref[0] + g1 * mix) * l1g_ref[...] + l1b_ref[...]
    h_out[...] = (_norm_rows(x1) * (1.0 + sc2) + sh2).astype(_BF)
    x1_out[...] = x1


def _post(x, o_ret, yg, proj, mod, mod_row, pw):
    b, l, _ = x.shape
    tm = min(TOKEN_TILE, l)
    spt, nbg, lt = _tile_geometry(b, l, tm)
    rows = (tm // S5_T) * nbg
    n_steps = (lt // tm) * nbg
    item_a = lambda i: jnp.minimum(i, n_steps - 1)
    item_b = lambda i: jnp.maximum(i - 1, 0)
    tok = lambda w, item, cb=0: pl.BlockSpec((1, tm, w), lambda i: (item(i) % nbg, item(i) // nbg, cb))
    modspec = lambda item: pl.BlockSpec((1, 6, D_MODEL), lambda i: (mod_row(item(i) % nbg), 0, 0))
    row = lambda n: _const_spec((1, n))
    return pl.pallas_call(
        functools.partial(_post_kernel, nb=b, spt=spt, tm=tm, nbg=nbg, n_steps=n_steps),
        grid=(n_steps + 1,),
        in_specs=[tok(D_MODEL, item_a), tok(RET_WIDTH, item_a),
                  (_const_spec((S5_GROUPS, 2, rows, 128)) if lt == tm else
                   pl.BlockSpec((S5_GROUPS, 2, rows, 128), lambda i: (0, 0, item_a(i) // nbg, 0))),
                  tok(S5_WIDTH, item_a, 4),
                  modspec(item_a), modspec(item_b),
                  row(S5_WIDTH), _const_spec((S5_WIDTH, S5_WIDTH)), row(S5_WIDTH),
                  _const_spec((D_MODEL, D_MODEL)), row(D_MODEL), row(D_MODEL),
                  _const_spec((D_MODEL, D_FF)), row(D_FF), _const_spec((D_FF, D_MODEL)), row(D_MODEL),
                  row(D_MODEL), row(D_MODEL)],
        out_specs=tok(D_MODEL, item_b),
        out_shape=jax.ShapeDtypeStruct((nbg, lt, D_MODEL), _F32),
        scratch_shapes=[pltpu.VMEM((S5_WIDTH // 128, tm, 128), _F32),
                        pltpu.VMEM((tm, D_MODEL), _BF), pltpu.VMEM((tm, D_MODEL), _F32),
                        pltpu.VMEM((tm, D_MODEL), _BF), pltpu.VMEM((tm, D_MODEL), _F32)],
        compiler_params=pltpu.CompilerParams(
            dimension_semantics=("arbitrary",), vmem_limit_bytes=VMEM_LIMIT),
        name="post",
    )(x.reshape(nbg, lt, D_MODEL), o_ret.reshape(nbg, lt, RET_WIDTH), yg,
      proj.reshape(nbg, lt, IN_COLS), mod, mod, *pw).reshape(b, l, D_MODEL)


def _pack_s5_state(b, s5_s0_re, s5_s0_im):
    g = S5_GROUPS
    if s5_s0_re is None:
        return jnp.zeros((g, b, 256), _F32)
    s0p = jnp.concatenate([s5_s0_re, s5_s0_im], axis=-1)
    return s0p.transpose(2, 0, 1, 3).reshape(g, b, 256)


def kernel(x_prompt, x_sample, state_ret, state_s5_re, state_s5_im, c, c_ctx, w_ada, b_ada, w_in,
           ret_decay, s5_a_re, s5_a_im, s5_log_dt, s5_b_re, s5_b_im, s5_c_re, s5_c_im, s5_d, w_glu,
           b_glu, w_out, ln1_g, ln1_b, w_ff1, b_ff1, w_ff2, b_ff2, ln2_g, ln2_b):
    depth = w_ada.shape[0]
    bs = x_sample.shape[0]
    g = S5_GROUPS
    y_p, y_s = x_prompt, x_sample
    rets, s5rs, s5is = [], [], []
    for layer in range(depth):
        cond8 = jnp.concatenate(
            [c_ctx[None, :], c, jnp.zeros((8 - 1 - bs, D_MODEL), _F32)], axis=0)
        *s5_ops, w_in_bf, mod = _s5_operators(
            s5_a_re[layer], s5_a_im[layer], s5_log_dt[layer], s5_b_re[layer], s5_b_im[layer],
            s5_c_re[layer], s5_c_im[layer], w_in[layer], cond8, w_ada[layer], b_ada[layer])
        mod = mod.reshape(8, 6, D_MODEL)
        r2 = lambda a: a[layer].reshape(1, -1)
        mats = (w_glu[layer], w_out[layer], w_ff1[layer], w_ff2[layer])
        rows_p, rows_s = (lambda i: 0), (lambda i: i + 1)
        proj_p, xg_p, (w_glu_bf, w_out_bf, w_ff1_bf, w_ff2_bf) = _projection(y_p, mod, rows_p, w_in_bf, mats)
        proj_s, xg_s, _ = _projection(y_s, mod, rows_s, w_in_bf)
        o_p, r_st = _retention(proj_p, ret_decay[layer], None)
        o_s, _ = _retention(proj_s, ret_decay[layer], state_ret[:, layer])
        bp, lp, _ = y_p.shape
        segs = ((bp, lp // S5_T), (bs, y_s.shape[1] // S5_T))
        (yg_p, yg_s), (fin_p, _) = _s5_scan(
            (xg_p, xg_s), *s5_ops,
            (_pack_s5_state(bp, None, None), _pack_s5_state(bs, state_s5_re[:, layer], state_s5_im[:, layer])),
            segs)
        fin_p = fin_p.reshape(g, bp, 2, 2, S5_STATE).transpose(1, 2, 0, 3, 4)
        pw = (r2(s5_d), w_glu_bf, r2(b_glu), w_out_bf, r2(ln1_g), r2(ln1_b), w_ff1_bf, r2(b_ff1),
              w_ff2_bf, r2(b_ff2), r2(ln2_g), r2(ln2_b))
        y_p = _post(y_p, o_p, yg_p, proj_p, mod, rows_p, pw)
        y_s = _post(y_s, o_s, yg_s, proj_s, mod, rows_s, pw)
        rets.append(r_st)
        s5rs.append(fin_p[:, :, :, 0, :])
        s5is.append(fin_p[:, :, :, 1, :])
    return (y_p, y_s, jnp.stack(rets, axis=1), jnp.stack(s5rs, axis=1), jnp.stack(s5is, axis=1))
```

```python
import functools

import jax
import jax.numpy as jnp
from jax import lax
from jax.experimental import pallas as pl
from jax.experimental.pallas import tpu as pltpu

D_MODEL = 1024
RET_HEADS = 4
RET_DK = 128
RET_WIDTH = RET_HEADS * RET_DK
S5_CH = 16
S5_GROUPS = 32
S5_STATE = 64
S5_WIDTH = S5_GROUPS * S5_CH
D_FF = 4 * D_MODEL
IN_COLS = 4 * RET_WIDTH + S5_WIDTH
ALPHA = 2.0 ** 0.25
LN_EPS = 1e-5

S5_T = 16
S5_ROW = S5_T * S5_CH
S5_PREP_GROUP_TILE = 4
S5_GROUP_TILE = 8
RET_CHUNK = 256
RET_UNIT_TOKENS = 1024
TOKEN_TILE = 512
MLP_CHUNK = 2048
VMEM_LIMIT = 56 * 1024 * 1024

_BF = jnp.bfloat16
_F32 = jnp.float32


def _norm_rows(x):
    mu = jnp.mean(x, axis=-1, keepdims=True)
    xc = x - mu
    var = jnp.mean(xc * xc, axis=-1, keepdims=True)
    return xc * lax.rsqrt(var + LN_EPS)


def _const_spec(shape):
    nd = len(shape)
    return pl.BlockSpec(shape, lambda *_: (0,) * nd, pipeline_mode=pl.Buffered(1))


def _granule_transpose(v):
    lane = lax.broadcasted_iota(jnp.int32, (1, 128), 1)
    v = list(v)
    for d in (4, 2, 1):
        bit = ((lane // S5_CH) & d) != 0
        nv = list(v)
        for a in range(8):
            if a & d:
                continue
            lo, hi = v[a], v[a + d]
            nv[a] = jnp.where(bit, pltpu.roll(hi, S5_CH * d, axis=1), lo)
            nv[a + d] = jnp.where(bit, hi, pltpu.roll(lo, 128 - S5_CH * d, axis=1))
        v = nv
    return v


def _tile_geometry(b, l, tm):
    spt = max(1, tm // l)
    return spt, b // spt, l * spt


def _octet_row_start(q, pb, *, nb, spt, tm):
    per_seq = (tm // spt) // 128
    return 8 * (q % per_seq) * nb + pb * spt + q // per_seq


def _proj_kernel(*refs, nb, spt, tm, n_cast):
    x_ref, mod_ref, w_ref = refs[:3]
    cast_in = refs[3:3 + n_cast]
    o_ref, xg_ref = refs[3 + n_cast:5 + n_cast]
    cast_out = refs[5 + n_cast:5 + 2 * n_cast]
    u_scr = refs[5 + 2 * n_cast]
    for src, dst in zip(cast_in, cast_out):
        dst[...] = src[...].astype(_BF)
    x = x_ref[0]
    sh = mod_ref[0, 0:1, :]
    sc = mod_ref[0, 1:2, :]
    h = (_norm_rows(x) * (1.0 + sc) + sh).astype(_BF)
    u = jnp.dot(h, w_ref[:, 4 * RET_WIDTH:], preferred_element_type=_F32)
    o_ref[0, :, 4 * RET_WIDTH:] = u.astype(_BF)
    for j in range(S5_WIDTH // 128):
        u_scr[j] = u[:, 128 * j:128 * (j + 1)]
    o_ref[0, :, 0:4 * RET_WIDTH] = jnp.dot(
        h, w_ref[:, 0:4 * RET_WIDTH], preferred_element_type=_F32).astype(_BF)
    pb = pl.program_id(1)
    for j in range(S5_WIDTH // 128):
        for hf in range(2):
            for q in range(tm // 128):
                v = [u_scr[j, pl.ds(128 * q + 8 * hf + a, 8, stride=S5_T), :] for a in range(8)]
                w = _granule_transpose(v)
                start = _octet_row_start(q, pb, nb=nb, spt=spt, tm=tm)
                for gg in range(8):
                    xg_ref[8 * j + gg, hf, pl.ds(start, 8, stride=nb), :] = w[gg]


def _projection(x, mod, mod_row, w_in_bf, cast_weights=()):
    b, l, _ = x.shape
    tm = TOKEN_TILE
    spt, nbg, lt = _tile_geometry(b, l, tm)
    rows = (tm // S5_T) * nbg
    n_steps = (lt // tm) * nbg
    cast_specs = [pl.BlockSpec((w.shape[0] // n_steps, w.shape[1]), lambda j, i: (j * nbg + i, 0))
                  for w in cast_weights]
    outs = pl.pallas_call(
        functools.partial(_proj_kernel, nb=b, spt=spt, tm=tm, n_cast=len(cast_weights)),
        grid=(lt // tm, nbg),
        in_specs=[pl.BlockSpec((1, tm, D_MODEL), lambda j, i: (i, j, 0)),
                  pl.BlockSpec((1, 6, D_MODEL), lambda j, i: (mod_row(i), 0, 0)),
                  _const_spec((D_MODEL, IN_COLS))] + cast_specs,
        out_specs=[pl.BlockSpec((1, tm, IN_COLS), lambda j, i: (i, j, 0)),
                   pl.BlockSpec((S5_GROUPS, 2, rows, 128), lambda j, i: (0, 0, j, 0))] + cast_specs,
        out_shape=[jax.ShapeDtypeStruct((nbg, lt, IN_COLS), _BF),
                   jax.ShapeDtypeStruct((S5_GROUPS, 2, (l // S5_T) * b, 128), _F32)]
                  + [jax.ShapeDtypeStruct(w.shape, _BF) for w in cast_weights],
        scratch_shapes=[pltpu.VMEM((S5_WIDTH // 128, tm, 128), _F32)],
        compiler_params=pltpu.CompilerParams(
            dimension_semantics=("arbitrary", "arbitrary"), vmem_limit_bytes=VMEM_LIMIT),
        name="projection",
    )(x.reshape(nbg, lt, D_MODEL), mod, w_in_bf, *cast_weights)
    return outs[0].reshape(b, l, IN_COLS), outs[1], tuple(outs[2:])


def _ret_kernel(*refs, chunk, n_chunks, nbb, has_s0):
    if has_s0:
        dec_ref, q_ref, k_ref, v_ref, g_ref, s0_ref, o_ref, st_ref, kv_scr, dec_scr = refs
    else:
        dec_ref, q_ref, k_ref, v_ref, g_ref, o_ref, st_ref, kv_scr, dec_scr = refs
        s0_ref = None
    c = chunk
    scale = RET_DK ** -0.5
    pos = lax.broadcasted_iota(jnp.int32, (c, 1), 0).astype(_F32)
    ri = lax.broadcasted_iota(jnp.int32, (c, c), 0)
    ci = lax.broadcasted_iota(jnp.int32, (c, c), 1)
    rel = (ri - ci).astype(_F32)

    for hd in range(RET_HEADS):
        lanes = slice(hd * RET_DK, (hd + 1) * RET_DK)

        def log_gamma(d):
            z = jnp.full((1, 1), dec_ref[d, hd], _F32)
            return jnp.minimum(z, 0.0) - jnp.log(1.0 + jnp.exp(-jnp.abs(z)))

        lg_f, lg_b = log_gamma(0), log_gamma(1)
        wide = lambda col: jnp.broadcast_to(col, (c, RET_DK))
        dec_scr[hd, 0] = wide(jnp.exp(lg_f * (c - 1.0 - pos)) * scale)
        dec_scr[hd, 1] = wide(jnp.exp(lg_b * pos) * scale)
        dec_scr[hd, 2] = wide(jnp.exp(lg_f * (pos + 1.0)))
        dec_scr[hd, 3] = wide(jnp.exp(lg_b * (c - pos)))
        dmat = (jnp.where(ri >= ci, jnp.exp(lg_f * jnp.maximum(rel, 0.0)), 0.0)
                + jnp.where(ci >= ri, jnp.exp(lg_b * jnp.maximum(-rel, 0.0)), 0.0)) * scale
        cdec_f = jnp.exp(lg_f * float(c))
        cdec_b = jnp.exp(lg_b * float(c))

        for bb in range(nbb):
            unit = bb * RET_HEADS + hd
            for n in range(n_chunks):
                kc = k_ref[bb, n * c:(n + 1) * c, lanes]
                vc = v_ref[bb, n * c:(n + 1) * c, lanes].astype(_F32)
                vcat = jnp.concatenate([(vc * dec_scr[hd, 0]).astype(_BF), (vc * dec_scr[hd, 1]).astype(_BF)], axis=1)
                kv_scr[unit, n] = lax.dot_general(kc, vcat, (((0,), (0,)), ((), ())),
                                                  preferred_element_type=_F32)
            if has_s0:
                carry_f = s0_ref[bb, 0, hd]
                carry_b = s0_ref[bb, 1, hd]
            else:
                carry_f = jnp.zeros((RET_DK, RET_DK), _F32)
                carry_b = jnp.zeros((RET_DK, RET_DK), _F32)
            for n in range(n_chunks):
                inc = kv_scr[unit, n, :, 0:RET_DK]
                kv_scr[unit, n, :, 0:RET_DK] = carry_f
                carry_f = cdec_f * carry_f + inc
            for n in range(n_chunks - 1, -1, -1):
                inc = kv_scr[unit, n, :, RET_DK:2 * RET_DK]
                kv_scr[unit, n, :, RET_DK:2 * RET_DK] = carry_b
                carry_b = cdec_b * carry_b + inc
            st_ref[bb, 0, hd] = carry_f
            st_ref[bb, 1, hd] = carry_b

            for n in range(n_chunks):
                sl = slice(n * c, (n + 1) * c)
                qc = q_ref[bb, sl, lanes]
                kc = k_ref[bb, sl, lanes]
                vc = v_ref[bb, sl, lanes]
                scores = lax.dot_general(qc, kc, (((1,), (1,)), ((), ())), preferred_element_type=_F32)
                o = jnp.dot((scores * dmat).astype(_BF), vc, preferred_element_type=_F32)
                oi = jnp.dot(qc, kv_scr[unit, n].astype(_BF), preferred_element_type=_F32)
                o = o + oi[:, 0:RET_DK] * dec_scr[hd, 2] + oi[:, RET_DK:2 * RET_DK] * dec_scr[hd, 3]
                gate = g_ref[bb, sl, lanes].astype(_F32)
                o_ref[bb, sl, lanes] = (_norm_rows(o) * (gate * jax.nn.sigmoid(gate))).astype(_BF)


def _retention(proj, ret_decay, s0):
    b, l, _ = proj.shape
    c = min(RET_CHUNK, l)
    n_chunks = l // c
    nbb = max(1, min(b, RET_UNIT_TOKENS // l))
    has_s0 = s0 is not None
    col = lambda j: pl.BlockSpec((nbb, l, RET_WIDTH), lambda i: (i, 0, j))
    st_spec = pl.BlockSpec((nbb, 2, RET_HEADS, RET_DK, RET_DK), lambda i: (i, 0, 0, 0, 0))
    in_specs = [pl.BlockSpec(memory_space=pltpu.SMEM), col(0), col(1), col(2), col(3)]
    args = [ret_decay, proj, proj, proj, proj]
    if has_s0:
        in_specs.append(st_spec)
        args.append(s0)
    return pl.pallas_call(
        functools.partial(_ret_kernel, chunk=c, n_chunks=n_chunks, nbb=nbb, has_s0=has_s0),
        grid=(b // nbb,),
        in_specs=in_specs,
        out_specs=[pl.BlockSpec((nbb, l, RET_WIDTH), lambda i: (i, 0, 0)), st_spec],
        out_shape=[jax.ShapeDtypeStruct((b, l, RET_WIDTH), _BF),
                   jax.ShapeDtypeStruct((b, 2, RET_HEADS, RET_DK, RET_DK), _F32)],
        scratch_shapes=[pltpu.VMEM((nbb * RET_HEADS, n_chunks, RET_DK, 2 * RET_DK), _F32),
                        pltpu.VMEM((RET_HEADS, 4, c, RET_DK), _F32)],
        compiler_params=pltpu.CompilerParams(
            dimension_semantics=("arbitrary",), vmem_limit_bytes=VMEM_LIMIT),
        name="retention",
    )(*args)


def _swap_halves(z):
    return pltpu.roll(z, S5_STATE, axis=z.ndim - 1)


def _s5_prep_kernel(ar_ref, ai_ref, ldt_ref, bt_ref, cp_ref, win_ref, cond_ref, wada_ref, bada_ref,
                    m_ref, wst_ref, cout_ref, apow_ref, winb_ref, mod_ref, *, ng):
    winb_ref[...] = win_ref[...].astype(_BF)
    cond = cond_ref[...]
    mod_ref[...] = jnp.dot((cond * jax.nn.sigmoid(cond)).astype(_BF), wada_ref[...].astype(_BF),
                           preferred_element_type=_F32) + bada_ref[...]
    lane = lax.broadcasted_iota(jnp.int32, (1, 2 * S5_STATE), 1)
    sgn = jnp.where(lane < S5_STATE, -1.0, 1.0).astype(_F32)
    gran = lax.broadcasted_iota(jnp.int32, (1, S5_ROW), 1) // S5_CH
    ar = ar_ref[...]
    ai = ai_ref[...]
    dt = jnp.exp(ldt_ref[...])
    mag = jnp.exp(ar * dt)
    ang = ai * dt
    pr1 = mag * jnp.cos(ang)
    pi1 = mag * jnp.sin(ang)
    pw_r = [jnp.ones_like(pr1), pr1]
    pw_i = [jnp.zeros_like(pi1), pi1]
    for _ in range(2, S5_T + 1):
        pr, pi = pw_r[-1], pw_i[-1]
        pw_r.append(pr * pr1 - pi * pi1)
        pw_i.append(pr * pi1 + pi * pr1)
    pw_is = [p * sgn for p in pw_i]
    x2 = pr1 - 1.0
    den = ar * ar + ai * ai
    coef_re = (x2 * ar + pi1 * ai) / den
    coef_im_s = ((pi1 * ar - x2 * ai) / den) * sgn
    for gi in range(ng):
        bt = bt_ref[gi]
        cp = cp_ref[gi]
        cc = cp * (-sgn)
        cps = _swap_halves(cp)
        bts = _swap_halves(bt)
        gens = []
        cout_rows = [[], []]
        for d in range(2):
            r = 2 * gi + d
            row = lambda a: a[r:r + 1, :]
            bb = row(coef_re) * bt + row(coef_im_s) * bts
            bbs = _swap_halves(bb)
            w_rows = []
            for m in range(S5_T):
                e = (S5_T - 1 - m) if d == 0 else m
                w = row(pw_r[e]) * bb + row(pw_is[e]) * bbs
                rows = slice(m * S5_CH, (m + 1) * S5_CH)
                wst_ref[gi, rows, d * 128:(d + 1) * 128] = w.astype(_BF)
                wst_ref[gi, rows, 256 + d * 128:256 + (d + 1) * 128] = _swap_halves(w).astype(_BF)
                w_rows.append(w)
            for i in range(S5_T):
                e = (i + 1) if d == 0 else (S5_T - i)
                gmat = row(pw_r[e]) * cp + row(pw_is[e]) * cps
                cout_rows[d].append(gmat * (-sgn))
            gens.append(lax.dot_general(cc, jnp.concatenate(w_rows, axis=0), (((1,), (1,)), ((), ())),
                                        preferred_element_type=_F32, precision=lax.Precision.HIGHEST))
            apow_ref[gi, 2 * d:2 * d + 1, :] = row(pw_r[S5_T])
            apow_ref[gi, 2 * d + 1:2 * d + 2, :] = row(pw_is[S5_T])
        blocks = []
        for t in range(S5_T):
            sf = (S5_CH * (t + 1)) % S5_ROW
            rf = pltpu.roll(gens[0], sf, axis=1) if sf else gens[0]
            rb = pltpu.roll(gens[1], S5_CH * t, axis=1) if t else gens[1]
            blocks.append(jnp.where(gran <= t, rf, 0.0) + jnp.where(gran >= t, rb, 0.0))
        m_ref[gi] = jnp.concatenate(blocks, axis=0).T.astype(_BF)
        cout_t = jnp.concatenate([jnp.concatenate(cout_rows[0], axis=0),
                                  jnp.concatenate(cout_rows[1], axis=0)], axis=1)
        cout_ref[gi] = cout_t.T.astype(_BF)


def _s5_operators(s5_a_re, s5_a_im, s5_log_dt, s5_b_re, s5_b_im, s5_c_re, s5_c_im, w_in, cond8, w_ada, b_ada):
    g = S5_GROUPS
    ng = S5_PREP_GROUP_TILE
    steps = g // ng
    wspec = pl.BlockSpec((w_in.shape[0] // steps, w_in.shape[1]), lambda i: (i, 0))
    n_mod = w_ada.shape[1]
    tn = n_mod // steps
    dup = lambda a: jnp.concatenate([a, a], axis=-1).transpose(1, 0, 2).reshape(2 * g, 128)
    ar2 = dup(s5_a_re)
    ai2 = dup(s5_a_im)
    ldt = s5_log_dt.T.reshape(2 * g, 1)
    btp = jnp.concatenate([s5_b_re.transpose(0, 2, 1), s5_b_im.transpose(0, 2, 1)], axis=-1)
    cpk = jnp.concatenate([s5_c_re, s5_c_im], axis=-1)
    gspec = lambda *shape: pl.BlockSpec((ng,) + shape, lambda i: (i,) + (0,) * len(shape))
    rspec = lambda w: pl.BlockSpec((2 * ng, w), lambda i: (i, 0))
    return pl.pallas_call(
        functools.partial(_s5_prep_kernel, ng=ng),
        grid=(g // ng,),
        in_specs=[rspec(128), rspec(128), rspec(1), gspec(S5_CH, 128), gspec(S5_CH, 128), wspec,
                  pl.BlockSpec((8, D_MODEL), lambda i: (0, 0)),
                  pl.BlockSpec((D_MODEL, tn), lambda i: (0, i)),
                  pl.BlockSpec((1, tn), lambda i: (0, i))],
        out_specs=[gspec(S5_ROW, S5_ROW), gspec(S5_ROW, 512), gspec(256, S5_ROW), gspec(4, 128), wspec,
                   pl.BlockSpec((8, tn), lambda i: (0, i))],
        out_shape=[jax.ShapeDtypeStruct((g, S5_ROW, S5_ROW), _BF),
                   jax.ShapeDtypeStruct((g, S5_ROW, 512), _BF),
                   jax.ShapeDtypeStruct((g, 256, S5_ROW), _BF),
                   jax.ShapeDtypeStruct((g, 4, 128), _F32),
                   jax.ShapeDtypeStruct(w_in.shape, _BF),
                   jax.ShapeDtypeStruct((8, n_mod), _F32)],
        compiler_params=pltpu.CompilerParams(dimension_semantics=("arbitrary",)),
        name="s5_operators",
    )(ar2, ai2, ldt, btp, cpk, w_in, cond8, w_ada, b_ada.reshape(1, n_mod))


def _s5_kernel(*refs, segs, ng):
    n = len(segs)
    x_refs = refs[0:n]
    m_ref, wst_ref, cout_ref, apow_ref = refs[n:n + 4]
    s0_refs = refs[n + 4:2 * n + 4]
    y_refs = refs[2 * n + 4:3 * n + 4]
    fin_refs = refs[3 * n + 4:4 * n + 4]
    scr = refs[4 * n + 4:]
    coef = [[apow_ref[gi, r:r + 1, :] for r in range(4)] for gi in range(ng)]
    for si, (nb, nk) in enumerate(segs):
        x_ref, s0_ref, y_ref, fin_ref = x_refs[si], s0_refs[si], y_refs[si], fin_refs[si]
        loc_scr, prev_scr = scr[2 * si], scr[2 * si + 1]
        load_x = lambda gi: jnp.concatenate([x_ref[gi, 0], x_ref[gi, 1]], axis=1).astype(_BF)
        for gi in range(ng):
            loc_scr[gi] = jnp.dot(load_x(gi), wst_ref[gi], preferred_element_type=_F32)
        st = []
        for gi in range(ng):
            zf = s0_ref[gi, :, 0:128]
            zb = s0_ref[gi, :, 128:256]
            st.append([zf, _swap_halves(zf), zb, _swap_halves(zb)])
        for k in range(nk):
            kb = nk - 1 - k
            rows_f = slice(k * nb, (k + 1) * nb)
            rows_b = slice(kb * nb, (kb + 1) * nb)
            for gi in range(ng):
                ar_f, ai_f, ar_b, ai_b = coef[gi]
                zf, zfs, zb, zbs = st[gi]
                prev_scr[gi, rows_f, 0:128] = zf
                prev_scr[gi, rows_b, 128:256] = zb
                st[gi] = [ar_f * zf + ai_f * zfs + loc_scr[gi, rows_f, 0:128],
                          ar_f * zfs - ai_f * zf + loc_scr[gi, rows_f, 256:384],
                          ar_b * zb + ai_b * zbs + loc_scr[gi, rows_b, 128:256],
                          ar_b * zbs - ai_b * zb + loc_scr[gi, rows_b, 384:512]]
        for gi in range(ng):
            fin_ref[gi, :, 0:128] = st[gi][0]
            fin_ref[gi, :, 128:256] = st[gi][2]
            y = jnp.dot(load_x(gi), m_ref[gi], preferred_element_type=_F32)
            y = y + jnp.dot(prev_scr[gi].astype(_BF), cout_ref[gi], preferred_element_type=_F32)
            y_ref[gi, 0] = y[:, 0:128]
            y_ref[gi, 1] = y[:, 128:256]


def _s5_scan(xgs, m_op, wst, cout, apow, s0ps, segs):
    g = S5_GROUPS
    ng = S5_GROUP_TILE
    n = len(segs)
    gspec = lambda *shape: pl.BlockSpec((ng,) + shape, lambda i: (i,) + (0,) * len(shape))
    rows = [nb * nk for nb, nk in segs]
    outs = pl.pallas_call(
        functools.partial(_s5_kernel, segs=tuple(segs), ng=ng),
        grid=(g // ng,),
        in_specs=[gspec(2, r, 128) for r in rows]
                 + [gspec(S5_ROW, S5_ROW), gspec(S5_ROW, 512), gspec(256, S5_ROW), gspec(4, 128)]
                 + [gspec(nb, 256) for nb, _ in segs],
        out_specs=[gspec(2, r, 128) for r in rows] + [gspec(nb, 256) for nb, _ in segs],
        out_shape=[jax.ShapeDtypeStruct((g, 2, r, 128), _F32) for r in rows]
                  + [jax.ShapeDtypeStruct((g, nb, 256), _F32) for nb, _ in segs],
        scratch_shapes=[buf for r in rows
                        for buf in (pltpu.VMEM((ng, r, 512), _F32), pltpu.VMEM((ng, r, 256), _F32))],
        compiler_params=pltpu.CompilerParams(
            dimension_semantics=("arbitrary",), vmem_limit_bytes=VMEM_LIMIT),
        name="s5_scan",
    )(*xgs, m_op, wst, cout, apow, *s0ps)
    return outs[:n], outs[n:]


def _post_kernel(x_ref, o_ref, yg_ref, u_ref, mod_ref, modp_ref, dsk_ref, wglu_ref, bglu_ref, wout_ref,
                 l1g_ref, l1b_ref, w1_ref, b1_ref, w2_ref, b2_ref, l2g_ref, l2b_ref, out_ref,
                 y_scr, h_next, x1_next, h_cur, x1_cur, *, nb, spt, tm, nbg, n_steps):
    i = pl.program_id(0)
    pb = jnp.minimum(i, n_steps - 1) % nbg
    front = functools.partial(_post_mix_front, o_ref, yg_ref, u_ref, dsk_ref, wglu_ref, bglu_ref,
                              wout_ref, y_scr, nb=nb, spt=spt, tm=tm, pb=pb)
    back = functools.partial(_post_mix_back, x_ref, mod_ref, wout_ref, l1g_ref, l1b_ref,
                             h_next, x1_next)

    @pl.when(i == 0)
    def _():
        back(*front())

    @pl.when(i > 0)
    def _():
        h_cur[...] = h_next[...]
        x1_cur[...] = x1_next[...]
        h = h_cur[...]
        up = functools.partial(_mlp_up, h, w1_ref, b1_ref)
        down = functools.partial(_mlp_down, w2_ref)
        acts = [up(0)]
        y, mix = front()
        acts += [up(j) for j in range(1, D_FF // MLP_CHUNK)]
        mix = mix + jnp.dot(y, wout_ref[RET_WIDTH:, :], preferred_element_type=_F32)
        back(None, mix)
        acc = down(acts[0], 0)
        for j in range(1, len(acts)):
            acc = acc + down(acts[j], j)
        f = acc + b2_ref[...]
        g2 = modp_ref[0, 5:6, :]
        out_ref[0] = _norm_rows(ALPHA * x1_cur[...] + g2 * f) * l2g_ref[...] + l2b_ref[...]


def _mlp_up(h, w1_ref, b1_ref, j):
    cols = slice(j * MLP_CHUNK, (j + 1) * MLP_CHUNK)
    a = jnp.dot(h, w1_ref[:, cols], preferred_element_type=_F32) + b1_ref[:, cols]
    return jnp.square(jnp.maximum(a, 0.0)).astype(_BF)


def _mlp_down(w2_ref, a, j):
    return jnp.dot(a, w2_ref[j * MLP_CHUNK:(j + 1) * MLP_CHUNK, :], preferred_element_type=_F32)


def _post_mix_front(o_ref, yg_ref, u_ref, dsk_ref, wglu_ref, bglu_ref, wout_ref, y_scr, *, nb, spt, tm, pb):
    mix = jnp.dot(o_ref[0], wout_ref[0:RET_WIDTH, :], preferred_element_type=_F32)
    for j in range(S5_WIDTH // 128):
        for hf in range(2):
            for q in range(tm // 128):
                start = _octet_row_start(q, pb, nb=nb, spt=spt, tm=tm)
                w = [yg_ref[8 * j + gg, hf, pl.ds(start, 8, stride=nb), :] for gg in range(8)]
                v = _granule_transpose(w)
                for a in range(8):
                    y_scr[j, pl.ds(128 * q + 8 * hf + a, 8, stride=S5_T), :] = v[a]
    y = jnp.concatenate([y_scr[j] for j in range(S5_WIDTH // 128)], axis=1)
    y = y + dsk_ref[...] * u_ref[0].astype(_F32)
    y = jax.nn.gelu(y)
    y = y * jax.nn.sigmoid(jnp.dot(y.astype(_BF), wglu_ref[...], preferred_element_type=_F32)
                           + bglu_ref[...])
    return y.astype(_BF), mix


def _post_mix_back(x_ref, mod_ref, wout_ref, l1g_ref, l1b_ref, h_out, x1_out, y, mix):
    if y is not None:
        mix = mix + jnp.dot(y, wout_ref[RET_WIDTH:, :], preferred_element_type=_F32)
    g1 = mod_ref[0, 2:3, :]
    sh2 = mod_ref[0, 3:4, :]
    sc2 = mod_ref[0, 4:5, :]
    x1 = _norm_rows(ALPHA * x_ref[0] + g1 * mix) * l1g_ref[...] + l1b_ref[...]
    h_out[...] = (_norm_rows(x1) * (1.0 + sc2) + sh2).astype(_BF)
    x1_out[...] = x1


def _post(x, o_ret, yg, proj, mod, mod_row, pw):
    b, l, _ = x.shape
    tm = min(TOKEN_TILE, l)
    spt, nbg, lt = _tile_geometry(b, l, tm)
    rows = (tm // S5_T) * nbg
    n_steps = (lt // tm) * nbg
    item_a = lambda i: jnp.minimum(i, n_steps - 1)
    item_b = lambda i: jnp.maximum(i - 1, 0)
    tok = lambda w, item, cb=0: pl.BlockSpec((1, tm, w), lambda i: (item(i) % nbg, item(i) // nbg, cb))
    modspec = lambda item: pl.BlockSpec((1, 6, D_MODEL), lambda i: (mod_row(item(i) % nbg), 0, 0))
    row = lambda n: _const_spec((1, n))
    return pl.pallas_call(
        functools.partial(_post_kernel, nb=b, spt=spt, tm=tm, nbg=nbg, n_steps=n_steps),
        grid=(n_steps + 1,),
        in_specs=[tok(D_MODEL, item_a), tok(RET_WIDTH, item_a),
                  (_const_spec((S5_GROUPS, 2, rows, 128)) if lt == tm else
                   pl.BlockSpec((S5_GROUPS, 2, rows, 128), lambda i: (0, 0, item_a(i) // nbg, 0))),
                  tok(S5_WIDTH, item_a, 4),
                  modspec(item_a), modspec(item_b),
                  row(S5_WIDTH), _const_spec((S5_WIDTH, S5_WIDTH)), row(S5_WIDTH),
                  _const_spec((D_MODEL, D_MODEL)), row(D_MODEL), row(D_MODEL),
                  _const_spec((D_MODEL, D_FF)), row(D_FF), _const_spec((D_FF, D_MODEL)), row(D_MODEL),
                  row(D_MODEL), row(D_MODEL)],
        out_specs=tok(D_MODEL, item_b),
        out_shape=jax.ShapeDtypeStruct((nbg, lt, D_MODEL), _F32),
        scratch_shapes=[pltpu.VMEM((S5_WIDTH // 128, tm, 128), _F32),
                        pltpu.VMEM((tm, D_MODEL), _BF), pltpu.VMEM((tm, D_MODEL), _F32),
                        pltpu.VMEM((tm, D_MODEL), _BF), pltpu.VMEM((tm, D_MODEL), _F32)],
        compiler_params=pltpu.CompilerParams(
            dimension_semantics=("arbitrary",), vmem_limit_bytes=VMEM_LIMIT),
        name="post",
    )(x.reshape(nbg, lt, D_MODEL), o_ret.reshape(nbg, lt, RET_WIDTH), yg,
      proj.reshape(nbg, lt, IN_COLS), mod, mod, *pw).reshape(b, l, D_MODEL)


def _pack_s5_state(b, s5_s0_re, s5_s0_im):
    g = S5_GROUPS
    if s5_s0_re is None:
        return jnp.zeros((g, b, 256), _F32)
    s0p = jnp.concatenate([s5_s0_re, s5_s0_im], axis=-1)
    return s0p.transpose(2, 0, 1, 3).reshape(g, b, 256)


def kernel(x_prompt, x_sample, state_ret, state_s5_re, state_s5_im, c, c_ctx, w_ada, b_ada, w_in,
           ret_decay, s5_a_re, s5_a_im, s5_log_dt, s5_b_re, s5_b_im, s5_c_re, s5_c_im, s5_d, w_glu,
           b_glu, w_out, ln1_g, ln1_b, w_ff1, b_ff1, w_ff2, b_ff2, ln2_g, ln2_b):
    depth = w_ada.shape[0]
    bs = x_sample.shape[0]
    g = S5_GROUPS
    y_p, y_s = x_prompt, x_sample
    rets, s5rs, s5is = [], [], []
    for layer in range(depth):
        cond8 = jnp.concatenate(
            [c_ctx[None, :], c, jnp.zeros((8 - 1 - bs, D_MODEL), _F32)], axis=0)
        *s5_ops, w_in_bf, mod = _s5_operators(
            s5_a_re[layer], s5_a_im[layer], s5_log_dt[layer], s5_b_re[layer], s5_b_im[layer],
            s5_c_re[layer], s5_c_im[layer], w_in[layer], cond8, w_ada[layer], b_ada[layer])
        mod = mod.reshape(8, 6, D_MODEL)
        r2 = lambda a: a[layer].reshape(1, -1)
        rows_p, rows_s = (lambda i: 0), (lambda i: i + 1)
        proj_p, xg_p, (w_ff1_bf,) = _projection(y_p, mod, rows_p, w_in_bf, (w_ff1[layer],))
        proj_s, xg_s, (w_glu_bf, w_out_bf, w_ff2_bf) = _projection(
            y_s, mod, rows_s, w_in_bf, (w_glu[layer], w_out[layer], w_ff2[layer]))
        o_p, r_st = _retention(proj_p, ret_decay[layer], None)
        o_s, _ = _retention(proj_s, ret_decay[layer], state_ret[:, layer])
        bp, lp, _ = y_p.shape
        segs = ((bp, lp // S5_T), (bs, y_s.shape[1] // S5_T))
        (yg_p, yg_s), (fin_p, _) = _s5_scan(
            (xg_p, xg_s), *s5_ops,
            (_pack_s5_state(bp, None, None), _pack_s5_state(bs, state_s5_re[:, layer], state_s5_im[:, layer])),
            segs)
        fin_p = fin_p.reshape(g, bp, 2, 2, S5_STATE).transpose(1, 2, 0, 3, 4)
        pw = (r2(s5_d), w_glu_bf, r2(b_glu), w_out_bf, r2(ln1_g), r2(ln1_b), w_ff1_bf, r2(b_ff1),
              w_ff2_bf, r2(b_ff2), r2(ln2_g), r2(ln2_b))
        y_p = _post(y_p, o_p, yg_p, proj_p, mod, rows_p, pw)
        y_s = _post(y_s, o_s, yg_s, proj_s, mod, rows_s, pw)
        rets.append(r_st)
        s5rs.append(fin_p[:, :, :, 0, :])
        s5is.append(fin_p[:, :, :, 1, :])
    return (y_p, y_s, jnp.stack(rets, axis=1), jnp.stack(s5rs, axis=1), jnp.stack(s5is, axis=1))
```

```python
import functools

import jax
import jax.numpy as jnp
from jax import lax
from jax.experimental import pallas as pl
from jax.experimental.pallas import tpu as pltpu

D_MODEL = 1024
RET_HEADS = 4
RET_DK = 128
RET_WIDTH = RET_HEADS * RET_DK
S5_CH = 16
S5_GROUPS = 32
S5_STATE = 64
S5_WIDTH = S5_GROUPS * S5_CH
D_FF = 4 * D_MODEL
IN_COLS = 4 * RET_WIDTH + S5_WIDTH
ALPHA = 2.0 ** 0.25
LN_EPS = 1e-5

S5_T = 16
S5_ROW = S5_T * S5_CH
S5_PREP_GROUP_TILE = 4
S5_GROUP_TILE = 8
RET_CHUNK = 256
RET_UNIT_TOKENS = 1024
TOKEN_TILE = 512
MLP_CHUNK = 2048
VMEM_LIMIT = 56 * 1024 * 1024

_BF = jnp.bfloat16
_F32 = jnp.float32


def _norm_rows(x):
    mu = jnp.mean(x, axis=-1, keepdims=True)
    xc = x - mu
    var = jnp.mean(xc * xc, axis=-1, keepdims=True)
    return xc * lax.rsqrt(var + LN_EPS)


def _const_spec(shape):
    nd = len(shape)
    return pl.BlockSpec(shape, lambda *_: (0,) * nd, pipeline_mode=pl.Buffered(1))


def _granule_transpose(v):
    lane = lax.broadcasted_iota(jnp.int32, (1, 128), 1)
    v = list(v)
    for d in (4, 2, 1):
        bit = ((lane // S5_CH) & d) != 0
        nv = list(v)
        for a in range(8):
            if a & d:
                continue
            lo, hi = v[a], v[a + d]
            nv[a] = jnp.where(bit, pltpu.roll(hi, S5_CH * d, axis=1), lo)
            nv[a + d] = jnp.where(bit, hi, pltpu.roll(lo, 128 - S5_CH * d, axis=1))
        v = nv
    return v


def _tile_geometry(b, l, tm):
    spt = max(1, tm // l)
    return spt, b // spt, l * spt


def _octet_row_start(q, pb, *, nb, spt, tm):
    per_seq = (tm // spt) // 128
    return 8 * (q % per_seq) * nb + pb * spt + q // per_seq


def _proj_kernel(*refs, nb, spt, tm, n_cast):
    x_ref, mod_ref, w_ref = refs[:3]
    cast_in = refs[3:3 + n_cast]
    o_ref, xg_ref = refs[3 + n_cast:5 + n_cast]
    cast_out = refs[5 + n_cast:5 + 2 * n_cast]
    u_scr = refs[5 + 2 * n_cast]
    for src, dst in zip(cast_in, cast_out):
        dst[...] = src[...].astype(_BF)
    x = x_ref[0]
    sh = mod_ref[0, 0:1, :]
    sc = mod_ref[0, 1:2, :]
    h = (_norm_rows(x) * (1.0 + sc) + sh).astype(_BF)
    u = jnp.dot(h, w_ref[:, 4 * RET_WIDTH:], preferred_element_type=_F32)
    o_ref[0, :, 4 * RET_WIDTH:] = u.astype(_BF)
    for j in range(S5_WIDTH // 128):
        u_scr[j] = u[:, 128 * j:128 * (j + 1)]
    o_ref[0, :, 0:4 * RET_WIDTH] = jnp.dot(
        h, w_ref[:, 0:4 * RET_WIDTH], preferred_element_type=_F32).astype(_BF)
    pb = pl.program_id(1)
    for j in range(S5_WIDTH // 128):
        for hf in range(2):
            for q in range(tm // 128):
                v = [u_scr[j, pl.ds(128 * q + 8 * hf + a, 8, stride=S5_T), :] for a in range(8)]
                w = _granule_transpose(v)
                start = _octet_row_start(q, pb, nb=nb, spt=spt, tm=tm)
                for gg in range(8):
                    xg_ref[8 * j + gg, hf, pl.ds(start, 8, stride=nb), :] = w[gg]


def _projection(x, mod, mod_row, w_in_bf, cast_weights=()):
    b, l, _ = x.shape
    tm = TOKEN_TILE
    spt, nbg, lt = _tile_geometry(b, l, tm)
    rows = (tm // S5_T) * nbg
    n_steps = (lt // tm) * nbg
    cast_specs = [pl.BlockSpec((w.shape[0] // n_steps, w.shape[1]), lambda j, i: (j * nbg + i, 0))
                  for w in cast_weights]
    outs = pl.pallas_call(
        functools.partial(_proj_kernel, nb=b, spt=spt, tm=tm, n_cast=len(cast_weights)),
        grid=(lt // tm, nbg),
        in_specs=[pl.BlockSpec((1, tm, D_MODEL), lambda j, i: (i, j, 0)),
                  pl.BlockSpec((1, 6, D_MODEL), lambda j, i: (mod_row(i), 0, 0)),
                  _const_spec((D_MODEL, IN_COLS))] + cast_specs,
        out_specs=[pl.BlockSpec((1, tm, IN_COLS), lambda j, i: (i, j, 0)),
                   pl.BlockSpec((S5_GROUPS, 2, rows, 128), lambda j, i: (0, 0, j, 0))] + cast_specs,
        out_shape=[jax.ShapeDtypeStruct((nbg, lt, IN_COLS), _BF),
                   jax.ShapeDtypeStruct((S5_GROUPS, 2, (l // S5_T) * b, 128), _F32)]
                  + [jax.ShapeDtypeStruct(w.shape, _BF) for w in cast_weights],
        scratch_shapes=[pltpu.VMEM((S5_WIDTH // 128, tm, 128), _F32)],
        compiler_params=pltpu.CompilerParams(
            dimension_semantics=("arbitrary", "arbitrary"), vmem_limit_bytes=VMEM_LIMIT),
        name="projection",
    )(x.reshape(nbg, lt, D_MODEL), mod, w_in_bf, *cast_weights)
    return outs[0].reshape(b, l, IN_COLS), outs[1], tuple(outs[2:])


def _ret_kernel(*refs, chunk, n_chunks, nbb, has_s0):
    if has_s0:
        dec_ref, q_ref, k_ref, v_ref, g_ref, s0_ref, o_ref, st_ref, kv_scr, dec_scr = refs
    else:
        dec_ref, q_ref, k_ref, v_ref, g_ref, o_ref, st_ref, kv_scr, dec_scr = refs
        s0_ref = None
    c = chunk
    scale = RET_DK ** -0.5
    pos = lax.broadcasted_iota(jnp.int32, (c, 1), 0).astype(_F32)
    ri = lax.broadcasted_iota(jnp.int32, (c, c), 0)
    ci = lax.broadcasted_iota(jnp.int32, (c, c), 1)
    rel = (ri - ci).astype(_F32)

    for hd in range(RET_HEADS):
        lanes = slice(hd * RET_DK, (hd + 1) * RET_DK)

        def log_gamma(d):
            z = jnp.full((1, 1), dec_ref[d, hd], _F32)
            return jnp.minimum(z, 0.0) - jnp.log(1.0 + jnp.exp(-jnp.abs(z)))

        lg_f, lg_b = log_gamma(0), log_gamma(1)
        wide = lambda col: jnp.broadcast_to(col, (c, RET_DK))
        dec_scr[hd, 0] = wide(jnp.exp(lg_f * (c - 1.0 - pos)) * scale)
        dec_scr[hd, 1] = wide(jnp.exp(lg_b * pos) * scale)
        dec_scr[hd, 2] = wide(jnp.exp(lg_f * (pos + 1.0)))
        dec_scr[hd, 3] = wide(jnp.exp(lg_b * (c - pos)))
        dmat = (jnp.where(ri >= ci, jnp.exp(lg_f * jnp.maximum(rel, 0.0)), 0.0)
                + jnp.where(ci >= ri, jnp.exp(lg_b * jnp.maximum(-rel, 0.0)), 0.0)) * scale
        cdec_f = jnp.exp(lg_f * float(c))
        cdec_b = jnp.exp(lg_b * float(c))

        for bb in range(nbb):
            unit = bb * RET_HEADS + hd
            for n in range(n_chunks):
                kc = k_ref[bb, n * c:(n + 1) * c, lanes]
                vc = v_ref[bb, n * c:(n + 1) * c, lanes].astype(_F32)
                vcat = jnp.concatenate([(vc * dec_scr[hd, 0]).astype(_BF), (vc * dec_scr[hd, 1]).astype(_BF)], axis=1)
                kv_scr[unit, n] = lax.dot_general(kc, vcat, (((0,), (0,)), ((), ())),
                                                  preferred_element_type=_F32)
            if has_s0:
                carry_f = s0_ref[bb, 0, hd]
                carry_b = s0_ref[bb, 1, hd]
            else:
                carry_f = jnp.zeros((RET_DK, RET_DK), _F32)
                carry_b = jnp.zeros((RET_DK, RET_DK), _F32)
            for n in range(n_chunks):
                inc = kv_scr[unit, n, :, 0:RET_DK]
                kv_scr[unit, n, :, 0:RET_DK] = carry_f
                carry_f = cdec_f * carry_f + inc
            for n in range(n_chunks - 1, -1, -1):
                inc = kv_scr[unit, n, :, RET_DK:2 * RET_DK]
                kv_scr[unit, n, :, RET_DK:2 * RET_DK] = carry_b
                carry_b = cdec_b * carry_b + inc
            st_ref[bb, 0, hd] = carry_f
            st_ref[bb, 1, hd] = carry_b

            for n in range(n_chunks):
                sl = slice(n * c, (n + 1) * c)
                qc = q_ref[bb, sl, lanes]
                kc = k_ref[bb, sl, lanes]
                vc = v_ref[bb, sl, lanes]
                scores = lax.dot_general(qc, kc, (((1,), (1,)), ((), ())), preferred_element_type=_F32)
                o = jnp.dot((scores * dmat).astype(_BF), vc, preferred_element_type=_F32)
                oi = jnp.dot(qc, kv_scr[unit, n].astype(_BF), preferred_element_type=_F32)
                o = o + oi[:, 0:RET_DK] * dec_scr[hd, 2] + oi[:, RET_DK:2 * RET_DK] * dec_scr[hd, 3]
                gate = g_ref[bb, sl, lanes].astype(_F32)
                o_ref[bb, sl, lanes] = (_norm_rows(o) * (gate * jax.nn.sigmoid(gate))).astype(_BF)


def _retention(proj, ret_decay, s0):
    b, l, _ = proj.shape
    c = min(RET_CHUNK, l)
    n_chunks = l // c
    nbb = max(1, min(b, RET_UNIT_TOKENS // l))
    has_s0 = s0 is not None
    col = lambda j: pl.BlockSpec((nbb, l, RET_WIDTH), lambda i: (i, 0, j))
    st_spec = pl.BlockSpec((nbb, 2, RET_HEADS, RET_DK, RET_DK), lambda i: (i, 0, 0, 0, 0))
    in_specs = [pl.BlockSpec(memory_space=pltpu.SMEM), col(0), col(1), col(2), col(3)]
    args = [ret_decay, proj, proj, proj, proj]
    if has_s0:
        in_specs.append(st_spec)
        args.append(s0)
    return pl.pallas_call(
        functools.partial(_ret_kernel, chunk=c, n_chunks=n_chunks, nbb=nbb, has_s0=has_s0),
        grid=(b // nbb,),
        in_specs=in_specs,
        out_specs=[pl.BlockSpec((nbb, l, RET_WIDTH), lambda i: (i, 0, 0)), st_spec],
        out_shape=[jax.ShapeDtypeStruct((b, l, RET_WIDTH), _BF),
                   jax.ShapeDtypeStruct((b, 2, RET_HEADS, RET_DK, RET_DK), _F32)],
        scratch_shapes=[pltpu.VMEM((nbb * RET_HEADS, n_chunks, RET_DK, 2 * RET_DK), _F32),
                        pltpu.VMEM((RET_HEADS, 4, c, RET_DK), _F32)],
        compiler_params=pltpu.CompilerParams(
            dimension_semantics=("arbitrary",), vmem_limit_bytes=VMEM_LIMIT),
        name="retention",
    )(*args)


def _swap_halves(z):
    return pltpu.roll(z, S5_STATE, axis=z.ndim - 1)


def _s5_prep_kernel(rows_ref, mats_ref, win_ref, cctx_ref, c_ref, wada_ref, bada_ref,
                    m_ref, wst_ref, cout_ref, apow_ref, winb_ref, mod_ref, *, ng):
    winb_ref[...] = win_ref[...].astype(_BF)
    wada = wada_ref[...].astype(_BF)
    silu_dot = lambda cond: jnp.dot((cond * jax.nn.sigmoid(cond)).astype(_BF), wada,
                                    preferred_element_type=_F32) + bada_ref[...]
    nlat = c_ref.shape[0]
    mod_ref[0:1, :] = silu_dot(cctx_ref[...])
    mod_ref[1:1 + nlat, :] = silu_dot(c_ref[...])
    mod_ref[1 + nlat:, :] = jnp.zeros((mod_ref.shape[0] - 1 - nlat, mod_ref.shape[1]), _F32)
    lane = lax.broadcasted_iota(jnp.int32, (1, 2 * S5_STATE), 1)
    sgn = jnp.where(lane < S5_STATE, -1.0, 1.0).astype(_F32)
    gran = lax.broadcasted_iota(jnp.int32, (1, S5_ROW), 1) // S5_CH
    ar = rows_ref[:, 0:128]
    ai = rows_ref[:, 128:256]
    dt = jnp.exp(rows_ref[:, 256:384])
    mag = jnp.exp(ar * dt)
    ang = ai * dt
    pr1 = mag * jnp.cos(ang)
    pi1 = mag * jnp.sin(ang)
    pw_r = [jnp.ones_like(pr1), pr1]
    pw_i = [jnp.zeros_like(pi1), pi1]
    for _ in range(2, S5_T + 1):
        pr, pi = pw_r[-1], pw_i[-1]
        pw_r.append(pr * pr1 - pi * pi1)
        pw_i.append(pr * pi1 + pi * pr1)
    pw_is = [p * sgn for p in pw_i]
    x2 = pr1 - 1.0
    den = ar * ar + ai * ai
    coef_re = (x2 * ar + pi1 * ai) / den
    coef_im_s = ((pi1 * ar - x2 * ai) / den) * sgn
    for gi in range(ng):
        bt = mats_ref[gi, 0:S5_CH, :]
        cp = mats_ref[gi, S5_CH:2 * S5_CH, :]
        cc = cp * (-sgn)
        cps = _swap_halves(cp)
        bts = _swap_halves(bt)
        gens = []
        cout_rows = [[], []]
        for d in range(2):
            r = 2 * gi + d
            row = lambda a: a[r:r + 1, :]
            bb = row(coef_re) * bt + row(coef_im_s) * bts
            bbs = _swap_halves(bb)
            w_rows = []
            for m in range(S5_T):
                e = (S5_T - 1 - m) if d == 0 else m
                w = row(pw_r[e]) * bb + row(pw_is[e]) * bbs
                rows = slice(m * S5_CH, (m + 1) * S5_CH)
                wst_ref[gi, rows, d * 128:(d + 1) * 128] = w.astype(_BF)
                wst_ref[gi, rows, 256 + d * 128:256 + (d + 1) * 128] = _swap_halves(w).astype(_BF)
                w_rows.append(w)
            for i in range(S5_T):
                e = (i + 1) if d == 0 else (S5_T - i)
                gmat = row(pw_r[e]) * cp + row(pw_is[e]) * cps
                cout_rows[d].append(gmat * (-sgn))
            gens.append(lax.dot_general(cc, jnp.concatenate(w_rows, axis=0), (((1,), (1,)), ((), ())),
                                        preferred_element_type=_F32, precision=lax.Precision.HIGHEST))
            apow_ref[gi, 2 * d:2 * d + 1, :] = row(pw_r[S5_T])
            apow_ref[gi, 2 * d + 1:2 * d + 2, :] = row(pw_is[S5_T])
        blocks = []
        for t in range(S5_T):
            sf = (S5_CH * (t + 1)) % S5_ROW
            rf = pltpu.roll(gens[0], sf, axis=1) if sf else gens[0]
            rb = pltpu.roll(gens[1], S5_CH * t, axis=1) if t else gens[1]
            blocks.append(jnp.where(gran <= t, rf, 0.0) + jnp.where(gran >= t, rb, 0.0))
        m_ref[gi] = jnp.concatenate(blocks, axis=0).T.astype(_BF)
        cout_t = jnp.concatenate([jnp.concatenate(cout_rows[0], axis=0),
                                  jnp.concatenate(cout_rows[1], axis=0)], axis=1)
        cout_ref[gi] = cout_t.T.astype(_BF)


def _s5_operators(s5_a_re, s5_a_im, s5_log_dt, s5_b_re, s5_b_im, s5_c_re, s5_c_im, w_in, c_ctx, c, w_ada, b_ada):
    g = S5_GROUPS
    ng = S5_PREP_GROUP_TILE
    steps = g // ng
    wspec = pl.BlockSpec((w_in.shape[0] // steps, w_in.shape[1]), lambda i: (i, 0))
    n_mod = w_ada.shape[1]
    tn = n_mod // steps
    ldt = jnp.broadcast_to(s5_log_dt[:, :, None], s5_a_re.shape[:2] + (128,))
    rows = jnp.concatenate([s5_a_re, s5_a_re, s5_a_im, s5_a_im, ldt], axis=-1)
    rows = rows.transpose(1, 0, 2).reshape(2 * g, 384)
    mats = jnp.concatenate(
        [jnp.concatenate([s5_b_re.transpose(0, 2, 1), s5_b_im.transpose(0, 2, 1)], axis=-1),
         jnp.concatenate([s5_c_re, s5_c_im], axis=-1)], axis=1)
    gspec = lambda *shape: pl.BlockSpec((ng,) + shape, lambda i: (i,) + (0,) * len(shape))
    return pl.pallas_call(
        functools.partial(_s5_prep_kernel, ng=ng),
        grid=(g // ng,),
        in_specs=[pl.BlockSpec((2 * ng, 384), lambda i: (i, 0)), gspec(2 * S5_CH, 128), wspec,
                  pl.BlockSpec((1, D_MODEL), lambda i: (0, 0)),
                  pl.BlockSpec(c.shape, lambda i: (0, 0)),
                  pl.BlockSpec((D_MODEL, tn), lambda i: (0, i)),
                  pl.BlockSpec((1, tn), lambda i: (0, i))],
        out_specs=[gspec(S5_ROW, S5_ROW), gspec(S5_ROW, 512), gspec(256, S5_ROW), gspec(4, 128), wspec,
                   pl.BlockSpec((8, tn), lambda i: (0, i))],
        out_shape=[jax.ShapeDtypeStruct((g, S5_ROW, S5_ROW), _BF),
                   jax.ShapeDtypeStruct((g, S5_ROW, 512), _BF),
                   jax.ShapeDtypeStruct((g, 256, S5_ROW), _BF),
                   jax.ShapeDtypeStruct((g, 4, 128), _F32),
                   jax.ShapeDtypeStruct(w_in.shape, _BF),
                   jax.ShapeDtypeStruct((8, n_mod), _F32)],
        compiler_params=pltpu.CompilerParams(dimension_semantics=("arbitrary",)),
        name="s5_operators",
    )(rows, mats, w_in, c_ctx.reshape(1, D_MODEL), c, w_ada, b_ada.reshape(1, n_mod))


def _s5_kernel(*refs, segs, ng):
    n = len(segs)
    x_refs = refs[0:n]
    m_ref, wst_ref, cout_ref, apow_ref = refs[n:n + 4]
    s0_refs = refs[n + 4:2 * n + 4]
    y_refs = refs[2 * n + 4:3 * n + 4]
    fin_refs = refs[3 * n + 4:4 * n + 4]
    scr = refs[4 * n + 4:]
    coef = [[apow_ref[gi, r:r + 1, :] for r in range(4)] for gi in range(ng)]
    for si, (nb, nk) in enumerate(segs):
        x_ref, s0_ref, y_ref, fin_ref = x_refs[si], s0_refs[si], y_refs[si], fin_refs[si]
        loc_scr, prev_scr = scr[2 * si], scr[2 * si + 1]
        load_x = lambda gi: jnp.concatenate([x_ref[gi, 0], x_ref[gi, 1]], axis=1).astype(_BF)
        for gi in range(ng):
            loc_scr[gi] = jnp.dot(load_x(gi), wst_ref[gi], preferred_element_type=_F32)
        st = []
        for gi in range(ng):
            zf = s0_ref[gi, :, 0:128]
            zb = s0_ref[gi, :, 128:256]
            st.append([zf, _swap_halves(zf), zb, _swap_halves(zb)])
        for k in range(nk):
            kb = nk - 1 - k
            rows_f = slice(k * nb, (k + 1) * nb)
            rows_b = slice(kb * nb, (kb + 1) * nb)
            for gi in range(ng):
                ar_f, ai_f, ar_b, ai_b = coef[gi]
                zf, zfs, zb, zbs = st[gi]
                prev_scr[gi, rows_f, 0:128] = zf
                prev_scr[gi, rows_b, 128:256] = zb
                st[gi] = [ar_f * zf + ai_f * zfs + loc_scr[gi, rows_f, 0:128],
                          ar_f * zfs - ai_f * zf + loc_scr[gi, rows_f, 256:384],
                          ar_b * zb + ai_b * zbs + loc_scr[gi, rows_b, 128:256],
                          ar_b * zbs - ai_b * zb + loc_scr[gi, rows_b, 384:512]]
        for gi in range(ng):
            fin_ref[gi, :, 0:128] = st[gi][0]
            fin_ref[gi, :, 128:256] = st[gi][2]
            y = jnp.dot(load_x(gi), m_ref[gi], preferred_element_type=_F32)
            y = y + jnp.dot(prev_scr[gi].astype(_BF), cout_ref[gi], preferred_element_type=_F32)
            y_ref[gi, 0] = y[:, 0:128]
            y_ref[gi, 1] = y[:, 128:256]


def _s5_scan(xgs, m_op, wst, cout, apow, s0ps, segs):
    g = S5_GROUPS
    ng = S5_GROUP_TILE
    n = len(segs)
    gspec = lambda *shape: pl.BlockSpec((ng,) + shape, lambda i: (i,) + (0,) * len(shape))
    rows = [nb * nk for nb, nk in segs]
    outs = pl.pallas_call(
        functools.partial(_s5_kernel, segs=tuple(segs), ng=ng),
        grid=(g // ng,),
        in_specs=[gspec(2, r, 128) for r in rows]
                 + [gspec(S5_ROW, S5_ROW), gspec(S5_ROW, 512), gspec(256, S5_ROW), gspec(4, 128)]
                 + [gspec(nb, 256) for nb, _ in segs],
        out_specs=[gspec(2, r, 128) for r in rows] + [gspec(nb, 256) for nb, _ in segs],
        out_shape=[jax.ShapeDtypeStruct((g, 2, r, 128), _F32) for r in rows]
                  + [jax.ShapeDtypeStruct((g, nb, 256), _F32) for nb, _ in segs],
        scratch_shapes=[buf for r in rows
                        for buf in (pltpu.VMEM((ng, r, 512), _F32), pltpu.VMEM((ng, r, 256), _F32))],
        compiler_params=pltpu.CompilerParams(
            dimension_semantics=("arbitrary",), vmem_limit_bytes=VMEM_LIMIT),
        name="s5_scan",
    )(*xgs, m_op, wst, cout, apow, *s0ps)
    return outs[:n], outs[n:]


def _post_kernel(x_ref, o_ref, yg_ref, u_ref, mod_ref, modp_ref, dsk_ref, wglu_ref, bglu_ref, wout_ref,
                 l1g_ref, l1b_ref, w1_ref, b1_ref, w2_ref, b2_ref, l2g_ref, l2b_ref, out_ref,
                 y_scr, h_next, x1_next, h_cur, x1_cur, *, nb, spt, tm, nbg, n_steps):
    i = pl.program_id(0)
    pb = jnp.minimum(i, n_steps - 1) % nbg
    front = functools.partial(_post_mix_front, o_ref, yg_ref, u_ref, dsk_ref, wglu_ref, bglu_ref,
                              wout_ref, y_scr, nb=nb, spt=spt, tm=tm, pb=pb)
    back = functools.partial(_post_mix_back, x_ref, mod_ref, wout_ref, l1g_ref, l1b_ref,
                             h_next, x1_next)

    @pl.when(i == 0)
    def _():
        back(*front())

    @pl.when(i > 0)
    def _():
        h_cur[...] = h_next[...]
        x1_cur[...] = x1_next[...]
        h = h_cur[...]
        up = functools.partial(_mlp_up, h, w1_ref, b1_ref)
        down = functools.partial(_mlp_down, w2_ref)
        acts = [up(0)]
        y, mix = front()
        acts += [up(j) for j in range(1, D_FF // MLP_CHUNK)]
        mix = mix + jnp.dot(y, wout_ref[RET_WIDTH:, :], preferred_element_type=_F32)
        back(None, mix)
        acc = down(acts[0], 0)
        for j in range(1, len(acts)):
            acc = acc + down(acts[j], j)
        f = acc + b2_ref[...]
        g2 = modp_ref[0, 5:6, :]
        out_ref[0] = _norm_rows(ALPHA * x1_cur[...] + g2 * f) * l2g_ref[...] + l2b_ref[...]


def _mlp_up(h, w1_ref, b1_ref, j):
    cols = slice(j * MLP_CHUNK, (j + 1) * MLP_CHUNK)
    a = jnp.dot(h, w1_ref[:, cols], preferred_element_type=_F32) + b1_ref[:, cols]
    return jnp.square(jnp.maximum(a, 0.0)).astype(_BF)


def _mlp_down(w2_ref, a, j):
    return jnp.dot(a, w2_ref[j * MLP_CHUNK:(j + 1) * MLP_CHUNK, :], preferred_element_type=_F32)


def _post_mix_front(o_ref, yg_ref, u_ref, dsk_ref, wglu_ref, bglu_ref, wout_ref, y_scr, *, nb, spt, tm, pb):
    mix = jnp.dot(o_ref[0], wout_ref[0:RET_WIDTH, :], preferred_element_type=_F32)
    for j in range(S5_WIDTH // 128):
        for hf in range(2):
            for q in range(tm // 128):
                start = _octet_row_start(q, pb, nb=nb, spt=spt, tm=tm)
                w = [yg_ref[8 * j + gg, hf, pl.ds(start, 8, stride=nb), :] for gg in range(8)]
                v = _granule_transpose(w)
                for a in range(8):
                    y_scr[j, pl.ds(128 * q + 8 * hf + a, 8, stride=S5_T), :] = v[a]
    y = jnp.concatenate([y_scr[j] for j in range(S5_WIDTH // 128)], axis=1)
    y = y + dsk_ref[...] * u_ref[0].astype(_F32)
    y = jax.nn.gelu(y)
    y = y * jax.nn.sigmoid(jnp.dot(y.astype(_BF), wglu_ref[...], preferred_element_type=_F32)
                           + bglu_ref[...])
    return y.astype(_BF), mix


def _post_mix_back(x_ref, mod_ref, wout_ref, l1g_ref, l1b_ref, h_out, x1_out, y, mix):
    if y is not None:
        mix = mix + jnp.dot(y, wout_ref[RET_WIDTH:, :], preferred_element_type=_F32)
    g1 = mod_ref[0, 2:3, :]
    sh2 = mod_ref[0, 3:4, :]
    sc2 = mod_ref[0, 4:5, :]
    x1 = _norm_rows(ALPHA * x_ref[0] + g1 * mix) * l1g_ref[...] + l1b_ref[...]
    h_out[...] = (_norm_rows(x1) * (1.0 + sc2) + sh2).astype(_BF)
    x1_out[...] = x1


def _post(x, o_ret, yg, proj, mod, mod_row, pw):
    b, l, _ = x.shape
    tm = min(TOKEN_TILE, l)
    spt, nbg, lt = _tile_geometry(b, l, tm)
    rows = (tm // S5_T) * nbg
    n_steps = (lt // tm) * nbg
    item_a = lambda i: jnp.minimum(i, n_steps - 1)
    item_b = lambda i: jnp.maximum(i - 1, 0)
    tok = lambda w, item, cb=0: pl.BlockSpec((1, tm, w), lambda i: (item(i) % nbg, item(i) // nbg, cb))
    modspec = lambda item: pl.BlockSpec((1, 6, D_MODEL), lambda i: (mod_row(item(i) % nbg), 0, 0))
    row = lambda n: _const_spec((1, n))
    return pl.pallas_call(
        functools.partial(_post_kernel, nb=b, spt=spt, tm=tm, nbg=nbg, n_steps=n_steps),
        grid=(n_steps + 1,),
        in_specs=[tok(D_MODEL, item_a), tok(RET_WIDTH, item_a),
                  (_const_spec((S5_GROUPS, 2, rows, 128)) if lt == tm else
                   pl.BlockSpec((S5_GROUPS, 2, rows, 128), lambda i: (0, 0, item_a(i) // nbg, 0))),
                  tok(S5_WIDTH, item_a, 4),
                  modspec(item_a), modspec(item_b),
                  row(S5_WIDTH), _const_spec((S5_WIDTH, S5_WIDTH)), row(S5_WIDTH),
                  _const_spec((D_MODEL, D_MODEL)), row(D_MODEL), row(D_MODEL),
                  _const_spec((D_MODEL, D_FF)), row(D_FF), _const_spec((D_FF, D_MODEL)), row(D_MODEL),
                  row(D_MODEL), row(D_MODEL)],
        out_specs=tok(D_MODEL, item_b),
        out_shape=jax.ShapeDtypeStruct((nbg, lt, D_MODEL), _F32),
        scratch_shapes=[pltpu.VMEM((S5_WIDTH // 128, tm, 128), _F32),
                        pltpu.VMEM((tm, D_MODEL), _BF), pltpu.VMEM((tm, D_MODEL), _F32),
                        pltpu.VMEM((tm, D_MODEL), _BF), pltpu.VMEM((tm, D_MODEL), _F32)],
        compiler_params=pltpu.CompilerParams(
            dimension_semantics=("arbitrary",), vmem_limit_bytes=VMEM_LIMIT),
        name="post",
    )(x.reshape(nbg, lt, D_MODEL), o_ret.reshape(nbg, lt, RET_WIDTH), yg,
      proj.reshape(nbg, lt, IN_COLS), mod, mod, *pw).reshape(b, l, D_MODEL)


def _pack_s5_state(b, s5_s0_re, s5_s0_im):
    g = S5_GROUPS
    if s5_s0_re is None:
        return jnp.zeros((g, b, 256), _F32)
    s0p = jnp.concatenate([s5_s0_re, s5_s0_im], axis=-1)
    return s0p.transpose(2, 0, 1, 3).reshape(g, b, 256)


def kernel(x_prompt, x_sample, state_ret, state_s5_re, state_s5_im, c, c_ctx, w_ada, b_ada, w_in,
           ret_decay, s5_a_re, s5_a_im, s5_log_dt, s5_b_re, s5_b_im, s5_c_re, s5_c_im, s5_d, w_glu,
           b_glu, w_out, ln1_g, ln1_b, w_ff1, b_ff1, w_ff2, b_ff2, ln2_g, ln2_b):
    depth = w_ada.shape[0]
    bs = x_sample.shape[0]
    g = S5_GROUPS
    y_p, y_s = x_prompt, x_sample
    rets, s5rs, s5is = [], [], []
    for layer in range(depth):
        *s5_ops, w_in_bf, mod = _s5_operators(
            s5_a_re[layer], s5_a_im[layer], s5_log_dt[layer], s5_b_re[layer], s5_b_im[layer],
            s5_c_re[layer], s5_c_im[layer], w_in[layer], c_ctx, c, w_ada[layer], b_ada[layer])
        mod = mod.reshape(8, 6, D_MODEL)
        r2 = lambda a: a[layer].reshape(1, -1)
        rows_p, rows_s = (lambda i: 0), (lambda i: i + 1)
        proj_p, xg_p, (w_ff1_bf,) = _projection(y_p, mod, rows_p, w_in_bf, (w_ff1[layer],))
        proj_s, xg_s, (w_glu_bf, w_out_bf, w_ff2_bf) = _projection(
            y_s, mod, rows_s, w_in_bf, (w_glu[layer], w_out[layer], w_ff2[layer]))
        o_p, r_st = _retention(proj_p, ret_decay[layer], None)
        o_s, _ = _retention(proj_s, ret_decay[layer], state_ret[:, layer])
        bp, lp, _ = y_p.shape
        segs = ((bp, lp // S5_T), (bs, y_s.shape[1] // S5_T))
        (yg_p, yg_s), (fin_p, _) = _s5_scan(
            (xg_p, xg_s), *s5_ops,
            (_pack_s5_state(bp, None, None), _pack_s5_state(bs, state_s5_re[:, layer], state_s5_im[:, layer])),
            segs)
        fin_p = fin_p.reshape(g, bp, 2, 2, S5_STATE).transpose(1, 2, 0, 3, 4)
        pw = (r2(s5_d), w_glu_bf, r2(b_glu), w_out_bf, r2(ln1_g), r2(ln1_b), w_ff1_bf, r2(b_ff1),
              w_ff2_bf, r2(b_ff2), r2(ln2_g), r2(ln2_b))
        y_p = _post(y_p, o_p, yg_p, proj_p, mod, rows_p, pw)
        y_s = _post(y_s, o_s, yg_s, proj_s, mod, rows_s, pw)
        rets.append(r_st)
        s5rs.append(fin_p[:, :, :, 0, :])
        s5is.append(fin_p[:, :, :, 1, :])
    return (y_p, y_s, jnp.stack(rets, axis=1), jnp.stack(s5rs, axis=1), jnp.stack(s5is, axis=1))
```

```python
import functools

import jax
import jax.numpy as jnp
from jax import lax
from jax.experimental import pallas as pl
from jax.experimental.pallas import tpu as pltpu

D_MODEL = 1024
RET_HEADS = 4
RET_DK = 128
RET_WIDTH = RET_HEADS * RET_DK
S5_CH = 16
S5_GROUPS = 32
S5_STATE = 64
S5_WIDTH = S5_GROUPS * S5_CH
D_FF = 4 * D_MODEL
IN_COLS = 4 * RET_WIDTH + S5_WIDTH
ALPHA = 2.0 ** 0.25
LN_EPS = 1e-5

S5_T = 16
S5_ROW = S5_T * S5_CH
S5_PREP_GROUP_TILE = 4
S5_GROUP_TILE = 8
RET_CHUNK = 256
RET_UNIT_TOKENS = 1024
TOKEN_TILE = 512
MLP_CHUNK = 2048
VMEM_LIMIT = 56 * 1024 * 1024

_BF = jnp.bfloat16
_F32 = jnp.float32


def _norm_rows(x):
    mu = jnp.mean(x, axis=-1, keepdims=True)
    xc = x - mu
    var = jnp.mean(xc * xc, axis=-1, keepdims=True)
    return xc * lax.rsqrt(var + LN_EPS)


def _const_spec(shape):
    nd = len(shape)
    return pl.BlockSpec(shape, lambda *_: (0,) * nd, pipeline_mode=pl.Buffered(1))


def _granule_transpose(v):
    lane = lax.broadcasted_iota(jnp.int32, (1, 128), 1)
    v = list(v)
    for d in (4, 2, 1):
        bit = ((lane // S5_CH) & d) != 0
        nv = list(v)
        for a in range(8):
            if a & d:
                continue
            lo, hi = v[a], v[a + d]
            nv[a] = jnp.where(bit, pltpu.roll(hi, S5_CH * d, axis=1), lo)
            nv[a + d] = jnp.where(bit, hi, pltpu.roll(lo, 128 - S5_CH * d, axis=1))
        v = nv
    return v


def _mod_chunk(mod_ref, row, j):
    return mod_ref[pl.ds(row, 1), j * D_MODEL:(j + 1) * D_MODEL]


def _tile_geometry(b, l, tm):
    spt = max(1, tm // l)
    return spt, b // spt, l * spt


def _octet_row_start(q, pb, *, nb, spt, tm):
    per_seq = (tm // spt) // 128
    return 8 * (q % per_seq) * nb + pb * spt + q // per_seq


def _proj_kernel(*refs, nb, spt, tm, n_cast, mod_rows):
    x_ref, mod_ref, w_ref = refs[:3]
    cast_in = refs[3:3 + n_cast]
    o_ref, xg_ref = refs[3 + n_cast:5 + n_cast]
    cast_out = refs[5 + n_cast:5 + 2 * n_cast]
    u_scr = refs[5 + 2 * n_cast]
    for src, dst in zip(cast_in, cast_out):
        dst[...] = src[...].astype(_BF)
    x = x_ref[0]
    pb = pl.program_id(1)
    mrow = mod_rows[0] + mod_rows[1] * pb
    h = (_norm_rows(x) * (1.0 + _mod_chunk(mod_ref, mrow, 1)) + _mod_chunk(mod_ref, mrow, 0)).astype(_BF)
    u = jnp.dot(h, w_ref[:, 4 * RET_WIDTH:], preferred_element_type=_F32)
    o_ref[0, :, 4 * RET_WIDTH:] = u.astype(_BF)
    for j in range(S5_WIDTH // 128):
        u_scr[j] = u[:, 128 * j:128 * (j + 1)]
    o_ref[0, :, 0:4 * RET_WIDTH] = jnp.dot(
        h, w_ref[:, 0:4 * RET_WIDTH], preferred_element_type=_F32).astype(_BF)
    for j in range(S5_WIDTH // 128):
        for hf in range(2):
            for q in range(tm // 128):
                v = [u_scr[j, pl.ds(128 * q + 8 * hf + a, 8, stride=S5_T), :] for a in range(8)]
                w = _granule_transpose(v)
                start = _octet_row_start(q, pb, nb=nb, spt=spt, tm=tm)
                for gg in range(8):
                    xg_ref[8 * j + gg, hf, pl.ds(start, 8, stride=nb), :] = w[gg]


def _projection(x, mod, mod_rows, w_in_bf, cast_weights=()):
    b, l, _ = x.shape
    tm = TOKEN_TILE
    spt, nbg, lt = _tile_geometry(b, l, tm)
    rows = (tm // S5_T) * nbg
    n_steps = (lt // tm) * nbg
    cast_specs = [pl.BlockSpec((w.shape[0] // n_steps, w.shape[1]), lambda j, i: (j * nbg + i, 0))
                  for w in cast_weights]
    outs = pl.pallas_call(
        functools.partial(_proj_kernel, nb=b, spt=spt, tm=tm, n_cast=len(cast_weights), mod_rows=mod_rows),
        grid=(lt // tm, nbg),
        in_specs=[pl.BlockSpec((1, tm, D_MODEL), lambda j, i: (i, j, 0)),
                  _const_spec(mod.shape),
                  _const_spec((D_MODEL, IN_COLS))] + cast_specs,
        out_specs=[pl.BlockSpec((1, tm, IN_COLS), lambda j, i: (i, j, 0)),
                   pl.BlockSpec((S5_GROUPS, 2, rows, 128), lambda j, i: (0, 0, j, 0))] + cast_specs,
        out_shape=[jax.ShapeDtypeStruct((nbg, lt, IN_COLS), _BF),
                   jax.ShapeDtypeStruct((S5_GROUPS, 2, (l // S5_T) * b, 128), _F32)]
                  + [jax.ShapeDtypeStruct(w.shape, _BF) for w in cast_weights],
        scratch_shapes=[pltpu.VMEM((S5_WIDTH // 128, tm, 128), _F32)],
        compiler_params=pltpu.CompilerParams(
            dimension_semantics=("arbitrary", "arbitrary"), vmem_limit_bytes=VMEM_LIMIT),
        name="projection",
    )(x.reshape(nbg, lt, D_MODEL), mod, w_in_bf, *cast_weights)
    return outs[0].reshape(b, l, IN_COLS), outs[1], tuple(outs[2:])


def _ret_kernel(*refs, chunk, n_chunks, nbb, has_s0):
    if has_s0:
        dec_ref, q_ref, k_ref, v_ref, g_ref, s0_ref, o_ref, st_ref, kv_scr, dec_scr = refs
    else:
        dec_ref, q_ref, k_ref, v_ref, g_ref, o_ref, st_ref, kv_scr, dec_scr = refs
        s0_ref = None
    c = chunk
    scale = RET_DK ** -0.5
    pos = lax.broadcasted_iota(jnp.int32, (c, 1), 0).astype(_F32)
    ri = lax.broadcasted_iota(jnp.int32, (c, c), 0)
    ci = lax.broadcasted_iota(jnp.int32, (c, c), 1)
    rel = (ri - ci).astype(_F32)

    for hd in range(RET_HEADS):
        lanes = slice(hd * RET_DK, (hd + 1) * RET_DK)

        def log_gamma(d):
            z = jnp.full((1, 1), dec_ref[d, hd], _F32)
            return jnp.minimum(z, 0.0) - jnp.log(1.0 + jnp.exp(-jnp.abs(z)))

        lg_f, lg_b = log_gamma(0), log_gamma(1)
        wide = lambda col: jnp.broadcast_to(col, (c, RET_DK))
        dec_scr[hd, 0] = wide(jnp.exp(lg_f * (c - 1.0 - pos)) * scale)
        dec_scr[hd, 1] = wide(jnp.exp(lg_b * pos) * scale)
        dec_scr[hd, 2] = wide(jnp.exp(lg_f * (pos + 1.0)))
        dec_scr[hd, 3] = wide(jnp.exp(lg_b * (c - pos)))
        dmat = (jnp.where(ri >= ci, jnp.exp(lg_f * jnp.maximum(rel, 0.0)), 0.0)
                + jnp.where(ci >= ri, jnp.exp(lg_b * jnp.maximum(-rel, 0.0)), 0.0)) * scale
        cdec_f = jnp.exp(lg_f * float(c))
        cdec_b = jnp.exp(lg_b * float(c))

        for bb in range(nbb):
            unit = bb * RET_HEADS + hd
            for n in range(n_chunks):
                kc = k_ref[bb, n * c:(n + 1) * c, lanes]
                vc = v_ref[bb, n * c:(n + 1) * c, lanes].astype(_F32)
                vcat = jnp.concatenate([(vc * dec_scr[hd, 0]).astype(_BF), (vc * dec_scr[hd, 1]).astype(_BF)], axis=1)
                kv_scr[unit, n] = lax.dot_general(kc, vcat, (((0,), (0,)), ((), ())),
                                                  preferred_element_type=_F32)
            if has_s0:
                carry_f = s0_ref[bb, 0, hd]
                carry_b = s0_ref[bb, 1, hd]
            else:
                carry_f = jnp.zeros((RET_DK, RET_DK), _F32)
                carry_b = jnp.zeros((RET_DK, RET_DK), _F32)
            for n in range(n_chunks):
                inc = kv_scr[unit, n, :, 0:RET_DK]
                kv_scr[unit, n, :, 0:RET_DK] = carry_f
                carry_f = cdec_f * carry_f + inc
            for n in range(n_chunks - 1, -1, -1):
                inc = kv_scr[unit, n, :, RET_DK:2 * RET_DK]
                kv_scr[unit, n, :, RET_DK:2 * RET_DK] = carry_b
                carry_b = cdec_b * carry_b + inc
            st_ref[bb, 0, hd] = carry_f
            st_ref[bb, 1, hd] = carry_b

            for n in range(n_chunks):
                sl = slice(n * c, (n + 1) * c)
                qc = q_ref[bb, sl, lanes]
                kc = k_ref[bb, sl, lanes]
                vc = v_ref[bb, sl, lanes]
                scores = lax.dot_general(qc, kc, (((1,), (1,)), ((), ())), preferred_element_type=_F32)
                o = jnp.dot((scores * dmat).astype(_BF), vc, preferred_element_type=_F32)
                oi = jnp.dot(qc, kv_scr[unit, n].astype(_BF), preferred_element_type=_F32)
                o = o + oi[:, 0:RET_DK] * dec_scr[hd, 2] + oi[:, RET_DK:2 * RET_DK] * dec_scr[hd, 3]
                gate = g_ref[bb, sl, lanes].astype(_F32)
                o_ref[bb, sl, lanes] = (_norm_rows(o) * (gate * jax.nn.sigmoid(gate))).astype(_BF)


def _retention(proj, ret_decay, s0):
    b, l, _ = proj.shape
    c = min(RET_CHUNK, l)
    n_chunks = l // c
    nbb = max(1, min(b, RET_UNIT_TOKENS // l))
    has_s0 = s0 is not None
    col = lambda j: pl.BlockSpec((nbb, l, RET_WIDTH), lambda i: (i, 0, j))
    st_spec = pl.BlockSpec((nbb, 2, RET_HEADS, RET_DK, RET_DK), lambda i: (i, 0, 0, 0, 0))
    in_specs = [pl.BlockSpec(memory_space=pltpu.SMEM), col(0), col(1), col(2), col(3)]
    args = [ret_decay, proj, proj, proj, proj]
    if has_s0:
        in_specs.append(st_spec)
        args.append(s0)
    return pl.pallas_call(
        functools.partial(_ret_kernel, chunk=c, n_chunks=n_chunks, nbb=nbb, has_s0=has_s0),
        grid=(b // nbb,),
        in_specs=in_specs,
        out_specs=[pl.BlockSpec((nbb, l, RET_WIDTH), lambda i: (i, 0, 0)), st_spec],
        out_shape=[jax.ShapeDtypeStruct((b, l, RET_WIDTH), _BF),
                   jax.ShapeDtypeStruct((b, 2, RET_HEADS, RET_DK, RET_DK), _F32)],
        scratch_shapes=[pltpu.VMEM((nbb * RET_HEADS, n_chunks, RET_DK, 2 * RET_DK), _F32),
                        pltpu.VMEM((RET_HEADS, 4, c, RET_DK), _F32)],
        compiler_params=pltpu.CompilerParams(
            dimension_semantics=("arbitrary",), vmem_limit_bytes=VMEM_LIMIT),
        name="retention",
    )(*args)


def _swap_halves(z):
    return pltpu.roll(z, S5_STATE, axis=z.ndim - 1)


def _s5_prep_kernel(rows_ref, mats_ref, win_ref, cctx_ref, c_ref, wada_ref, bada_ref,
                    m_ref, wst_ref, cout_ref, apow_ref, winb_ref, mod_ref, *, ng):
    winb_ref[...] = win_ref[...].astype(_BF)
    wada = wada_ref[...].astype(_BF)
    silu_dot = lambda cond: jnp.dot((cond * jax.nn.sigmoid(cond)).astype(_BF), wada,
                                    preferred_element_type=_F32) + bada_ref[...]
    nlat = c_ref.shape[0]
    mod_ref[0:1, :] = silu_dot(cctx_ref[...])
    mod_ref[1:1 + nlat, :] = silu_dot(c_ref[...])
    mod_ref[1 + nlat:, :] = jnp.zeros((mod_ref.shape[0] - 1 - nlat, mod_ref.shape[1]), _F32)
    lane = lax.broadcasted_iota(jnp.int32, (1, 2 * S5_STATE), 1)
    sgn = jnp.where(lane < S5_STATE, -1.0, 1.0).astype(_F32)
    gran = lax.broadcasted_iota(jnp.int32, (1, S5_ROW), 1) // S5_CH
    ar = rows_ref[:, 0:128]
    ai = rows_ref[:, 128:256]
    dt = jnp.exp(rows_ref[:, 256:384])
    mag = jnp.exp(ar * dt)
    ang = ai * dt
    pr1 = mag * jnp.cos(ang)
    pi1 = mag * jnp.sin(ang)
    pw_r = [jnp.ones_like(pr1), pr1]
    pw_i = [jnp.zeros_like(pi1), pi1]
    for _ in range(2, S5_T + 1):
        pr, pi = pw_r[-1], pw_i[-1]
        pw_r.append(pr * pr1 - pi * pi1)
        pw_i.append(pr * pi1 + pi * pr1)
    pw_is = [p * sgn for p in pw_i]
    x2 = pr1 - 1.0
    den = ar * ar + ai * ai
    coef_re = (x2 * ar + pi1 * ai) / den
    coef_im_s = ((pi1 * ar - x2 * ai) / den) * sgn
    for gi in range(ng):
        bt = mats_ref[gi, 0:S5_CH, :]
        cp = mats_ref[gi, S5_CH:2 * S5_CH, :]
        cc = cp * (-sgn)
        cps = _swap_halves(cp)
        bts = _swap_halves(bt)
        gens = []
        cout_rows = [[], []]
        for d in range(2):
            r = 2 * gi + d
            row = lambda a: a[r:r + 1, :]
            bb = row(coef_re) * bt + row(coef_im_s) * bts
            bbs = _swap_halves(bb)
            w_rows = []
            for m in range(S5_T):
                e = (S5_T - 1 - m) if d == 0 else m
                w = row(pw_r[e]) * bb + row(pw_is[e]) * bbs
                rows = slice(m * S5_CH, (m + 1) * S5_CH)
                wst_ref[gi, rows, d * 128:(d + 1) * 128] = w.astype(_BF)
                wst_ref[gi, rows, 256 + d * 128:256 + (d + 1) * 128] = _swap_halves(w).astype(_BF)
                w_rows.append(w)
            for i in range(S5_T):
                e = (i + 1) if d == 0 else (S5_T - i)
                gmat = row(pw_r[e]) * cp + row(pw_is[e]) * cps
                cout_rows[d].append(gmat * (-sgn))
            gens.append(lax.dot_general(cc, jnp.concatenate(w_rows, axis=0), (((1,), (1,)), ((), ())),
                                        preferred_element_type=_F32, precision=lax.Precision.HIGHEST))
            apow_ref[gi, 2 * d:2 * d + 1, :] = row(pw_r[S5_T])
            apow_ref[gi, 2 * d + 1:2 * d + 2, :] = row(pw_is[S5_T])
        blocks = []
        for t in range(S5_T):
            sf = (S5_CH * (t + 1)) % S5_ROW
            rf = pltpu.roll(gens[0], sf, axis=1) if sf else gens[0]
            rb = pltpu.roll(gens[1], S5_CH * t, axis=1) if t else gens[1]
            blocks.append(jnp.where(gran <= t, rf, 0.0) + jnp.where(gran >= t, rb, 0.0))
        m_ref[gi] = jnp.concatenate(blocks, axis=0).T.astype(_BF)
        cout_t = jnp.concatenate([jnp.concatenate(cout_rows[0], axis=0),
                                  jnp.concatenate(cout_rows[1], axis=0)], axis=1)
        cout_ref[gi] = cout_t.T.astype(_BF)


def _s5_operators(s5_a_re, s5_a_im, s5_log_dt, s5_b_re, s5_b_im, s5_c_re, s5_c_im, w_in, c_ctx, c, w_ada, b_ada):
    g = S5_GROUPS
    ng = S5_PREP_GROUP_TILE
    steps = g // ng
    wspec = pl.BlockSpec((w_in.shape[0] // steps, w_in.shape[1]), lambda i: (i, 0))
    n_mod = w_ada.shape[1]
    tn = n_mod // steps
    ldt = jnp.broadcast_to(s5_log_dt[:, :, None], s5_a_re.shape[:2] + (128,))
    rows = jnp.concatenate([s5_a_re, s5_a_re, s5_a_im, s5_a_im, ldt], axis=-1)
    rows = rows.transpose(1, 0, 2).reshape(2 * g, 384)
    mats = jnp.concatenate(
        [jnp.concatenate([s5_b_re.transpose(0, 2, 1), s5_b_im.transpose(0, 2, 1)], axis=-1),
         jnp.concatenate([s5_c_re, s5_c_im], axis=-1)], axis=1)
    gspec = lambda *shape: pl.BlockSpec((ng,) + shape, lambda i: (i,) + (0,) * len(shape))
    return pl.pallas_call(
        functools.partial(_s5_prep_kernel, ng=ng),
        grid=(g // ng,),
        in_specs=[pl.BlockSpec((2 * ng, 384), lambda i: (i, 0)), gspec(2 * S5_CH, 128), wspec,
                  pl.BlockSpec((1, D_MODEL), lambda i: (0, 0)),
                  pl.BlockSpec(c.shape, lambda i: (0, 0)),
                  pl.BlockSpec((D_MODEL, tn), lambda i: (0, i)),
                  pl.BlockSpec((1, tn), lambda i: (0, i))],
        out_specs=[gspec(S5_ROW, S5_ROW), gspec(S5_ROW, 512), gspec(256, S5_ROW), gspec(4, 128), wspec,
                   pl.BlockSpec((8, tn), lambda i: (0, i))],
        out_shape=[jax.ShapeDtypeStruct((g, S5_ROW, S5_ROW), _BF),
                   jax.ShapeDtypeStruct((g, S5_ROW, 512), _BF),
                   jax.ShapeDtypeStruct((g, 256, S5_ROW), _BF),
                   jax.ShapeDtypeStruct((g, 4, 128), _F32),
                   jax.ShapeDtypeStruct(w_in.shape, _BF),
                   jax.ShapeDtypeStruct((8, n_mod), _F32)],
        compiler_params=pltpu.CompilerParams(dimension_semantics=("arbitrary",)),
        name="s5_operators",
    )(rows, mats, w_in, c_ctx.reshape(1, D_MODEL), c, w_ada, b_ada.reshape(1, n_mod))


def _s5_kernel(*refs, segs, ng):
    n = len(segs)
    x_refs = refs[0:n]
    m_ref, wst_ref, cout_ref, apow_ref = refs[n:n + 4]
    s0_refs = refs[n + 4:2 * n + 4]
    y_refs = refs[2 * n + 4:3 * n + 4]
    fin_refs = refs[3 * n + 4:4 * n + 4]
    scr = refs[4 * n + 4:]
    coef = [[apow_ref[gi, r:r + 1, :] for r in range(4)] for gi in range(ng)]
    for si, (nb, nk) in enumerate(segs):
        x_ref, s0_ref, y_ref, fin_ref = x_refs[si], s0_refs[si], y_refs[si], fin_refs[si]
        loc_scr, prev_scr = scr[2 * si], scr[2 * si + 1]
        load_x = lambda gi: jnp.concatenate([x_ref[gi, 0], x_ref[gi, 1]], axis=1).astype(_BF)
        for gi in range(ng):
            loc_scr[gi] = jnp.dot(load_x(gi), wst_ref[gi], preferred_element_type=_F32)
        st = []
        for gi in range(ng):
            zf = s0_ref[gi, :, 0:128]
            zb = s0_ref[gi, :, 128:256]
            st.append([zf, _swap_halves(zf), zb, _swap_halves(zb)])
        for k in range(nk):
            kb = nk - 1 - k
            rows_f = slice(k * nb, (k + 1) * nb)
            rows_b = slice(kb * nb, (kb + 1) * nb)
            for gi in range(ng):
                ar_f, ai_f, ar_b, ai_b = coef[gi]
                zf, zfs, zb, zbs = st[gi]
                prev_scr[gi, rows_f, 0:128] = zf
                prev_scr[gi, rows_b, 128:256] = zb
                st[gi] = [ar_f * zf + ai_f * zfs + loc_scr[gi, rows_f, 0:128],
                          ar_f * zfs - ai_f * zf + loc_scr[gi, rows_f, 256:384],
                          ar_b * zb + ai_b * zbs + loc_scr[gi, rows_b, 128:256],
                          ar_b * zbs - ai_b * zb + loc_scr[gi, rows_b, 384:512]]
        for gi in range(ng):
            fin_ref[gi, :, 0:128] = st[gi][0]
            fin_ref[gi, :, 128:256] = st[gi][2]
            y = jnp.dot(load_x(gi), m_ref[gi], preferred_element_type=_F32)
            y = y + jnp.dot(prev_scr[gi].astype(_BF), cout_ref[gi], preferred_element_type=_F32)
            y_ref[gi, 0] = y[:, 0:128]
            y_ref[gi, 1] = y[:, 128:256]


def _s5_scan(xgs, m_op, wst, cout, apow, s0ps, segs):
    g = S5_GROUPS
    ng = S5_GROUP_TILE
    n = len(segs)
    gspec = lambda *shape: pl.BlockSpec((ng,) + shape, lambda i: (i,) + (0,) * len(shape))
    rows = [nb * nk for nb, nk in segs]
    outs = pl.pallas_call(
        functools.partial(_s5_kernel, segs=tuple(segs), ng=ng),
        grid=(g // ng,),
        in_specs=[gspec(2, r, 128) for r in rows]
                 + [gspec(S5_ROW, S5_ROW), gspec(S5_ROW, 512), gspec(256, S5_ROW), gspec(4, 128)]
                 + [gspec(nb, 256) for nb, _ in segs],
        out_specs=[gspec(2, r, 128) for r in rows] + [gspec(nb, 256) for nb, _ in segs],
        out_shape=[jax.ShapeDtypeStruct((g, 2, r, 128), _F32) for r in rows]
                  + [jax.ShapeDtypeStruct((g, nb, 256), _F32) for nb, _ in segs],
        scratch_shapes=[buf for r in rows
                        for buf in (pltpu.VMEM((ng, r, 512), _F32), pltpu.VMEM((ng, r, 256), _F32))],
        compiler_params=pltpu.CompilerParams(
            dimension_semantics=("arbitrary",), vmem_limit_bytes=VMEM_LIMIT),
        name="s5_scan",
    )(*xgs, m_op, wst, cout, apow, *s0ps)
    return outs[:n], outs[n:]


def _post_kernel(x_ref, o_ref, yg_ref, u_ref, mod_ref, dsk_ref, wglu_ref, bglu_ref, wout_ref,
                 l1g_ref, l1b_ref, w1_ref, b1_ref, w2_ref, b2_ref, l2g_ref, l2b_ref, out_ref,
                 y_scr, h_next, x1_next, h_cur, x1_cur, *, nb, spt, tm, nbg, n_steps, mod_rows):
    i = pl.program_id(0)
    pb = jnp.minimum(i, n_steps - 1) % nbg
    front = functools.partial(_post_mix_front, o_ref, yg_ref, u_ref, dsk_ref, wglu_ref, bglu_ref,
                              wout_ref, y_scr, nb=nb, spt=spt, tm=tm, pb=pb)
    back = functools.partial(_post_mix_back, x_ref, mod_ref, mod_rows[0] + mod_rows[1] * pb, wout_ref,
                             l1g_ref, l1b_ref, h_next, x1_next)

    @pl.when(i == 0)
    def _():
        back(*front())

    @pl.when(i > 0)
    def _():
        h_cur[...] = h_next[...]
        x1_cur[...] = x1_next[...]
        h = h_cur[...]
        up = functools.partial(_mlp_up, h, w1_ref, b1_ref)
        down = functools.partial(_mlp_down, w2_ref)
        acts = [up(0)]
        y, mix = front()
        acts += [up(j) for j in range(1, D_FF // MLP_CHUNK)]
        mix = mix + jnp.dot(y, wout_ref[RET_WIDTH:, :], preferred_element_type=_F32)
        back(None, mix)
        acc = down(acts[0], 0)
        for j in range(1, len(acts)):
            acc = acc + down(acts[j], j)
        f = acc + b2_ref[...]
        g2 = _mod_chunk(mod_ref, mod_rows[0] + mod_rows[1] * ((i - 1) % nbg), 5)
        out_ref[0] = _norm_rows(ALPHA * x1_cur[...] + g2 * f) * l2g_ref[...] + l2b_ref[...]


def _mlp_up(h, w1_ref, b1_ref, j):
    cols = slice(j * MLP_CHUNK, (j + 1) * MLP_CHUNK)
    a = jnp.dot(h, w1_ref[:, cols], preferred_element_type=_F32) + b1_ref[:, cols]
    return jnp.square(jnp.maximum(a, 0.0)).astype(_BF)


def _mlp_down(w2_ref, a, j):
    return jnp.dot(a, w2_ref[j * MLP_CHUNK:(j + 1) * MLP_CHUNK, :], preferred_element_type=_F32)


def _post_mix_front(o_ref, yg_ref, u_ref, dsk_ref, wglu_ref, bglu_ref, wout_ref, y_scr, *, nb, spt, tm, pb):
    mix = jnp.dot(o_ref[0], wout_ref[0:RET_WIDTH, :], preferred_element_type=_F32)
    for j in range(S5_WIDTH // 128):
        for hf in range(2):
            for q in range(tm // 128):
                start = _octet_row_start(q, pb, nb=nb, spt=spt, tm=tm)
                w = [yg_ref[8 * j + gg, hf, pl.ds(start, 8, stride=nb), :] for gg in range(8)]
                v = _granule_transpose(w)
                for a in range(8):
                    y_scr[j, pl.ds(128 * q + 8 * hf + a, 8, stride=S5_T), :] = v[a]
    y = jnp.concatenate([y_scr[j] for j in range(S5_WIDTH // 128)], axis=1)
    y = y + dsk_ref[...] * u_ref[0].astype(_F32)
    y = jax.nn.gelu(y)
    y = y * jax.nn.sigmoid(jnp.dot(y.astype(_BF), wglu_ref[...], preferred_element_type=_F32)
                           + bglu_ref[...])
    return y.astype(_BF), mix


def _post_mix_back(x_ref, mod_ref, mrow, wout_ref, l1g_ref, l1b_ref, h_out, x1_out, y, mix):
    if y is not None:
        mix = mix + jnp.dot(y, wout_ref[RET_WIDTH:, :], preferred_element_type=_F32)
    g1 = _mod_chunk(mod_ref, mrow, 2)
    sh2 = _mod_chunk(mod_ref, mrow, 3)
    sc2 = _mod_chunk(mod_ref, mrow, 4)
    x1 = _norm_rows(ALPHA * x_ref[0] + g1 * mix) * l1g_ref[...] + l1b_ref[...]
    h_out[...] = (_norm_rows(x1) * (1.0 + sc2) + sh2).astype(_BF)
    x1_out[...] = x1


def _post(x, o_ret, yg, proj, mod, mod_rows, pw):
    b, l, _ = x.shape
    tm = min(TOKEN_TILE, l)
    spt, nbg, lt = _tile_geometry(b, l, tm)
    rows = (tm // S5_T) * nbg
    n_steps = (lt // tm) * nbg
    item_a = lambda i: jnp.minimum(i, n_steps - 1)
    item_b = lambda i: jnp.maximum(i - 1, 0)
    tok = lambda w, item, cb=0: pl.BlockSpec((1, tm, w), lambda i: (item(i) % nbg, item(i) // nbg, cb))
    row = lambda n: _const_spec((1, n))
    return pl.pallas_call(
        functools.partial(_post_kernel, nb=b, spt=spt, tm=tm, nbg=nbg, n_steps=n_steps, mod_rows=mod_rows),
        grid=(n_steps + 1,),
        in_specs=[tok(D_MODEL, item_a), tok(RET_WIDTH, item_a),
                  (_const_spec((S5_GROUPS, 2, rows, 128)) if lt == tm else
                   pl.BlockSpec((S5_GROUPS, 2, rows, 128), lambda i: (0, 0, item_a(i) // nbg, 0))),
                  tok(S5_WIDTH, item_a, 4),
                  _const_spec(mod.shape),
                  row(S5_WIDTH), _const_spec((S5_WIDTH, S5_WIDTH)), row(S5_WIDTH),
                  _const_spec((D_MODEL, D_MODEL)), row(D_MODEL), row(D_MODEL),
                  _const_spec((D_MODEL, D_FF)), row(D_FF), _const_spec((D_FF, D_MODEL)), row(D_MODEL),
                  row(D_MODEL), row(D_MODEL)],
        out_specs=tok(D_MODEL, item_b),
        out_shape=jax.ShapeDtypeStruct((nbg, lt, D_MODEL), _F32),
        scratch_shapes=[pltpu.VMEM((S5_WIDTH // 128, tm, 128), _F32),
                        pltpu.VMEM((tm, D_MODEL), _BF), pltpu.VMEM((tm, D_MODEL), _F32),
                        pltpu.VMEM((tm, D_MODEL), _BF), pltpu.VMEM((tm, D_MODEL), _F32)],
        compiler_params=pltpu.CompilerParams(
            dimension_semantics=("arbitrary",), vmem_limit_bytes=VMEM_LIMIT),
        name="post",
    )(x.reshape(nbg, lt, D_MODEL), o_ret.reshape(nbg, lt, RET_WIDTH), yg,
      proj.reshape(nbg, lt, IN_COLS), mod, *pw).reshape(b, l, D_MODEL)


def _pack_s5_state(b, s5_s0_re, s5_s0_im):
    g = S5_GROUPS
    if s5_s0_re is None:
        return jnp.zeros((g, b, 256), _F32)
    s0p = jnp.concatenate([s5_s0_re, s5_s0_im], axis=-1)
    return s0p.transpose(2, 0, 1, 3).reshape(g, b, 256)


def kernel(x_prompt, x_sample, state_ret, state_s5_re, state_s5_im, c, c_ctx, w_ada, b_ada, w_in,
           ret_decay, s5_a_re, s5_a_im, s5_log_dt, s5_b_re, s5_b_im, s5_c_re, s5_c_im, s5_d, w_glu,
           b_glu, w_out, ln1_g, ln1_b, w_ff1, b_ff1, w_ff2, b_ff2, ln2_g, ln2_b):
    depth = w_ada.shape[0]
    bs = x_sample.shape[0]
    g = S5_GROUPS
    y_p, y_s = x_prompt, x_sample
    rets, s5rs, s5is = [], [], []
    for layer in range(depth):
        *s5_ops, w_in_bf, mod = _s5_operators(
            s5_a_re[layer], s5_a_im[layer], s5_log_dt[layer], s5_b_re[layer], s5_b_im[layer],
            s5_c_re[layer], s5_c_im[layer], w_in[layer], c_ctx, c, w_ada[layer], b_ada[layer])
        r2 = lambda a: a[layer].reshape(1, -1)
        rows_p, rows_s = (0, 0), (1, 1)
        proj_p, xg_p, (w_ff1_bf,) = _projection(y_p, mod, rows_p, w_in_bf, (w_ff1[layer],))
        proj_s, xg_s, (w_glu_bf, w_out_bf, w_ff2_bf) = _projection(
            y_s, mod, rows_s, w_in_bf, (w_glu[layer], w_out[layer], w_ff2[layer]))
        o_p, r_st = _retention(proj_p, ret_decay[layer], None)
        o_s, _ = _retention(proj_s, ret_decay[layer], state_ret[:, layer])
        bp, lp, _ = y_p.shape
        segs = ((bp, lp // S5_T), (bs, y_s.shape[1] // S5_T))
        (yg_p, yg_s), (fin_p, _) = _s5_scan(
            (xg_p, xg_s), *s5_ops,
            (_pack_s5_state(bp, None, None), _pack_s5_state(bs, state_s5_re[:, layer], state_s5_im[:, layer])),
            segs)
        fin_p = fin_p.reshape(g, bp, 2, 2, S5_STATE).transpose(1, 2, 0, 3, 4)
        pw = (r2(s5_d), w_glu_bf, r2(b_glu), w_out_bf, r2(ln1_g), r2(ln1_b), w_ff1_bf, r2(b_ff1),
              w_ff2_bf, r2(b_ff2), r2(ln2_g), r2(ln2_b))
        y_p = _post(y_p, o_p, yg_p, proj_p, mod, rows_p, pw)
        y_s = _post(y_s, o_s, yg_s, proj_s, mod, rows_s, pw)
        rets.append(r_st)
        s5rs.append(fin_p[:, :, :, 0, :])
        s5is.append(fin_p[:, :, :, 1, :])
    return (y_p, y_s, jnp.stack(rets, axis=1), jnp.stack(s5rs, axis=1), jnp.stack(s5is, axis=1))
```

```python
import functools

import jax
import jax.numpy as jnp
from jax import lax
from jax.experimental import pallas as pl
from jax.experimental.pallas import tpu as pltpu

D_MODEL = 1024
RET_HEADS = 4
RET_DK = 128
RET_WIDTH = RET_HEADS * RET_DK
S5_CH = 16
S5_GROUPS = 32
S5_STATE = 64
S5_WIDTH = S5_GROUPS * S5_CH
D_FF = 4 * D_MODEL
IN_COLS = 4 * RET_WIDTH + S5_WIDTH
ALPHA = 2.0 ** 0.25
LN_EPS = 1e-5

S5_T = 16
S5_ROW = S5_T * S5_CH
S5_PREP_GROUP_TILE = 4
S5_GROUP_TILE = 8
RET_CHUNK = 256
RET_UNIT_TOKENS = 2048
TOKEN_TILE = 512
MLP_CHUNK = 2048
VMEM_LIMIT = 56 * 1024 * 1024

_BF = jnp.bfloat16
_F32 = jnp.float32


def _norm_rows(x):
    mu = jnp.mean(x, axis=-1, keepdims=True)
    xc = x - mu
    var = jnp.mean(xc * xc, axis=-1, keepdims=True)
    return xc * lax.rsqrt(var + LN_EPS)


def _const_spec(shape):
    nd = len(shape)
    return pl.BlockSpec(shape, lambda *_: (0,) * nd, pipeline_mode=pl.Buffered(1))


def _granule_transpose(v):
    lane = lax.broadcasted_iota(jnp.int32, (1, 128), 1)
    v = list(v)
    for d in (4, 2, 1):
        bit = ((lane // S5_CH) & d) != 0
        nv = list(v)
        for a in range(8):
            if a & d:
                continue
            lo, hi = v[a], v[a + d]
            nv[a] = jnp.where(bit, pltpu.roll(hi, S5_CH * d, axis=1), lo)
            nv[a + d] = jnp.where(bit, hi, pltpu.roll(lo, 128 - S5_CH * d, axis=1))
        v = nv
    return v


def _mod_chunk(mod_ref, row, j):
    return mod_ref[pl.ds(row, 1), j * D_MODEL:(j + 1) * D_MODEL]


def _tile_geometry(b, l, tm):
    spt = max(1, tm // l)
    return spt, b // spt, l * spt


def _octet_row_start(q, pb, *, nb, spt, tm):
    per_seq = (tm // spt) // 128
    return 8 * (q % per_seq) * nb + pb * spt + q // per_seq


def _proj_kernel(*refs, nb, spt, tm, n_cast, mod_rows):
    x_ref, mod_ref, w_ref = refs[:3]
    cast_in = refs[3:3 + n_cast]
    o_ref, xg_ref = refs[3 + n_cast:5 + n_cast]
    cast_out = refs[5 + n_cast:5 + 2 * n_cast]
    u_scr = refs[5 + 2 * n_cast]
    for src, dst in zip(cast_in, cast_out):
        dst[...] = src[...].astype(_BF)
    x = x_ref[0]
    pb = pl.program_id(1)
    mrow = mod_rows[0] + mod_rows[1] * pb
    h = (_norm_rows(x) * (1.0 + _mod_chunk(mod_ref, mrow, 1)) + _mod_chunk(mod_ref, mrow, 0)).astype(_BF)
    u = jnp.dot(h, w_ref[:, 4 * RET_WIDTH:], preferred_element_type=_F32)
    o_ref[0, :, 4 * RET_WIDTH:] = u.astype(_BF)
    for j in range(S5_WIDTH // 128):
        u_scr[j] = u[:, 128 * j:128 * (j + 1)]
    o_ref[0, :, 0:4 * RET_WIDTH] = jnp.dot(
        h, w_ref[:, 0:4 * RET_WIDTH], preferred_element_type=_F32).astype(_BF)
    for j in range(S5_WIDTH // 128):
        for hf in range(2):
            for q in range(tm // 128):
                v = [u_scr[j, pl.ds(128 * q + 8 * hf + a, 8, stride=S5_T), :] for a in range(8)]
                w = _granule_transpose(v)
                start = _octet_row_start(q, pb, nb=nb, spt=spt, tm=tm)
                for gg in range(8):
                    xg_ref[8 * j + gg, hf, pl.ds(start, 8, stride=nb), :] = w[gg]


def _projection(x, mod, mod_rows, w_in_bf, cast_weights=()):
    b, l, _ = x.shape
    tm = TOKEN_TILE
    spt, nbg, lt = _tile_geometry(b, l, tm)
    rows = (tm // S5_T) * nbg
    n_steps = (lt // tm) * nbg
    cast_specs = [pl.BlockSpec((w.shape[0] // n_steps, w.shape[1]), lambda j, i: (j * nbg + i, 0))
                  for w in cast_weights]
    outs = pl.pallas_call(
        functools.partial(_proj_kernel, nb=b, spt=spt, tm=tm, n_cast=len(cast_weights), mod_rows=mod_rows),
        grid=(lt // tm, nbg),
        in_specs=[pl.BlockSpec((1, tm, D_MODEL), lambda j, i: (i, j, 0)),
                  _const_spec(mod.shape),
                  _const_spec((D_MODEL, IN_COLS))] + cast_specs,
        out_specs=[pl.BlockSpec((1, tm, IN_COLS), lambda j, i: (i, j, 0)),
                   pl.BlockSpec((S5_GROUPS, 2, rows, 128), lambda j, i: (0, 0, j, 0))] + cast_specs,
        out_shape=[jax.ShapeDtypeStruct((nbg, lt, IN_COLS), _BF),
                   jax.ShapeDtypeStruct((S5_GROUPS, 2, (l // S5_T) * b, 128), _F32)]
                  + [jax.ShapeDtypeStruct(w.shape, _BF) for w in cast_weights],
        scratch_shapes=[pltpu.VMEM((S5_WIDTH // 128, tm, 128), _F32)],
        compiler_params=pltpu.CompilerParams(
            dimension_semantics=("arbitrary", "arbitrary"), vmem_limit_bytes=VMEM_LIMIT),
        name="projection",
    )(x.reshape(nbg, lt, D_MODEL), mod, w_in_bf, *cast_weights)
    return outs[0].reshape(b, l, IN_COLS), outs[1], tuple(outs[2:])


def _ret_kernel(*refs, chunk, n_chunks, nbb, has_s0):
    if has_s0:
        dec_ref, q_ref, k_ref, v_ref, g_ref, s0_ref, o_ref, st_ref, kv_scr, dec_scr = refs
    else:
        dec_ref, q_ref, k_ref, v_ref, g_ref, o_ref, st_ref, kv_scr, dec_scr = refs
        s0_ref = None
    c = chunk
    scale = RET_DK ** -0.5
    pos = lax.broadcasted_iota(jnp.int32, (c, 1), 0).astype(_F32)
    ri = lax.broadcasted_iota(jnp.int32, (c, c), 0)
    ci = lax.broadcasted_iota(jnp.int32, (c, c), 1)
    rel = (ri - ci).astype(_F32)

    for hd in range(RET_HEADS):
        lanes = slice(hd * RET_DK, (hd + 1) * RET_DK)

        def log_gamma(d):
            z = jnp.full((1, 1), dec_ref[d, hd], _F32)
            return jnp.minimum(z, 0.0) - jnp.log(1.0 + jnp.exp(-jnp.abs(z)))

        lg_f, lg_b = log_gamma(0), log_gamma(1)
        wide = lambda col: jnp.broadcast_to(col, (c, RET_DK))
        dec_scr[hd, 0] = wide(jnp.exp(lg_f * (c - 1.0 - pos)) * scale)
        dec_scr[hd, 1] = wide(jnp.exp(lg_b * pos) * scale)
        dec_scr[hd, 2] = wide(jnp.exp(lg_f * (pos + 1.0)))
        dec_scr[hd, 3] = wide(jnp.exp(lg_b * (c - pos)))
        dmat = (jnp.where(ri >= ci, jnp.exp(lg_f * jnp.maximum(rel, 0.0)), 0.0)
                + jnp.where(ci >= ri, jnp.exp(lg_b * jnp.maximum(-rel, 0.0)), 0.0)) * scale
        cdec_f = jnp.exp(lg_f * float(c))
        cdec_b = jnp.exp(lg_b * float(c))

        for bb in range(nbb):
            unit = bb * RET_HEADS + hd
            for n in range(n_chunks):
                kc = k_ref[bb, n * c:(n + 1) * c, lanes]
                vc = v_ref[bb, n * c:(n + 1) * c, lanes].astype(_F32)
                vcat = jnp.concatenate([(vc * dec_scr[hd, 0]).astype(_BF), (vc * dec_scr[hd, 1]).astype(_BF)], axis=1)
                kv_scr[unit, n] = lax.dot_general(kc, vcat, (((0,), (0,)), ((), ())),
                                                  preferred_element_type=_F32)
            if has_s0:
                carry_f = s0_ref[bb, 0, hd]
                carry_b = s0_ref[bb, 1, hd]
            else:
                carry_f = jnp.zeros((RET_DK, RET_DK), _F32)
                carry_b = jnp.zeros((RET_DK, RET_DK), _F32)
            for n in range(n_chunks):
                inc = kv_scr[unit, n, :, 0:RET_DK]
                kv_scr[unit, n, :, 0:RET_DK] = carry_f
                carry_f = cdec_f * carry_f + inc
            for n in range(n_chunks - 1, -1, -1):
                inc = kv_scr[unit, n, :, RET_DK:2 * RET_DK]
                kv_scr[unit, n, :, RET_DK:2 * RET_DK] = carry_b
                carry_b = cdec_b * carry_b + inc
            st_ref[bb, 0, hd] = carry_f
            st_ref[bb, 1, hd] = carry_b

            for n in range(n_chunks):
                sl = slice(n * c, (n + 1) * c)
                qc = q_ref[bb, sl, lanes]
                kc = k_ref[bb, sl, lanes]
                vc = v_ref[bb, sl, lanes]
                scores = lax.dot_general(qc, kc, (((1,), (1,)), ((), ())), preferred_element_type=_F32)
                o = jnp.dot((scores * dmat).astype(_BF), vc, preferred_element_type=_F32)
                oi = jnp.dot(qc, kv_scr[unit, n].astype(_BF), preferred_element_type=_F32)
                o = o + oi[:, 0:RET_DK] * dec_scr[hd, 2] + oi[:, RET_DK:2 * RET_DK] * dec_scr[hd, 3]
                gate = g_ref[bb, sl, lanes].astype(_F32)
                o_ref[bb, sl, lanes] = (_norm_rows(o) * (gate * jax.nn.sigmoid(gate))).astype(_BF)


def _retention(proj, ret_decay, s0):
    b, l, _ = proj.shape
    c = min(RET_CHUNK, l)
    n_chunks = l // c
    nbb = max(1, min(b, RET_UNIT_TOKENS // l))
    has_s0 = s0 is not None
    col = lambda j: pl.BlockSpec((nbb, l, RET_WIDTH), lambda i: (i, 0, j))
    st_spec = pl.BlockSpec((nbb, 2, RET_HEADS, RET_DK, RET_DK), lambda i: (i, 0, 0, 0, 0))
    in_specs = [pl.BlockSpec(memory_space=pltpu.SMEM), col(0), col(1), col(2), col(3)]
    args = [ret_decay, proj, proj, proj, proj]
    if has_s0:
        in_specs.append(st_spec)
        args.append(s0)
    return pl.pallas_call(
        functools.partial(_ret_kernel, chunk=c, n_chunks=n_chunks, nbb=nbb, has_s0=has_s0),
        grid=(b // nbb,),
        in_specs=in_specs,
        out_specs=[pl.BlockSpec((nbb, l, RET_WIDTH), lambda i: (i, 0, 0)), st_spec],
        out_shape=[jax.ShapeDtypeStruct((b, l, RET_WIDTH), _BF),
                   jax.ShapeDtypeStruct((b, 2, RET_HEADS, RET_DK, RET_DK), _F32)],
        scratch_shapes=[pltpu.VMEM((nbb * RET_HEADS, n_chunks, RET_DK, 2 * RET_DK), _F32),
                        pltpu.VMEM((RET_HEADS, 4, c, RET_DK), _F32)],
        compiler_params=pltpu.CompilerParams(
            dimension_semantics=("arbitrary",), vmem_limit_bytes=VMEM_LIMIT),
        name="retention",
    )(*args)


def _swap_halves(z):
    return pltpu.roll(z, S5_STATE, axis=z.ndim - 1)


def _s5_prep_kernel(rows_ref, mats_ref, win_ref, cctx_ref, c_ref, wada_ref, bada_ref,
                    m_ref, wst_ref, cout_ref, apow_ref, winb_ref, mod_ref, *, ng):
    winb_ref[...] = win_ref[...].astype(_BF)
    wada = wada_ref[...].astype(_BF)
    silu_dot = lambda cond: jnp.dot((cond * jax.nn.sigmoid(cond)).astype(_BF), wada,
                                    preferred_element_type=_F32) + bada_ref[...]
    nlat = c_ref.shape[0]
    mod_ref[0:1, :] = silu_dot(cctx_ref[...])
    mod_ref[1:1 + nlat, :] = silu_dot(c_ref[...])
    mod_ref[1 + nlat:, :] = jnp.zeros((mod_ref.shape[0] - 1 - nlat, mod_ref.shape[1]), _F32)
    lane = lax.broadcasted_iota(jnp.int32, (1, 2 * S5_STATE), 1)
    sgn = jnp.where(lane < S5_STATE, -1.0, 1.0).astype(_F32)
    gran = lax.broadcasted_iota(jnp.int32, (1, S5_ROW), 1) // S5_CH
    ar = rows_ref[:, 0:128]
    ai = rows_ref[:, 128:256]
    dt = jnp.exp(rows_ref[:, 256:384])
    mag = jnp.exp(ar * dt)
    ang = ai * dt
    pr1 = mag * jnp.cos(ang)
    pi1 = mag * jnp.sin(ang)
    pw_r = [jnp.ones_like(pr1), pr1]
    pw_i = [jnp.zeros_like(pi1), pi1]
    for _ in range(2, S5_T + 1):
        pr, pi = pw_r[-1], pw_i[-1]
        pw_r.append(pr * pr1 - pi * pi1)
        pw_i.append(pr * pi1 + pi * pr1)
    pw_is = [p * sgn for p in pw_i]
    x2 = pr1 - 1.0
    den = ar * ar + ai * ai
    coef_re = (x2 * ar + pi1 * ai) / den
    coef_im_s = ((pi1 * ar - x2 * ai) / den) * sgn
    for gi in range(ng):
        bt = mats_ref[gi, 0:S5_CH, :]
        cp = mats_ref[gi, S5_CH:2 * S5_CH, :]
        cc = cp * (-sgn)
        cps = _swap_halves(cp)
        bts = _swap_halves(bt)
        gens = []
        cout_rows = [[], []]
        for d in range(2):
            r = 2 * gi + d
            row = lambda a: a[r:r + 1, :]
            bb = row(coef_re) * bt + row(coef_im_s) * bts
            bbs = _swap_halves(bb)
            w_rows = []
            for m in range(S5_T):
                e = (S5_T - 1 - m) if d == 0 else m
                w = row(pw_r[e]) * bb + row(pw_is[e]) * bbs
                rows = slice(m * S5_CH, (m + 1) * S5_CH)
                wst_ref[gi, rows, d * 128:(d + 1) * 128] = w.astype(_BF)
                wst_ref[gi, rows, 256 + d * 128:256 + (d + 1) * 128] = _swap_halves(w).astype(_BF)
                w_rows.append(w)
            for i in range(S5_T):
                e = (i + 1) if d == 0 else (S5_T - i)
                gmat = row(pw_r[e]) * cp + row(pw_is[e]) * cps
                cout_rows[d].append(gmat * (-sgn))
            gens.append(lax.dot_general(cc, jnp.concatenate(w_rows, axis=0), (((1,), (1,)), ((), ())),
                                        preferred_element_type=_F32, precision=lax.Precision.HIGHEST))
            apow_ref[gi, 2 * d:2 * d + 1, :] = row(pw_r[S5_T])
            apow_ref[gi, 2 * d + 1:2 * d + 2, :] = row(pw_is[S5_T])
        blocks = []
        for t in range(S5_T):
            sf = (S5_CH * (t + 1)) % S5_ROW
            rf = pltpu.roll(gens[0], sf, axis=1) if sf else gens[0]
            rb = pltpu.roll(gens[1], S5_CH * t, axis=1) if t else gens[1]
            blocks.append(jnp.where(gran <= t, rf, 0.0) + jnp.where(gran >= t, rb, 0.0))
        m_ref[gi] = jnp.concatenate(blocks, axis=0).T.astype(_BF)
        cout_t = jnp.concatenate([jnp.concatenate(cout_rows[0], axis=0),
                                  jnp.concatenate(cout_rows[1], axis=0)], axis=1)
        cout_ref[gi] = cout_t.T.astype(_BF)


def _s5_operators(s5_a_re, s5_a_im, s5_log_dt, s5_b_re, s5_b_im, s5_c_re, s5_c_im, w_in, c_ctx, c, w_ada, b_ada):
    g = S5_GROUPS
    ng = S5_PREP_GROUP_TILE
    steps = g // ng
    wspec = pl.BlockSpec((w_in.shape[0] // steps, w_in.shape[1]), lambda i: (i, 0))
    n_mod = w_ada.shape[1]
    tn = n_mod // steps
    ldt = jnp.broadcast_to(s5_log_dt[:, :, None], s5_a_re.shape[:2] + (128,))
    rows = jnp.concatenate([s5_a_re, s5_a_re, s5_a_im, s5_a_im, ldt], axis=-1)
    rows = rows.transpose(1, 0, 2).reshape(2 * g, 384)
    mats = jnp.concatenate(
        [jnp.concatenate([s5_b_re.transpose(0, 2, 1), s5_b_im.transpose(0, 2, 1)], axis=-1),
         jnp.concatenate([s5_c_re, s5_c_im], axis=-1)], axis=1)
    gspec = lambda *shape: pl.BlockSpec((ng,) + shape, lambda i: (i,) + (0,) * len(shape))
    return pl.pallas_call(
        functools.partial(_s5_prep_kernel, ng=ng),
        grid=(g // ng,),
        in_specs=[pl.BlockSpec((2 * ng, 384), lambda i: (i, 0)), gspec(2 * S5_CH, 128), wspec,
                  pl.BlockSpec((1, D_MODEL), lambda i: (0, 0)),
                  pl.BlockSpec(c.shape, lambda i: (0, 0)),
                  pl.BlockSpec((D_MODEL, tn), lambda i: (0, i)),
                  pl.BlockSpec((1, tn), lambda i: (0, i))],
        out_specs=[gspec(S5_ROW, S5_ROW), gspec(S5_ROW, 512), gspec(256, S5_ROW), gspec(4, 128), wspec,
                   pl.BlockSpec((8, tn), lambda i: (0, i))],
        out_shape=[jax.ShapeDtypeStruct((g, S5_ROW, S5_ROW), _BF),
                   jax.ShapeDtypeStruct((g, S5_ROW, 512), _BF),
                   jax.ShapeDtypeStruct((g, 256, S5_ROW), _BF),
                   jax.ShapeDtypeStruct((g, 4, 128), _F32),
                   jax.ShapeDtypeStruct(w_in.shape, _BF),
                   jax.ShapeDtypeStruct((8, n_mod), _F32)],
        compiler_params=pltpu.CompilerParams(dimension_semantics=("arbitrary",)),
        name="s5_operators",
    )(rows, mats, w_in, c_ctx.reshape(1, D_MODEL), c, w_ada, b_ada.reshape(1, n_mod))


def _s5_kernel(*refs, segs, ng):
    n = len(segs)
    n_s0 = sum(1 for seg in segs if seg[2])
    x_refs = refs[0:n]
    m_ref, wst_ref, cout_ref, apow_ref = refs[n:n + 4]
    s0_iter = iter(refs[n + 4:n + 4 + n_s0])
    y_refs = refs[n + 4 + n_s0:2 * n + 4 + n_s0]
    fin_refs = refs[2 * n + 4 + n_s0:3 * n + 4 + n_s0]
    scr = refs[3 * n + 4 + n_s0:]
    coef = [[apow_ref[gi, r:r + 1, :] for r in range(4)] for gi in range(ng)]
    for si, (nb, nk, has_state) in enumerate(segs):
        x_ref, y_ref, fin_ref = x_refs[si], y_refs[si], fin_refs[si]
        s0_ref = next(s0_iter) if has_state else None
        loc_scr, prev_scr = scr[2 * si], scr[2 * si + 1]
        load_x = lambda gi: jnp.concatenate([x_ref[gi, 0], x_ref[gi, 1]], axis=1).astype(_BF)
        for gi in range(ng):
            loc_scr[gi] = jnp.dot(load_x(gi), wst_ref[gi], preferred_element_type=_F32)
        st = []
        for gi in range(ng):
            if has_state:
                zf = s0_ref[gi, :, 0:128]
                zb = s0_ref[gi, :, 128:256]
                st.append([zf, _swap_halves(zf), zb, _swap_halves(zb)])
            else:
                st.append([jnp.zeros((nb, 128), _F32)] * 4)
        for k in range(nk):
            kb = nk - 1 - k
            rows_f = slice(k * nb, (k + 1) * nb)
            rows_b = slice(kb * nb, (kb + 1) * nb)
            for gi in range(ng):
                ar_f, ai_f, ar_b, ai_b = coef[gi]
                zf, zfs, zb, zbs = st[gi]
                prev_scr[gi, rows_f, 0:128] = zf
                prev_scr[gi, rows_b, 128:256] = zb
                st[gi] = [ar_f * zf + ai_f * zfs + loc_scr[gi, rows_f, 0:128],
                          ar_f * zfs - ai_f * zf + loc_scr[gi, rows_f, 256:384],
                          ar_b * zb + ai_b * zbs + loc_scr[gi, rows_b, 128:256],
                          ar_b * zbs - ai_b * zb + loc_scr[gi, rows_b, 384:512]]
        for gi in range(ng):
            fin_ref[gi, :, 0:128] = st[gi][0]
            fin_ref[gi, :, 128:256] = st[gi][2]
            y = jnp.dot(load_x(gi), m_ref[gi], preferred_element_type=_F32)
            y = y + jnp.dot(prev_scr[gi].astype(_BF), cout_ref[gi], preferred_element_type=_F32)
            y_ref[gi, 0] = y[:, 0:128]
            y_ref[gi, 1] = y[:, 128:256]


def _s5_scan(xgs, m_op, wst, cout, apow, s0ps, segs):
    g = S5_GROUPS
    ng = S5_GROUP_TILE
    n = len(segs)
    gspec = lambda *shape: pl.BlockSpec((ng,) + shape, lambda i: (i,) + (0,) * len(shape))
    rows = [nb * nk for nb, nk in segs]
    kseg = tuple((nb, nk, s0 is not None) for (nb, nk), s0 in zip(segs, s0ps))
    outs = pl.pallas_call(
        functools.partial(_s5_kernel, segs=kseg, ng=ng),
        grid=(g // ng,),
        in_specs=[gspec(2, r, 128) for r in rows]
                 + [gspec(S5_ROW, S5_ROW), gspec(S5_ROW, 512), gspec(256, S5_ROW), gspec(4, 128)]
                 + [gspec(nb, 256) for (nb, _), s0 in zip(segs, s0ps) if s0 is not None],
        out_specs=[gspec(2, r, 128) for r in rows] + [gspec(nb, 256) for nb, _ in segs],
        out_shape=[jax.ShapeDtypeStruct((g, 2, r, 128), _F32) for r in rows]
                  + [jax.ShapeDtypeStruct((g, nb, 256), _F32) for nb, _ in segs],
        scratch_shapes=[buf for r in rows
                        for buf in (pltpu.VMEM((ng, r, 512), _F32), pltpu.VMEM((ng, r, 256), _F32))],
        compiler_params=pltpu.CompilerParams(
            dimension_semantics=("arbitrary",), vmem_limit_bytes=VMEM_LIMIT),
        name="s5_scan",
    )(*xgs, m_op, wst, cout, apow, *[s0 for s0 in s0ps if s0 is not None])
    return outs[:n], outs[n:]


def _post_kernel(x_ref, o_ref, yg_ref, u_ref, mod_ref, dsk_ref, wglu_ref, bglu_ref, wout_ref,
                 l1g_ref, l1b_ref, w1_ref, b1_ref, w2_ref, b2_ref, l2g_ref, l2b_ref, out_ref,
                 y_scr, h_next, x1_next, h_cur, x1_cur, *, nb, spt, tm, nbg, n_steps, mod_rows):
    i = pl.program_id(0)
    pb = jnp.minimum(i, n_steps - 1) % nbg
    front = functools.partial(_post_mix_front, o_ref, yg_ref, u_ref, dsk_ref, wglu_ref, bglu_ref,
                              wout_ref, y_scr, nb=nb, spt=spt, tm=tm, pb=pb)
    back = functools.partial(_post_mix_back, x_ref, mod_ref, mod_rows[0] + mod_rows[1] * pb, wout_ref,
                             l1g_ref, l1b_ref, h_next, x1_next)

    @pl.when(i == 0)
    def _():
        back(*front())

    @pl.when(i > 0)
    def _():
        h_cur[...] = h_next[...]
        x1_cur[...] = x1_next[...]
        h = h_cur[...]
        up = functools.partial(_mlp_up, h, w1_ref, b1_ref)
        down = functools.partial(_mlp_down, w2_ref)
        acts = [up(0)]
        y, mix = front()
        acts += [up(j) for j in range(1, D_FF // MLP_CHUNK)]
        mix = mix + jnp.dot(y, wout_ref[RET_WIDTH:, :], preferred_element_type=_F32)
        back(None, mix)
        acc = down(acts[0], 0)
        for j in range(1, len(acts)):
            acc = acc + down(acts[j], j)
        f = acc + b2_ref[...]
        g2 = _mod_chunk(mod_ref, mod_rows[0] + mod_rows[1] * ((i - 1) % nbg), 5)
        out_ref[0] = _norm_rows(ALPHA * x1_cur[...] + g2 * f) * l2g_ref[...] + l2b_ref[...]


def _mlp_up(h, w1_ref, b1_ref, j):
    cols = slice(j * MLP_CHUNK, (j + 1) * MLP_CHUNK)
    a = jnp.dot(h, w1_ref[:, cols], preferred_element_type=_F32) + b1_ref[:, cols]
    return jnp.square(jnp.maximum(a, 0.0)).astype(_BF)


def _mlp_down(w2_ref, a, j):
    return jnp.dot(a, w2_ref[j * MLP_CHUNK:(j + 1) * MLP_CHUNK, :], preferred_element_type=_F32)


def _post_mix_front(o_ref, yg_ref, u_ref, dsk_ref, wglu_ref, bglu_ref, wout_ref, y_scr, *, nb, spt, tm, pb):
    mix = jnp.dot(o_ref[0], wout_ref[0:RET_WIDTH, :], preferred_element_type=_F32)
    for j in range(S5_WIDTH // 128):
        for hf in range(2):
            for q in range(tm // 128):
                start = _octet_row_start(q, pb, nb=nb, spt=spt, tm=tm)
                w = [yg_ref[8 * j + gg, hf, pl.ds(start, 8, stride=nb), :] for gg in range(8)]
                v = _granule_transpose(w)
                for a in range(8):
                    y_scr[j, pl.ds(128 * q + 8 * hf + a, 8, stride=S5_T), :] = v[a]
    y = jnp.concatenate([y_scr[j] for j in range(S5_WIDTH // 128)], axis=1)
    y = y + dsk_ref[...] * u_ref[0].astype(_F32)
    y = jax.nn.gelu(y)
    y = y * jax.nn.sigmoid(jnp.dot(y.astype(_BF), wglu_ref[...], preferred_element_type=_F32)
                           + bglu_ref[...])
    return y.astype(_BF), mix


def _post_mix_back(x_ref, mod_ref, mrow, wout_ref, l1g_ref, l1b_ref, h_out, x1_out, y, mix):
    if y is not None:
        mix = mix + jnp.dot(y, wout_ref[RET_WIDTH:, :], preferred_element_type=_F32)
    g1 = _mod_chunk(mod_ref, mrow, 2)
    sh2 = _mod_chunk(mod_ref, mrow, 3)
    sc2 = _mod_chunk(mod_ref, mrow, 4)
    x1 = _norm_rows(ALPHA * x_ref[0] + g1 * mix) * l1g_ref[...] + l1b_ref[...]
    h_out[...] = (_norm_rows(x1) * (1.0 + sc2) + sh2).astype(_BF)
    x1_out[...] = x1


def _post(x, o_ret, yg, proj, mod, mod_rows, pw):
    b, l, _ = x.shape
    tm = min(TOKEN_TILE, l)
    spt, nbg, lt = _tile_geometry(b, l, tm)
    rows = (tm // S5_T) * nbg
    n_steps = (lt // tm) * nbg
    item_a = lambda i: jnp.minimum(i, n_steps - 1)
    item_b = lambda i: jnp.maximum(i - 1, 0)
    tok = lambda w, item, cb=0: pl.BlockSpec((1, tm, w), lambda i: (item(i) % nbg, item(i) // nbg, cb))
    row = lambda n: _const_spec((1, n))
    return pl.pallas_call(
        functools.partial(_post_kernel, nb=b, spt=spt, tm=tm, nbg=nbg, n_steps=n_steps, mod_rows=mod_rows),
        grid=(n_steps + 1,),
        in_specs=[tok(D_MODEL, item_a), tok(RET_WIDTH, item_a),
                  (_const_spec((S5_GROUPS, 2, rows, 128)) if lt == tm else
                   pl.BlockSpec((S5_GROUPS, 2, rows, 128), lambda i: (0, 0, item_a(i) // nbg, 0))),
                  tok(S5_WIDTH, item_a, 4),
                  _const_spec(mod.shape),
                  row(S5_WIDTH), _const_spec((S5_WIDTH, S5_WIDTH)), row(S5_WIDTH),
                  _const_spec((D_MODEL, D_MODEL)), row(D_MODEL), row(D_MODEL),
                  _const_spec((D_MODEL, D_FF)), row(D_FF), _const_spec((D_FF, D_MODEL)), row(D_MODEL),
                  row(D_MODEL), row(D_MODEL)],
        out_specs=tok(D_MODEL, item_b),
        out_shape=jax.ShapeDtypeStruct((nbg, lt, D_MODEL), _F32),
        scratch_shapes=[pltpu.VMEM((S5_WIDTH // 128, tm, 128), _F32),
                        pltpu.VMEM((tm, D_MODEL), _BF), pltpu.VMEM((tm, D_MODEL), _F32),
                        pltpu.VMEM((tm, D_MODEL), _BF), pltpu.VMEM((tm, D_MODEL), _F32)],
        compiler_params=pltpu.CompilerParams(
            dimension_semantics=("arbitrary",), vmem_limit_bytes=VMEM_LIMIT),
        name="post",
    )(x.reshape(nbg, lt, D_MODEL), o_ret.reshape(nbg, lt, RET_WIDTH), yg,
      proj.reshape(nbg, lt, IN_COLS), mod, *pw).reshape(b, l, D_MODEL)


def _pack_s5_state(b, s5_s0_re, s5_s0_im):
    g = S5_GROUPS
    s0p = jnp.concatenate([s5_s0_re, s5_s0_im], axis=-1)
    return s0p.transpose(2, 0, 1, 3).reshape(g, b, 256)


def kernel(x_prompt, x_sample, state_ret, state_s5_re, state_s5_im, c, c_ctx, w_ada, b_ada, w_in,
           ret_decay, s5_a_re, s5_a_im, s5_log_dt, s5_b_re, s5_b_im, s5_c_re, s5_c_im, s5_d, w_glu,
           b_glu, w_out, ln1_g, ln1_b, w_ff1, b_ff1, w_ff2, b_ff2, ln2_g, ln2_b):
    depth = w_ada.shape[0]
    bs = x_sample.shape[0]
    g = S5_GROUPS
    y_p, y_s = x_prompt, x_sample
    rets, s5rs, s5is = [], [], []
    for layer in range(depth):
        *s5_ops, w_in_bf, mod = _s5_operators(
            s5_a_re[layer], s5_a_im[layer], s5_log_dt[layer], s5_b_re[layer], s5_b_im[layer],
            s5_c_re[layer], s5_c_im[layer], w_in[layer], c_ctx, c, w_ada[layer], b_ada[layer])
        r2 = lambda a: a[layer].reshape(1, -1)
        rows_p, rows_s = (0, 0), (1, 1)
        proj_p, xg_p, (w_ff1_bf,) = _projection(y_p, mod, rows_p, w_in_bf, (w_ff1[layer],))
        proj_s, xg_s, (w_glu_bf, w_out_bf, w_ff2_bf) = _projection(
            y_s, mod, rows_s, w_in_bf, (w_glu[layer], w_out[layer], w_ff2[layer]))
        o_p, r_st = _retention(proj_p, ret_decay[layer], None)
        o_s, _ = _retention(proj_s, ret_decay[layer], state_ret[:, layer])
        bp, lp, _ = y_p.shape
        segs = ((bp, lp // S5_T), (bs, y_s.shape[1] // S5_T))
        (yg_p, yg_s), (fin_p, _) = _s5_scan(
            (xg_p, xg_s), *s5_ops,
            (None, _pack_s5_state(bs, state_s5_re[:, layer], state_s5_im[:, layer])),
            segs)
        fin_p = fin_p.reshape(g, bp, 2, 2, S5_STATE).transpose(1, 2, 0, 3, 4)
        pw = (r2(s5_d), w_glu_bf, r2(b_glu), w_out_bf, r2(ln1_g), r2(ln1_b), w_ff1_bf, r2(b_ff1),
              w_ff2_bf, r2(b_ff2), r2(ln2_g), r2(ln2_b))
        y_p = _post(y_p, o_p, yg_p, proj_p, mod, rows_p, pw)
        y_s = _post(y_s, o_s, yg_s, proj_s, mod, rows_s, pw)
        rets.append(r_st)
        s5rs.append(fin_p[:, :, :, 0, :])
        s5is.append(fin_p[:, :, :, 1, :])
    return (y_p, y_s, jnp.stack(rets, axis=1), jnp.stack(s5rs, axis=1), jnp.stack(s5is, axis=1))
```

```python
import functools

import jax
import jax.numpy as jnp
from jax import lax
from jax.experimental import pallas as pl
from jax.experimental.pallas import tpu as pltpu

D_MODEL = 1024
RET_HEADS = 4
RET_DK = 128
RET_WIDTH = RET_HEADS * RET_DK
S5_CH = 16
S5_GROUPS = 32
S5_STATE = 64
S5_WIDTH = S5_GROUPS * S5_CH
D_FF = 4 * D_MODEL
IN_COLS = 4 * RET_WIDTH + S5_WIDTH
ALPHA = 2.0 ** 0.25
LN_EPS = 1e-5

S5_T = 16
S5_ROW = S5_T * S5_CH
S5_PREP_GROUP_TILE = 4
S5_GROUP_TILE = 8
RET_CHUNK = 256
RET_UNIT_TOKENS = 1024
TOKEN_TILE = 512
MLP_CHUNK = 2048
VMEM_LIMIT = 56 * 1024 * 1024

_BF = jnp.bfloat16
_F32 = jnp.float32


def _norm_rows(x):
    mu = jnp.mean(x, axis=-1, keepdims=True)
    xc = x - mu
    var = jnp.mean(xc * xc, axis=-1, keepdims=True)
    return xc * lax.rsqrt(var + LN_EPS)


def _const_spec(shape):
    nd = len(shape)
    return pl.BlockSpec(shape, lambda *_: (0,) * nd, pipeline_mode=pl.Buffered(1))


def _granule_transpose(v):
    lane = lax.broadcasted_iota(jnp.int32, (1, 128), 1)
    v = list(v)
    for d in (4, 2, 1):
        bit = ((lane // S5_CH) & d) != 0
        nv = list(v)
        for a in range(8):
            if a & d:
                continue
            lo, hi = v[a], v[a + d]
            nv[a] = jnp.where(bit, pltpu.roll(hi, S5_CH * d, axis=1), lo)
            nv[a + d] = jnp.where(bit, hi, pltpu.roll(lo, 128 - S5_CH * d, axis=1))
        v = nv
    return v


def _mod_chunk(mod_ref, row, j):
    return mod_ref[pl.ds(row, 1), j * D_MODEL:(j + 1) * D_MODEL]


def _tile_geometry(b, l, tm):
    spt = max(1, tm // l)
    return spt, b // spt, l * spt


def _octet_row_start(q, pb, *, nb, spt, tm):
    per_seq = (tm // spt) // 128
    return 8 * (q % per_seq) * nb + pb * spt + q // per_seq


def _proj_kernel(*refs, nb, spt, tm, n_cast, mod_rows):
    x_ref, mod_ref, w_ref = refs[:3]
    cast_in = refs[3:3 + n_cast]
    o_ref, xg_ref = refs[3 + n_cast:5 + n_cast]
    cast_out = refs[5 + n_cast:5 + 2 * n_cast]
    u_scr = refs[5 + 2 * n_cast]
    for src, dst in zip(cast_in, cast_out):
        dst[...] = src[...].astype(_BF)
    x = x_ref[0]
    pb = pl.program_id(1)
    mrow = mod_rows[0] + mod_rows[1] * pb
    h = (_norm_rows(x) * (1.0 + _mod_chunk(mod_ref, mrow, 1)) + _mod_chunk(mod_ref, mrow, 0)).astype(_BF)
    u = jnp.dot(h, w_ref[:, 4 * RET_WIDTH:], preferred_element_type=_F32)
    o_ref[0, :, 4 * RET_WIDTH:] = u.astype(_BF)
    for j in range(S5_WIDTH // 128):
        u_scr[j] = u[:, 128 * j:128 * (j + 1)]
    o_ref[0, :, 0:4 * RET_WIDTH] = jnp.dot(
        h, w_ref[:, 0:4 * RET_WIDTH], preferred_element_type=_F32).astype(_BF)
    for j in range(S5_WIDTH // 128):
        for hf in range(2):
            for q in range(tm // 128):
                v = [u_scr[j, pl.ds(128 * q + 8 * hf + a, 8, stride=S5_T), :] for a in range(8)]
                w = _granule_transpose(v)
                start = _octet_row_start(q, pb, nb=nb, spt=spt, tm=tm)
                for gg in range(8):
                    xg_ref[8 * j + gg, hf, pl.ds(start, 8, stride=nb), :] = w[gg]


def _projection(x, mod, mod_rows, w_in_bf, cast_weights=()):
    b, l, _ = x.shape
    tm = TOKEN_TILE
    spt, nbg, lt = _tile_geometry(b, l, tm)
    rows = (tm // S5_T) * nbg
    n_steps = (lt // tm) * nbg
    cast_specs = [pl.BlockSpec((w.shape[0] // n_steps, w.shape[1]), lambda j, i: (j * nbg + i, 0))
                  for w in cast_weights]
    outs = pl.pallas_call(
        functools.partial(_proj_kernel, nb=b, spt=spt, tm=tm, n_cast=len(cast_weights), mod_rows=mod_rows),
        grid=(lt // tm, nbg),
        in_specs=[pl.BlockSpec((1, tm, D_MODEL), lambda j, i: (i, j, 0)),
                  _const_spec(mod.shape),
                  _const_spec((D_MODEL, IN_COLS))] + cast_specs,
        out_specs=[pl.BlockSpec((1, tm, IN_COLS), lambda j, i: (i, j, 0)),
                   pl.BlockSpec((S5_GROUPS, 2, rows, 128), lambda j, i: (0, 0, j, 0))] + cast_specs,
        out_shape=[jax.ShapeDtypeStruct((nbg, lt, IN_COLS), _BF),
                   jax.ShapeDtypeStruct((S5_GROUPS, 2, (l // S5_T) * b, 128), _F32)]
                  + [jax.ShapeDtypeStruct(w.shape, _BF) for w in cast_weights],
        scratch_shapes=[pltpu.VMEM((S5_WIDTH // 128, tm, 128), _F32)],
        compiler_params=pltpu.CompilerParams(
            dimension_semantics=("arbitrary", "arbitrary"), vmem_limit_bytes=VMEM_LIMIT),
        name="projection",
    )(x.reshape(nbg, lt, D_MODEL), mod, w_in_bf, *cast_weights)
    return outs[0].reshape(b, l, IN_COLS), outs[1], tuple(outs[2:])


def _ret_kernel(*refs, chunk, n_chunks, nbb, has_s0):
    if has_s0:
        dec_ref, q_ref, k_ref, v_ref, g_ref, s0_ref, o_ref, st_ref, kv_scr, dec_scr = refs
    else:
        dec_ref, q_ref, k_ref, v_ref, g_ref, o_ref, st_ref, kv_scr, dec_scr = refs
        s0_ref = None
    c = chunk
    scale = RET_DK ** -0.5
    pos = lax.broadcasted_iota(jnp.int32, (c, 1), 0).astype(_F32)
    ri = lax.broadcasted_iota(jnp.int32, (c, c), 0)
    ci = lax.broadcasted_iota(jnp.int32, (c, c), 1)
    rel = (ri - ci).astype(_F32)

    for hd in range(RET_HEADS):
        lanes = slice(hd * RET_DK, (hd + 1) * RET_DK)

        def log_gamma(d):
            z = jnp.full((1, 1), dec_ref[d, hd], _F32)
            return jnp.minimum(z, 0.0) - jnp.log(1.0 + jnp.exp(-jnp.abs(z)))

        lg_f, lg_b = log_gamma(0), log_gamma(1)
        wide = lambda col: jnp.broadcast_to(col, (c, RET_DK))
        dec_scr[hd, 0] = wide(jnp.exp(lg_f * (c - 1.0 - pos)) * scale)
        dec_scr[hd, 1] = wide(jnp.exp(lg_b * pos) * scale)
        dec_scr[hd, 2] = wide(jnp.exp(lg_f * (pos + 1.0)))
        dec_scr[hd, 3] = wide(jnp.exp(lg_b * (c - pos)))
        dmat = (jnp.where(ri >= ci, jnp.exp(lg_f * jnp.maximum(rel, 0.0)), 0.0)
                + jnp.where(ci >= ri, jnp.exp(lg_b * jnp.maximum(-rel, 0.0)), 0.0)) * scale
        cdec_f = jnp.exp(lg_f * float(c))
        cdec_b = jnp.exp(lg_b * float(c))

        for bb in range(nbb):
            unit = bb * RET_HEADS + hd
            for n in range(n_chunks):
                kc = k_ref[bb, n * c:(n + 1) * c, lanes]
                vc = v_ref[bb, n * c:(n + 1) * c, lanes].astype(_F32)
                vcat = jnp.concatenate([(vc * dec_scr[hd, 0]).astype(_BF), (vc * dec_scr[hd, 1]).astype(_BF)], axis=1)
                kv_scr[unit, n] = lax.dot_general(kc, vcat, (((0,), (0,)), ((), ())),
                                                  preferred_element_type=_F32)
            if has_s0:
                carry_f = s0_ref[bb, 0, hd]
                carry_b = s0_ref[bb, 1, hd]
            else:
                carry_f = jnp.zeros((RET_DK, RET_DK), _F32)
                carry_b = jnp.zeros((RET_DK, RET_DK), _F32)
            for n in range(n_chunks):
                inc = kv_scr[unit, n, :, 0:RET_DK]
                kv_scr[unit, n, :, 0:RET_DK] = carry_f
                carry_f = cdec_f * carry_f + inc
            for n in range(n_chunks - 1, -1, -1):
                inc = kv_scr[unit, n, :, RET_DK:2 * RET_DK]
                kv_scr[unit, n, :, RET_DK:2 * RET_DK] = carry_b
                carry_b = cdec_b * carry_b + inc
            st_ref[bb, 0, hd] = carry_f
            st_ref[bb, 1, hd] = carry_b

            for n in range(n_chunks):
                sl = slice(n * c, (n + 1) * c)
                qc = q_ref[bb, sl, lanes]
                kc = k_ref[bb, sl, lanes]
                vc = v_ref[bb, sl, lanes]
                scores = lax.dot_general(qc, kc, (((1,), (1,)), ((), ())), preferred_element_type=_F32)
                o = jnp.dot((scores * dmat).astype(_BF), vc, preferred_element_type=_F32)
                oi = jnp.dot(qc, kv_scr[unit, n].astype(_BF), preferred_element_type=_F32)
                o = o + oi[:, 0:RET_DK] * dec_scr[hd, 2] + oi[:, RET_DK:2 * RET_DK] * dec_scr[hd, 3]
                gate = g_ref[bb, sl, lanes].astype(_F32)
                o_ref[bb, sl, lanes] = (_norm_rows(o) * (gate * jax.nn.sigmoid(gate))).astype(_BF)


def _retention(proj, ret_decay, s0):
    b, l, _ = proj.shape
    c = min(RET_CHUNK, l)
    n_chunks = l // c
    nbb = max(1, min(b, RET_UNIT_TOKENS // l))
    has_s0 = s0 is not None
    col = lambda j: pl.BlockSpec((nbb, l, RET_WIDTH), lambda i: (i, 0, j))
    st_spec = pl.BlockSpec((nbb, 2, RET_HEADS, RET_DK, RET_DK), lambda i: (i, 0, 0, 0, 0))
    in_specs = [pl.BlockSpec(memory_space=pltpu.SMEM), col(0), col(1), col(2), col(3)]
    args = [ret_decay, proj, proj, proj, proj]
    if has_s0:
        in_specs.append(st_spec)
        args.append(s0)
    return pl.pallas_call(
        functools.partial(_ret_kernel, chunk=c, n_chunks=n_chunks, nbb=nbb, has_s0=has_s0),
        grid=(b // nbb,),
        in_specs=in_specs,
        out_specs=[pl.BlockSpec((nbb, l, RET_WIDTH), lambda i: (i, 0, 0)), st_spec],
        out_shape=[jax.ShapeDtypeStruct((b, l, RET_WIDTH), _BF),
                   jax.ShapeDtypeStruct((b, 2, RET_HEADS, RET_DK, RET_DK), _F32)],
        scratch_shapes=[pltpu.VMEM((nbb * RET_HEADS, n_chunks, RET_DK, 2 * RET_DK), _F32),
                        pltpu.VMEM((RET_HEADS, 4, c, RET_DK), _F32)],
        compiler_params=pltpu.CompilerParams(
            dimension_semantics=("arbitrary",), vmem_limit_bytes=VMEM_LIMIT),
        name="retention",
    )(*args)


def _swap_halves(z):
    return pltpu.roll(z, S5_STATE, axis=z.ndim - 1)


def _s5_prep_kernel(rows_ref, mats_ref, win_ref, cctx_ref, c_ref, wada_ref, bada_ref,
                    m_ref, wst_ref, cout_ref, apow_ref, winb_ref, mod_ref, *, ng):
    winb_ref[...] = win_ref[...].astype(_BF)
    wada = wada_ref[...].astype(_BF)
    silu_dot = lambda cond: jnp.dot((cond * jax.nn.sigmoid(cond)).astype(_BF), wada,
                                    preferred_element_type=_F32) + bada_ref[...]
    nlat = c_ref.shape[0]
    mod_ref[0:1, :] = silu_dot(cctx_ref[...])
    mod_ref[1:1 + nlat, :] = silu_dot(c_ref[...])
    mod_ref[1 + nlat:, :] = jnp.zeros((mod_ref.shape[0] - 1 - nlat, mod_ref.shape[1]), _F32)
    lane = lax.broadcasted_iota(jnp.int32, (1, 2 * S5_STATE), 1)
    sgn = jnp.where(lane < S5_STATE, -1.0, 1.0).astype(_F32)
    gran = lax.broadcasted_iota(jnp.int32, (1, S5_ROW), 1) // S5_CH
    ar = rows_ref[:, 0:128]
    ai = rows_ref[:, 128:256]
    dt = jnp.exp(rows_ref[:, 256:384])
    mag = jnp.exp(ar * dt)
    ang = ai * dt
    pr1 = mag * jnp.cos(ang)
    pi1 = mag * jnp.sin(ang)
    pw_r = [jnp.ones_like(pr1), pr1]
    pw_i = [jnp.zeros_like(pi1), pi1]
    for _ in range(2, S5_T + 1):
        pr, pi = pw_r[-1], pw_i[-1]
        pw_r.append(pr * pr1 - pi * pi1)
        pw_i.append(pr * pi1 + pi * pr1)
    pw_is = [p * sgn for p in pw_i]
    x2 = pr1 - 1.0
    den = ar * ar + ai * ai
    coef_re = (x2 * ar + pi1 * ai) / den
    coef_im_s = ((pi1 * ar - x2 * ai) / den) * sgn
    for gi in range(ng):
        bt = mats_ref[gi, 0:S5_CH, :]
        cp = mats_ref[gi, S5_CH:2 * S5_CH, :]
        cc = cp * (-sgn)
        cps = _swap_halves(cp)
        bts = _swap_halves(bt)
        gens = []
        cout_rows = [[], []]
        for d in range(2):
            r = 2 * gi + d
            row = lambda a: a[r:r + 1, :]
            bb = row(coef_re) * bt + row(coef_im_s) * bts
            bbs = _swap_halves(bb)
            w_rows = []
            for m in range(S5_T):
                e = (S5_T - 1 - m) if d == 0 else m
                w = row(pw_r[e]) * bb + row(pw_is[e]) * bbs
                rows = slice(m * S5_CH, (m + 1) * S5_CH)
                wst_ref[gi, rows, d * 128:(d + 1) * 128] = w.astype(_BF)
                wst_ref[gi, rows, 256 + d * 128:256 + (d + 1) * 128] = _swap_halves(w).astype(_BF)
                w_rows.append(w)
            for i in range(S5_T):
                e = (i + 1) if d == 0 else (S5_T - i)
                gmat = row(pw_r[e]) * cp + row(pw_is[e]) * cps
                cout_rows[d].append(gmat * (-sgn))
            gens.append(lax.dot_general(cc, jnp.concatenate(w_rows, axis=0), (((1,), (1,)), ((), ())),
                                        preferred_element_type=_F32, precision=lax.Precision.HIGHEST))
            apow_ref[gi, 2 * d:2 * d + 1, :] = row(pw_r[S5_T])
            apow_ref[gi, 2 * d + 1:2 * d + 2, :] = row(pw_is[S5_T])
        blocks = []
        for t in range(S5_T):
            sf = (S5_CH * (t + 1)) % S5_ROW
            rf = pltpu.roll(gens[0], sf, axis=1) if sf else gens[0]
            rb = pltpu.roll(gens[1], S5_CH * t, axis=1) if t else gens[1]
            blocks.append(jnp.where(gran <= t, rf, 0.0) + jnp.where(gran >= t, rb, 0.0))
        m_ref[gi] = jnp.concatenate(blocks, axis=0).T.astype(_BF)
        cout_t = jnp.concatenate([jnp.concatenate(cout_rows[0], axis=0),
                                  jnp.concatenate(cout_rows[1], axis=0)], axis=1)
        cout_ref[gi] = cout_t.T.astype(_BF)


def _s5_operators(s5_a_re, s5_a_im, s5_log_dt, s5_b_re, s5_b_im, s5_c_re, s5_c_im, w_in, c_ctx, c, w_ada, b_ada):
    g = S5_GROUPS
    ng = S5_PREP_GROUP_TILE
    steps = g // ng
    wspec = pl.BlockSpec((w_in.shape[0] // steps, w_in.shape[1]), lambda i: (i, 0))
    n_mod = w_ada.shape[1]
    tn = n_mod // steps
    ldt = jnp.broadcast_to(s5_log_dt[:, :, None], s5_a_re.shape[:2] + (128,))
    rows = jnp.concatenate([s5_a_re, s5_a_re, s5_a_im, s5_a_im, ldt], axis=-1)
    rows = rows.transpose(1, 0, 2).reshape(2 * g, 384)
    mats = jnp.concatenate(
        [jnp.concatenate([s5_b_re.transpose(0, 2, 1), s5_b_im.transpose(0, 2, 1)], axis=-1),
         jnp.concatenate([s5_c_re, s5_c_im], axis=-1)], axis=1)
    gspec = lambda *shape: pl.BlockSpec((ng,) + shape, lambda i: (i,) + (0,) * len(shape))
    return pl.pallas_call(
        functools.partial(_s5_prep_kernel, ng=ng),
        grid=(g // ng,),
        in_specs=[pl.BlockSpec((2 * ng, 384), lambda i: (i, 0)), gspec(2 * S5_CH, 128), wspec,
                  pl.BlockSpec((1, D_MODEL), lambda i: (0, 0)),
                  pl.BlockSpec(c.shape, lambda i: (0, 0)),
                  pl.BlockSpec((D_MODEL, tn), lambda i: (0, i)),
                  pl.BlockSpec((1, tn), lambda i: (0, i))],
        out_specs=[gspec(S5_ROW, S5_ROW), gspec(S5_ROW, 512), gspec(256, S5_ROW), gspec(4, 128), wspec,
                   pl.BlockSpec((8, tn), lambda i: (0, i))],
        out_shape=[jax.ShapeDtypeStruct((g, S5_ROW, S5_ROW), _BF),
                   jax.ShapeDtypeStruct((g, S5_ROW, 512), _BF),
                   jax.ShapeDtypeStruct((g, 256, S5_ROW), _BF),
                   jax.ShapeDtypeStruct((g, 4, 128), _F32),
                   jax.ShapeDtypeStruct(w_in.shape, _BF),
                   jax.ShapeDtypeStruct((8, n_mod), _F32)],
        compiler_params=pltpu.CompilerParams(dimension_semantics=("arbitrary",)),
        name="s5_operators",
    )(rows, mats, w_in, c_ctx.reshape(1, D_MODEL), c, w_ada, b_ada.reshape(1, n_mod))


def _s5_kernel(*refs, segs, ng):
    n = len(segs)
    n_s0 = sum(1 for seg in segs if seg[2])
    x_refs = refs[0:n]
    m_ref, wst_ref, cout_ref, apow_ref = refs[n:n + 4]
    s0_iter = iter(refs[n + 4:n + 4 + n_s0])
    y_refs = refs[n + 4 + n_s0:2 * n + 4 + n_s0]
    fin_refs = refs[2 * n + 4 + n_s0:3 * n + 4 + n_s0]
    scr = refs[3 * n + 4 + n_s0:]
    coef = [[apow_ref[gi, r:r + 1, :] for r in range(4)] for gi in range(ng)]
    for si, (nb, nk, has_state) in enumerate(segs):
        x_ref, y_ref, fin_ref = x_refs[si], y_refs[si], fin_refs[si]
        s0_ref = next(s0_iter) if has_state else None
        loc_scr, prev_scr = scr[2 * si], scr[2 * si + 1]
        load_x = lambda gi: jnp.concatenate([x_ref[gi, 0], x_ref[gi, 1]], axis=1).astype(_BF)
        for gi in range(ng):
            loc_scr[gi] = jnp.dot(load_x(gi), wst_ref[gi], preferred_element_type=_F32)
        st = []
        for gi in range(ng):
            if has_state:
                zf = s0_ref[gi, :, 0:128]
                zb = s0_ref[gi, :, 128:256]
                st.append([zf, _swap_halves(zf), zb, _swap_halves(zb)])
            else:
                st.append([jnp.zeros((nb, 128), _F32)] * 4)
        for k in range(nk):
            kb = nk - 1 - k
            rows_f = slice(k * nb, (k + 1) * nb)
            rows_b = slice(kb * nb, (kb + 1) * nb)
            for gi in range(ng):
                ar_f, ai_f, ar_b, ai_b = coef[gi]
                zf, zfs, zb, zbs = st[gi]
                prev_scr[gi, rows_f, 0:128] = zf
                prev_scr[gi, rows_b, 128:256] = zb
                st[gi] = [ar_f * zf + ai_f * zfs + loc_scr[gi, rows_f, 0:128],
                          ar_f * zfs - ai_f * zf + loc_scr[gi, rows_f, 256:384],
                          ar_b * zb + ai_b * zbs + loc_scr[gi, rows_b, 128:256],
                          ar_b * zbs - ai_b * zb + loc_scr[gi, rows_b, 384:512]]
        for gi in range(ng):
            fin_ref[gi, :, 0:128] = st[gi][0]
            fin_ref[gi, :, 128:256] = st[gi][2]
            y = jnp.dot(load_x(gi), m_ref[gi], preferred_element_type=_F32)
            y = y + jnp.dot(prev_scr[gi].astype(_BF), cout_ref[gi], preferred_element_type=_F32)
            y_ref[gi, 0] = y[:, 0:128]
            y_ref[gi, 1] = y[:, 128:256]


def _s5_scan(xgs, m_op, wst, cout, apow, s0ps, segs):
    g = S5_GROUPS
    ng = S5_GROUP_TILE
    n = len(segs)
    gspec = lambda *shape: pl.BlockSpec((ng,) + shape, lambda i: (i,) + (0,) * len(shape))
    rows = [nb * nk for nb, nk in segs]
    kseg = tuple((nb, nk, s0 is not None) for (nb, nk), s0 in zip(segs, s0ps))
    outs = pl.pallas_call(
        functools.partial(_s5_kernel, segs=kseg, ng=ng),
        grid=(g // ng,),
        in_specs=[gspec(2, r, 128) for r in rows]
                 + [gspec(S5_ROW, S5_ROW), gspec(S5_ROW, 512), gspec(256, S5_ROW), gspec(4, 128)]
                 + [gspec(nb, 256) for (nb, _), s0 in zip(segs, s0ps) if s0 is not None],
        out_specs=[gspec(2, r, 128) for r in rows] + [gspec(nb, 256) for nb, _ in segs],
        out_shape=[jax.ShapeDtypeStruct((g, 2, r, 128), _F32) for r in rows]
                  + [jax.ShapeDtypeStruct((g, nb, 256), _F32) for nb, _ in segs],
        scratch_shapes=[buf for r in rows
                        for buf in (pltpu.VMEM((ng, r, 512), _F32), pltpu.VMEM((ng, r, 256), _F32))],
        compiler_params=pltpu.CompilerParams(
            dimension_semantics=("arbitrary",), vmem_limit_bytes=VMEM_LIMIT),
        name="s5_scan",
    )(*xgs, m_op, wst, cout, apow, *[s0 for s0 in s0ps if s0 is not None])
    return outs[:n], outs[n:]


def _post_kernel(x_ref, o_ref, yg_ref, u_ref, mod_ref, dsk_ref, wglu_ref, bglu_ref, wout_ref,
                 l1g_ref, l1b_ref, w1_ref, b1_ref, w2_ref, b2_ref, l2g_ref, l2b_ref, out_ref,
                 y_scr, h_next, x1_next, h_cur, x1_cur, *, nb, spt, tm, nbg, n_steps, mod_rows):
    i = pl.program_id(0)
    pb = jnp.minimum(i, n_steps - 1) % nbg
    front = functools.partial(_post_mix_front, o_ref, yg_ref, u_ref, dsk_ref, wglu_ref, bglu_ref,
                              wout_ref, y_scr, nb=nb, spt=spt, tm=tm, pb=pb)
    back = functools.partial(_post_mix_back, x_ref, mod_ref, mod_rows[0] + mod_rows[1] * pb, wout_ref,
                             l1g_ref, l1b_ref, h_next, x1_next)

    @pl.when(i == 0)
    def _():
        back(*front())

    @pl.when(i > 0)
    def _():
        h_cur[...] = h_next[...]
        x1_cur[...] = x1_next[...]
        h = h_cur[...]
        up = functools.partial(_mlp_up, h, w1_ref, b1_ref)
        down = functools.partial(_mlp_down, w2_ref)
        acts = [up(0)]
        y, mix = front()
        acts += [up(j) for j in range(1, D_FF // MLP_CHUNK)]
        mix = mix + jnp.dot(y, wout_ref[RET_WIDTH:, :], preferred_element_type=_F32)
        back(None, mix)
        acc = down(acts[0], 0)
        for j in range(1, len(acts)):
            acc = acc + down(acts[j], j)
        f = acc + b2_ref[...]
        g2 = _mod_chunk(mod_ref, mod_rows[0] + mod_rows[1] * ((i - 1) % nbg), 5)
        out_ref[0] = _norm_rows(ALPHA * x1_cur[...] + g2 * f) * l2g_ref[...] + l2b_ref[...]


def _mlp_up(h, w1_ref, b1_ref, j):
    cols = slice(j * MLP_CHUNK, (j + 1) * MLP_CHUNK)
    a = jnp.dot(h, w1_ref[:, cols], preferred_element_type=_F32) + b1_ref[:, cols]
    return jnp.square(jnp.maximum(a, 0.0)).astype(_BF)


def _mlp_down(w2_ref, a, j):
    return jnp.dot(a, w2_ref[j * MLP_CHUNK:(j + 1) * MLP_CHUNK, :], preferred_element_type=_F32)


def _post_mix_front(o_ref, yg_ref, u_ref, dsk_ref, wglu_ref, bglu_ref, wout_ref, y_scr, *, nb, spt, tm, pb):
    mix = jnp.dot(o_ref[0], wout_ref[0:RET_WIDTH, :], preferred_element_type=_F32)
    for j in range(S5_WIDTH // 128):
        for hf in range(2):
            for q in range(tm // 128):
                start = _octet_row_start(q, pb, nb=nb, spt=spt, tm=tm)
                w = [yg_ref[8 * j + gg, hf, pl.ds(start, 8, stride=nb), :] for gg in range(8)]
                v = _granule_transpose(w)
                for a in range(8):
                    y_scr[j, pl.ds(128 * q + 8 * hf + a, 8, stride=S5_T), :] = v[a]
    y = jnp.concatenate([y_scr[j] for j in range(S5_WIDTH // 128)], axis=1)
    y = y + dsk_ref[...] * u_ref[0].astype(_F32)
    y = jax.nn.gelu(y)
    y = y * jax.nn.sigmoid(jnp.dot(y.astype(_BF), wglu_ref[...], preferred_element_type=_F32)
                           + bglu_ref[...])
    return y.astype(_BF), mix


def _post_mix_back(x_ref, mod_ref, mrow, wout_ref, l1g_ref, l1b_ref, h_out, x1_out, y, mix):
    if y is not None:
        mix = mix + jnp.dot(y, wout_ref[RET_WIDTH:, :], preferred_element_type=_F32)
    g1 = _mod_chunk(mod_ref, mrow, 2)
    sh2 = _mod_chunk(mod_ref, mrow, 3)
    sc2 = _mod_chunk(mod_ref, mrow, 4)
    x1 = _norm_rows(ALPHA * x_ref[0] + g1 * mix) * l1g_ref[...] + l1b_ref[...]
    h_out[...] = (_norm_rows(x1) * (1.0 + sc2) + sh2).astype(_BF)
    x1_out[...] = x1


def _post(x, o_ret, yg, proj, mod, mod_rows, pw):
    b, l, _ = x.shape
    tm = min(TOKEN_TILE, l)
    spt, nbg, lt = _tile_geometry(b, l, tm)
    rows = (tm // S5_T) * nbg
    n_steps = (lt // tm) * nbg
    item_a = lambda i: jnp.minimum(i, n_steps - 1)
    item_b = lambda i: jnp.maximum(i - 1, 0)
    tok = lambda w, item, cb=0: pl.BlockSpec((1, tm, w), lambda i: (item(i) % nbg, item(i) // nbg, cb))
    row = lambda n: _const_spec((1, n))
    return pl.pallas_call(
        functools.partial(_post_kernel, nb=b, spt=spt, tm=tm, nbg=nbg, n_steps=n_steps, mod_rows=mod_rows),
        grid=(n_steps + 1,),
        in_specs=[tok(D_MODEL, item_a), tok(RET_WIDTH, item_a),
                  (_const_spec((S5_GROUPS, 2, rows, 128)) if lt == tm else
                   pl.BlockSpec((S5_GROUPS, 2, rows, 128), lambda i: (0, 0, item_a(i) // nbg, 0))),
                  tok(S5_WIDTH, item_a, 4),
                  _const_spec(mod.shape),
                  row(S5_WIDTH), _const_spec((S5_WIDTH, S5_WIDTH)), row(S5_WIDTH),
                  _const_spec((D_MODEL, D_MODEL)), row(D_MODEL), row(D_MODEL),
                  _const_spec((D_MODEL, D_FF)), row(D_FF), _const_spec((D_FF, D_MODEL)), row(D_MODEL),
                  row(D_MODEL), row(D_MODEL)],
        out_specs=tok(D_MODEL, item_b),
        out_shape=jax.ShapeDtypeStruct((nbg, lt, D_MODEL), _F32),
        scratch_shapes=[pltpu.VMEM((S5_WIDTH // 128, tm, 128), _F32),
                        pltpu.VMEM((tm, D_MODEL), _BF), pltpu.VMEM((tm, D_MODEL), _F32),
                        pltpu.VMEM((tm, D_MODEL), _BF), pltpu.VMEM((tm, D_MODEL), _F32)],
        compiler_params=pltpu.CompilerParams(
            dimension_semantics=("arbitrary",), vmem_limit_bytes=VMEM_LIMIT),
        name="post",
    )(x.reshape(nbg, lt, D_MODEL), o_ret.reshape(nbg, lt, RET_WIDTH), yg,
      proj.reshape(nbg, lt, IN_COLS), mod, *pw).reshape(b, l, D_MODEL)


def _pack_s5_state(b, s5_s0_re, s5_s0_im):
    g = S5_GROUPS
    s0p = jnp.concatenate([s5_s0_re, s5_s0_im], axis=-1)
    return s0p.transpose(2, 0, 1, 3).reshape(g, b, 256)


def kernel(x_prompt, x_sample, state_ret, state_s5_re, state_s5_im, c, c_ctx, w_ada, b_ada, w_in,
           ret_decay, s5_a_re, s5_a_im, s5_log_dt, s5_b_re, s5_b_im, s5_c_re, s5_c_im, s5_d, w_glu,
           b_glu, w_out, ln1_g, ln1_b, w_ff1, b_ff1, w_ff2, b_ff2, ln2_g, ln2_b):
    depth = w_ada.shape[0]
    bs = x_sample.shape[0]
    g = S5_GROUPS
    y_p, y_s = x_prompt, x_sample
    rets, s5rs, s5is = [], [], []
    for layer in range(depth):
        *s5_ops, w_in_bf, mod = _s5_operators(
            s5_a_re[layer], s5_a_im[layer], s5_log_dt[layer], s5_b_re[layer], s5_b_im[layer],
            s5_c_re[layer], s5_c_im[layer], w_in[layer], c_ctx, c, w_ada[layer], b_ada[layer])
        r2 = lambda a: a[layer].reshape(1, -1)
        rows_p, rows_s = (0, 0), (1, 1)
        proj_p, xg_p, (w_ff1_bf,) = _projection(y_p, mod, rows_p, w_in_bf, (w_ff1[layer],))
        proj_s, xg_s, (w_glu_bf, w_out_bf, w_ff2_bf) = _projection(
            y_s, mod, rows_s, w_in_bf, (w_glu[layer], w_out[layer], w_ff2[layer]))
        o_p, r_st = _retention(proj_p, ret_decay[layer], None)
        o_s, _ = _retention(proj_s, ret_decay[layer], state_ret[:, layer])
        bp, lp, _ = y_p.shape
        segs = ((bp, lp // S5_T), (bs, y_s.shape[1] // S5_T))
        (yg_p, yg_s), (fin_p, _) = _s5_scan(
            (xg_p, xg_s), *s5_ops,
            (None, _pack_s5_state(bs, state_s5_re[:, layer], state_s5_im[:, layer])),
            segs)
        fin_p = fin_p.reshape(g, bp, 2, 2, S5_STATE).transpose(1, 2, 0, 3, 4)
        pw = (r2(s5_d), w_glu_bf, r2(b_glu), w_out_bf, r2(ln1_g), r2(ln1_b), w_ff1_bf, r2(b_ff1),
              w_ff2_bf, r2(b_ff2), r2(ln2_g), r2(ln2_b))
        y_p = _post(y_p, o_p, yg_p, proj_p, mod, rows_p, pw)
        y_s = _post(y_s, o_s, yg_s, proj_s, mod, rows_s, pw)
        rets.append(r_st)
        s5rs.append(fin_p[:, :, :, 0, :])
        s5is.append(fin_p[:, :, :, 1, :])
    return (y_p, y_s, jnp.stack(rets, axis=1), jnp.stack(s5rs, axis=1), jnp.stack(s5is, axis=1))
```

```python
import functools

import jax
import jax.numpy as jnp
from jax import lax
from jax.experimental import pallas as pl
from jax.experimental.pallas import tpu as pltpu

D_MODEL = 1024
RET_HEADS = 4
RET_DK = 128
RET_WIDTH = RET_HEADS * RET_DK
S5_CH = 16
S5_GROUPS = 32
S5_STATE = 64
S5_WIDTH = S5_GROUPS * S5_CH
D_FF = 4 * D_MODEL
IN_COLS = 4 * RET_WIDTH + S5_WIDTH
ALPHA = 2.0 ** 0.25
LN_EPS = 1e-5

S5_T = 16
S5_ROW = S5_T * S5_CH
S5_PREP_GROUP_TILE = 4
S5_GROUP_TILE = 8
RET_CHUNK = 256
RET_UNIT_TOKENS = 1024
TOKEN_TILE = 512
MLP_CHUNK = 2048
VMEM_LIMIT = 56 * 1024 * 1024

_BF = jnp.bfloat16
_F32 = jnp.float32


def _norm_rows(x):
    mu = jnp.mean(x, axis=-1, keepdims=True)
    xc = x - mu
    var = jnp.mean(xc * xc, axis=-1, keepdims=True)
    return xc * lax.rsqrt(var + LN_EPS)


def _const_spec(shape):
    nd = len(shape)
    return pl.BlockSpec(shape, lambda *_: (0,) * nd, pipeline_mode=pl.Buffered(1))


def _granule_transpose(v):
    lane = lax.broadcasted_iota(jnp.int32, (1, 128), 1)
    v = list(v)
    for d in (4, 2, 1):
        bit = ((lane // S5_CH) & d) != 0
        nv = list(v)
        for a in range(8):
            if a & d:
                continue
            lo, hi = v[a], v[a + d]
            nv[a] = jnp.where(bit, pltpu.roll(hi, S5_CH * d, axis=1), lo)
            nv[a + d] = jnp.where(bit, hi, pltpu.roll(lo, 128 - S5_CH * d, axis=1))
        v = nv
    return v


def _mod_chunk(mod_ref, row, j):
    return mod_ref[pl.ds(row, 1), j * D_MODEL:(j + 1) * D_MODEL]


def _tile_geometry(b, l, tm):
    spt = max(1, tm // l)
    return spt, b // spt, l * spt


def _octet_row_start(q, pb, *, nb, spt, tm):
    per_seq = (tm // spt) // 128
    return 8 * (q % per_seq) * nb + pb * spt + q // per_seq


def _proj_kernel(*refs, nb, spt, tm, n_cast, mod_rows):
    x_ref, mod_ref, w_ref = refs[:3]
    cast_in = refs[3:3 + n_cast]
    o_ref, xg_ref = refs[3 + n_cast:5 + n_cast]
    cast_out = refs[5 + n_cast:5 + 2 * n_cast]
    u_scr = refs[5 + 2 * n_cast]
    for src, dst in zip(cast_in, cast_out):
        dst[...] = src[...].astype(_BF)
    x = x_ref[0]
    pb = pl.program_id(1)
    mrow = mod_rows[0] + mod_rows[1] * pb
    h = (_norm_rows(x) * (1.0 + _mod_chunk(mod_ref, mrow, 1)) + _mod_chunk(mod_ref, mrow, 0)).astype(_BF)
    u = jnp.dot(h, w_ref[:, 4 * RET_WIDTH:], preferred_element_type=_F32)
    o_ref[0, :, 4 * RET_WIDTH:] = u.astype(_BF)
    for j in range(S5_WIDTH // 128):
        u_scr[j] = u[:, 128 * j:128 * (j + 1)]
    o_ref[0, :, 0:4 * RET_WIDTH] = jnp.dot(
        h, w_ref[:, 0:4 * RET_WIDTH], preferred_element_type=_F32).astype(_BF)
    for j in range(S5_WIDTH // 128):
        for hf in range(2):
            for q in range(tm // 128):
                v = [u_scr[j, pl.ds(128 * q + 8 * hf + a, 8, stride=S5_T), :] for a in range(8)]
                w = _granule_transpose(v)
                start = _octet_row_start(q, pb, nb=nb, spt=spt, tm=tm)
                for gg in range(8):
                    xg_ref[8 * j + gg, hf, pl.ds(start, 8, stride=nb), :] = w[gg]


def _projection(x, mod, mod_rows, w_in_bf, cast_weights=()):
    b, l, _ = x.shape
    tm = TOKEN_TILE
    spt, nbg, lt = _tile_geometry(b, l, tm)
    rows = (tm // S5_T) * nbg
    n_steps = (lt // tm) * nbg
    cast_specs = [pl.BlockSpec((w.shape[0] // n_steps, w.shape[1]), lambda j, i: (j * nbg + i, 0))
                  for w in cast_weights]
    outs = pl.pallas_call(
        functools.partial(_proj_kernel, nb=b, spt=spt, tm=tm, n_cast=len(cast_weights), mod_rows=mod_rows),
        grid=(lt // tm, nbg),
        in_specs=[pl.BlockSpec((1, tm, D_MODEL), lambda j, i: (i, j, 0)),
                  _const_spec(mod.shape),
                  _const_spec((D_MODEL, IN_COLS))] + cast_specs,
        out_specs=[pl.BlockSpec((1, tm, IN_COLS), lambda j, i: (i, j, 0)),
                   pl.BlockSpec((S5_GROUPS, 2, rows, 128), lambda j, i: (0, 0, j, 0))] + cast_specs,
        out_shape=[jax.ShapeDtypeStruct((nbg, lt, IN_COLS), _BF),
                   jax.ShapeDtypeStruct((S5_GROUPS, 2, (l // S5_T) * b, 128), _F32)]
                  + [jax.ShapeDtypeStruct(w.shape, _BF) for w in cast_weights],
        scratch_shapes=[pltpu.VMEM((S5_WIDTH // 128, tm, 128), _F32)],
        compiler_params=pltpu.CompilerParams(
            dimension_semantics=("arbitrary", "arbitrary"), vmem_limit_bytes=VMEM_LIMIT),
        name="projection",
    )(x.reshape(nbg, lt, D_MODEL), mod, w_in_bf, *cast_weights)
    return outs[0].reshape(b, l, IN_COLS), outs[1], tuple(outs[2:])


def _ret_kernel(*refs, chunk, n_chunks, nbb, has_s0):
    if has_s0:
        dec_ref, q_ref, k_ref, v_ref, g_ref, s0_ref, o_ref, st_ref, kv_scr, dec_scr = refs
    else:
        dec_ref, q_ref, k_ref, v_ref, g_ref, o_ref, st_ref, kv_scr, dec_scr = refs
        s0_ref = None
    c = chunk
    scale = RET_DK ** -0.5
    pos = lax.broadcasted_iota(jnp.int32, (c, 1), 0).astype(_F32)
    ri = lax.broadcasted_iota(jnp.int32, (c, c), 0)
    ci = lax.broadcasted_iota(jnp.int32, (c, c), 1)
    rel = (ri - ci).astype(_F32)

    for hd in range(RET_HEADS):
        lanes = slice(hd * RET_DK, (hd + 1) * RET_DK)

        def log_gamma(d):
            z = jnp.full((1, 1), dec_ref[d, hd], _F32)
            return jnp.minimum(z, 0.0) - jnp.log(1.0 + jnp.exp(-jnp.abs(z)))

        lg_f, lg_b = log_gamma(0), log_gamma(1)
        wide = lambda col: jnp.broadcast_to(col, (c, RET_DK))
        dec_scr[hd, 0] = wide(jnp.exp(lg_f * (c - 1.0 - pos)) * scale)
        dec_scr[hd, 1] = wide(jnp.exp(lg_b * pos) * scale)
        dec_scr[hd, 2] = wide(jnp.exp(lg_f * (pos + 1.0)))
        dec_scr[hd, 3] = wide(jnp.exp(lg_b * (c - pos)))
        dmat = (jnp.where(ri >= ci, jnp.exp(lg_f * jnp.maximum(rel, 0.0)), 0.0)
                + jnp.where(ci >= ri, jnp.exp(lg_b * jnp.maximum(-rel, 0.0)), 0.0)) * scale
        cdec_f = jnp.exp(lg_f * float(c))
        cdec_b = jnp.exp(lg_b * float(c))

        for bb in range(nbb):
            unit = bb * RET_HEADS + hd
            for n in range(n_chunks):
                kc = k_ref[bb, n * c:(n + 1) * c, lanes]
                vc = v_ref[bb, n * c:(n + 1) * c, lanes].astype(_F32)
                vcat = jnp.concatenate([(vc * dec_scr[hd, 0]).astype(_BF), (vc * dec_scr[hd, 1]).astype(_BF)], axis=1)
                kv_scr[unit, n] = lax.dot_general(kc, vcat, (((0,), (0,)), ((), ())),
                                                  preferred_element_type=_F32)
            if has_s0:
                carry_f = s0_ref[bb, 0, hd]
                carry_b = s0_ref[bb, 1, hd]
            else:
                carry_f = jnp.zeros((RET_DK, RET_DK), _F32)
                carry_b = jnp.zeros((RET_DK, RET_DK), _F32)
            for n in range(n_chunks):
                inc = kv_scr[unit, n, :, 0:RET_DK]
                kv_scr[unit, n, :, 0:RET_DK] = carry_f
                carry_f = cdec_f * carry_f + inc
            for n in range(n_chunks - 1, -1, -1):
                inc = kv_scr[unit, n, :, RET_DK:2 * RET_DK]
                kv_scr[unit, n, :, RET_DK:2 * RET_DK] = carry_b
                carry_b = cdec_b * carry_b + inc
            st_ref[bb, 0, hd] = carry_f
            st_ref[bb, 1, hd] = carry_b

            for n in range(n_chunks):
                sl = slice(n * c, (n + 1) * c)
                qc = q_ref[bb, sl, lanes]
                kc = k_ref[bb, sl, lanes]
                vc = v_ref[bb, sl, lanes]
                scores = lax.dot_general(qc, kc, (((1,), (1,)), ((), ())), preferred_element_type=_F32)
                o = jnp.dot((scores * dmat).astype(_BF), vc, preferred_element_type=_F32)
                oi = jnp.dot(qc, kv_scr[unit, n].astype(_BF), preferred_element_type=_F32)
                o = o + oi[:, 0:RET_DK] * dec_scr[hd, 2] + oi[:, RET_DK:2 * RET_DK] * dec_scr[hd, 3]
                gate = g_ref[bb, sl, lanes].astype(_F32)
                o_ref[bb, sl, lanes] = (_norm_rows(o) * (gate * jax.nn.sigmoid(gate))).astype(_BF)


def _retention(proj, ret_decay, s0):
    b, l, _ = proj.shape
    c = min(RET_CHUNK, l)
    n_chunks = l // c
    nbb = max(1, min(b, RET_UNIT_TOKENS // l))
    has_s0 = s0 is not None
    col = lambda j: pl.BlockSpec((nbb, l, RET_WIDTH), lambda i: (i, 0, j))
    st_spec = pl.BlockSpec((nbb, 2, RET_HEADS, RET_DK, RET_DK), lambda i: (i, 0, 0, 0, 0))
    in_specs = [pl.BlockSpec(memory_space=pltpu.SMEM), col(0), col(1), col(2), col(3)]
    args = [ret_decay, proj, proj, proj, proj]
    if has_s0:
        in_specs.append(st_spec)
        args.append(s0)
    return pl.pallas_call(
        functools.partial(_ret_kernel, chunk=c, n_chunks=n_chunks, nbb=nbb, has_s0=has_s0),
        grid=(b // nbb,),
        in_specs=in_specs,
        out_specs=[pl.BlockSpec((nbb, l, RET_WIDTH), lambda i: (i, 0, 0)), st_spec],
        out_shape=[jax.ShapeDtypeStruct((b, l, RET_WIDTH), _BF),
                   jax.ShapeDtypeStruct((b, 2, RET_HEADS, RET_DK, RET_DK), _F32)],
        scratch_shapes=[pltpu.VMEM((nbb * RET_HEADS, n_chunks, RET_DK, 2 * RET_DK), _F32),
                        pltpu.VMEM((RET_HEADS, 4, c, RET_DK), _F32)],
        compiler_params=pltpu.CompilerParams(
            dimension_semantics=("arbitrary",), vmem_limit_bytes=VMEM_LIMIT),
        name="retention",
    )(*args)


def _swap_halves(z):
    return pltpu.roll(z, S5_STATE, axis=z.ndim - 1)


def _s5_prep_kernel(rows_ref, mats_ref, win_ref, cctx_ref, c_ref, wada_ref, bada_ref,
                    m_ref, wst_ref, cout_ref, apow_ref, winb_ref, mod_ref, *, ng):
    winb_ref[...] = win_ref[...].astype(_BF)
    wada = wada_ref[...].astype(_BF)
    silu_dot = lambda cond: jnp.dot((cond * jax.nn.sigmoid(cond)).astype(_BF), wada,
                                    preferred_element_type=_F32) + bada_ref[...]
    nlat = c_ref.shape[0]
    mod_ref[0:1, :] = silu_dot(cctx_ref[...])
    mod_ref[1:1 + nlat, :] = silu_dot(c_ref[...])
    mod_ref[1 + nlat:, :] = jnp.zeros((mod_ref.shape[0] - 1 - nlat, mod_ref.shape[1]), _F32)
    lane = lax.broadcasted_iota(jnp.int32, (1, 2 * S5_STATE), 1)
    sgn = jnp.where(lane < S5_STATE, -1.0, 1.0).astype(_F32)
    gran = lax.broadcasted_iota(jnp.int32, (1, S5_ROW), 1) // S5_CH
    ar = rows_ref[:, 0:128]
    ai = rows_ref[:, 128:256]
    dt = jnp.exp(rows_ref[:, 256:384])
    mag = jnp.exp(ar * dt)
    ang = ai * dt
    pr1 = mag * jnp.cos(ang)
    pi1 = mag * jnp.sin(ang)
    pw_r = [jnp.ones_like(pr1), pr1]
    pw_i = [jnp.zeros_like(pi1), pi1]
    for _ in range(2, S5_T + 1):
        pr, pi = pw_r[-1], pw_i[-1]
        pw_r.append(pr * pr1 - pi * pi1)
        pw_i.append(pr * pi1 + pi * pr1)
    pw_is = [p * sgn for p in pw_i]
    x2 = pr1 - 1.0
    den = ar * ar + ai * ai
    coef_re = (x2 * ar + pi1 * ai) / den
    coef_im_s = ((pi1 * ar - x2 * ai) / den) * sgn
    for gi in range(ng):
        bt = mats_ref[gi, 0:S5_CH, :]
        cp = mats_ref[gi, S5_CH:2 * S5_CH, :]
        cc = cp * (-sgn)
        cps = _swap_halves(cp)
        bts = _swap_halves(bt)
        gens = []
        cout_rows = [[], []]
        for d in range(2):
            r = 2 * gi + d
            row = lambda a: a[r:r + 1, :]
            bb = row(coef_re) * bt + row(coef_im_s) * bts
            bbs = _swap_halves(bb)
            w_rows = []
            for m in range(S5_T):
                e = (S5_T - 1 - m) if d == 0 else m
                w = row(pw_r[e]) * bb + row(pw_is[e]) * bbs
                rows = slice(m * S5_CH, (m + 1) * S5_CH)
                wst_ref[gi, rows, d * 128:(d + 1) * 128] = w.astype(_BF)
                w_rows.append(w)
            for i in range(S5_T):
                e = (i + 1) if d == 0 else (S5_T - i)
                gmat = row(pw_r[e]) * cp + row(pw_is[e]) * cps
                cout_rows[d].append(gmat * (-sgn))
            gens.append(lax.dot_general(cc, jnp.concatenate(w_rows, axis=0), (((1,), (1,)), ((), ())),
                                        preferred_element_type=_F32, precision=lax.Precision.HIGHEST))
            apow_ref[gi, 2 * d:2 * d + 1, :] = row(pw_r[S5_T])
            apow_ref[gi, 2 * d + 1:2 * d + 2, :] = row(pw_is[S5_T])
        blocks = []
        for t in range(S5_T):
            sf = (S5_CH * (t + 1)) % S5_ROW
            rf = pltpu.roll(gens[0], sf, axis=1) if sf else gens[0]
            rb = pltpu.roll(gens[1], S5_CH * t, axis=1) if t else gens[1]
            blocks.append(jnp.where(gran <= t, rf, 0.0) + jnp.where(gran >= t, rb, 0.0))
        m_ref[gi] = jnp.concatenate(blocks, axis=0).T.astype(_BF)
        cout_t = jnp.concatenate([jnp.concatenate(cout_rows[0], axis=0),
                                  jnp.concatenate(cout_rows[1], axis=0)], axis=1)
        cout_ref[gi] = cout_t.T.astype(_BF)


def _s5_operators(s5_a_re, s5_a_im, s5_log_dt, s5_b_re, s5_b_im, s5_c_re, s5_c_im, w_in, c_ctx, c, w_ada, b_ada):
    g = S5_GROUPS
    ng = S5_PREP_GROUP_TILE
    steps = g // ng
    wspec = pl.BlockSpec((w_in.shape[0] // steps, w_in.shape[1]), lambda i: (i, 0))
    n_mod = w_ada.shape[1]
    tn = n_mod // steps
    ldt = jnp.broadcast_to(s5_log_dt[:, :, None], s5_a_re.shape[:2] + (128,))
    rows = jnp.concatenate([s5_a_re, s5_a_re, s5_a_im, s5_a_im, ldt], axis=-1)
    rows = rows.transpose(1, 0, 2).reshape(2 * g, 384)
    mats = jnp.concatenate(
        [jnp.concatenate([s5_b_re.transpose(0, 2, 1), s5_b_im.transpose(0, 2, 1)], axis=-1),
         jnp.concatenate([s5_c_re, s5_c_im], axis=-1)], axis=1)
    gspec = lambda *shape: pl.BlockSpec((ng,) + shape, lambda i: (i,) + (0,) * len(shape))
    return pl.pallas_call(
        functools.partial(_s5_prep_kernel, ng=ng),
        grid=(g // ng,),
        in_specs=[pl.BlockSpec((2 * ng, 384), lambda i: (i, 0)), gspec(2 * S5_CH, 128), wspec,
                  pl.BlockSpec((1, D_MODEL), lambda i: (0, 0)),
                  pl.BlockSpec(c.shape, lambda i: (0, 0)),
                  pl.BlockSpec((D_MODEL, tn), lambda i: (0, i)),
                  pl.BlockSpec((1, tn), lambda i: (0, i))],
        out_specs=[gspec(S5_ROW, S5_ROW), gspec(S5_ROW, 256), gspec(256, S5_ROW), gspec(4, 128), wspec,
                   pl.BlockSpec((8, tn), lambda i: (0, i))],
        out_shape=[jax.ShapeDtypeStruct((g, S5_ROW, S5_ROW), _BF),
                   jax.ShapeDtypeStruct((g, S5_ROW, 256), _BF),
                   jax.ShapeDtypeStruct((g, 256, S5_ROW), _BF),
                   jax.ShapeDtypeStruct((g, 4, 128), _F32),
                   jax.ShapeDtypeStruct(w_in.shape, _BF),
                   jax.ShapeDtypeStruct((8, n_mod), _F32)],
        compiler_params=pltpu.CompilerParams(dimension_semantics=("arbitrary",)),
        name="s5_operators",
    )(rows, mats, w_in, c_ctx.reshape(1, D_MODEL), c, w_ada, b_ada.reshape(1, n_mod))


def _s5_kernel(*refs, segs, ng):
    n = len(segs)
    n_s0 = sum(1 for seg in segs if seg[2])
    x_refs = refs[0:n]
    m_ref, wst_ref, cout_ref, apow_ref = refs[n:n + 4]
    s0_iter = iter(refs[n + 4:n + 4 + n_s0])
    y_refs = refs[n + 4 + n_s0:2 * n + 4 + n_s0]
    fin_refs = refs[2 * n + 4 + n_s0:3 * n + 4 + n_s0]
    scr = refs[3 * n + 4 + n_s0:]
    coef = [[apow_ref[gi, r:r + 1, :] for r in range(4)] for gi in range(ng)]
    for si, (nb, nk, has_state) in enumerate(segs):
        x_ref, y_ref, fin_ref = x_refs[si], y_refs[si], fin_refs[si]
        s0_ref = next(s0_iter) if has_state else None
        loc_scr, prev_scr = scr[2 * si], scr[2 * si + 1]
        load_x = lambda gi: jnp.concatenate([x_ref[gi, 0], x_ref[gi, 1]], axis=1).astype(_BF)
        for gi in range(ng):
            loc = jnp.dot(load_x(gi), wst_ref[gi], preferred_element_type=_F32)
            loc_scr[gi, :, 0:256] = loc
            loc_scr[gi, :, 256:384] = _swap_halves(loc[:, 0:128])
            loc_scr[gi, :, 384:512] = _swap_halves(loc[:, 128:256])
        st = []
        for gi in range(ng):
            if has_state:
                zf = s0_ref[gi, :, 0:128]
                zb = s0_ref[gi, :, 128:256]
                st.append([zf, _swap_halves(zf), zb, _swap_halves(zb)])
            else:
                st.append([jnp.zeros((nb, 128), _F32)] * 4)
        for k in range(nk):
            kb = nk - 1 - k
            rows_f = slice(k * nb, (k + 1) * nb)
            rows_b = slice(kb * nb, (kb + 1) * nb)
            for gi in range(ng):
                ar_f, ai_f, ar_b, ai_b = coef[gi]
                zf, zfs, zb, zbs = st[gi]
                prev_scr[gi, rows_f, 0:128] = zf
                prev_scr[gi, rows_b, 128:256] = zb
                st[gi] = [ar_f * zf + ai_f * zfs + loc_scr[gi, rows_f, 0:128],
                          ar_f * zfs - ai_f * zf + loc_scr[gi, rows_f, 256:384],
                          ar_b * zb + ai_b * zbs + loc_scr[gi, rows_b, 128:256],
                          ar_b * zbs - ai_b * zb + loc_scr[gi, rows_b, 384:512]]
        for gi in range(ng):
            fin_ref[gi, :, 0:128] = st[gi][0]
            fin_ref[gi, :, 128:256] = st[gi][2]
            y = jnp.dot(load_x(gi), m_ref[gi], preferred_element_type=_F32)
            y = y + jnp.dot(prev_scr[gi].astype(_BF), cout_ref[gi], preferred_element_type=_F32)
            y_ref[gi, 0] = y[:, 0:128]
            y_ref[gi, 1] = y[:, 128:256]


def _s5_scan(xgs, m_op, wst, cout, apow, s0ps, segs):
    g = S5_GROUPS
    ng = S5_GROUP_TILE
    n = len(segs)
    gspec = lambda *shape: pl.BlockSpec((ng,) + shape, lambda i: (i,) + (0,) * len(shape))
    rows = [nb * nk for nb, nk in segs]
    kseg = tuple((nb, nk, s0 is not None) for (nb, nk), s0 in zip(segs, s0ps))
    outs = pl.pallas_call(
        functools.partial(_s5_kernel, segs=kseg, ng=ng),
        grid=(g // ng,),
        in_specs=[gspec(2, r, 128) for r in rows]
                 + [gspec(S5_ROW, S5_ROW), gspec(S5_ROW, 256), gspec(256, S5_ROW), gspec(4, 128)]
                 + [gspec(nb, 256) for (nb, _), s0 in zip(segs, s0ps) if s0 is not None],
        out_specs=[gspec(2, r, 128) for r in rows] + [gspec(nb, 256) for nb, _ in segs],
        out_shape=[jax.ShapeDtypeStruct((g, 2, r, 128), _F32) for r in rows]
                  + [jax.ShapeDtypeStruct((g, nb, 256), _F32) for nb, _ in segs],
        scratch_shapes=[buf for r in rows
                        for buf in (pltpu.VMEM((ng, r, 512), _F32), pltpu.VMEM((ng, r, 256), _F32))],
        compiler_params=pltpu.CompilerParams(
            dimension_semantics=("arbitrary",), vmem_limit_bytes=VMEM_LIMIT),
        name="s5_scan",
    )(*xgs, m_op, wst, cout, apow, *[s0 for s0 in s0ps if s0 is not None])
    return outs[:n], outs[n:]


def _post_kernel(x_ref, o_ref, yg_ref, u_ref, mod_ref, dsk_ref, wglu_ref, bglu_ref, wout_ref,
                 l1g_ref, l1b_ref, w1_ref, b1_ref, w2_ref, b2_ref, l2g_ref, l2b_ref, out_ref,
                 y_scr, h_next, x1_next, h_cur, x1_cur, *, nb, spt, tm, nbg, n_steps, mod_rows):
    i = pl.program_id(0)
    pb = jnp.minimum(i, n_steps - 1) % nbg
    front = functools.partial(_post_mix_front, o_ref, yg_ref, u_ref, dsk_ref, wglu_ref, bglu_ref,
                              wout_ref, y_scr, nb=nb, spt=spt, tm=tm, pb=pb)
    back = functools.partial(_post_mix_back, x_ref, mod_ref, mod_rows[0] + mod_rows[1] * pb, wout_ref,
                             l1g_ref, l1b_ref, h_next, x1_next)

    @pl.when(i == 0)
    def _():
        back(*front())

    @pl.when(i > 0)
    def _():
        h_cur[...] = h_next[...]
        x1_cur[...] = x1_next[...]
        h = h_cur[...]
        up = functools.partial(_mlp_up, h, w1_ref, b1_ref)
        down = functools.partial(_mlp_down, w2_ref)
        acts = [up(0)]
        y, mix = front()
        acts += [up(j) for j in range(1, D_FF // MLP_CHUNK)]
        mix = mix + jnp.dot(y, wout_ref[RET_WIDTH:, :], preferred_element_type=_F32)
        back(None, mix)
        acc = down(acts[0], 0)
        for j in range(1, len(acts)):
            acc = acc + down(acts[j], j)
        f = acc + b2_ref[...]
        g2 = _mod_chunk(mod_ref, mod_rows[0] + mod_rows[1] * ((i - 1) % nbg), 5)
        out_ref[0] = _norm_rows(ALPHA * x1_cur[...] + g2 * f) * l2g_ref[...] + l2b_ref[...]


def _mlp_up(h, w1_ref, b1_ref, j):
    cols = slice(j * MLP_CHUNK, (j + 1) * MLP_CHUNK)
    a = jnp.dot(h, w1_ref[:, cols], preferred_element_type=_F32) + b1_ref[:, cols]
    return jnp.square(jnp.maximum(a, 0.0)).astype(_BF)


def _mlp_down(w2_ref, a, j):
    return jnp.dot(a, w2_ref[j * MLP_CHUNK:(j + 1) * MLP_CHUNK, :], preferred_element_type=_F32)


def _post_mix_front(o_ref, yg_ref, u_ref, dsk_ref, wglu_ref, bglu_ref, wout_ref, y_scr, *, nb, spt, tm, pb):
    mix = jnp.dot(o_ref[0], wout_ref[0:RET_WIDTH, :], preferred_element_type=_F32)
    for j in range(S5_WIDTH // 128):
        for hf in range(2):
            for q in range(tm // 128):
                start = _octet_row_start(q, pb, nb=nb, spt=spt, tm=tm)
                w = [yg_ref[8 * j + gg, hf, pl.ds(start, 8, stride=nb), :] for gg in range(8)]
                v = _granule_transpose(w)
                for a in range(8):
                    y_scr[j, pl.ds(128 * q + 8 * hf + a, 8, stride=S5_T), :] = v[a]
    y = jnp.concatenate([y_scr[j] for j in range(S5_WIDTH // 128)], axis=1)
    y = y + dsk_ref[...] * u_ref[0].astype(_F32)
    y = jax.nn.gelu(y)
    y = y * jax.nn.sigmoid(jnp.dot(y.astype(_BF), wglu_ref[...], preferred_element_type=_F32)
                           + bglu_ref[...])
    return y.astype(_BF), mix


def _post_mix_back(x_ref, mod_ref, mrow, wout_ref, l1g_ref, l1b_ref, h_out, x1_out, y, mix):
    if y is not None:
        mix = mix + jnp.dot(y, wout_ref[RET_WIDTH:, :], preferred_element_type=_F32)
    g1 = _mod_chunk(mod_ref, mrow, 2)
    sh2 = _mod_chunk(mod_ref, mrow, 3)
    sc2 = _mod_chunk(mod_ref, mrow, 4)
    x1 = _norm_rows(ALPHA * x_ref[0] + g1 * mix) * l1g_ref[...] + l1b_ref[...]
    h_out[...] = (_norm_rows(x1) * (1.0 + sc2) + sh2).astype(_BF)
    x1_out[...] = x1


def _post(x, o_ret, yg, proj, mod, mod_rows, pw):
    b, l, _ = x.shape
    tm = min(TOKEN_TILE, l)
    spt, nbg, lt = _tile_geometry(b, l, tm)
    rows = (tm // S5_T) * nbg
    n_steps = (lt // tm) * nbg
    item_a = lambda i: jnp.minimum(i, n_steps - 1)
    item_b = lambda i: jnp.maximum(i - 1, 0)
    tok = lambda w, item, cb=0: pl.BlockSpec((1, tm, w), lambda i: (item(i) % nbg, item(i) // nbg, cb))
    row = lambda n: _const_spec((1, n))
    return pl.pallas_call(
        functools.partial(_post_kernel, nb=b, spt=spt, tm=tm, nbg=nbg, n_steps=n_steps, mod_rows=mod_rows),
        grid=(n_steps + 1,),
        in_specs=[tok(D_MODEL, item_a), tok(RET_WIDTH, item_a),
                  (_const_spec((S5_GROUPS, 2, rows, 128)) if lt == tm else
                   pl.BlockSpec((S5_GROUPS, 2, rows, 128), lambda i: (0, 0, item_a(i) // nbg, 0))),
                  tok(S5_WIDTH, item_a, 4),
                  _const_spec(mod.shape),
                  row(S5_WIDTH), _const_spec((S5_WIDTH, S5_WIDTH)), row(S5_WIDTH),
                  _const_spec((D_MODEL, D_MODEL)), row(D_MODEL), row(D_MODEL),
                  _const_spec((D_MODEL, D_FF)), row(D_FF), _const_spec((D_FF, D_MODEL)), row(D_MODEL),
                  row(D_MODEL), row(D_MODEL)],
        out_specs=tok(D_MODEL, item_b),
        out_shape=jax.ShapeDtypeStruct((nbg, lt, D_MODEL), _F32),
        scratch_shapes=[pltpu.VMEM((S5_WIDTH // 128, tm, 128), _F32),
                        pltpu.VMEM((tm, D_MODEL), _BF), pltpu.VMEM((tm, D_MODEL), _F32),
                        pltpu.VMEM((tm, D_MODEL), _BF), pltpu.VMEM((tm, D_MODEL), _F32)],
        compiler_params=pltpu.CompilerParams(
            dimension_semantics=("arbitrary",), vmem_limit_bytes=VMEM_LIMIT),
        name="post",
    )(x.reshape(nbg, lt, D_MODEL), o_ret.reshape(nbg, lt, RET_WIDTH), yg,
      proj.reshape(nbg, lt, IN_COLS), mod, *pw).reshape(b, l, D_MODEL)


def _pack_s5_state(b, s5_s0_re, s5_s0_im):
    g = S5_GROUPS
    s0p = jnp.concatenate([s5_s0_re, s5_s0_im], axis=-1)
    return s0p.transpose(2, 0, 1, 3).reshape(g, b, 256)


def kernel(x_prompt, x_sample, state_ret, state_s5_re, state_s5_im, c, c_ctx, w_ada, b_ada, w_in,
           ret_decay, s5_a_re, s5_a_im, s5_log_dt, s5_b_re, s5_b_im, s5_c_re, s5_c_im, s5_d, w_glu,
           b_glu, w_out, ln1_g, ln1_b, w_ff1, b_ff1, w_ff2, b_ff2, ln2_g, ln2_b):
    depth = w_ada.shape[0]
    bs = x_sample.shape[0]
    g = S5_GROUPS
    y_p, y_s = x_prompt, x_sample
    rets, s5rs, s5is = [], [], []
    for layer in range(depth):
        *s5_ops, w_in_bf, mod = _s5_operators(
            s5_a_re[layer], s5_a_im[layer], s5_log_dt[layer], s5_b_re[layer], s5_b_im[layer],
            s5_c_re[layer], s5_c_im[layer], w_in[layer], c_ctx, c, w_ada[layer], b_ada[layer])
        r2 = lambda a: a[layer].reshape(1, -1)
        rows_p, rows_s = (0, 0), (1, 1)
        proj_p, xg_p, (w_ff1_bf,) = _projection(y_p, mod, rows_p, w_in_bf, (w_ff1[layer],))
        proj_s, xg_s, (w_glu_bf, w_out_bf, w_ff2_bf) = _projection(
            y_s, mod, rows_s, w_in_bf, (w_glu[layer], w_out[layer], w_ff2[layer]))
        o_p, r_st = _retention(proj_p, ret_decay[layer], None)
        o_s, _ = _retention(proj_s, ret_decay[layer], state_ret[:, layer])
        bp, lp, _ = y_p.shape
        segs = ((bp, lp // S5_T), (bs, y_s.shape[1] // S5_T))
        (yg_p, yg_s), (fin_p, _) = _s5_scan(
            (xg_p, xg_s), *s5_ops,
            (None, _pack_s5_state(bs, state_s5_re[:, layer], state_s5_im[:, layer])),
            segs)
        fin_p = fin_p.reshape(g, bp, 2, 2, S5_STATE).transpose(1, 2, 0, 3, 4)
        pw = (r2(s5_d), w_glu_bf, r2(b_glu), w_out_bf, r2(ln1_g), r2(ln1_b), w_ff1_bf, r2(b_ff1),
              w_ff2_bf, r2(b_ff2), r2(ln2_g), r2(ln2_b))
        y_p = _post(y_p, o_p, yg_p, proj_p, mod, rows_p, pw)
        y_s = _post(y_s, o_s, yg_s, proj_s, mod, rows_s, pw)
        rets.append(r_st)
        s5rs.append(fin_p[:, :, :, 0, :])
        s5is.append(fin_p[:, :, :, 1, :])
    return (y_p, y_s, jnp.stack(rets, axis=1), jnp.stack(s5rs, axis=1), jnp.stack(s5is, axis=1))
```

```python
import functools

import jax
import jax.numpy as jnp
from jax import lax
from jax.experimental import pallas as pl
from jax.experimental.pallas import tpu as pltpu

D_MODEL = 1024
RET_HEADS = 4
RET_DK = 128
RET_WIDTH = RET_HEADS * RET_DK
S5_CH = 16
S5_GROUPS = 32
S5_STATE = 64
S5_WIDTH = S5_GROUPS * S5_CH
D_FF = 4 * D_MODEL
IN_COLS = 4 * RET_WIDTH + S5_WIDTH
ALPHA = 2.0 ** 0.25
LN_EPS = 1e-5

S5_T = 16
S5_ROW = S5_T * S5_CH
S5_PREP_GROUP_TILE = 4
S5_GROUP_TILE = 8
RET_CHUNK = 256
RET_UNIT_TOKENS = 1024
TOKEN_TILE = 512
MLP_CHUNK = 2048
VMEM_LIMIT = 56 * 1024 * 1024

_BF = jnp.bfloat16
_F32 = jnp.float32


def _norm_rows(x):
    mu = jnp.mean(x, axis=-1, keepdims=True)
    xc = x - mu
    var = jnp.mean(xc * xc, axis=-1, keepdims=True)
    return xc * lax.rsqrt(var + LN_EPS)


def _const_spec(shape):
    nd = len(shape)
    return pl.BlockSpec(shape, lambda *_: (0,) * nd, pipeline_mode=pl.Buffered(1))


def _granule_transpose(v):
    lane = lax.broadcasted_iota(jnp.int32, (1, 128), 1)
    v = list(v)
    for d in (4, 2, 1):
        bit = ((lane // S5_CH) & d) != 0
        nv = list(v)
        for a in range(8):
            if a & d:
                continue
            lo, hi = v[a], v[a + d]
            nv[a] = jnp.where(bit, pltpu.roll(hi, S5_CH * d, axis=1), lo)
            nv[a + d] = jnp.where(bit, hi, pltpu.roll(lo, 128 - S5_CH * d, axis=1))
        v = nv
    return v


def _mod_chunk(mod_ref, row, j):
    return mod_ref[pl.ds(row, 1), j * D_MODEL:(j + 1) * D_MODEL]


def _tile_geometry(b, l, tm):
    spt = max(1, tm // l)
    return spt, b // spt, l * spt


def _octet_row_start(q, pb, *, nb, spt, tm):
    per_seq = (tm // spt) // 128
    return 8 * (q % per_seq) * nb + pb * spt + q // per_seq


def _proj_kernel(*refs, nb, spt, tm, n_cast, mod_rows):
    x_ref, mod_ref, w_ref = refs[:3]
    cast_in = refs[3:3 + n_cast]
    o_ref, xg_ref = refs[3 + n_cast:5 + n_cast]
    cast_out = refs[5 + n_cast:5 + 2 * n_cast]
    u_scr = refs[5 + 2 * n_cast]
    for src, dst in zip(cast_in, cast_out):
        dst[...] = src[...].astype(_BF)
    x = x_ref[0]
    pb = pl.program_id(1)
    mrow = mod_rows[0] + mod_rows[1] * pb
    h = (_norm_rows(x) * (1.0 + _mod_chunk(mod_ref, mrow, 1)) + _mod_chunk(mod_ref, mrow, 0)).astype(_BF)
    u = jnp.dot(h, w_ref[:, 4 * RET_WIDTH:], preferred_element_type=_F32)
    o_ref[0, :, 4 * RET_WIDTH:] = u.astype(_BF)
    for j in range(S5_WIDTH // 128):
        u_scr[j] = u[:, 128 * j:128 * (j + 1)]
    o_ref[0, :, 0:4 * RET_WIDTH] = jnp.dot(
        h, w_ref[:, 0:4 * RET_WIDTH], preferred_element_type=_F32).astype(_BF)
    for j in range(S5_WIDTH // 128):
        for hf in range(2):
            for q in range(tm // 128):
                v = [u_scr[j, pl.ds(128 * q + 8 * hf + a, 8, stride=S5_T), :] for a in range(8)]
                w = _granule_transpose(v)
                start = _octet_row_start(q, pb, nb=nb, spt=spt, tm=tm)
                for gg in range(8):
                    xg_ref[8 * j + gg, hf, pl.ds(start, 8, stride=nb), :] = w[gg]


def _projection(x, mod, mod_rows, w_in_bf, cast_weights=()):
    b, l, _ = x.shape
    tm = TOKEN_TILE
    spt, nbg, lt = _tile_geometry(b, l, tm)
    rows = (tm // S5_T) * nbg
    n_steps = (lt // tm) * nbg
    cast_specs = [pl.BlockSpec((w.shape[0] // n_steps, w.shape[1]), lambda j, i: (j * nbg + i, 0))
                  for w in cast_weights]
    outs = pl.pallas_call(
        functools.partial(_proj_kernel, nb=b, spt=spt, tm=tm, n_cast=len(cast_weights), mod_rows=mod_rows),
        grid=(lt // tm, nbg),
        in_specs=[pl.BlockSpec((1, tm, D_MODEL), lambda j, i: (i, j, 0)),
                  _const_spec(mod.shape),
                  _const_spec((D_MODEL, IN_COLS))] + cast_specs,
        out_specs=[pl.BlockSpec((1, tm, IN_COLS), lambda j, i: (i, j, 0)),
                   pl.BlockSpec((S5_GROUPS, 2, rows, 128), lambda j, i: (0, 0, j, 0))] + cast_specs,
        out_shape=[jax.ShapeDtypeStruct((nbg, lt, IN_COLS), _BF),
                   jax.ShapeDtypeStruct((S5_GROUPS, 2, (l // S5_T) * b, 128), _F32)]
                  + [jax.ShapeDtypeStruct(w.shape, _BF) for w in cast_weights],
        scratch_shapes=[pltpu.VMEM((S5_WIDTH // 128, tm, 128), _F32)],
        compiler_params=pltpu.CompilerParams(
            dimension_semantics=("arbitrary", "arbitrary"), vmem_limit_bytes=VMEM_LIMIT),
        name="projection",
    )(x.reshape(nbg, lt, D_MODEL), mod, w_in_bf, *cast_weights)
    return outs[0].reshape(b, l, IN_COLS), outs[1], tuple(outs[2:])


def _ret_kernel(*refs, chunk, n_chunks, nbb, has_s0):
    if has_s0:
        dec_ref, q_ref, k_ref, v_ref, g_ref, s0_ref, o_ref, st_ref, kv_scr, dec_scr = refs
    else:
        dec_ref, q_ref, k_ref, v_ref, g_ref, o_ref, st_ref, kv_scr, dec_scr = refs
        s0_ref = None
    c = chunk
    scale = RET_DK ** -0.5
    pos = lax.broadcasted_iota(jnp.int32, (c, 1), 0).astype(_F32)
    ri = lax.broadcasted_iota(jnp.int32, (c, c), 0)
    ci = lax.broadcasted_iota(jnp.int32, (c, c), 1)
    rel = (ri - ci).astype(_F32)

    for hd in range(RET_HEADS):
        lanes = slice(hd * RET_DK, (hd + 1) * RET_DK)

        def log_gamma(d):
            z = jnp.full((1, 1), dec_ref[d, hd], _F32)
            return jnp.minimum(z, 0.0) - jnp.log(1.0 + jnp.exp(-jnp.abs(z)))

        lg_f, lg_b = log_gamma(0), log_gamma(1)
        wide = lambda col: jnp.broadcast_to(col, (c, RET_DK))
        dec_scr[hd, 0] = wide(jnp.exp(lg_f * (c - 1.0 - pos)) * scale)
        dec_scr[hd, 1] = wide(jnp.exp(lg_b * pos) * scale)
        dec_scr[hd, 2] = wide(jnp.exp(lg_f * (pos + 1.0)))
        dec_scr[hd, 3] = wide(jnp.exp(lg_b * (c - pos)))
        dmat = (jnp.where(ri >= ci, jnp.exp(lg_f * jnp.maximum(rel, 0.0)), 0.0)
                + jnp.where(ci >= ri, jnp.exp(lg_b * jnp.maximum(-rel, 0.0)), 0.0)) * scale
        cdec_f = jnp.exp(lg_f * float(c))
        cdec_b = jnp.exp(lg_b * float(c))

        for bb in range(nbb):
            unit = bb * RET_HEADS + hd
            for n in range(n_chunks):
                kc = k_ref[bb, n * c:(n + 1) * c, lanes]
                vc = v_ref[bb, n * c:(n + 1) * c, lanes].astype(_F32)
                vcat = jnp.concatenate([(vc * dec_scr[hd, 0]).astype(_BF), (vc * dec_scr[hd, 1]).astype(_BF)], axis=1)
                kv_scr[unit, n] = lax.dot_general(kc, vcat, (((0,), (0,)), ((), ())),
                                                  preferred_element_type=_F32)
            if has_s0:
                carry_f = s0_ref[bb, 0, hd]
                carry_b = s0_ref[bb, 1, hd]
            else:
                carry_f = jnp.zeros((RET_DK, RET_DK), _F32)
                carry_b = jnp.zeros((RET_DK, RET_DK), _F32)
            for n in range(n_chunks):
                inc = kv_scr[unit, n, :, 0:RET_DK]
                kv_scr[unit, n, :, 0:RET_DK] = carry_f
                carry_f = cdec_f * carry_f + inc
            for n in range(n_chunks - 1, -1, -1):
                inc = kv_scr[unit, n, :, RET_DK:2 * RET_DK]
                kv_scr[unit, n, :, RET_DK:2 * RET_DK] = carry_b
                carry_b = cdec_b * carry_b + inc
            st_ref[bb, 0, hd] = carry_f
            st_ref[bb, 1, hd] = carry_b

            for n in range(n_chunks):
                sl = slice(n * c, (n + 1) * c)
                qc = q_ref[bb, sl, lanes]
                kc = k_ref[bb, sl, lanes]
                vc = v_ref[bb, sl, lanes]
                scores = lax.dot_general(qc, kc, (((1,), (1,)), ((), ())), preferred_element_type=_F32)
                o = jnp.dot((scores * dmat).astype(_BF), vc, preferred_element_type=_F32)
                oi = jnp.dot(qc, kv_scr[unit, n].astype(_BF), preferred_element_type=_F32)
                o = o + oi[:, 0:RET_DK] * dec_scr[hd, 2] + oi[:, RET_DK:2 * RET_DK] * dec_scr[hd, 3]
                gate = g_ref[bb, sl, lanes].astype(_F32)
                o_ref[bb, sl, lanes] = (_norm_rows(o) * (gate * jax.nn.sigmoid(gate))).astype(_BF)


def _retention(proj, ret_decay, s0):
    b, l, _ = proj.shape
    c = min(RET_CHUNK, l)
    n_chunks = l // c
    nbb = max(1, min(b, RET_UNIT_TOKENS // l))
    has_s0 = s0 is not None
    col = lambda j: pl.BlockSpec((nbb, l, RET_WIDTH), lambda i: (i, 0, j))
    st_spec = pl.BlockSpec((nbb, 2, RET_HEADS, RET_DK, RET_DK), lambda i: (i, 0, 0, 0, 0))
    in_specs = [pl.BlockSpec(memory_space=pltpu.SMEM), col(0), col(1), col(2), col(3)]
    args = [ret_decay, proj, proj, proj, proj]
    if has_s0:
        in_specs.append(st_spec)
        args.append(s0)
    return pl.pallas_call(
        functools.partial(_ret_kernel, chunk=c, n_chunks=n_chunks, nbb=nbb, has_s0=has_s0),
        grid=(b // nbb,),
        in_specs=in_specs,
        out_specs=[pl.BlockSpec((nbb, l, RET_WIDTH), lambda i: (i, 0, 0)), st_spec],
        out_shape=[jax.ShapeDtypeStruct((b, l, RET_WIDTH), _BF),
                   jax.ShapeDtypeStruct((b, 2, RET_HEADS, RET_DK, RET_DK), _F32)],
        scratch_shapes=[pltpu.VMEM((nbb * RET_HEADS, n_chunks, RET_DK, 2 * RET_DK), _F32),
                        pltpu.VMEM((RET_HEADS, 4, c, RET_DK), _F32)],
        compiler_params=pltpu.CompilerParams(
            dimension_semantics=("arbitrary",), vmem_limit_bytes=VMEM_LIMIT),
        name="retention",
    )(*args)


def _swap_halves(z):
    return pltpu.roll(z, S5_STATE, axis=z.ndim - 1)


def _s5_prep_kernel(rows_ref, mats_ref, win_ref, cctx_ref, c_ref, wada_ref, bada_ref,
                    m_ref, wst_ref, cout_ref, apow_ref, winb_ref, mod_ref, *, ng):
    winb_ref[...] = win_ref[...].astype(_BF)
    wada = wada_ref[...].astype(_BF)
    silu_dot = lambda cond: jnp.dot((cond * jax.nn.sigmoid(cond)).astype(_BF), wada,
                                    preferred_element_type=_F32) + bada_ref[...]
    nlat = c_ref.shape[0]
    mod_ref[0:1, :] = silu_dot(cctx_ref[...])
    mod_ref[1:1 + nlat, :] = silu_dot(c_ref[...])
    mod_ref[1 + nlat:, :] = jnp.zeros((mod_ref.shape[0] - 1 - nlat, mod_ref.shape[1]), _F32)
    lane = lax.broadcasted_iota(jnp.int32, (1, 2 * S5_STATE), 1)
    sgn = jnp.where(lane < S5_STATE, -1.0, 1.0).astype(_F32)
    gran = lax.broadcasted_iota(jnp.int32, (1, S5_ROW), 1) // S5_CH
    ar = rows_ref[:, 0:128]
    ai = rows_ref[:, 128:256]
    dt = jnp.exp(rows_ref[:, 256:384])
    mag = jnp.exp(ar * dt)
    ang = ai * dt
    pr1 = mag * jnp.cos(ang)
    pi1 = mag * jnp.sin(ang)
    pw_r = [jnp.ones_like(pr1), pr1]
    pw_i = [jnp.zeros_like(pi1), pi1]
    for _ in range(2, S5_T + 1):
        pr, pi = pw_r[-1], pw_i[-1]
        pw_r.append(pr * pr1 - pi * pi1)
        pw_i.append(pr * pi1 + pi * pr1)
    pw_is = [p * sgn for p in pw_i]
    x2 = pr1 - 1.0
    den = ar * ar + ai * ai
    coef_re = (x2 * ar + pi1 * ai) / den
    coef_im_s = ((pi1 * ar - x2 * ai) / den) * sgn
    for gi in range(ng):
        bt = mats_ref[gi, 0:S5_CH, :]
        cp = mats_ref[gi, S5_CH:2 * S5_CH, :]
        cc = cp * (-sgn)
        cps = _swap_halves(cp)
        bts = _swap_halves(bt)
        gens = []
        cout_rows = [[], []]
        for d in range(2):
            r = 2 * gi + d
            row = lambda a: a[r:r + 1, :]
            bb = row(coef_re) * bt + row(coef_im_s) * bts
            bbs = _swap_halves(bb)
            w_rows = []
            for m in range(S5_T):
                e = (S5_T - 1 - m) if d == 0 else m
                w = row(pw_r[e]) * bb + row(pw_is[e]) * bbs
                rows = slice(m * S5_CH, (m + 1) * S5_CH)
                wst_ref[gi, rows, d * 128:(d + 1) * 128] = w.astype(_BF)
                w_rows.append(w)
            for i in range(S5_T):
                e = (i + 1) if d == 0 else (S5_T - i)
                gmat = row(pw_r[e]) * cp + row(pw_is[e]) * cps
                cout_rows[d].append(gmat * (-sgn))
            gens.append(lax.dot_general(cc, jnp.concatenate(w_rows, axis=0), (((1,), (1,)), ((), ())),
                                        preferred_element_type=_F32, precision=lax.Precision.HIGHEST))
            apow_ref[gi, 2 * d:2 * d + 1, :] = row(pw_r[S5_T])
            apow_ref[gi, 2 * d + 1:2 * d + 2, :] = row(pw_is[S5_T])
        blocks = []
        for t in range(S5_T):
            sf = (S5_CH * (t + 1)) % S5_ROW
            rf = pltpu.roll(gens[0], sf, axis=1) if sf else gens[0]
            rb = pltpu.roll(gens[1], S5_CH * t, axis=1) if t else gens[1]
            blocks.append(jnp.where(gran <= t, rf, 0.0) + jnp.where(gran >= t, rb, 0.0))
        m_ref[gi] = jnp.concatenate(blocks, axis=0).T.astype(_BF)
        cout_t = jnp.concatenate([jnp.concatenate(cout_rows[0], axis=0),
                                  jnp.concatenate(cout_rows[1], axis=0)], axis=1)
        cout_ref[gi] = cout_t.T.astype(_BF)


def _s5_operators(s5_a_re, s5_a_im, s5_log_dt, s5_b_re, s5_b_im, s5_c_re, s5_c_im, w_in, c_ctx, c, w_ada, b_ada):
    g = S5_GROUPS
    ng = S5_PREP_GROUP_TILE
    steps = g // ng
    wspec = pl.BlockSpec((w_in.shape[0] // steps, w_in.shape[1]), lambda i: (i, 0))
    n_mod = w_ada.shape[1]
    tn = n_mod // steps
    ldt = jnp.broadcast_to(s5_log_dt[:, :, None], s5_a_re.shape[:2] + (128,))
    rows = jnp.concatenate([s5_a_re, s5_a_re, s5_a_im, s5_a_im, ldt], axis=-1)
    rows = rows.transpose(1, 0, 2).reshape(2 * g, 384)
    mats = jnp.concatenate(
        [jnp.concatenate([s5_b_re.transpose(0, 2, 1), s5_b_im.transpose(0, 2, 1)], axis=-1),
         jnp.concatenate([s5_c_re, s5_c_im], axis=-1)], axis=1)
    gspec = lambda *shape: pl.BlockSpec((ng,) + shape, lambda i: (i,) + (0,) * len(shape))
    return pl.pallas_call(
        functools.partial(_s5_prep_kernel, ng=ng),
        grid=(g // ng,),
        in_specs=[pl.BlockSpec((2 * ng, 384), lambda i: (i, 0)), gspec(2 * S5_CH, 128), wspec,
                  pl.BlockSpec((1, D_MODEL), lambda i: (0, 0)),
                  pl.BlockSpec(c.shape, lambda i: (0, 0)),
                  pl.BlockSpec((D_MODEL, tn), lambda i: (0, i)),
                  pl.BlockSpec((1, tn), lambda i: (0, i))],
        out_specs=[gspec(S5_ROW, S5_ROW), gspec(S5_ROW, 256), gspec(256, S5_ROW), gspec(4, 128), wspec,
                   pl.BlockSpec((8, tn), lambda i: (0, i))],
        out_shape=[jax.ShapeDtypeStruct((g, S5_ROW, S5_ROW), _BF),
                   jax.ShapeDtypeStruct((g, S5_ROW, 256), _BF),
                   jax.ShapeDtypeStruct((g, 256, S5_ROW), _BF),
                   jax.ShapeDtypeStruct((g, 4, 128), _F32),
                   jax.ShapeDtypeStruct(w_in.shape, _BF),
                   jax.ShapeDtypeStruct((8, n_mod), _F32)],
        compiler_params=pltpu.CompilerParams(dimension_semantics=("arbitrary",)),
        name="s5_operators",
    )(rows, mats, w_in, c_ctx.reshape(1, D_MODEL), c, w_ada, b_ada.reshape(1, n_mod))


def _s5_kernel(*refs, segs, ng):
    n = len(segs)
    n_s0 = sum(1 for seg in segs if seg[2])
    x_refs = refs[0:n]
    m_ref, wst_ref, cout_ref, apow_ref = refs[n:n + 4]
    s0_iter = iter(refs[n + 4:n + 4 + n_s0])
    y_refs = refs[n + 4 + n_s0:2 * n + 4 + n_s0]
    fin_refs = refs[2 * n + 4 + n_s0:3 * n + 4 + n_s0]
    scr = refs[3 * n + 4 + n_s0:]
    coef = [[apow_ref[gi, r:r + 1, :] for r in range(4)] for gi in range(ng)]
    for si, (nb, nk, has_state) in enumerate(segs):
        x_ref, y_ref, fin_ref = x_refs[si], y_refs[si], fin_refs[si]
        s0_ref = next(s0_iter) if has_state else None
        loc_scr, prev_scr = scr[2 * si], scr[2 * si + 1]
        load_x = lambda gi: jnp.concatenate([x_ref[gi, 0], x_ref[gi, 1]], axis=1).astype(_BF)
        for gi in range(ng):
            loc = jnp.dot(load_x(gi), wst_ref[gi], preferred_element_type=_F32)
            loc_scr[gi, :, 0:256] = loc
            loc_scr[gi, :, 256:384] = _swap_halves(loc[:, 0:128])
            loc_scr[gi, :, 384:512] = _swap_halves(loc[:, 128:256])
        st = []
        for gi in range(ng):
            if has_state:
                zf = s0_ref[gi, :, 0:128]
                zb = s0_ref[gi, :, 128:256]
                st.append([zf, _swap_halves(zf), zb, _swap_halves(zb)])
            else:
                st.append([jnp.zeros((nb, 128), _F32)] * 4)
        for k in range(nk):
            kb = nk - 1 - k
            rows_f = slice(k * nb, (k + 1) * nb)
            rows_b = slice(kb * nb, (kb + 1) * nb)
            for gi in range(ng):
                ar_f, ai_f, ar_b, ai_b = coef[gi]
                zf, zfs, zb, zbs = st[gi]
                prev_scr[gi, rows_f, 0:128] = zf
                prev_scr[gi, rows_b, 128:256] = zb
                st[gi] = [ar_f * zf + ai_f * zfs + loc_scr[gi, rows_f, 0:128],
                          ar_f * zfs - ai_f * zf + loc_scr[gi, rows_f, 256:384],
                          ar_b * zb + ai_b * zbs + loc_scr[gi, rows_b, 128:256],
                          ar_b * zbs - ai_b * zb + loc_scr[gi, rows_b, 384:512]]
        for gi in range(ng):
            fin_ref[gi, :, 0:128] = st[gi][0]
            fin_ref[gi, :, 128:256] = st[gi][2]
            y = jnp.dot(load_x(gi), m_ref[gi], preferred_element_type=_F32)
            y = y + jnp.dot(prev_scr[gi].astype(_BF), cout_ref[gi], preferred_element_type=_F32)
            y_ref[gi, 0] = y[:, 0:128]
            y_ref[gi, 1] = y[:, 128:256]


def _s5_scan(xgs, m_op, wst, cout, apow, s0ps, segs):
    g = S5_GROUPS
    ng = S5_GROUP_TILE
    n = len(segs)
    gspec = lambda *shape: pl.BlockSpec((ng,) + shape, lambda i: (i,) + (0,) * len(shape))
    rows = [nb * nk for nb, nk in segs]
    kseg = tuple((nb, nk, s0 is not None) for (nb, nk), s0 in zip(segs, s0ps))
    outs = pl.pallas_call(
        functools.partial(_s5_kernel, segs=kseg, ng=ng),
        grid=(g // ng,),
        in_specs=[gspec(2, r, 128) for r in rows]
                 + [gspec(S5_ROW, S5_ROW), gspec(S5_ROW, 256), gspec(256, S5_ROW), gspec(4, 128)]
                 + [gspec(nb, 256) for (nb, _), s0 in zip(segs, s0ps) if s0 is not None],
        out_specs=[gspec(2, r, 128) for r in rows] + [gspec(nb, 256) for nb, _ in segs],
        out_shape=[jax.ShapeDtypeStruct((g, 2, r, 128), _F32) for r in rows]
                  + [jax.ShapeDtypeStruct((g, nb, 256), _F32) for nb, _ in segs],
        scratch_shapes=[buf for r in rows
                        for buf in (pltpu.VMEM((ng, r, 512), _F32), pltpu.VMEM((ng, r, 256), _F32))],
        compiler_params=pltpu.CompilerParams(
            dimension_semantics=("arbitrary",), vmem_limit_bytes=VMEM_LIMIT),
        name="s5_scan",
    )(*xgs, m_op, wst, cout, apow, *[s0 for s0 in s0ps if s0 is not None])
    return outs[:n], outs[n:]


def _post_kernel(x_ref, o_ref, yg_ref, u_ref, mod_ref, dsk_ref, wglu_ref, bglu_ref, wout_ref,
                 l1g_ref, l1b_ref, w1_hbm, b1_ref, w2_hbm, b2_ref, l2g_ref, l2b_ref, out_ref,
                 y_scr, h_next, x1_next, h_cur, x1_cur, w1_ref, w2_ref, w_sem,
                 *, nb, spt, tm, nbg, n_steps, mod_rows):
    i = pl.program_id(0)
    w_copies = (pltpu.make_async_copy(w1_hbm, w1_ref, w_sem.at[0]),
                pltpu.make_async_copy(w2_hbm, w2_ref, w_sem.at[1]))

    @pl.when(i == 0)
    def _():
        for cp in w_copies:
            cp.start()

    @pl.when(i == 1)
    def _():
        for cp in w_copies:
            cp.wait()

    pb = jnp.minimum(i, n_steps - 1) % nbg
    front = functools.partial(_post_mix_front, o_ref, yg_ref, u_ref, dsk_ref, wglu_ref, bglu_ref,
                              wout_ref, y_scr, nb=nb, spt=spt, tm=tm, pb=pb)
    back = functools.partial(_post_mix_back, x_ref, mod_ref, mod_rows[0] + mod_rows[1] * pb, wout_ref,
                             l1g_ref, l1b_ref, h_next, x1_next)

    @pl.when(i == 0)
    def _():
        back(*front())

    @pl.when(i > 0)
    def _():
        h_cur[...] = h_next[...]
        x1_cur[...] = x1_next[...]
        h = h_cur[...]
        up = functools.partial(_mlp_up, h, w1_ref, b1_ref)
        down = functools.partial(_mlp_down, w2_ref)
        acts = [up(0)]
        y, mix = front()
        acts += [up(j) for j in range(1, D_FF // MLP_CHUNK)]
        mix = mix + jnp.dot(y, wout_ref[RET_WIDTH:, :], preferred_element_type=_F32)
        back(None, mix)
        acc = down(acts[0], 0)
        for j in range(1, len(acts)):
            acc = acc + down(acts[j], j)
        f = acc + b2_ref[...]
        g2 = _mod_chunk(mod_ref, mod_rows[0] + mod_rows[1] * ((i - 1) % nbg), 5)
        out_ref[0] = _norm_rows(ALPHA * x1_cur[...] + g2 * f) * l2g_ref[...] + l2b_ref[...]


def _mlp_up(h, w1_ref, b1_ref, j):
    cols = slice(j * MLP_CHUNK, (j + 1) * MLP_CHUNK)
    a = jnp.dot(h, w1_ref[:, cols], preferred_element_type=_F32) + b1_ref[:, cols]
    return jnp.square(jnp.maximum(a, 0.0)).astype(_BF)


def _mlp_down(w2_ref, a, j):
    return jnp.dot(a, w2_ref[j * MLP_CHUNK:(j + 1) * MLP_CHUNK, :], preferred_element_type=_F32)


def _post_mix_front(o_ref, yg_ref, u_ref, dsk_ref, wglu_ref, bglu_ref, wout_ref, y_scr, *, nb, spt, tm, pb):
    mix = jnp.dot(o_ref[0], wout_ref[0:RET_WIDTH, :], preferred_element_type=_F32)
    for j in range(S5_WIDTH // 128):
        for hf in range(2):
            for q in range(tm // 128):
                start = _octet_row_start(q, pb, nb=nb, spt=spt, tm=tm)
                w = [yg_ref[8 * j + gg, hf, pl.ds(start, 8, stride=nb), :] for gg in range(8)]
                v = _granule_transpose(w)
                for a in range(8):
                    y_scr[j, pl.ds(128 * q + 8 * hf + a, 8, stride=S5_T), :] = v[a]
    y = jnp.concatenate([y_scr[j] for j in range(S5_WIDTH // 128)], axis=1)
    y = y + dsk_ref[...] * u_ref[0].astype(_F32)
    y = jax.nn.gelu(y)
    y = y * jax.nn.sigmoid(jnp.dot(y.astype(_BF), wglu_ref[...], preferred_element_type=_F32)
                           + bglu_ref[...])
    return y.astype(_BF), mix


def _post_mix_back(x_ref, mod_ref, mrow, wout_ref, l1g_ref, l1b_ref, h_out, x1_out, y, mix):
    if y is not None:
        mix = mix + jnp.dot(y, wout_ref[RET_WIDTH:, :], preferred_element_type=_F32)
    g1 = _mod_chunk(mod_ref, mrow, 2)
    sh2 = _mod_chunk(mod_ref, mrow, 3)
    sc2 = _mod_chunk(mod_ref, mrow, 4)
    x1 = _norm_rows(ALPHA * x_ref[0] + g1 * mix) * l1g_ref[...] + l1b_ref[...]
    h_out[...] = (_norm_rows(x1) * (1.0 + sc2) + sh2).astype(_BF)
    x1_out[...] = x1


def _post(x, o_ret, yg, proj, mod, mod_rows, pw):
    b, l, _ = x.shape
    tm = min(TOKEN_TILE, l)
    spt, nbg, lt = _tile_geometry(b, l, tm)
    rows = (tm // S5_T) * nbg
    n_steps = (lt // tm) * nbg
    item_a = lambda i: jnp.minimum(i, n_steps - 1)
    item_b = lambda i: jnp.maximum(i - 1, 0)
    tok = lambda w, item, cb=0: pl.BlockSpec((1, tm, w), lambda i: (item(i) % nbg, item(i) // nbg, cb))
    row = lambda n: _const_spec((1, n))
    return pl.pallas_call(
        functools.partial(_post_kernel, nb=b, spt=spt, tm=tm, nbg=nbg, n_steps=n_steps, mod_rows=mod_rows),
        grid=(n_steps + 1,),
        in_specs=[tok(D_MODEL, item_a), tok(RET_WIDTH, item_a),
                  (_const_spec((S5_GROUPS, 2, rows, 128)) if lt == tm else
                   pl.BlockSpec((S5_GROUPS, 2, rows, 128), lambda i: (0, 0, item_a(i) // nbg, 0))),
                  tok(S5_WIDTH, item_a, 4),
                  _const_spec(mod.shape),
                  row(S5_WIDTH), _const_spec((S5_WIDTH, S5_WIDTH)), row(S5_WIDTH),
                  _const_spec((D_MODEL, D_MODEL)), row(D_MODEL), row(D_MODEL),
                  pl.BlockSpec(memory_space=pl.ANY), row(D_FF), pl.BlockSpec(memory_space=pl.ANY), row(D_MODEL),
                  row(D_MODEL), row(D_MODEL)],
        out_specs=tok(D_MODEL, item_b),
        out_shape=jax.ShapeDtypeStruct((nbg, lt, D_MODEL), _F32),
        scratch_shapes=[pltpu.VMEM((S5_WIDTH // 128, tm, 128), _F32),
                        pltpu.VMEM((tm, D_MODEL), _BF), pltpu.VMEM((tm, D_MODEL), _F32),
                        pltpu.VMEM((tm, D_MODEL), _BF), pltpu.VMEM((tm, D_MODEL), _F32),
                        pltpu.VMEM((D_MODEL, D_FF), _BF), pltpu.VMEM((D_FF, D_MODEL), _BF),
                        pltpu.SemaphoreType.DMA((2,))],
        compiler_params=pltpu.CompilerParams(
            dimension_semantics=("arbitrary",), vmem_limit_bytes=VMEM_LIMIT),
        name="post",
    )(x.reshape(nbg, lt, D_MODEL), o_ret.reshape(nbg, lt, RET_WIDTH), yg,
      proj.reshape(nbg, lt, IN_COLS), mod, *pw).reshape(b, l, D_MODEL)


def _pack_s5_state(b, s5_s0_re, s5_s0_im):
    g = S5_GROUPS
    s0p = jnp.concatenate([s5_s0_re, s5_s0_im], axis=-1)
    return s0p.transpose(2, 0, 1, 3).reshape(g, b, 256)


def kernel(x_prompt, x_sample, state_ret, state_s5_re, state_s5_im, c, c_ctx, w_ada, b_ada, w_in,
           ret_decay, s5_a_re, s5_a_im, s5_log_dt, s5_b_re, s5_b_im, s5_c_re, s5_c_im, s5_d, w_glu,
           b_glu, w_out, ln1_g, ln1_b, w_ff1, b_ff1, w_ff2, b_ff2, ln2_g, ln2_b):
    depth = w_ada.shape[0]
    bs = x_sample.shape[0]
    g = S5_GROUPS
    y_p, y_s = x_prompt, x_sample
    rets, s5rs, s5is = [], [], []
    for layer in range(depth):
        *s5_ops, w_in_bf, mod = _s5_operators(
            s5_a_re[layer], s5_a_im[layer], s5_log_dt[layer], s5_b_re[layer], s5_b_im[layer],
            s5_c_re[layer], s5_c_im[layer], w_in[layer], c_ctx, c, w_ada[layer], b_ada[layer])
        r2 = lambda a: a[layer].reshape(1, -1)
        rows_p, rows_s = (0, 0), (1, 1)
        proj_p, xg_p, (w_ff1_bf,) = _projection(y_p, mod, rows_p, w_in_bf, (w_ff1[layer],))
        proj_s, xg_s, (w_glu_bf, w_out_bf, w_ff2_bf) = _projection(
            y_s, mod, rows_s, w_in_bf, (w_glu[layer], w_out[layer], w_ff2[layer]))
        o_p, r_st = _retention(proj_p, ret_decay[layer], None)
        o_s, _ = _retention(proj_s, ret_decay[layer], state_ret[:, layer])
        bp, lp, _ = y_p.shape
        segs = ((bp, lp // S5_T), (bs, y_s.shape[1] // S5_T))
        (yg_p, yg_s), (fin_p, _) = _s5_scan(
            (xg_p, xg_s), *s5_ops,
            (None, _pack_s5_state(bs, state_s5_re[:, layer], state_s5_im[:, layer])),
            segs)
        fin_p = fin_p.reshape(g, bp, 2, 2, S5_STATE).transpose(1, 2, 0, 3, 4)
        pw = (r2(s5_d), w_glu_bf, r2(b_glu), w_out_bf, r2(ln1_g), r2(ln1_b), w_ff1_bf, r2(b_ff1),
              w_ff2_bf, r2(b_ff2), r2(ln2_g), r2(ln2_b))
        y_p = _post(y_p, o_p, yg_p, proj_p, mod, rows_p, pw)
        y_s = _post(y_s, o_s, yg_s, proj_s, mod, rows_s, pw)
        rets.append(r_st)
        s5rs.append(fin_p[:, :, :, 0, :])
        s5is.append(fin_p[:, :, :, 1, :])
    return (y_p, y_s, jnp.stack(rets, axis=1), jnp.stack(s5rs, axis=1), jnp.stack(s5is, axis=1))
```
